```python
import math
import jax, jax.numpy as jnp
from jax import lax
import numpy as np

D_MODEL = 4096
BATCH = 8
SEQ = 2048
DEPTH = 1

H_A = 16
QK_NOPE = 128
QK_ROPE = 64
QK_DIM_A = QK_NOPE + QK_ROPE
V_DIM_A = 128
Q_LORA = 1024
KV_LORA = 512
ROPE_THETA = 10000.0
H_B = 16
HEAD_DIM_B = 128
IDX_HEADS = 32
IDX_DIM = 64
TOPK_MAX = 256
N_BUCKETS = 32
MAX_DISTANCE = 128
Q_BLOCK = 128
EPS = 1e-6

WIDTH_A = H_A * V_DIM_A
WIDTH_B = H_B * HEAD_DIM_B

COL_SIZES = [
    Q_LORA,
    KV_LORA,
    QK_ROPE,
    H_B * HEAD_DIM_B,
    HEAD_DIM_B,
    HEAD_DIM_B,
    IDX_HEADS * IDX_DIM,
    IDX_DIM,
    IDX_HEADS,
    WIDTH_A,
    WIDTH_B,
    D_MODEL,
    D_MODEL,
]
IN_COLS = int(sum(COL_SIZES))
SPLIT_POINTS = [int(v) for v in np.cumsum(COL_SIZES)[:-1]]

kernel_name = "hybrid_mla_dsa_gated_merge"


def rmsnorm(x, g):
    xf = x.astype(jnp.float32)
    y = xf * lax.rsqrt(jnp.mean(xf * xf, axis=-1, keepdims=True) + EPS)
    return (y * g.astype(jnp.float32)).astype(x.dtype)


def apply_rope(x, pos):
    half = x.shape[-1] // 2
    inv = ROPE_THETA ** (-jnp.arange(half, dtype=jnp.float32) / half)
    ang = pos.astype(jnp.float32)[:, :, None, None] * inv
    cos, sin = jnp.cos(ang), jnp.sin(ang)
    x1 = x[..., :half].astype(jnp.float32)
    x2 = x[..., half:].astype(jnp.float32)
    return jnp.concatenate([x1 * cos - x2 * sin, x1 * sin + x2 * cos], axis=-1).astype(x.dtype)


def t5_bucket(dist):
    max_exact = N_BUCKETS // 2
    n = jnp.maximum(dist, 0)
    nf = jnp.maximum(n, 1).astype(jnp.float32)
    large = max_exact + (jnp.log(nf / max_exact) / math.log(MAX_DISTANCE / max_exact)
                         * (N_BUCKETS - max_exact)).astype(jnp.int32)
    large = jnp.minimum(large, N_BUCKETS - 1)
    return jnp.where(n < max_exact, n, large)


def to_blocks(a):
    b, l = a.shape[0], a.shape[1]
    a = a.reshape((b, l // Q_BLOCK, Q_BLOCK) + a.shape[2:])
    return jnp.moveaxis(a, 1, 0)


def from_blocks(a):
    a = jnp.moveaxis(a, 0, 1)
    return a.reshape((a.shape[0], a.shape[1] * a.shape[2]) + a.shape[3:])


def causal_dense_attention(q, k, v, pos, scale):
    def one(args):
        qi, pi = args
        s = jnp.einsum('bqhd,bshd->bhqs', qi, k, preferred_element_type=jnp.float32) * scale
        mask = pi[:, None, :, None] >= pos[:, None, None, :]
        s = jnp.where(mask, s, -jnp.inf)
        p = jax.nn.softmax(s, axis=-1).astype(v.dtype)
        return jnp.einsum('bhqs,bshd->bqhd', p, v)
    out = lax.map(one, (to_blocks(q), to_blocks(pos)))
    return from_blocks(out)


def indexed_sparse_attention(q, k, v, q_idx, k_idx, w_idx, pos, t5_bias, scale):
    n_keys = k.shape[1]
    topk = min(TOPK_MAX, n_keys // 4)
    gather = jax.vmap(lambda table, idx: table[idx])

    def one(args):
        qi, qxi, wi, pi = args
        logits = jnp.einsum('bqhd,bsd->bqsh', qxi, k_idx, preferred_element_type=jnp.float32)
        score = jnp.einsum('bqsh,bqh->bqs', jax.nn.relu(logits), wi.astype(jnp.float32))
        admissible = pi[:, :, None] >= pos[:, None, :]
        score = jnp.where(admissible, score, -jnp.inf)
        _, sel = lax.top_k(score, topk)
        kg = gather(k, sel)
        vg = gather(v, sel)
        dist = pi[:, :, None] - gather(pos, sel)
        bias = t5_bias[t5_bucket(dist)].astype(jnp.float32)
        s = jnp.einsum('bqhd,bqkd->bqhk', qi, kg, preferred_element_type=jnp.float32) * scale
        s = s + jnp.swapaxes(bias, -1, -2)
        s = jnp.where((dist >= 0)[:, :, None, :], s, -jnp.inf)
        p = jax.nn.softmax(s, axis=-1).astype(vg.dtype)
        return jnp.einsum('bqhk,bqkd->bqhd', p, vg)

    out = lax.map(one, (to_blocks(q), to_blocks(q_idx), to_blocks(w_idx), to_blocks(pos)))
    return from_blocks(out)


def setup_inputs(seed: int = 0) -> dict:
    key = jax.random.key(seed)
    ks = jax.random.split(key, 20)
    f32 = jnp.float32
    nrm = lambda k, shape, fan: jax.random.normal(k, shape, f32) * (fan ** -0.5)
    gain = lambda k, n: 1.0 + 0.05 * jax.random.normal(k, (n,), f32)
    x = jax.random.normal(ks[0], (BATCH, SEQ, D_MODEL), f32)
    offset = jax.random.randint(ks[1], (BATCH, 1), 0, 1024, dtype=jnp.int32)
    positions = offset + jnp.arange(SEQ, dtype=jnp.int32)[None, :]
    return {
        "x": x,
        "positions": positions,
        "g_pre": gain(ks[2], D_MODEL),
        "w_in": nrm(ks[3], (D_MODEL, IN_COLS), D_MODEL),
        "g_q_lat": gain(ks[4], Q_LORA),
        "g_kv_lat": gain(ks[5], KV_LORA),
        "w_uq": nrm(ks[6], (Q_LORA, H_A * QK_DIM_A), Q_LORA),
        "w_ukv": nrm(ks[7], (KV_LORA, H_A * (QK_NOPE + V_DIM_A)), KV_LORA),
        "g_qn_a": gain(ks[8], QK_DIM_A),
        "g_kn_a": gain(ks[9], QK_DIM_A),
        "g_qn_b": gain(ks[10], HEAD_DIM_B),
        "g_kn_b": gain(ks[11], HEAD_DIM_B),
        "t5_bias": 0.5 * jax.random.normal(ks[12], (N_BUCKETS, H_B), f32),
        "p_a": nrm(ks[13], (WIDTH_A, D_MODEL), WIDTH_A),
        "p_b": nrm(ks[14], (WIDTH_B, D_MODEL), WIDTH_B),
        "w_o": nrm(ks[15], (D_MODEL, D_MODEL), D_MODEL),
    }


def reference(x, positions, g_pre, w_in, g_q_lat, g_kv_lat, w_uq, w_ukv, g_qn_a, g_kn_a,
              g_qn_b, g_kn_b, t5_bias, p_a, p_b, w_o):
    b, l, _ = x.shape
    h = x
    for _layer in range(DEPTH):
        hn = rmsnorm(h, g_pre)
        proj = hn @ w_in
        (cq, ckv, k_rope, q_b, k_b, v_b, q_idx, k_idx, w_idx,
         gate_a, gate_b, merge_a, merge_b) = jnp.split(proj, SPLIT_POINTS, axis=-1)

        q_a = (rmsnorm(cq, g_q_lat) @ w_uq).reshape(b, l, H_A, QK_DIM_A)
        kv = (rmsnorm(ckv, g_kv_lat) @ w_ukv).reshape(b, l, H_A, QK_NOPE + V_DIM_A)
        k_nope, v_a = kv[..., :QK_NOPE], kv[..., QK_NOPE:]
        k_r = jnp.broadcast_to(k_rope[:, :, None, :], (b, l, H_A, QK_ROPE))
        k_a = jnp.concatenate([k_nope, k_r], axis=-1)
        q_a = rmsnorm(q_a, g_qn_a)
        k_a = rmsnorm(k_a, g_kn_a)
        q_a = jnp.concatenate([q_a[..., :QK_NOPE], apply_rope(q_a[..., QK_NOPE:], positions)], -1)
        k_a = jnp.concatenate([k_a[..., :QK_NOPE], apply_rope(k_a[..., QK_NOPE:], positions)], -1)
        o_a = causal_dense_attention(q_a, k_a, v_a, positions, QK_DIM_A ** -0.5)
        o_a = o_a.reshape(b, l, WIDTH_A) * jax.nn.silu(gate_a)

        q_bh = rmsnorm(q_b.reshape(b, l, H_B, HEAD_DIM_B), g_qn_b)
        k_bh = rmsnorm(k_b, g_kn_b)
        q_ix = q_idx.reshape(b, l, IDX_HEADS, IDX_DIM)
        w_ix = w_idx * (IDX_HEADS ** -0.5)
        o_b = indexed_sparse_attention(q_bh, k_bh, v_b, q_ix, k_idx, w_ix, positions,
                                       t5_bias, HEAD_DIM_B ** -0.5)
        o_b = o_b.reshape(b, l, WIDTH_B) * jax.nn.silu(gate_b)

        merged = jax.nn.sigmoid(merge_a) * (o_a @ p_a) + jax.nn.sigmoid(merge_b) * (o_b @ p_b)
        h = h + merged @ w_o
    return h
```

```python
import functools
import math

import jax
import jax.numpy as jnp
from jax import lax
from jax.experimental import pallas as pl
from jax.experimental.pallas import tpu as pltpu

F32 = jnp.float32
I32 = jnp.int32
MXU_DTYPE = jnp.bfloat16

H_A = 16
QK_NOPE = 128
QK_ROPE = 64
QK_DIM_A = QK_NOPE + QK_ROPE
V_DIM_A = 128
Q_LORA = 1024
KV_LORA = 512
ROPE_THETA = 10000.0
H_B = 16
HEAD_DIM_B = 128
IDX_HEADS = 32
IDX_DIM = 64
TOPK_MAX = 256
N_BUCKETS = 32
MAX_DISTANCE = 128
EPS = 1e-6
WIDTH_A = H_A * V_DIM_A
WIDTH_B = H_B * HEAD_DIM_B

LANE = 128
HALF_ROPE = QK_ROPE // 2
HEAD_PAD_A = 2 * LANE
VMEM_LIMIT = 56 * 1024 * 1024

NEG_INF = float("-inf")
POS_INF = float("inf")


def _nt_dot(a, b):
    return lax.dot_general(a, b, (((1,), (1,)), ((), ())), preferred_element_type=F32)


def _params(sem, vmem=VMEM_LIMIT):
    return pltpu.CompilerParams(dimension_semantics=sem, vmem_limit_bytes=vmem)


def _in_proj_kernel(x_ref, g_ref, w_ref, o_ref, hn_ref, *, row_chunk):
    tm = x_ref.shape[0]

    @pl.when(pl.program_id(1) == 0)
    def _():
        def body(r, carry):
            sl = pl.ds(pl.multiple_of(r * row_chunk, row_chunk), row_chunk)
            xx = x_ref[sl, :]
            ms = jnp.mean(xx * xx, axis=-1, keepdims=True)
            hn_ref[sl, :] = (xx * lax.rsqrt(ms + EPS) * g_ref[...]).astype(hn_ref.dtype)
            return carry

        lax.fori_loop(0, tm // row_chunk, body, 0)

    o_ref[...] = jnp.dot(hn_ref[...], w_ref[...], preferred_element_type=F32).astype(o_ref.dtype)


def _in_proj(x2, g_pre, w_pad, *, tm, tn):
    t, d = x2.shape
    n = w_pad.shape[1]
    return pl.pallas_call(
        functools.partial(_in_proj_kernel, row_chunk=64),
        out_shape=jax.ShapeDtypeStruct((t, n), MXU_DTYPE),
        grid=(t // tm, n // tn),
        in_specs=[
            pl.BlockSpec((tm, d), lambda i, j: (i, 0)),
            pl.BlockSpec((1, d), lambda i, j: (0, 0)),
            pl.BlockSpec((d, tn), lambda i, j: (0, j)),
        ],
        out_specs=pl.BlockSpec((tm, tn), lambda i, j: (i, j)),
        scratch_shapes=[pltpu.VMEM((tm, d), MXU_DTYPE)],
        compiler_params=_params(("arbitrary", "arbitrary")),
        name="in_proj",
    )(x2, g_pre.reshape(1, d), w_pad)


def _rope_lanes(r, cos_ref, sin_ref):
    return r * cos_ref[...] + pltpu.roll(r, 2 * HALF_ROPE, 1) * sin_ref[...]


def _qa_proj_kernel(cq_ref, gl_ref, w_ref, gq_ref, cos_ref, sin_ref, o_ref):
    c = cq_ref[...].astype(F32)
    ms = jnp.mean(c * c, axis=-1, keepdims=True)
    cn = (c * lax.rsqrt(ms + EPS) * gl_ref[...]).astype(MXU_DTYPE)
    q = jnp.dot(cn, w_ref[...], preferred_element_type=F32)
    for h in range(H_A):
        lo = h * HEAD_PAD_A
        qh = q[:, lo:lo + HEAD_PAD_A]
        ss = jnp.sum(qh * qh, axis=-1, keepdims=True) * (1.0 / QK_DIM_A)
        qn = qh * lax.rsqrt(ss + EPS) * gq_ref[:, lo:lo + HEAD_PAD_A]
        o_ref[:, lo:lo + LANE] = qn[:, :LANE].astype(o_ref.dtype)
        o_ref[:, lo + LANE:lo + HEAD_PAD_A] = _rope_lanes(qn[:, LANE:], cos_ref, sin_ref).astype(o_ref.dtype)


def _qa_proj(proj, cq_blk, g_q_lat, w_uq_pad, gq_pad, cos_t, sin_t, *, tm):
    t = proj.shape[0]
    nq = H_A * HEAD_PAD_A
    return pl.pallas_call(
        _qa_proj_kernel,
        out_shape=jax.ShapeDtypeStruct((t, nq), MXU_DTYPE),
        grid=(t // tm,),
        in_specs=[
            pl.BlockSpec((tm, Q_LORA), lambda i: (i, cq_blk)),
            pl.BlockSpec((1, Q_LORA), lambda i: (0, 0)),
            pl.BlockSpec((Q_LORA, nq), lambda i: (0, 0)),
            pl.BlockSpec((1, nq), lambda i: (0, 0)),
            pl.BlockSpec((tm, LANE), lambda i: (i, 0)),
            pl.BlockSpec((tm, LANE), lambda i: (i, 0)),
        ],
        out_specs=pl.BlockSpec((tm, nq), lambda i: (i, 0)),
        compiler_params=_params(("arbitrary",)),
        name="qa_proj",
    )(proj, g_q_lat.reshape(1, Q_LORA), w_uq_pad, gq_pad, cos_t, sin_t)


def _kva_proj_kernel(ckv_ref, kr_ref, gl_ref, wk_ref, wv_ref, gkn_ref, gkr_ref, cos_ref, sin_ref,
                     k_ref, v_ref):
    c = ckv_ref[...].astype(F32)
    ms = jnp.mean(c * c, axis=-1, keepdims=True)
    cn = (c * lax.rsqrt(ms + EPS) * gl_ref[...]).astype(MXU_DTYPE)
    kn = jnp.dot(cn, wk_ref[...], preferred_element_type=F32)
    v_ref[...] = jnp.dot(cn, wv_ref[...], preferred_element_type=F32).astype(v_ref.dtype)
    kr = kr_ref[...].astype(F32)
    ss_r = jnp.sum(kr * kr, axis=-1, keepdims=True)
    krr = _rope_lanes(kr * gkr_ref[...], cos_ref, sin_ref)
    for h in range(H_A):
        kh = kn[:, h * LANE:(h + 1) * LANE]
        ss = (jnp.sum(kh * kh, axis=-1, keepdims=True) + ss_r) * (1.0 / QK_DIM_A)
        rs = lax.rsqrt(ss + EPS)
        lo = h * HEAD_PAD_A
        k_ref[:, lo:lo + LANE] = (kh * rs * gkn_ref[...]).astype(k_ref.dtype)
        k_ref[:, lo + LANE:lo + HEAD_PAD_A] = (krr * rs).astype(k_ref.dtype)


def _kva_proj(proj, ckv_blk, krope_blk, g_kv_lat, w_uk, w_uv, gk_nope, gk_rope, cos_t, sin_t, *, tm):
    t = proj.shape[0]
    return pl.pallas_call(
        _kva_proj_kernel,
        out_shape=(jax.ShapeDtypeStruct((t, H_A * HEAD_PAD_A), MXU_DTYPE),
                   jax.ShapeDtypeStruct((t, WIDTH_A), MXU_DTYPE)),
        grid=(t // tm,),
        in_specs=[
            pl.BlockSpec((tm, KV_LORA), lambda i: (i, ckv_blk)),
            pl.BlockSpec((tm, LANE), lambda i: (i, krope_blk)),
            pl.BlockSpec((1, KV_LORA), lambda i: (0, 0)),
            pl.BlockSpec((KV_LORA, H_A * QK_NOPE), lambda i: (0, 0)),
            pl.BlockSpec((KV_LORA, WIDTH_A), lambda i: (0, 0)),
            pl.BlockSpec((1, LANE), lambda i: (0, 0)),
            pl.BlockSpec((1, LANE), lambda i: (0, 0)),
            pl.BlockSpec((tm, LANE), lambda i: (i, 0)),
            pl.BlockSpec((tm, LANE), lambda i: (i, 0)),
        ],
        out_specs=(pl.BlockSpec((tm, H_A * HEAD_PAD_A), lambda i: (i, 0)),
                   pl.BlockSpec((tm, WIDTH_A), lambda i: (i, 0))),
        compiler_params=_params(("arbitrary",)),
        name="kva_proj",
    )(proj, proj, g_kv_lat.reshape(1, KV_LORA), w_uk, w_uv, gk_nope, gk_rope, cos_t, sin_t)


def _silu(g):
    return g * (1.0 / (1.0 + jnp.exp(-g)))


def _attn_a_kernel(q_ref, k_ref, v_ref, gate_ref, o_ref, *, tq):
    qi = pl.program_id(2)
    q = q_ref[...]

    def step(j, carry, masked):
        m, l, acc = carry
        sl = pl.ds(pl.multiple_of(j * tq, tq), tq)
        s = _nt_dot(q, k_ref[sl, :])
        if masked:
            row = lax.broadcasted_iota(I32, s.shape, 0)
            col = lax.broadcasted_iota(I32, s.shape, 1)
            s = jnp.where(row >= col, s, NEG_INF)
        m_new = jnp.maximum(m, jnp.max(s, axis=-1, keepdims=True))
        p = jnp.exp(s - m_new)
        alpha = jnp.exp(m - m_new)
        l = alpha * l + jnp.sum(p, axis=-1, keepdims=True)
        acc = alpha * acc + jnp.dot(p.astype(MXU_DTYPE), v_ref[sl, :], preferred_element_type=F32)
        return m_new, l, acc

    init = (jnp.full((tq, 1), NEG_INF, F32), jnp.zeros((tq, 1), F32), jnp.zeros((tq, V_DIM_A), F32))
    carry = lax.fori_loop(0, qi, lambda j, c: step(j, c, False), init)
    _, l, acc = step(qi, carry, True)
    o = acc * (1.0 / l)
    o_ref[...] = (o * _silu(gate_ref[...].astype(F32))).astype(o_ref.dtype)


def _attn_a(q_a, k_a, v_a, proj, gate_blk0, *, b, l, tq):
    t = q_a.shape[0]
    nq = l // tq
    return pl.pallas_call(
        functools.partial(_attn_a_kernel, tq=tq),
        out_shape=jax.ShapeDtypeStruct((t, WIDTH_A), MXU_DTYPE),
        grid=(b, H_A, nq),
        in_specs=[
            pl.BlockSpec((tq, HEAD_PAD_A), lambda bi, h, qi: (bi * nq + qi, h)),
            pl.BlockSpec((l, HEAD_PAD_A), lambda bi, h, qi: (bi, h)),
            pl.BlockSpec((l, V_DIM_A), lambda bi, h, qi: (bi, h)),
            pl.BlockSpec((tq, V_DIM_A), lambda bi, h, qi: (bi * nq + qi, gate_blk0 + h)),
        ],
        out_specs=pl.BlockSpec((tq, V_DIM_A), lambda bi, h, qi: (bi * nq + qi, h)),
        compiler_params=_params(("arbitrary", "arbitrary", "arbitrary")),
        name="attn_a",
    )(q_a, k_a, v_a, proj)


def _t5_bucket(dist):
    max_exact = N_BUCKETS // 2
    n = jnp.maximum(dist, 0)
    nf = jnp.maximum(n, 1).astype(F32)
    large = max_exact + (jnp.log(nf / max_exact) / math.log(MAX_DISTANCE / max_exact)
                         * (N_BUCKETS - max_exact)).astype(I32)
    large = jnp.minimum(large, N_BUCKETS - 1)
    return jnp.where(n < max_exact, n, large)


def _attn_b_kernel(t5_ref, qidx_ref, kidx_ref, wt_ref, qb_ref, kb_ref, vt_ref, gate_ref, gq_ref, gk_ref,
                   o_ref, sc_ref, qn_ref, acc_ref, m_ref, l_ref, bias_ref, *, tq, topk, max_iters):
    bi = pl.program_id(0)
    qi = pl.program_id(1)
    ck = tq
    shape = (ck, tq)

    @pl.when((bi == 0) & (qi == 0))
    def _():
        s_loc = lax.broadcasted_iota(I32, shape, 0)
        t_loc = lax.broadcasted_iota(I32, shape, 1)
        for near in range(2):
            bucket = _t5_bucket(t_loc - s_loc + (1 - near) * ck)

            def per_head(h, carry, bucket=bucket, near=near):
                tab = jnp.zeros(shape, F32)
                for bk in range(N_BUCKETS):
                    tab = jnp.where(bucket == bk, t5_ref[bk, h], tab)
                bias_ref[h, near] = tab
                return carry

            lax.fori_loop(0, H_B, per_head, 0)

    w_all = wt_ref[...] * (IDX_HEADS ** -0.5)

    def score_chunk(c, diag):
        rows = pl.ds(pl.multiple_of(c * ck, ck), ck)
        ka = kidx_ref[rows, 0:LANE]
        kb = kidx_ref[rows, LANE:2 * LANE]
        score = jnp.zeros(shape, F32)
        for j in range(IDX_HEADS // 2):
            qp = qidx_ref[:, j * LANE:(j + 1) * LANE]
            score = score + jnp.maximum(_nt_dot(ka, qp), 0.0) * w_all[2 * j:2 * j + 1, :]
            score = score + jnp.maximum(_nt_dot(kb, qp), 0.0) * w_all[2 * j + 1:2 * j + 2, :]
        if diag:
            adm = lax.broadcasted_iota(I32, shape, 0) <= lax.broadcasted_iota(I32, shape, 1)
            lo_src = jnp.where(adm, score, POS_INF)
            score = jnp.where(adm, score, NEG_INF)
        else:
            lo_src = score
        sc_ref[c] = score
        return jnp.max(score, axis=0, keepdims=True), jnp.min(lo_src, axis=0, keepdims=True)

    def score_body(c, carry):
        mx, mn = carry
        cmx, cmn = score_chunk(c, False)
        return jnp.maximum(mx, cmx), jnp.minimum(mn, cmn)

    mx, mn = lax.fori_loop(0, qi, score_body,
                           (jnp.full((1, tq), NEG_INF, F32), jnp.full((1, tq), POS_INF, F32)))
    dmx, dmn = score_chunk(qi, True)
    mx = jnp.maximum(mx, dmx)
    mn = jnp.minimum(mn, dmn)

    n_adm = qi * tq + lax.broadcasted_iota(I32, (1, tq), 1) + 1
    kp = jnp.minimum(n_adm, topk)

    def count_ge(x):
        def body(c, acc):
            ge = (sc_ref[c] >= x).astype(I32)
            return acc + jnp.sum(ge.reshape(ck // 8, 8, tq), axis=0)

        acc = lax.fori_loop(0, qi + 1, body, jnp.zeros((8, tq), I32))
        return jnp.sum(acc, axis=0, keepdims=True)

    def bis_cond(st):
        it, lo, hi, mid, cnt_lo = st
        active = (cnt_lo != kp) & (mid > lo) & (mid < hi)
        return jnp.logical_and(it < max_iters, jnp.max(active.astype(I32)) > 0)

    def bis_body(st):
        it, lo, hi, mid, cnt_lo = st
        cnt = count_ge(mid)
        ge = cnt >= kp
        lo = jnp.where(ge, mid, lo)
        cnt_lo = jnp.where(ge, cnt, cnt_lo)
        hi = jnp.where(ge, hi, mid)
        mid = jnp.where(hi == POS_INF, mx, lo + 0.5 * (hi - lo))
        return it + 1, lo, hi, mid, cnt_lo

    _, thr, _, _, _ = lax.while_loop(
        bis_cond, bis_body, (jnp.int32(0), mn, jnp.full((1, tq), POS_INF, F32), mx, n_adm))

    for h in range(H_B):
        qh = qb_ref[:, h * LANE:(h + 1) * LANE].astype(F32)
        ms = jnp.mean(qh * qh, axis=-1, keepdims=True)
        qn_ref[h] = (qh * lax.rsqrt(ms + EPS) * gq_ref[...]).astype(qn_ref.dtype)
    m_ref[...] = jnp.full(m_ref.shape, NEG_INF, F32)
    l_ref[...] = jnp.zeros(l_ref.shape, F32)
    acc_ref[...] = jnp.zeros(acc_ref.shape, F32)

    def attend_chunk(c, near):
        rows = pl.ds(pl.multiple_of(c * ck, ck), ck)
        kc = kb_ref[rows, :].astype(F32)
        ms = jnp.mean(kc * kc, axis=-1, keepdims=True)
        kc = (kc * lax.rsqrt(ms + EPS) * gk_ref[...]).astype(MXU_DTYPE)
        vt = vt_ref[c]
        sel = sc_ref[c] >= thr
        for h in range(H_B):
            s = _nt_dot(kc, qn_ref[h])
            if near is None:
                s = s + t5_ref[N_BUCKETS - 1, h]
            else:
                s = s + bias_ref[h, near]
            s = jnp.where(sel, s, NEG_INF)
            m_old = m_ref[h:h + 1, :]
            m_new = jnp.maximum(m_old, jnp.max(s, axis=0, keepdims=True))
            m_safe = jnp.where(m_new == NEG_INF, 0.0, m_new)
            p = jnp.exp(s - m_safe)
            alpha = jnp.exp(m_old - m_safe)
            l_ref[h:h + 1, :] = alpha * l_ref[h:h + 1, :] + jnp.sum(p, axis=0, keepdims=True)
            acc_ref[h] = alpha * acc_ref[h] + jnp.dot(vt, p.astype(MXU_DTYPE), preferred_element_type=F32)
            m_ref[h:h + 1, :] = m_new

    def far_body(c, carry):
        attend_chunk(c, None)
        return carry

    lax.fori_loop(0, jnp.maximum(qi - 1, 0), far_body, 0)

    @pl.when(qi >= 1)
    def _():
        attend_chunk(qi - 1, 0)

    attend_chunk(qi, 1)

    for h in range(H_B):
        o_t = acc_ref[h] * (1.0 / l_ref[h:h + 1, :])
        g = gate_ref[:, h * LANE:(h + 1) * LANE].astype(F32)
        o_ref[:, h * LANE:(h + 1) * LANE] = (o_t.T * _silu(g)).astype(o_ref.dtype)


def _attn_b(proj, wt, vt, t5_bias, gq_b, gk_b, blk, *, b, l, tq):
    t = proj.shape[0]
    nq = l // tq
    topk = min(TOPK_MAX, l // 4)
    row = lambda bi, qi: bi * nq + qi
    return pl.pallas_call(
        functools.partial(_attn_b_kernel, tq=tq, topk=topk, max_iters=128),
        out_shape=jax.ShapeDtypeStruct((t, WIDTH_B), MXU_DTYPE),
        grid=(b, nq),
        in_specs=[
            pl.BlockSpec(memory_space=pltpu.SMEM),
            pl.BlockSpec((tq, IDX_HEADS * IDX_DIM), lambda bi, qi: (row(bi, qi), blk["q_idx"])),
            pl.BlockSpec((l, 2 * LANE), lambda bi, qi: (bi, blk["k_idx"])),
            pl.BlockSpec((IDX_HEADS, tq), lambda bi, qi: (0, row(bi, qi))),
            pl.BlockSpec((tq, WIDTH_B), lambda bi, qi: (row(bi, qi), blk["q_b"])),
            pl.BlockSpec((l, LANE), lambda bi, qi: (bi, blk["k_b"])),
            pl.BlockSpec((None, nq, LANE, tq), lambda bi, qi: (bi, 0, 0, 0)),
            pl.BlockSpec((tq, WIDTH_B), lambda bi, qi: (row(bi, qi), blk["gate_b"])),
            pl.BlockSpec((1, LANE), lambda bi, qi: (0, 0)),
            pl.BlockSpec((1, LANE), lambda bi, qi: (0, 0)),
        ],
        out_specs=pl.BlockSpec((tq, WIDTH_B), lambda bi, qi: (row(bi, qi), 0)),
        scratch_shapes=[
            pltpu.VMEM((nq, tq, tq), F32),
            pltpu.VMEM((H_B, tq, LANE), MXU_DTYPE),
            pltpu.VMEM((H_B, LANE, tq), F32),
            pltpu.VMEM((H_B, tq), F32),
            pltpu.VMEM((H_B, tq), F32),
            pltpu.VMEM((H_B, 2, tq, tq), F32),
        ],
        compiler_params=_params(("arbitrary", "arbitrary")),
        name="attn_b",
    )(t5_bias, proj, proj, wt, proj, proj, vt, proj, gq_b, gk_b)


def _sigmoid(z):
    return 1.0 / (1.0 + jnp.exp(-z))


def _merge_kernel(oa_ref, ob_ref, pa_ref, pb_ref, ma_ref, mb_ref, o_ref):
    a = jnp.dot(oa_ref[...], pa_ref[...], preferred_element_type=F32)
    bb = jnp.dot(ob_ref[...], pb_ref[...], preferred_element_type=F32)
    o_ref[...] = (_sigmoid(ma_ref[...].astype(F32)) * a + _sigmoid(mb_ref[...].astype(F32)) * bb).astype(o_ref.dtype)


def _merge(o_a, o_b, p_a, p_b, proj, ma_off, mb_off, *, tm, tn):
    t = o_a.shape[0]
    d = p_a.shape[1]
    ma0, mb0 = ma_off // tn, mb_off // tn
    return pl.pallas_call(
        _merge_kernel,
        out_shape=jax.ShapeDtypeStruct((t, d), MXU_DTYPE),
        grid=(t // tm, d // tn),
        in_specs=[
            pl.BlockSpec((tm, WIDTH_A), lambda i, j: (i, 0)),
            pl.BlockSpec((tm, WIDTH_B), lambda i, j: (i, 0)),
            pl.BlockSpec((WIDTH_A, tn), lambda i, j: (0, j)),
            pl.BlockSpec((WIDTH_B, tn), lambda i, j: (0, j)),
            pl.BlockSpec((tm, tn), lambda i, j: (i, ma0 + j)),
            pl.BlockSpec((tm, tn), lambda i, j: (i, mb0 + j)),
        ],
        out_specs=pl.BlockSpec((tm, tn), lambda i, j: (i, j)),
        compiler_params=_params(("arbitrary", "arbitrary")),
        name="merge",
    )(o_a, o_b, p_a, p_b, proj, proj)


def _out_proj_kernel(m_ref, w_ref, x_ref, o_ref):
    o_ref[...] = x_ref[...] + jnp.dot(m_ref[...], w_ref[...], preferred_element_type=F32)


def _out_proj(merged, w_o, x2, *, tm, tn):
    t, d = x2.shape
    return pl.pallas_call(
        _out_proj_kernel,
        out_shape=jax.ShapeDtypeStruct((t, d), x2.dtype),
        grid=(t // tm, d // tn),
        in_specs=[
            pl.BlockSpec((tm, d), lambda i, j: (i, 0)),
            pl.BlockSpec((d, tn), lambda i, j: (0, j)),
            pl.BlockSpec((tm, tn), lambda i, j: (i, j)),
        ],
        out_specs=pl.BlockSpec((tm, tn), lambda i, j: (i, j)),
        compiler_params=_params(("arbitrary", "arbitrary")),
        name="out_proj",
    )(merged, w_o, x2)


def _rope_pad(a, axis):
    a1, a2 = jnp.split(a, 2, axis=axis)
    z = jnp.zeros_like(a1)
    return jnp.concatenate([a1, z, a2, z], axis=axis)


def _layout(d):
    names = [("merge_a", d), ("merge_b", d), ("gate_a", WIDTH_A), ("gate_b", WIDTH_B),
             ("q_b", WIDTH_B), ("q_idx", IDX_HEADS * IDX_DIM), ("cq", Q_LORA), ("ckv", KV_LORA),
             ("k_idx", 2 * LANE), ("k_rope", LANE), ("k_b", LANE), ("v_b", LANE), ("w_idx", LANE)]
    off, out = 0, {}
    for name, width in names:
        assert off % width == 0, (name, off, width)
        out[name] = off
        off += width
    return out, off


def kernel(x, positions, g_pre, w_in, g_q_lat, g_kv_lat, w_uq, w_ukv, g_qn_a, g_kn_a,
           g_qn_b, g_kn_b, t5_bias, p_a, p_b, w_o):
    b, l, d = x.shape
    t = b * l
    tq = 256
    tn_in = 512
    off, n_used = _layout(d)
    n_pad = -(-n_used // tn_in) * tn_in

    sizes = [Q_LORA, KV_LORA, QK_ROPE, WIDTH_B, HEAD_DIM_B, HEAD_DIM_B, IDX_HEADS * IDX_DIM, IDX_DIM,
             IDX_HEADS, WIDTH_A, WIDTH_B, d, d]
    splits = [int(v) for v in jnp.cumsum(jnp.array(sizes))[:-1]] if False else []
    acc = 0
    for s in sizes[:-1]:
        acc += s
        splits.append(acc)
    (w_cq, w_ckv, w_kr, w_qb, w_kb, w_vb, w_qi, w_ki, w_wi, w_ga, w_gb, w_ma, w_mb) = jnp.split(w_in, splits, axis=1)
    z = lambda n: jnp.zeros((d, n), w_in.dtype)
    w_pad = jnp.concatenate(
        [w_ma, w_mb, w_ga, w_gb, w_qb, w_qi, w_cq, w_ckv,
         w_ki, z(LANE - IDX_DIM), z(LANE - IDX_DIM), w_ki,
         _rope_pad(w_kr, 1), w_kb, w_vb, w_wi, z(LANE - IDX_HEADS), z(n_pad - n_used)],
        axis=1).astype(MXU_DTYPE)

    w_uq3 = w_uq.reshape(Q_LORA, H_A, QK_DIM_A)
    w_uq_pad = jnp.concatenate([w_uq3[:, :, :QK_NOPE], _rope_pad(w_uq3[:, :, QK_NOPE:], 2)], axis=2)
    w_uq_pad = w_uq_pad.reshape(Q_LORA, H_A * HEAD_PAD_A).astype(MXU_DTYPE)
    gq_head = jnp.concatenate([g_qn_a[:QK_NOPE], _rope_pad(g_qn_a[QK_NOPE:], 0)]) * (QK_DIM_A ** -0.5)
    gq_pad = jnp.tile(gq_head, H_A).reshape(1, H_A * HEAD_PAD_A).astype(F32)
    w_ukv3 = w_ukv.reshape(KV_LORA, H_A, QK_NOPE + V_DIM_A)
    w_uk = w_ukv3[:, :, :QK_NOPE].reshape(KV_LORA, H_A * QK_NOPE).astype(MXU_DTYPE)
    w_uv = w_ukv3[:, :, QK_NOPE:].reshape(KV_LORA, WIDTH_A).astype(MXU_DTYPE)
    gk_nope = g_kn_a[:QK_NOPE].reshape(1, LANE).astype(F32)
    gk_rope = _rope_pad(g_kn_a[QK_NOPE:], 0).reshape(1, LANE).astype(F32)
    gq_b = (g_qn_b * (HEAD_DIM_B ** -0.5)).reshape(1, LANE).astype(F32)
    gk_b = g_kn_b.reshape(1, LANE).astype(F32)

    inv = ROPE_THETA ** (-jnp.arange(HALF_ROPE, dtype=F32) / HALF_ROPE)
    ang = positions.reshape(t, 1).astype(F32) * inv
    cos, sin = jnp.cos(ang), jnp.sin(ang)
    zr = jnp.zeros_like(cos)
    cos_t = jnp.concatenate([cos, zr, cos, zr], axis=1)
    sin_t = jnp.concatenate([-sin, zr, sin, zr], axis=1)

    x2 = x.reshape(t, d)
    proj = _in_proj(x2, g_pre, w_pad, tm=512, tn=tn_in)

    q_a = _qa_proj(proj, off["cq"] // Q_LORA, g_q_lat, w_uq_pad, gq_pad, cos_t, sin_t, tm=256)
    k_a, v_a = _kva_proj(proj, off["ckv"] // KV_LORA, off["k_rope"] // LANE, g_kv_lat, w_uk, w_uv,
                         gk_nope, gk_rope, cos_t, sin_t, tm=256)
    o_a = _attn_a(q_a, k_a, v_a, proj, off["gate_a"] // LANE, b=b, l=l, tq=tq)

    nq = l // tq
    w_idx = proj[:, off["w_idx"]:off["w_idx"] + IDX_HEADS]
    wt = w_idx.astype(F32).T
    v_b = proj[:, off["v_b"]:off["v_b"] + HEAD_DIM_B]
    vt = v_b.reshape(b, nq, tq, HEAD_DIM_B).transpose(0, 1, 3, 2)
    blk = {"q_idx": off["q_idx"] // (IDX_HEADS * IDX_DIM), "k_idx": off["k_idx"] // (2 * LANE),
           "q_b": off["q_b"] // WIDTH_B, "k_b": off["k_b"] // LANE, "gate_b": off["gate_b"] // WIDTH_B}
    o_b = _attn_b(proj, wt, vt, t5_bias.astype(F32), gq_b, gk_b, blk, b=b, l=l, tq=tq)

    merged = _merge(o_a, o_b, p_a.astype(MXU_DTYPE), p_b.astype(MXU_DTYPE), proj,
                    off["merge_a"], off["merge_b"], tm=512, tn=512)
    out = _out_proj(merged, w_o.astype(MXU_DTYPE), x2, tm=512, tn=512)
    return out.reshape(b, l, d)
```

```python
import functools
import math

import jax
import jax.numpy as jnp
from jax import lax
from jax.experimental import pallas as pl
from jax.experimental.pallas import tpu as pltpu

F32 = jnp.float32
I32 = jnp.int32
MXU_DTYPE = jnp.bfloat16

H_A = 16
QK_NOPE = 128
QK_ROPE = 64
QK_DIM_A = QK_NOPE + QK_ROPE
V_DIM_A = 128
Q_LORA = 1024
KV_LORA = 512
ROPE_THETA = 10000.0
H_B = 16
HEAD_DIM_B = 128
IDX_HEADS = 32
IDX_DIM = 64
TOPK_MAX = 256
N_BUCKETS = 32
MAX_DISTANCE = 128
EPS = 1e-6
WIDTH_A = H_A * V_DIM_A
WIDTH_B = H_B * HEAD_DIM_B

LANE = 128
HALF_ROPE = QK_ROPE // 2
HEAD_PAD_A = 2 * LANE
VMEM_LIMIT = 56 * 1024 * 1024

BISECT_STEPS_PER_CHECK = 4
BISECT_MAX_CHECKS = 40
LOG2E = math.log2(math.e)
NEG_INF = float("-inf")
POS_INF = float("inf")


def _nt_dot(a, b):
    return lax.dot_general(a, b, (((1,), (1,)), ((), ())), preferred_element_type=F32)


def _params(sem, vmem=VMEM_LIMIT):
    return pltpu.CompilerParams(dimension_semantics=sem, vmem_limit_bytes=vmem)


def _in_proj_kernel(x_ref, g_ref, w_ref, o_ref, hn_ref, *, row_chunk):
    tm = x_ref.shape[0]

    @pl.when(pl.program_id(1) == 0)
    def _():
        def body(r, carry):
            sl = pl.ds(pl.multiple_of(r * row_chunk, row_chunk), row_chunk)
            xx = x_ref[sl, :]
            ms = jnp.mean(xx * xx, axis=-1, keepdims=True)
            hn_ref[sl, :] = (xx * lax.rsqrt(ms + EPS) * g_ref[...]).astype(hn_ref.dtype)
            return carry

        lax.fori_loop(0, tm // row_chunk, body, 0)

    o_ref[...] = jnp.dot(hn_ref[...], w_ref[...], preferred_element_type=F32).astype(o_ref.dtype)


def _in_proj(x2, g_pre, w_pad, *, tm, tn):
    t, d = x2.shape
    n = w_pad.shape[1]
    return pl.pallas_call(
        functools.partial(_in_proj_kernel, row_chunk=64),
        out_shape=jax.ShapeDtypeStruct((t, n), MXU_DTYPE),
        grid=(t // tm, n // tn),
        in_specs=[
            pl.BlockSpec((tm, d), lambda i, j: (i, 0)),
            pl.BlockSpec((1, d), lambda i, j: (0, 0)),
            pl.BlockSpec((d, tn), lambda i, j: (0, j)),
        ],
        out_specs=pl.BlockSpec((tm, tn), lambda i, j: (i, j)),
        scratch_shapes=[pltpu.VMEM((tm, d), MXU_DTYPE)],
        compiler_params=_params(("arbitrary", "arbitrary")),
        name="in_proj",
    )(x2, g_pre.reshape(1, d), w_pad)


def _rope_lanes(r, cos_ref, sin_ref):
    return r * cos_ref[...] + pltpu.roll(r, 2 * HALF_ROPE, 1) * sin_ref[...]


def _qa_proj_kernel(cq_ref, gl_ref, w_ref, gq_ref, cos_ref, sin_ref, o_ref):
    c = cq_ref[...].astype(F32)
    ms = jnp.mean(c * c, axis=-1, keepdims=True)
    cn = (c * lax.rsqrt(ms + EPS) * gl_ref[...]).astype(MXU_DTYPE)
    q = jnp.dot(cn, w_ref[...], preferred_element_type=F32)
    for h in range(H_A):
        lo = h * HEAD_PAD_A
        qh = q[:, lo:lo + HEAD_PAD_A]
        ss = jnp.sum(qh * qh, axis=-1, keepdims=True) * (1.0 / QK_DIM_A)
        qn = qh * lax.rsqrt(ss + EPS) * gq_ref[:, lo:lo + HEAD_PAD_A]
        o_ref[:, lo:lo + LANE] = qn[:, :LANE].astype(o_ref.dtype)
        o_ref[:, lo + LANE:lo + HEAD_PAD_A] = _rope_lanes(qn[:, LANE:], cos_ref, sin_ref).astype(o_ref.dtype)


def _qa_proj(proj, cq_blk, g_q_lat, w_uq_pad, gq_pad, cos_t, sin_t, *, tm):
    t = proj.shape[0]
    nq = H_A * HEAD_PAD_A
    return pl.pallas_call(
        _qa_proj_kernel,
        out_shape=jax.ShapeDtypeStruct((t, nq), MXU_DTYPE),
        grid=(t // tm,),
        in_specs=[
            pl.BlockSpec((tm, Q_LORA), lambda i: (i, cq_blk)),
            pl.BlockSpec((1, Q_LORA), lambda i: (0, 0)),
            pl.BlockSpec((Q_LORA, nq), lambda i: (0, 0)),
            pl.BlockSpec((1, nq), lambda i: (0, 0)),
            pl.BlockSpec((tm, LANE), lambda i: (i, 0)),
            pl.BlockSpec((tm, LANE), lambda i: (i, 0)),
        ],
        out_specs=pl.BlockSpec((tm, nq), lambda i: (i, 0)),
        compiler_params=_params(("arbitrary",)),
        name="qa_proj",
    )(proj, g_q_lat.reshape(1, Q_LORA), w_uq_pad, gq_pad, cos_t, sin_t)


def _kva_proj_kernel(ckv_ref, kr_ref, gl_ref, wk_ref, wv_ref, gkn_ref, gkr_ref, cos_ref, sin_ref,
                     k_ref, v_ref):
    c = ckv_ref[...].astype(F32)
    ms = jnp.mean(c * c, axis=-1, keepdims=True)
    cn = (c * lax.rsqrt(ms + EPS) * gl_ref[...]).astype(MXU_DTYPE)
    kn = jnp.dot(cn, wk_ref[...], preferred_element_type=F32)
    v_ref[...] = jnp.dot(cn, wv_ref[...], preferred_element_type=F32).astype(v_ref.dtype)
    kr = kr_ref[...].astype(F32)
    ss_r = jnp.sum(kr * kr, axis=-1, keepdims=True)
    krr = _rope_lanes(kr * gkr_ref[...], cos_ref, sin_ref)
    for h in range(H_A):
        kh = kn[:, h * LANE:(h + 1) * LANE]
        ss = (jnp.sum(kh * kh, axis=-1, keepdims=True) + ss_r) * (1.0 / QK_DIM_A)
        rs = lax.rsqrt(ss + EPS)
        lo = h * HEAD_PAD_A
        k_ref[:, lo:lo + LANE] = (kh * rs * gkn_ref[...]).astype(k_ref.dtype)
        k_ref[:, lo + LANE:lo + HEAD_PAD_A] = (krr * rs).astype(k_ref.dtype)


def _kva_proj(proj, ckv_blk, krope_blk, g_kv_lat, w_uk, w_uv, gk_nope, gk_rope, cos_t, sin_t, *, tm):
    t = proj.shape[0]
    return pl.pallas_call(
        _kva_proj_kernel,
        out_shape=(jax.ShapeDtypeStruct((t, H_A * HEAD_PAD_A), MXU_DTYPE),
                   jax.ShapeDtypeStruct((t, WIDTH_A), MXU_DTYPE)),
        grid=(t // tm,),
        in_specs=[
            pl.BlockSpec((tm, KV_LORA), lambda i: (i, ckv_blk)),
            pl.BlockSpec((tm, LANE), lambda i: (i, krope_blk)),
            pl.BlockSpec((1, KV_LORA), lambda i: (0, 0)),
            pl.BlockSpec((KV_LORA, H_A * QK_NOPE), lambda i: (0, 0)),
            pl.BlockSpec((KV_LORA, WIDTH_A), lambda i: (0, 0)),
            pl.BlockSpec((1, LANE), lambda i: (0, 0)),
            pl.BlockSpec((1, LANE), lambda i: (0, 0)),
            pl.BlockSpec((tm, LANE), lambda i: (i, 0)),
            pl.BlockSpec((tm, LANE), lambda i: (i, 0)),
        ],
        out_specs=(pl.BlockSpec((tm, H_A * HEAD_PAD_A), lambda i: (i, 0)),
                   pl.BlockSpec((tm, WIDTH_A), lambda i: (i, 0))),
        compiler_params=_params(("arbitrary",)),
        name="kva_proj",
    )(proj, proj, g_kv_lat.reshape(1, KV_LORA), w_uk, w_uv, gk_nope, gk_rope, cos_t, sin_t)


def _silu(g):
    return g * (1.0 / (1.0 + jnp.exp(-g)))


def _lane_tile_reduce(x, op):
    acc = x[:, :LANE]
    for t in range(1, x.shape[1] // LANE):
        acc = op(acc, x[:, t * LANE:(t + 1) * LANE])
    return acc


def _attn_a_kernel(q_ref, k_ref, v_ref, gate_ref, o_ref, *, tq, nq, heads):
    qi = pl.program_id(2)

    def branch(qv):
        n_off = qv * tq
        for g in range(heads):
            kc = slice(g * HEAD_PAD_A, (g + 1) * HEAD_PAD_A)
            vc = slice(g * V_DIM_A, (g + 1) * V_DIM_A)
            q = q_ref[:, kc]
            s_diag = _nt_dot(q, k_ref[n_off:n_off + tq, kc])
            row = lax.broadcasted_iota(I32, s_diag.shape, 0)
            col = lax.broadcasted_iota(I32, s_diag.shape, 1)
            s_diag = jnp.where(row >= col, s_diag, NEG_INF)
            m_t = _lane_tile_reduce(s_diag, jnp.maximum)
            if qv > 0:
                s_off = _nt_dot(q, k_ref[0:n_off, kc])
                m_t = jnp.maximum(m_t, _lane_tile_reduce(s_off, jnp.maximum))
            m = jnp.max(m_t, axis=-1, keepdims=True)
            p_diag = jnp.exp2(s_diag - m)
            l_t = _lane_tile_reduce(p_diag, jnp.add)
            acc = jnp.dot(p_diag.astype(MXU_DTYPE), v_ref[n_off:n_off + tq, vc], preferred_element_type=F32)
            if qv > 0:
                p_off = jnp.exp2(s_off - m)
                l_t = l_t + _lane_tile_reduce(p_off, jnp.add)
                acc = acc + jnp.dot(p_off.astype(MXU_DTYPE), v_ref[0:n_off, vc], preferred_element_type=F32)
            l = jnp.sum(l_t, axis=-1, keepdims=True)
            o = acc * (1.0 / l)
            o_ref[:, vc] = (o * _silu(gate_ref[:, vc].astype(F32))).astype(o_ref.dtype)

    for qv in range(nq):
        pl.when(qi == qv)(functools.partial(branch, qv))


def _attn_a(q_a, k_a, v_a, proj, gate_blk0, *, b, l, tq, heads):
    t = q_a.shape[0]
    nq = l // tq
    kw, vw = heads * HEAD_PAD_A, heads * V_DIM_A
    return pl.pallas_call(
        functools.partial(_attn_a_kernel, tq=tq, nq=nq, heads=heads),
        out_shape=jax.ShapeDtypeStruct((t, WIDTH_A), MXU_DTYPE),
        grid=(b, H_A // heads, nq),
        in_specs=[
            pl.BlockSpec((tq, kw), lambda bi, h, qi: (bi * nq + qi, h)),
            pl.BlockSpec((l, kw), lambda bi, h, qi: (bi, h)),
            pl.BlockSpec((l, vw), lambda bi, h, qi: (bi, h)),
            pl.BlockSpec((tq, vw), lambda bi, h, qi: (bi * nq + qi, gate_blk0 // heads + h)),
        ],
        out_specs=pl.BlockSpec((tq, vw), lambda bi, h, qi: (bi * nq + qi, h)),
        compiler_params=_params(("arbitrary", "arbitrary", "arbitrary")),
        name="attn_a",
    )(q_a, k_a, v_a, proj)


def _t5_bucket(dist):
    max_exact = N_BUCKETS // 2
    n = jnp.maximum(dist, 0)
    nf = jnp.maximum(n, 1).astype(F32)
    large = max_exact + (jnp.log(nf / max_exact) / math.log(MAX_DISTANCE / max_exact)
                         * (N_BUCKETS - max_exact)).astype(I32)
    large = jnp.minimum(large, N_BUCKETS - 1)
    return jnp.where(n < max_exact, n, large)


def _attn_b_kernel(t5_ref, qidx_ref, kidx_ref, wt_ref, qb_ref, kb_ref, vt_ref, gate_ref, gq_ref, gk_ref,
                   o_ref, sc_ref, qn_ref, acc_ref, m_ref, l_ref, bias_ref, thr_ref, *, tq, nq, topk, max_iters):
    bi = pl.program_id(0)
    qi = pl.program_id(1)
    ck = tq
    shape = (ck, tq)

    @pl.when((bi == 0) & (qi == 0))
    def _():
        s_loc = lax.broadcasted_iota(I32, shape, 0)
        t_loc = lax.broadcasted_iota(I32, shape, 1)
        for near in range(2):
            bucket = _t5_bucket(t_loc - s_loc + (1 - near) * ck)

            def per_head(h, carry, bucket=bucket, near=near):
                far = t5_ref[N_BUCKETS - 1, h]
                tab = jnp.zeros(shape, F32)
                for bk in range(N_BUCKETS - 1):
                    tab = jnp.where(bucket == bk, (t5_ref[bk, h] - far) * LOG2E, tab)
                bias_ref[h, near] = tab
                return carry

            lax.fori_loop(0, H_B, per_head, 0)

    w_all = wt_ref[...] * (IDX_HEADS ** -0.5)

    def score_chunk(c, diag):
        rows = pl.ds(pl.multiple_of(c * ck, ck), ck)
        ka = kidx_ref[rows, 0:LANE]
        kb = kidx_ref[rows, LANE:2 * LANE]
        score = jnp.zeros(shape, F32)
        for j in range(IDX_HEADS // 2):
            qp = qidx_ref[:, j * LANE:(j + 1) * LANE]
            score = score + jnp.maximum(_nt_dot(ka, qp), 0.0) * w_all[2 * j:2 * j + 1, :]
            score = score + jnp.maximum(_nt_dot(kb, qp), 0.0) * w_all[2 * j + 1:2 * j + 2, :]
        if diag:
            adm = lax.broadcasted_iota(I32, shape, 0) <= lax.broadcasted_iota(I32, shape, 1)
            lo_src = jnp.where(adm, score, POS_INF)
            score = jnp.where(adm, score, NEG_INF)
        else:
            lo_src = score
        sc_ref[c] = score
        return jnp.max(score, axis=0, keepdims=True), jnp.min(lo_src, axis=0, keepdims=True)

    def score_body(c, carry):
        mx, mn = carry
        cmx, cmn = score_chunk(c, False)
        return jnp.maximum(mx, cmx), jnp.minimum(mn, cmn)

    mx, mn = lax.fori_loop(0, qi, score_body,
                           (jnp.full((1, tq), NEG_INF, F32), jnp.full((1, tq), POS_INF, F32)))
    dmx, dmn = score_chunk(qi, True)
    mx = jnp.maximum(mx, dmx)
    mn = jnp.minimum(mn, dmn)

    rep = (8, tq)
    n_adm = qi * tq + lax.broadcasted_iota(I32, rep, 1) + 1
    kp = jnp.minimum(n_adm, topk)
    mx8 = jnp.broadcast_to(mx, rep)
    mn8 = jnp.broadcast_to(mn, rep)

    def bisect(nchunks):
        def count_ge(x):
            acc = jnp.zeros(rep, I32)
            for c in range(nchunks):
                ge = sc_ref[c].reshape(ck // 8, 8, tq) >= x[None]
                acc = acc + jnp.sum(ge.astype(I32), axis=0)
            for shift in (4, 2, 1):
                acc = acc + pltpu.roll(acc, shift, 0)
            return acc

        def bis_cond(st):
            it, lo, hi, mid, cnt_lo = st
            active = (cnt_lo != kp) & (mid > lo) & (mid < hi)
            return jnp.logical_and(it < max_iters, jnp.max(active.astype(I32)) > 0)

        def bis_body(st):
            it, lo, hi, mid, cnt_lo = st
            for _ in range(BISECT_STEPS_PER_CHECK):
                cnt = count_ge(mid)
                ge = cnt >= kp
                lo = jnp.where(ge, mid, lo)
                cnt_lo = jnp.where(ge, cnt, cnt_lo)
                hi = jnp.where(ge, hi, mid)
                mid = jnp.where(hi == POS_INF, mx8, lo + 0.5 * (hi - lo))
            return it + 1, lo, hi, mid, cnt_lo

        st = lax.while_loop(bis_cond, bis_body,
                            (jnp.int32(0), mn8, jnp.full(rep, POS_INF, F32), mx8, n_adm))
        thr_ref[...] = st[1]

    for qv in range(nq):
        pl.when(qi == qv)(functools.partial(bisect, qv + 1))
    thr = thr_ref[0:1, :]

    for h in range(H_B):
        qh = qb_ref[:, h * LANE:(h + 1) * LANE].astype(F32)
        ms = jnp.mean(qh * qh, axis=-1, keepdims=True)
        qn_ref[h * tq:(h + 1) * tq, :] = (qh * lax.rsqrt(ms + EPS) * gq_ref[...]).astype(qn_ref.dtype)
    m_ref[...] = jnp.full(m_ref.shape, NEG_INF, F32)
    l_ref[...] = jnp.zeros(l_ref.shape, F32)
    acc_ref[...] = jnp.zeros(acc_ref.shape, F32)

    def attend_chunk(c, near):
        rows = pl.ds(pl.multiple_of(c * ck, ck), ck)
        kc = kb_ref[rows, :].astype(F32)
        ms = jnp.mean(kc * kc, axis=-1, keepdims=True)
        kc = (kc * lax.rsqrt(ms + EPS) * gk_ref[...]).astype(MXU_DTYPE)
        vt = vt_ref[c]
        sel = sc_ref[c] >= thr
        s_all = _nt_dot(kc, qn_ref[...])
        for h in range(H_B):
            s = s_all[:, h * tq:(h + 1) * tq]
            if near is not None:
                s = s + bias_ref[h, near]
            s = jnp.where(sel, s, NEG_INF)
            m_old = m_ref[h]
            m_new = jnp.maximum(m_old, jnp.max(s, axis=0, keepdims=True))
            m_safe = jnp.where(m_new == NEG_INF, 0.0, m_new)
            p = jnp.exp2(s - m_safe)
            alpha = jnp.exp2(m_old - m_safe)
            l_ref[h] = alpha * l_ref[h] + jnp.sum(p, axis=0, keepdims=True)
            acc_ref[h] = alpha * acc_ref[h] + jnp.dot(vt, p.astype(MXU_DTYPE), preferred_element_type=F32)
            m_ref[h] = m_new

    def far_body(c, carry):
        attend_chunk(c, None)
        return carry

    lax.fori_loop(0, jnp.maximum(qi - 1, 0), far_body, 0)

    @pl.when(qi >= 1)
    def _():
        attend_chunk(qi - 1, 0)

    attend_chunk(qi, 1)

    for h in range(H_B):
        o_t = acc_ref[h] * (1.0 / l_ref[h])
        g = gate_ref[:, h * LANE:(h + 1) * LANE].astype(F32)
        o_ref[:, h * LANE:(h + 1) * LANE] = (o_t.T * _silu(g)).astype(o_ref.dtype)


def _attn_b(proj, wt, vt, t5_bias, gq_b, gk_b, blk, *, b, l, tq):
    t = proj.shape[0]
    nq = l // tq
    topk = min(TOPK_MAX, l // 4)
    row = lambda bi, qi: bi * nq + qi
    return pl.pallas_call(
        functools.partial(_attn_b_kernel, tq=tq, nq=nq, topk=topk, max_iters=BISECT_MAX_CHECKS),
        out_shape=jax.ShapeDtypeStruct((t, WIDTH_B), MXU_DTYPE),
        grid=(b, nq),
        in_specs=[
            pl.BlockSpec(memory_space=pltpu.SMEM),
            pl.BlockSpec((tq, IDX_HEADS * IDX_DIM), lambda bi, qi: (row(bi, qi), blk["q_idx"])),
            pl.BlockSpec((l, 2 * LANE), lambda bi, qi: (bi, blk["k_idx"])),
            pl.BlockSpec((IDX_HEADS, tq), lambda bi, qi: (0, row(bi, qi))),
            pl.BlockSpec((tq, WIDTH_B), lambda bi, qi: (row(bi, qi), blk["q_b"])),
            pl.BlockSpec((l, LANE), lambda bi, qi: (bi, blk["k_b"])),
            pl.BlockSpec((None, nq, LANE, tq), lambda bi, qi: (bi, 0, 0, 0)),
            pl.BlockSpec((tq, WIDTH_B), lambda bi, qi: (row(bi, qi), blk["gate_b"])),
            pl.BlockSpec((1, LANE), lambda bi, qi: (0, 0)),
            pl.BlockSpec((1, LANE), lambda bi, qi: (0, 0)),
        ],
        out_specs=pl.BlockSpec((tq, WIDTH_B), lambda bi, qi: (row(bi, qi), 0)),
        scratch_shapes=[
            pltpu.VMEM((nq, tq, tq), F32),
            pltpu.VMEM((H_B * tq, LANE), MXU_DTYPE),
            pltpu.VMEM((H_B, LANE, tq), F32),
            pltpu.VMEM((H_B, 1, tq), F32),
            pltpu.VMEM((H_B, 1, tq), F32),
            pltpu.VMEM((H_B, 2, tq, tq), F32),
            pltpu.VMEM((8, tq), F32),
        ],
        compiler_params=_params(("arbitrary", "arbitrary")),
        name="attn_b",
    )(t5_bias, proj, proj, wt, proj, proj, vt, proj, gq_b, gk_b)


def _sigmoid(z):
    return 1.0 / (1.0 + jnp.exp(-z))


def _merge_kernel(oa_ref, ob_ref, pa_ref, pb_ref, ma_ref, mb_ref, o_ref):
    a = jnp.dot(oa_ref[...], pa_ref[...], preferred_element_type=F32)
    bb = jnp.dot(ob_ref[...], pb_ref[...], preferred_element_type=F32)
    o_ref[...] = (_sigmoid(ma_ref[...].astype(F32)) * a + _sigmoid(mb_ref[...].astype(F32)) * bb).astype(o_ref.dtype)


def _merge(o_a, o_b, p_a, p_b, proj, ma_off, mb_off, *, tm, tn):
    t = o_a.shape[0]
    d = p_a.shape[1]
    ma0, mb0 = ma_off // tn, mb_off // tn
    return pl.pallas_call(
        _merge_kernel,
        out_shape=jax.ShapeDtypeStruct((t, d), MXU_DTYPE),
        grid=(t // tm, d // tn),
        in_specs=[
            pl.BlockSpec((tm, WIDTH_A), lambda i, j: (i, 0)),
            pl.BlockSpec((tm, WIDTH_B), lambda i, j: (i, 0)),
            pl.BlockSpec((WIDTH_A, tn), lambda i, j: (0, j)),
            pl.BlockSpec((WIDTH_B, tn), lambda i, j: (0, j)),
            pl.BlockSpec((tm, tn), lambda i, j: (i, ma0 + j)),
            pl.BlockSpec((tm, tn), lambda i, j: (i, mb0 + j)),
        ],
        out_specs=pl.BlockSpec((tm, tn), lambda i, j: (i, j)),
        compiler_params=_params(("arbitrary", "arbitrary")),
        name="merge",
    )(o_a, o_b, p_a, p_b, proj, proj)


def _out_proj_kernel(m_ref, w_ref, x_ref, o_ref):
    o_ref[...] = x_ref[...] + jnp.dot(m_ref[...], w_ref[...], preferred_element_type=F32)


def _out_proj(merged, w_o, x2, *, tm, tn):
    t, d = x2.shape
    return pl.pallas_call(
        _out_proj_kernel,
        out_shape=jax.ShapeDtypeStruct((t, d), x2.dtype),
        grid=(t // tm, d // tn),
        in_specs=[
            pl.BlockSpec((tm, d), lambda i, j: (i, 0)),
            pl.BlockSpec((d, tn), lambda i, j: (0, j)),
            pl.BlockSpec((tm, tn), lambda i, j: (i, j)),
        ],
        out_specs=pl.BlockSpec((tm, tn), lambda i, j: (i, j)),
        compiler_params=_params(("arbitrary", "arbitrary")),
        name="out_proj",
    )(merged, w_o, x2)


def _rope_pad(a, axis):
    a1, a2 = jnp.split(a, 2, axis=axis)
    z = jnp.zeros_like(a1)
    return jnp.concatenate([a1, z, a2, z], axis=axis)


def _layout(d):
    names = [("merge_a", d), ("merge_b", d), ("gate_a", WIDTH_A), ("gate_b", WIDTH_B),
             ("q_b", WIDTH_B), ("q_idx", IDX_HEADS * IDX_DIM), ("cq", Q_LORA), ("ckv", KV_LORA),
             ("k_idx", 2 * LANE), ("k_rope", LANE), ("k_b", LANE), ("v_b", LANE), ("w_idx", LANE)]
    off, out = 0, {}
    for name, width in names:
        assert off % width == 0, (name, off, width)
        out[name] = off
        off += width
    return out, off


def kernel(x, positions, g_pre, w_in, g_q_lat, g_kv_lat, w_uq, w_ukv, g_qn_a, g_kn_a,
           g_qn_b, g_kn_b, t5_bias, p_a, p_b, w_o):
    b, l, d = x.shape
    t = b * l
    tq = 256
    tn_in = 512
    off, n_used = _layout(d)
    n_pad = -(-n_used // tn_in) * tn_in

    sizes = [Q_LORA, KV_LORA, QK_ROPE, WIDTH_B, HEAD_DIM_B, HEAD_DIM_B, IDX_HEADS * IDX_DIM, IDX_DIM,
             IDX_HEADS, WIDTH_A, WIDTH_B, d, d]
    splits = [int(v) for v in jnp.cumsum(jnp.array(sizes))[:-1]] if False else []
    acc = 0
    for s in sizes[:-1]:
        acc += s
        splits.append(acc)
    (w_cq, w_ckv, w_kr, w_qb, w_kb, w_vb, w_qi, w_ki, w_wi, w_ga, w_gb, w_ma, w_mb) = jnp.split(w_in, splits, axis=1)
    z = lambda n: jnp.zeros((d, n), w_in.dtype)
    w_pad = jnp.concatenate(
        [w_ma, w_mb, w_ga, w_gb, w_qb, w_qi, w_cq, w_ckv,
         w_ki, z(LANE - IDX_DIM), z(LANE - IDX_DIM), w_ki,
         _rope_pad(w_kr, 1), w_kb, w_vb, w_wi, z(LANE - IDX_HEADS), z(n_pad - n_used)],
        axis=1).astype(MXU_DTYPE)

    w_uq3 = w_uq.reshape(Q_LORA, H_A, QK_DIM_A)
    w_uq_pad = jnp.concatenate([w_uq3[:, :, :QK_NOPE], _rope_pad(w_uq3[:, :, QK_NOPE:], 2)], axis=2)
    w_uq_pad = w_uq_pad.reshape(Q_LORA, H_A * HEAD_PAD_A).astype(MXU_DTYPE)
    gq_head = jnp.concatenate([g_qn_a[:QK_NOPE], _rope_pad(g_qn_a[QK_NOPE:], 0)]) * (QK_DIM_A ** -0.5 * LOG2E)
    gq_pad = jnp.tile(gq_head, H_A).reshape(1, H_A * HEAD_PAD_A).astype(F32)
    w_ukv3 = w_ukv.reshape(KV_LORA, H_A, QK_NOPE + V_DIM_A)
    w_uk = w_ukv3[:, :, :QK_NOPE].reshape(KV_LORA, H_A * QK_NOPE).astype(MXU_DTYPE)
    w_uv = w_ukv3[:, :, QK_NOPE:].reshape(KV_LORA, WIDTH_A).astype(MXU_DTYPE)
    gk_nope = g_kn_a[:QK_NOPE].reshape(1, LANE).astype(F32)
    gk_rope = _rope_pad(g_kn_a[QK_NOPE:], 0).reshape(1, LANE).astype(F32)
    gq_b = (g_qn_b * (HEAD_DIM_B ** -0.5 * LOG2E)).reshape(1, LANE).astype(F32)
    gk_b = g_kn_b.reshape(1, LANE).astype(F32)

    inv = ROPE_THETA ** (-jnp.arange(HALF_ROPE, dtype=F32) / HALF_ROPE)
    ang = positions.reshape(t, 1).astype(F32) * inv
    cos, sin = jnp.cos(ang), jnp.sin(ang)
    zr = jnp.zeros_like(cos)
    cos_t = jnp.concatenate([cos, zr, cos, zr], axis=1)
    sin_t = jnp.concatenate([-sin, zr, sin, zr], axis=1)

    x2 = x.reshape(t, d)
    proj = _in_proj(x2, g_pre, w_pad, tm=512, tn=tn_in)

    q_a = _qa_proj(proj, off["cq"] // Q_LORA, g_q_lat, w_uq_pad, gq_pad, cos_t, sin_t, tm=256)
    k_a, v_a = _kva_proj(proj, off["ckv"] // KV_LORA, off["k_rope"] // LANE, g_kv_lat, w_uk, w_uv,
                         gk_nope, gk_rope, cos_t, sin_t, tm=256)
    o_a = _attn_a(q_a, k_a, v_a, proj, off["gate_a"] // LANE, b=b, l=l, tq=tq, heads=4)

    nq = l // tq
    w_idx = proj[:, off["w_idx"]:off["w_idx"] + IDX_HEADS]
    wt = w_idx.astype(F32).T
    v_b = proj[:, off["v_b"]:off["v_b"] + HEAD_DIM_B]
    vt = v_b.reshape(b, nq, tq, HEAD_DIM_B).transpose(0, 1, 3, 2)
    blk = {"q_idx": off["q_idx"] // (IDX_HEADS * IDX_DIM), "k_idx": off["k_idx"] // (2 * LANE),
           "q_b": off["q_b"] // WIDTH_B, "k_b": off["k_b"] // LANE, "gate_b": off["gate_b"] // WIDTH_B}
    o_b = _attn_b(proj, wt, vt, t5_bias.astype(F32), gq_b, gk_b, blk, b=b, l=l, tq=tq)

    merged = _merge(o_a, o_b, p_a.astype(MXU_DTYPE), p_b.astype(MXU_DTYPE), proj,
                    off["merge_a"], off["merge_b"], tm=512, tn=512)
    out = _out_proj(merged, w_o.astype(MXU_DTYPE), x2, tm=512, tn=512)
    return out.reshape(b, l, d)
```

```python
import functools
import math

import jax
import jax.numpy as jnp
from jax import lax
from jax.experimental import pallas as pl
from jax.experimental.pallas import tpu as pltpu

F32 = jnp.float32
I32 = jnp.int32
MXU_DTYPE = jnp.bfloat16

H_A = 16
QK_NOPE = 128
QK_ROPE = 64
QK_DIM_A = QK_NOPE + QK_ROPE
V_DIM_A = 128
Q_LORA = 1024
KV_LORA = 512
ROPE_THETA = 10000.0
H_B = 16
HEAD_DIM_B = 128
IDX_HEADS = 32
IDX_DIM = 64
TOPK_MAX = 256
N_BUCKETS = 32
MAX_DISTANCE = 128
EPS = 1e-6
WIDTH_A = H_A * V_DIM_A
WIDTH_B = H_B * HEAD_DIM_B

LANE = 128
HALF_ROPE = QK_ROPE // 2
HEAD_PAD_A = 2 * LANE
VMEM_LIMIT = 56 * 1024 * 1024

BISECT_STEPS_PER_CHECK = 4
BISECT_MAX_CHECKS = 40
LOG2E = math.log2(math.e)
NEG_INF = float("-inf")
POS_INF = float("inf")


def _nt_dot(a, b):
    return lax.dot_general(a, b, (((1,), (1,)), ((), ())), preferred_element_type=F32)


def _params(sem, vmem=VMEM_LIMIT):
    return pltpu.CompilerParams(dimension_semantics=sem, vmem_limit_bytes=vmem)


def _w_relayout_kernel(tile_ref, shift_ref, *refs, shifts, tiles_per_block, n_main):
    src, tail_ref, o_ref = refs[:-2], refs[-2], refs[-1]
    j = pl.program_id(1)
    for r in shifts:
        @pl.when((shift_ref[j] == r) & (j < n_main))
        def _(r=r):
            for k in range(tiles_per_block):
                if r == 0:
                    piece = src[k][...]
                else:
                    piece = jnp.concatenate([src[k][:, r:], src[k + 1][:, :r]], axis=1)
                o_ref[:, k * LANE:(k + 1) * LANE] = piece.astype(o_ref.dtype)

    @pl.when(j >= n_main)
    def _():
        o_ref[...] = tail_ref[...]


def _w_relayout(w_in, src_starts, w_tail, *, tr, tn):
    d = w_in.shape[0]
    n_main, n_tail = len(src_starts), w_tail.shape[1] // tn
    tiles = tn // LANE
    pad = [0] * n_tail
    tile0 = jnp.array([s // LANE for s in src_starts] + pad, I32)
    shift = jnp.array([s % LANE for s in src_starts] + pad, I32)
    shifts = tuple(sorted({s % LANE for s in src_starts}))
    in_specs = [pl.BlockSpec((tr, LANE), functools.partial(lambda i, j, t0, sh, k: (i, t0[j] + k), k=k))
                for k in range(tiles + 1)]
    in_specs.append(pl.BlockSpec((tr, tn), lambda i, j, t0, sh: (i, jnp.maximum(j - n_main, 0))))
    return pl.pallas_call(
        functools.partial(_w_relayout_kernel, shifts=shifts, tiles_per_block=tiles, n_main=n_main),
        out_shape=jax.ShapeDtypeStruct((d, (n_main + n_tail) * tn), MXU_DTYPE),
        grid_spec=pltpu.PrefetchScalarGridSpec(
            num_scalar_prefetch=2,
            grid=(d // tr, n_main + n_tail),
            in_specs=in_specs,
            out_specs=pl.BlockSpec((tr, tn), lambda i, j, t0, sh: (i, j)),
        ),
        compiler_params=_params(("arbitrary", "arbitrary")),
        name="w_relayout",
    )(tile0, shift, *([w_in] * (tiles + 1)), w_tail)


def _in_proj_kernel(x_ref, g_ref, w_ref, o_ref, hn_ref, *, row_chunk):
    tm = x_ref.shape[0]

    @pl.when(pl.program_id(1) == 0)
    def _():
        def body(r, carry):
            sl = pl.ds(pl.multiple_of(r * row_chunk, row_chunk), row_chunk)
            xx = x_ref[sl, :]
            ms = jnp.mean(xx * xx, axis=-1, keepdims=True)
            hn_ref[sl, :] = (xx * lax.rsqrt(ms + EPS) * g_ref[...]).astype(hn_ref.dtype)
            return carry

        lax.fori_loop(0, tm // row_chunk, body, 0)

    o_ref[...] = jnp.dot(hn_ref[...], w_ref[...], preferred_element_type=F32).astype(o_ref.dtype)


def _in_proj(x2, g_pre, w_pad, *, tm, tn):
    t, d = x2.shape
    n = w_pad.shape[1]
    return pl.pallas_call(
        functools.partial(_in_proj_kernel, row_chunk=64),
        out_shape=jax.ShapeDtypeStruct((t, n), MXU_DTYPE),
        grid=(t // tm, n // tn),
        in_specs=[
            pl.BlockSpec((tm, d), lambda i, j: (i, 0)),
            pl.BlockSpec((1, d), lambda i, j: (0, 0)),
            pl.BlockSpec((d, tn), lambda i, j: (0, j)),
        ],
        out_specs=pl.BlockSpec((tm, tn), lambda i, j: (i, j)),
        scratch_shapes=[pltpu.VMEM((tm, d), MXU_DTYPE)],
        compiler_params=_params(("arbitrary", "arbitrary")),
        name="in_proj",
    )(x2, g_pre.reshape(1, d), w_pad)


def _rope_lanes(r, cos_ref, sin_ref):
    return r * cos_ref[...] + pltpu.roll(r, 2 * HALF_ROPE, 1) * sin_ref[...]


def _qa_proj_kernel(cq_ref, gl_ref, w_ref, gq_ref, cos_ref, sin_ref, o_ref):
    c = cq_ref[...].astype(F32)
    ms = jnp.mean(c * c, axis=-1, keepdims=True)
    cn = (c * lax.rsqrt(ms + EPS) * gl_ref[...]).astype(MXU_DTYPE)
    q = jnp.dot(cn, w_ref[...], preferred_element_type=F32)
    for h in range(H_A):
        lo = h * HEAD_PAD_A
        qh = q[:, lo:lo + HEAD_PAD_A]
        ss = jnp.sum(qh * qh, axis=-1, keepdims=True) * (1.0 / QK_DIM_A)
        qn = qh * lax.rsqrt(ss + EPS) * gq_ref[:, lo:lo + HEAD_PAD_A]
        o_ref[:, lo:lo + LANE] = qn[:, :LANE].astype(o_ref.dtype)
        o_ref[:, lo + LANE:lo + HEAD_PAD_A] = _rope_lanes(qn[:, LANE:], cos_ref, sin_ref).astype(o_ref.dtype)


def _qa_proj(proj, cq_blk, g_q_lat, w_uq_pad, gq_pad, cos_t, sin_t, *, tm):
    t = proj.shape[0]
    nq = H_A * HEAD_PAD_A
    return pl.pallas_call(
        _qa_proj_kernel,
        out_shape=jax.ShapeDtypeStruct((t, nq), MXU_DTYPE),
        grid=(t // tm,),
        in_specs=[
            pl.BlockSpec((tm, Q_LORA), lambda i: (i, cq_blk)),
            pl.BlockSpec((1, Q_LORA), lambda i: (0, 0)),
            pl.BlockSpec((Q_LORA, nq), lambda i: (0, 0)),
            pl.BlockSpec((1, nq), lambda i: (0, 0)),
            pl.BlockSpec((tm, LANE), lambda i: (i, 0)),
            pl.BlockSpec((tm, LANE), lambda i: (i, 0)),
        ],
        out_specs=pl.BlockSpec((tm, nq), lambda i: (i, 0)),
        compiler_params=_params(("arbitrary",)),
        name="qa_proj",
    )(proj, g_q_lat.reshape(1, Q_LORA), w_uq_pad, gq_pad, cos_t, sin_t)


def _kva_proj_kernel(ckv_ref, kr_ref, gl_ref, wk_ref, wv_ref, gkn_ref, gkr_ref, cos_ref, sin_ref,
                     k_ref, v_ref):
    c = ckv_ref[...].astype(F32)
    ms = jnp.mean(c * c, axis=-1, keepdims=True)
    cn = (c * lax.rsqrt(ms + EPS) * gl_ref[...]).astype(MXU_DTYPE)
    kn = jnp.dot(cn, wk_ref[...], preferred_element_type=F32)
    v_ref[...] = jnp.dot(cn, wv_ref[...], preferred_element_type=F32).astype(v_ref.dtype)
    kr = kr_ref[...].astype(F32)
    ss_r = jnp.sum(kr * kr, axis=-1, keepdims=True)
    krr = _rope_lanes(kr * gkr_ref[...], cos_ref, sin_ref)
    for h in range(H_A):
        kh = kn[:, h * LANE:(h + 1) * LANE]
        ss = (jnp.sum(kh * kh, axis=-1, keepdims=True) + ss_r) * (1.0 / QK_DIM_A)
        rs = lax.rsqrt(ss + EPS)
        lo = h * HEAD_PAD_A
        k_ref[:, lo:lo + LANE] = (kh * rs * gkn_ref[...]).astype(k_ref.dtype)
        k_ref[:, lo + LANE:lo + HEAD_PAD_A] = (krr * rs).astype(k_ref.dtype)


def _kva_proj(proj, ckv_blk, krope_blk, g_kv_lat, w_uk, w_uv, gk_nope, gk_rope, cos_t, sin_t, *, tm):
    t = proj.shape[0]
    return pl.pallas_call(
        _kva_proj_kernel,
        out_shape=(jax.ShapeDtypeStruct((t, H_A * HEAD_PAD_A), MXU_DTYPE),
                   jax.ShapeDtypeStruct((t, WIDTH_A), MXU_DTYPE)),
        grid=(t // tm,),
        in_specs=[
            pl.BlockSpec((tm, KV_LORA), lambda i: (i, ckv_blk)),
            pl.BlockSpec((tm, LANE), lambda i: (i, krope_blk)),
            pl.BlockSpec((1, KV_LORA), lambda i: (0, 0)),
            pl.BlockSpec((KV_LORA, H_A * QK_NOPE), lambda i: (0, 0)),
            pl.BlockSpec((KV_LORA, WIDTH_A), lambda i: (0, 0)),
            pl.BlockSpec((1, LANE), lambda i: (0, 0)),
            pl.BlockSpec((1, LANE), lambda i: (0, 0)),
            pl.BlockSpec((tm, LANE), lambda i: (i, 0)),
            pl.BlockSpec((tm, LANE), lambda i: (i, 0)),
        ],
        out_specs=(pl.BlockSpec((tm, H_A * HEAD_PAD_A), lambda i: (i, 0)),
                   pl.BlockSpec((tm, WIDTH_A), lambda i: (i, 0))),
        compiler_params=_params(("arbitrary",)),
        name="kva_proj",
    )(proj, proj, g_kv_lat.reshape(1, KV_LORA), w_uk, w_uv, gk_nope, gk_rope, cos_t, sin_t)


def _silu(g):
    return g * (1.0 / (1.0 + jnp.exp(-g)))


def _lane_tile_reduce(x, op):
    acc = x[:, :LANE]
    for t in range(1, x.shape[1] // LANE):
        acc = op(acc, x[:, t * LANE:(t + 1) * LANE])
    return acc


def _attn_a_kernel(q_ref, k_ref, v_ref, gate_ref, o_ref, *, tq, nq, heads):
    qi = pl.program_id(2)

    def branch(qv):
        n_off = qv * tq
        for g in range(heads):
            kc = slice(g * HEAD_PAD_A, (g + 1) * HEAD_PAD_A)
            vc = slice(g * V_DIM_A, (g + 1) * V_DIM_A)
            q = q_ref[:, kc]
            s_diag = _nt_dot(q, k_ref[n_off:n_off + tq, kc])
            row = lax.broadcasted_iota(I32, s_diag.shape, 0)
            col = lax.broadcasted_iota(I32, s_diag.shape, 1)
            s_diag = jnp.where(row >= col, s_diag, NEG_INF)
            m_t = _lane_tile_reduce(s_diag, jnp.maximum)
            if qv > 0:
                s_off = _nt_dot(q, k_ref[0:n_off, kc])
                m_t = jnp.maximum(m_t, _lane_tile_reduce(s_off, jnp.maximum))
            m = jnp.max(m_t, axis=-1, keepdims=True)
            p_diag = jnp.exp2(s_diag - m)
            l_t = _lane_tile_reduce(p_diag, jnp.add)
            acc = jnp.dot(p_diag.astype(MXU_DTYPE), v_ref[n_off:n_off + tq, vc], preferred_element_type=F32)
            if qv > 0:
                p_off = jnp.exp2(s_off - m)
                l_t = l_t + _lane_tile_reduce(p_off, jnp.add)
                acc = acc + jnp.dot(p_off.astype(MXU_DTYPE), v_ref[0:n_off, vc], preferred_element_type=F32)
            l = jnp.sum(l_t, axis=-1, keepdims=True)
            o = acc * (1.0 / l)
            o_ref[:, vc] = (o * _silu(gate_ref[:, vc].astype(F32))).astype(o_ref.dtype)

    for qv in range(nq):
        pl.when(qi == qv)(functools.partial(branch, qv))


def _attn_a(q_a, k_a, v_a, proj, gate_blk0, *, b, l, tq, heads):
    t = q_a.shape[0]
    nq = l // tq
    kw, vw = heads * HEAD_PAD_A, heads * V_DIM_A
    return pl.pallas_call(
        functools.partial(_attn_a_kernel, tq=tq, nq=nq, heads=heads),
        out_shape=jax.ShapeDtypeStruct((t, WIDTH_A), MXU_DTYPE),
        grid=(b, H_A // heads, nq),
        in_specs=[
            pl.BlockSpec((tq, kw), lambda bi, h, qi: (bi * nq + qi, h)),
            pl.BlockSpec((l, kw), lambda bi, h, qi: (bi, h)),
            pl.BlockSpec((l, vw), lambda bi, h, qi: (bi, h)),
            pl.BlockSpec((tq, vw), lambda bi, h, qi: (bi * nq + qi, gate_blk0 // heads + h)),
        ],
        out_specs=pl.BlockSpec((tq, vw), lambda bi, h, qi: (bi * nq + qi, h)),
        compiler_params=_params(("arbitrary", "arbitrary", "arbitrary")),
        name="attn_a",
    )(q_a, k_a, v_a, proj)


def _t5_bucket(dist):
    max_exact = N_BUCKETS // 2
    n = jnp.maximum(dist, 0)
    nf = jnp.maximum(n, 1).astype(F32)
    large = max_exact + (jnp.log(nf / max_exact) / math.log(MAX_DISTANCE / max_exact)
                         * (N_BUCKETS - max_exact)).astype(I32)
    large = jnp.minimum(large, N_BUCKETS - 1)
    return jnp.where(n < max_exact, n, large)


def _attn_b_kernel(t5_ref, qidx_ref, kidx_ref, wt_ref, qb_ref, kb_ref, vt_ref, gate_ref, gq_ref, gk_ref,
                   o_ref, sc_ref, qn_ref, acc_ref, m_ref, l_ref, bias_ref, thr_ref, *, tq, nq, topk, max_iters):
    bi = pl.program_id(0)
    qi = pl.program_id(1)
    ck = tq
    shape = (ck, tq)

    @pl.when((bi == 0) & (qi == 0))
    def _():
        s_loc = lax.broadcasted_iota(I32, shape, 0)
        t_loc = lax.broadcasted_iota(I32, shape, 1)
        for near in range(2):
            bucket = _t5_bucket(t_loc - s_loc + (1 - near) * ck)

            def per_head(h, carry, bucket=bucket, near=near):
                far = t5_ref[N_BUCKETS - 1, h]
                tab = jnp.zeros(shape, F32)
                for bk in range(N_BUCKETS - 1):
                    tab = jnp.where(bucket == bk, (t5_ref[bk, h] - far) * LOG2E, tab)
                bias_ref[h, near] = tab
                return carry

            lax.fori_loop(0, H_B, per_head, 0)

    w_all = wt_ref[...] * (IDX_HEADS ** -0.5)

    def score_chunk(c, diag):
        rows = pl.ds(pl.multiple_of(c * ck, ck), ck)
        ka = kidx_ref[rows, 0:LANE]
        kb = kidx_ref[rows, LANE:2 * LANE]
        score = jnp.zeros(shape, F32)
        for j in range(IDX_HEADS // 2):
            qp = qidx_ref[:, j * LANE:(j + 1) * LANE]
            score = score + jnp.maximum(_nt_dot(ka, qp), 0.0) * w_all[2 * j:2 * j + 1, :]
            score = score + jnp.maximum(_nt_dot(kb, qp), 0.0) * w_all[2 * j + 1:2 * j + 2, :]
        if diag:
            adm = lax.broadcasted_iota(I32, shape, 0) <= lax.broadcasted_iota(I32, shape, 1)
            lo_src = jnp.where(adm, score, POS_INF)
            score = jnp.where(adm, score, NEG_INF)
        else:
            lo_src = score
        sc_ref[c] = score
        return jnp.max(score, axis=0, keepdims=True), jnp.min(lo_src, axis=0, keepdims=True)

    def score_body(c, carry):
        mx, mn = carry
        cmx, cmn = score_chunk(c, False)
        return jnp.maximum(mx, cmx), jnp.minimum(mn, cmn)

    mx, mn = lax.fori_loop(0, qi, score_body,
                           (jnp.full((1, tq), NEG_INF, F32), jnp.full((1, tq), POS_INF, F32)))
    dmx, dmn = score_chunk(qi, True)
    mx = jnp.maximum(mx, dmx)
    mn = jnp.minimum(mn, dmn)

    rep = (8, tq)
    n_adm = qi * tq + lax.broadcasted_iota(I32, rep, 1) + 1
    kp = jnp.minimum(n_adm, topk)
    mx8 = jnp.broadcast_to(mx, rep)
    mn8 = jnp.broadcast_to(mn, rep)

    def bisect(nchunks):
        def count_ge(x):
            acc = jnp.zeros(rep, I32)
            for c in range(nchunks):
                ge = sc_ref[c].reshape(ck // 8, 8, tq) >= x[None]
                acc = acc + jnp.sum(ge.astype(I32), axis=0)
            for shift in (4, 2, 1):
                acc = acc + pltpu.roll(acc, shift, 0)
            return acc

        def bis_cond(st):
            it, lo, hi, mid, cnt_lo = st
            active = (cnt_lo != kp) & (mid > lo) & (mid < hi)
            return jnp.logical_and(it < max_iters, jnp.max(active.astype(I32)) > 0)

        def bis_body(st):
            it, lo, hi, mid, cnt_lo = st
            for _ in range(BISECT_STEPS_PER_CHECK):
                cnt = count_ge(mid)
                ge = cnt >= kp
                lo = jnp.where(ge, mid, lo)
                cnt_lo = jnp.where(ge, cnt, cnt_lo)
                hi = jnp.where(ge, hi, mid)
                mid = jnp.where(hi == POS_INF, mx8, lo + 0.5 * (hi - lo))
            return it + 1, lo, hi, mid, cnt_lo

        _, lo, hi, _, cnt_lo = lax.while_loop(
            bis_cond, bis_body, (jnp.int32(0), mn8, jnp.full(rep, POS_INF, F32), mx8, n_adm))
        thr_ref[...] = lo

        tied = cnt_lo > kp

        @pl.when(jnp.max(tied.astype(I32)) > 0)
        def _():
            n_keys = nchunks * ck
            sub = lax.broadcasted_iota(I32, (ck // 8, 8, tq), 0) * 8 + lax.broadcasted_iota(I32, (ck // 8, 8, tq), 1)

            def count_kept(j_last):
                acc = jnp.zeros(rep, I32)
                for c in range(nchunks):
                    s3 = sc_ref[c].reshape(ck // 8, 8, tq)
                    keep = (s3 >= hi[None]) | ((s3 >= lo[None]) & (sub + c * ck <= j_last[None]))
                    acc = acc + jnp.sum(keep.astype(I32), axis=0)
                for shift in (4, 2, 1):
                    acc = acc + pltpu.roll(acc, shift, 0)
                return acc

            def idx_step(_, carry):
                j_lo, j_hi = carry
                j_mid = j_lo + ((j_hi - j_lo) >> 1)
                ok = count_kept(j_mid) >= kp
                return jnp.where(ok, j_lo, j_mid), jnp.where(ok, j_mid, j_hi)

            _, j_hi = lax.fori_loop(0, max(1, (n_keys - 1).bit_length()), idx_step,
                                    (jnp.full(rep, -1, I32), jnp.full(rep, n_keys - 1, I32)))
            j_last = jnp.where(tied, j_hi, n_keys - 1)
            for c in range(nchunks):
                s3 = sc_ref[c].reshape(ck // 8, 8, tq)
                drop = (s3 >= lo[None]) & (s3 < hi[None]) & (sub + c * ck > j_last[None])
                sc_ref[c] = jnp.where(drop, NEG_INF, s3).reshape(ck, tq)

    for qv in range(nq):
        pl.when(qi == qv)(functools.partial(bisect, qv + 1))
    thr = thr_ref[0:1, :]

    for h in range(H_B):
        qh = qb_ref[:, h * LANE:(h + 1) * LANE].astype(F32)
        ms = jnp.mean(qh * qh, axis=-1, keepdims=True)
        qn_ref[h * tq:(h + 1) * tq, :] = (qh * lax.rsqrt(ms + EPS) * gq_ref[...]).astype(qn_ref.dtype)
    m_ref[...] = jnp.full(m_ref.shape, NEG_INF, F32)
    l_ref[...] = jnp.zeros(l_ref.shape, F32)
    acc_ref[...] = jnp.zeros(acc_ref.shape, F32)

    def attend_chunk(c, near):
        rows = pl.ds(pl.multiple_of(c * ck, ck), ck)
        kc = kb_ref[rows, :].astype(F32)
        ms = jnp.mean(kc * kc, axis=-1, keepdims=True)
        kc = (kc * lax.rsqrt(ms + EPS) * gk_ref[...]).astype(MXU_DTYPE)
        vt = vt_ref[c]
        sel = sc_ref[c] >= thr
        s_all = _nt_dot(kc, qn_ref[...])
        for h in range(H_B):
            s = s_all[:, h * tq:(h + 1) * tq]
            if near is not None:
                s = s + bias_ref[h, near]
            s = jnp.where(sel, s, NEG_INF)
            m_old = m_ref[h]
            m_new = jnp.maximum(m_old, jnp.max(s, axis=0, keepdims=True))
            m_safe = jnp.where(m_new == NEG_INF, 0.0, m_new)
            p = jnp.exp2(s - m_safe)
            alpha = jnp.exp2(m_old - m_safe)
            l_ref[h] = alpha * l_ref[h] + jnp.sum(p, axis=0, keepdims=True)
            acc_ref[h] = alpha * acc_ref[h] + jnp.dot(vt, p.astype(MXU_DTYPE), preferred_element_type=F32)
            m_ref[h] = m_new

    def far_body(c, carry):
        attend_chunk(c, None)
        return carry

    lax.fori_loop(0, jnp.maximum(qi - 1, 0), far_body, 0)

    @pl.when(qi >= 1)
    def _():
        attend_chunk(qi - 1, 0)

    attend_chunk(qi, 1)

    for h in range(H_B):
        o_t = acc_ref[h] * (1.0 / l_ref[h])
        g = gate_ref[:, h * LANE:(h + 1) * LANE].astype(F32)
        o_ref[:, h * LANE:(h + 1) * LANE] = (o_t.T * _silu(g)).astype(o_ref.dtype)


def _attn_b(proj, wt, vt, t5_bias, gq_b, gk_b, blk, *, b, l, tq):
    t = proj.shape[0]
    nq = l // tq
    topk = min(TOPK_MAX, l // 4)
    row = lambda bi, qi: bi * nq + qi
    return pl.pallas_call(
        functools.partial(_attn_b_kernel, tq=tq, nq=nq, topk=topk, max_iters=BISECT_MAX_CHECKS),
        out_shape=jax.ShapeDtypeStruct((t, WIDTH_B), MXU_DTYPE),
        grid=(b, nq),
        in_specs=[
            pl.BlockSpec(memory_space=pltpu.SMEM),
            pl.BlockSpec((tq, IDX_HEADS * IDX_DIM), lambda bi, qi: (row(bi, qi), blk["q_idx"])),
            pl.BlockSpec((l, 2 * LANE), lambda bi, qi: (bi, blk["k_idx"])),
            pl.BlockSpec((IDX_HEADS, tq), lambda bi, qi: (0, row(bi, qi))),
            pl.BlockSpec((tq, WIDTH_B), lambda bi, qi: (row(bi, qi), blk["q_b"])),
            pl.BlockSpec((l, LANE), lambda bi, qi: (bi, blk["k_b"])),
            pl.BlockSpec((None, nq, LANE, tq), lambda bi, qi: (bi, 0, 0, 0)),
            pl.BlockSpec((tq, WIDTH_B), lambda bi, qi: (row(bi, qi), blk["gate_b"])),
            pl.BlockSpec((1, LANE), lambda bi, qi: (0, 0)),
            pl.BlockSpec((1, LANE), lambda bi, qi: (0, 0)),
        ],
        out_specs=pl.BlockSpec((tq, WIDTH_B), lambda bi, qi: (row(bi, qi), 0)),
        scratch_shapes=[
            pltpu.VMEM((nq, tq, tq), F32),
            pltpu.VMEM((H_B * tq, LANE), MXU_DTYPE),
            pltpu.VMEM((H_B, LANE, tq), F32),
            pltpu.VMEM((H_B, 1, tq), F32),
            pltpu.VMEM((H_B, 1, tq), F32),
            pltpu.VMEM((H_B, 2, tq, tq), F32),
            pltpu.VMEM((8, tq), F32),
        ],
        compiler_params=_params(("arbitrary", "arbitrary")),
        name="attn_b",
    )(t5_bias, proj, proj, wt, proj, proj, vt, proj, gq_b, gk_b)


def _sigmoid(z):
    return 1.0 / (1.0 + jnp.exp(-z))


def _merge_kernel(oa_ref, ob_ref, pa_ref, pb_ref, ma_ref, mb_ref, o_ref):
    a = jnp.dot(oa_ref[...], pa_ref[...], preferred_element_type=F32)
    bb = jnp.dot(ob_ref[...], pb_ref[...], preferred_element_type=F32)
    o_ref[...] = (_sigmoid(ma_ref[...].astype(F32)) * a + _sigmoid(mb_ref[...].astype(F32)) * bb).astype(o_ref.dtype)


def _merge(o_a, o_b, p_a, p_b, proj, ma_off, mb_off, *, tm, tn):
    t = o_a.shape[0]
    d = p_a.shape[1]
    ma0, mb0 = ma_off // tn, mb_off // tn
    return pl.pallas_call(
        _merge_kernel,
        out_shape=jax.ShapeDtypeStruct((t, d), MXU_DTYPE),
        grid=(t // tm, d // tn),
        in_specs=[
            pl.BlockSpec((tm, WIDTH_A), lambda i, j: (i, 0)),
            pl.BlockSpec((tm, WIDTH_B), lambda i, j: (i, 0)),
            pl.BlockSpec((WIDTH_A, tn), lambda i, j: (0, j)),
            pl.BlockSpec((WIDTH_B, tn), lambda i, j: (0, j)),
            pl.BlockSpec((tm, tn), lambda i, j: (i, ma0 + j)),
            pl.BlockSpec((tm, tn), lambda i, j: (i, mb0 + j)),
        ],
        out_specs=pl.BlockSpec((tm, tn), lambda i, j: (i, j)),
        compiler_params=_params(("arbitrary", "arbitrary")),
        name="merge",
    )(o_a, o_b, p_a, p_b, proj, proj)


def _out_proj_kernel(m_ref, w_ref, x_ref, o_ref):
    o_ref[...] = x_ref[...] + jnp.dot(m_ref[...], w_ref[...], preferred_element_type=F32)


def _out_proj(merged, w_o, x2, *, tm, tn):
    t, d = x2.shape
    return pl.pallas_call(
        _out_proj_kernel,
        out_shape=jax.ShapeDtypeStruct((t, d), x2.dtype),
        grid=(t // tm, d // tn),
        in_specs=[
            pl.BlockSpec((tm, d), lambda i, j: (i, 0)),
            pl.BlockSpec((d, tn), lambda i, j: (0, j)),
            pl.BlockSpec((tm, tn), lambda i, j: (i, j)),
        ],
        out_specs=pl.BlockSpec((tm, tn), lambda i, j: (i, j)),
        compiler_params=_params(("arbitrary", "arbitrary")),
        name="out_proj",
    )(merged, w_o, x2)


def _rope_pad(a, axis):
    a1, a2 = jnp.split(a, 2, axis=axis)
    z = jnp.zeros_like(a1)
    return jnp.concatenate([a1, z, a2, z], axis=axis)


def _layout(d):
    names = [("merge_a", d), ("merge_b", d), ("gate_a", WIDTH_A), ("gate_b", WIDTH_B),
             ("q_b", WIDTH_B), ("q_idx", IDX_HEADS * IDX_DIM), ("cq", Q_LORA), ("ckv", KV_LORA),
             ("k_idx", 2 * LANE), ("k_rope", LANE), ("k_b", LANE), ("v_b", LANE), ("w_idx", LANE)]
    off, out = 0, {}
    for name, width in names:
        assert off % width == 0, (name, off, width)
        out[name] = off
        off += width
    return out, off


def kernel(x, positions, g_pre, w_in, g_q_lat, g_kv_lat, w_uq, w_ukv, g_qn_a, g_kn_a,
           g_qn_b, g_kn_b, t5_bias, p_a, p_b, w_o):
    b, l, d = x.shape
    t = b * l
    tq = 256
    tn_in = 512
    off, n_used = _layout(d)
    n_pad = -(-n_used // tn_in) * tn_in

    names = ["cq", "ckv", "k_rope", "q_b", "k_b", "v_b", "q_idx", "k_idx", "w_idx", "gate_a", "gate_b",
             "merge_a", "merge_b"]
    sizes = [Q_LORA, KV_LORA, QK_ROPE, WIDTH_B, HEAD_DIM_B, HEAD_DIM_B, IDX_HEADS * IDX_DIM, IDX_DIM,
             IDX_HEADS, WIDTH_A, WIDTH_B, d, d]
    src, acc = {}, 0
    for name, s in zip(names, sizes):
        src[name] = (acc, s)
        acc += s
    main_groups = ["merge_a", "merge_b", "gate_a", "gate_b", "q_b", "q_idx", "cq", "ckv"]
    src_starts, dst = [], 0
    for name in main_groups:
        assert off[name] == dst and src[name][1] % tn_in == 0
        src_starts += [src[name][0] + c for c in range(0, src[name][1], tn_in)]
        dst += src[name][1]
    cols = lambda name: w_in[:, src[name][0]:src[name][0] + src[name][1]]
    z = lambda n: jnp.zeros((d, n), w_in.dtype)
    w_ki = cols("k_idx")
    assert off["k_idx"] == dst
    w_tail = jnp.concatenate(
        [w_ki, z(LANE - IDX_DIM), z(LANE - IDX_DIM), w_ki, _rope_pad(cols("k_rope"), 1), cols("k_b"),
         cols("v_b"), cols("w_idx"), z(LANE - IDX_HEADS), z(n_pad - n_used)], axis=1).astype(MXU_DTYPE)
    w_pad = _w_relayout(w_in, src_starts, w_tail, tr=1024, tn=tn_in)

    w_uq3 = w_uq.reshape(Q_LORA, H_A, QK_DIM_A)
    w_uq_pad = jnp.concatenate([w_uq3[:, :, :QK_NOPE], _rope_pad(w_uq3[:, :, QK_NOPE:], 2)], axis=2)
    w_uq_pad = w_uq_pad.reshape(Q_LORA, H_A * HEAD_PAD_A).astype(MXU_DTYPE)
    gq_head = jnp.concatenate([g_qn_a[:QK_NOPE], _rope_pad(g_qn_a[QK_NOPE:], 0)]) * (QK_DIM_A ** -0.5 * LOG2E)
    gq_pad = jnp.tile(gq_head, H_A).reshape(1, H_A * HEAD_PAD_A).astype(F32)
    w_ukv3 = w_ukv.reshape(KV_LORA, H_A, QK_NOPE + V_DIM_A)
    w_uk = w_ukv3[:, :, :QK_NOPE].reshape(KV_LORA, H_A * QK_NOPE).astype(MXU_DTYPE)
    w_uv = w_ukv3[:, :, QK_NOPE:].reshape(KV_LORA, WIDTH_A).astype(MXU_DTYPE)
    gk_nope = g_kn_a[:QK_NOPE].reshape(1, LANE).astype(F32)
    gk_rope = _rope_pad(g_kn_a[QK_NOPE:], 0).reshape(1, LANE).astype(F32)
    gq_b = (g_qn_b * (HEAD_DIM_B ** -0.5 * LOG2E)).reshape(1, LANE).astype(F32)
    gk_b = g_kn_b.reshape(1, LANE).astype(F32)

    inv = ROPE_THETA ** (-jnp.arange(HALF_ROPE, dtype=F32) / HALF_ROPE)
    ang = positions.reshape(t, 1).astype(F32) * inv
    cos, sin = jnp.cos(ang), jnp.sin(ang)
    zr = jnp.zeros_like(cos)
    cos_t = jnp.concatenate([cos, zr, cos, zr], axis=1)
    sin_t = jnp.concatenate([-sin, zr, sin, zr], axis=1)

    x2 = x.reshape(t, d)
    proj = _in_proj(x2, g_pre, w_pad, tm=1024, tn=tn_in)

    q_a = _qa_proj(proj, off["cq"] // Q_LORA, g_q_lat, w_uq_pad, gq_pad, cos_t, sin_t, tm=256)
    k_a, v_a = _kva_proj(proj, off["ckv"] // KV_LORA, off["k_rope"] // LANE, g_kv_lat, w_uk, w_uv,
                         gk_nope, gk_rope, cos_t, sin_t, tm=256)
    o_a = _attn_a(q_a, k_a, v_a, proj, off["gate_a"] // LANE, b=b, l=l, tq=tq, heads=4)

    nq = l // tq
    w_idx = proj[:, off["w_idx"]:off["w_idx"] + IDX_HEADS]
    wt = w_idx.astype(F32).T
    v_b = proj[:, off["v_b"]:off["v_b"] + HEAD_DIM_B]
    vt = v_b.reshape(b, nq, tq, HEAD_DIM_B).transpose(0, 1, 3, 2)
    blk = {"q_idx": off["q_idx"] // (IDX_HEADS * IDX_DIM), "k_idx": off["k_idx"] // (2 * LANE),
           "q_b": off["q_b"] // WIDTH_B, "k_b": off["k_b"] // LANE, "gate_b": off["gate_b"] // WIDTH_B}
    o_b = _attn_b(proj, wt, vt, t5_bias.astype(F32), gq_b, gk_b, blk, b=b, l=l, tq=tq)

    merged = _merge(o_a, o_b, p_a.astype(MXU_DTYPE), p_b.astype(MXU_DTYPE), proj,
                    off["merge_a"], off["merge_b"], tm=1024, tn=1024)
    out = _out_proj(merged, w_o.astype(MXU_DTYPE), x2, tm=1024, tn=1024)
    return out.reshape(b, l, d)
```

```python
import functools
import math

import jax
import jax.numpy as jnp
from jax import lax
from jax.experimental import pallas as pl
from jax.experimental.pallas import tpu as pltpu

F32 = jnp.float32
I32 = jnp.int32
MXU_DTYPE = jnp.bfloat16

H_A = 16
QK_NOPE = 128
QK_ROPE = 64
QK_DIM_A = QK_NOPE + QK_ROPE
V_DIM_A = 128
Q_LORA = 1024
KV_LORA = 512
ROPE_THETA = 10000.0
H_B = 16
HEAD_DIM_B = 128
IDX_HEADS = 32
IDX_DIM = 64
TOPK_MAX = 256
N_BUCKETS = 32
MAX_DISTANCE = 128
EPS = 1e-6
WIDTH_A = H_A * V_DIM_A
WIDTH_B = H_B * HEAD_DIM_B

LANE = 128
ROW_ALIGN = 32
HALF_ROPE = QK_ROPE // 2
HEAD_PAD_A = 2 * LANE
VMEM_LIMIT = 56 * 1024 * 1024

BISECT_STEPS_PER_CHECK = 4
BISECT_MAX_CHECKS = 40
LOG2E = math.log2(math.e)
NEG_INF = float("-inf")
POS_INF = float("inf")


def _nt_dot(a, b):
    return lax.dot_general(a, b, (((1,), (1,)), ((), ())), preferred_element_type=F32)


def _params(sem, vmem=VMEM_LIMIT):
    return pltpu.CompilerParams(dimension_semantics=sem, vmem_limit_bytes=vmem)


def _w_relayout_kernel(start_ref, src_ref, tail_ref, o_ref, *, n_main):
    j = pl.program_id(0)

    @pl.when(j < n_main)
    def _():
        o_ref[...] = src_ref[...].astype(o_ref.dtype)

    @pl.when(j >= n_main)
    def _():
        o_ref[...] = tail_ref[...]


def _w_relayout(wt, src_starts, wt_tail, *, tn, tc):
    d = wt.shape[1]
    n_main, n_tail = len(src_starts), wt_tail.shape[0] // tn
    assert all(s % ROW_ALIGN == 0 for s in src_starts)
    starts = jnp.array([s // ROW_ALIGN for s in src_starts] + [0] * n_tail, I32)
    return pl.pallas_call(
        functools.partial(_w_relayout_kernel, n_main=n_main),
        out_shape=jax.ShapeDtypeStruct(((n_main + n_tail) * tn, d), MXU_DTYPE),
        grid_spec=pltpu.PrefetchScalarGridSpec(
            num_scalar_prefetch=1,
            grid=(n_main + n_tail, d // tc),
            in_specs=[
                pl.BlockSpec((pl.Element(tn), pl.Element(tc)), lambda j, c, st: (st[j] * ROW_ALIGN, c * tc)),
                pl.BlockSpec((tn, tc), lambda j, c, st: (jnp.maximum(j - n_main, 0), c)),
            ],
            out_specs=pl.BlockSpec((tn, tc), lambda j, c, st: (j, c)),
        ),
        compiler_params=_params(("arbitrary", "arbitrary")),
        name="w_relayout",
    )(starts, wt, wt_tail)


def _in_proj_kernel(x_ref, g_ref, w_ref, o_ref, hn_ref, *, row_chunk):
    tm = x_ref.shape[0]

    @pl.when(pl.program_id(1) == 0)
    def _():
        def body(r, carry):
            sl = pl.ds(pl.multiple_of(r * row_chunk, row_chunk), row_chunk)
            xx = x_ref[sl, :]
            ms = jnp.mean(xx * xx, axis=-1, keepdims=True)
            hn_ref[sl, :] = (xx * lax.rsqrt(ms + EPS) * g_ref[...]).astype(hn_ref.dtype)
            return carry

        lax.fori_loop(0, tm // row_chunk, body, 0)

    o_ref[...] = _nt_dot(hn_ref[...], w_ref[...]).astype(o_ref.dtype)


def _in_proj(x2, g_pre, wt_pad, *, tm, tn):
    t, d = x2.shape
    n = wt_pad.shape[0]
    return pl.pallas_call(
        functools.partial(_in_proj_kernel, row_chunk=64),
        out_shape=jax.ShapeDtypeStruct((t, n), MXU_DTYPE),
        grid=(t // tm, n // tn),
        in_specs=[
            pl.BlockSpec((tm, d), lambda i, j: (i, 0)),
            pl.BlockSpec((1, d), lambda i, j: (0, 0)),
            pl.BlockSpec((tn, d), lambda i, j: (j, 0)),
        ],
        out_specs=pl.BlockSpec((tm, tn), lambda i, j: (i, j)),
        scratch_shapes=[pltpu.VMEM((tm, d), MXU_DTYPE)],
        compiler_params=_params(("arbitrary", "arbitrary")),
        name="in_proj",
    )(x2, g_pre.reshape(1, d), wt_pad)


def _rope_lanes(r, cos_ref, sin_ref):
    return r * cos_ref[...] + pltpu.roll(r, 2 * HALF_ROPE, 1) * sin_ref[...]


def _qa_proj_kernel(cq_ref, gl_ref, w_ref, gq_ref, cos_ref, sin_ref, o_ref):
    c = cq_ref[...].astype(F32)
    ms = jnp.mean(c * c, axis=-1, keepdims=True)
    cn = (c * lax.rsqrt(ms + EPS) * gl_ref[...]).astype(MXU_DTYPE)
    q = jnp.dot(cn, w_ref[...], preferred_element_type=F32)
    for h in range(H_A):
        lo = h * HEAD_PAD_A
        qh = q[:, lo:lo + HEAD_PAD_A]
        ss = jnp.sum(qh * qh, axis=-1, keepdims=True) * (1.0 / QK_DIM_A)
        qn = qh * lax.rsqrt(ss + EPS) * gq_ref[:, lo:lo + HEAD_PAD_A]
        o_ref[:, lo:lo + LANE] = qn[:, :LANE].astype(o_ref.dtype)
        o_ref[:, lo + LANE:lo + HEAD_PAD_A] = _rope_lanes(qn[:, LANE:], cos_ref, sin_ref).astype(o_ref.dtype)


def _qa_proj(proj, cq_blk, g_q_lat, w_uq_pad, gq_pad, cos_t, sin_t, *, tm):
    t = proj.shape[0]
    nq = H_A * HEAD_PAD_A
    return pl.pallas_call(
        _qa_proj_kernel,
        out_shape=jax.ShapeDtypeStruct((t, nq), MXU_DTYPE),
        grid=(t // tm,),
        in_specs=[
            pl.BlockSpec((tm, Q_LORA), lambda i: (i, cq_blk)),
            pl.BlockSpec((1, Q_LORA), lambda i: (0, 0)),
            pl.BlockSpec((Q_LORA, nq), lambda i: (0, 0)),
            pl.BlockSpec((1, nq), lambda i: (0, 0)),
            pl.BlockSpec((tm, LANE), lambda i: (i, 0)),
            pl.BlockSpec((tm, LANE), lambda i: (i, 0)),
        ],
        out_specs=pl.BlockSpec((tm, nq), lambda i: (i, 0)),
        compiler_params=_params(("arbitrary",)),
        name="qa_proj",
    )(proj, g_q_lat.reshape(1, Q_LORA), w_uq_pad, gq_pad, cos_t, sin_t)


def _kva_proj_kernel(ckv_ref, kr_ref, gl_ref, wk_ref, wv_ref, gkn_ref, gkr_ref, cos_ref, sin_ref,
                     k_ref, v_ref):
    c = ckv_ref[...].astype(F32)
    ms = jnp.mean(c * c, axis=-1, keepdims=True)
    cn = (c * lax.rsqrt(ms + EPS) * gl_ref[...]).astype(MXU_DTYPE)
    kn = jnp.dot(cn, wk_ref[...], preferred_element_type=F32)
    v_ref[...] = jnp.dot(cn, wv_ref[...], preferred_element_type=F32).astype(v_ref.dtype)
    kr = kr_ref[...].astype(F32)
    ss_r = jnp.sum(kr * kr, axis=-1, keepdims=True)
    krr = _rope_lanes(kr * gkr_ref[...], cos_ref, sin_ref)
    for h in range(H_A):
        kh = kn[:, h * LANE:(h + 1) * LANE]
        ss = (jnp.sum(kh * kh, axis=-1, keepdims=True) + ss_r) * (1.0 / QK_DIM_A)
        rs = lax.rsqrt(ss + EPS)
        lo = h * HEAD_PAD_A
        k_ref[:, lo:lo + LANE] = (kh * rs * gkn_ref[...]).astype(k_ref.dtype)
        k_ref[:, lo + LANE:lo + HEAD_PAD_A] = (krr * rs).astype(k_ref.dtype)


def _kva_proj(proj, ckv_blk, krope_blk, g_kv_lat, w_uk, w_uv, gk_nope, gk_rope, cos_t, sin_t, *, tm):
    t = proj.shape[0]
    return pl.pallas_call(
        _kva_proj_kernel,
        out_shape=(jax.ShapeDtypeStruct((t, H_A * HEAD_PAD_A), MXU_DTYPE),
                   jax.ShapeDtypeStruct((t, WIDTH_A), MXU_DTYPE)),
        grid=(t // tm,),
        in_specs=[
            pl.BlockSpec((tm, KV_LORA), lambda i: (i, ckv_blk)),
            pl.BlockSpec((tm, LANE), lambda i: (i, krope_blk)),
            pl.BlockSpec((1, KV_LORA), lambda i: (0, 0)),
            pl.BlockSpec((KV_LORA, H_A * QK_NOPE), lambda i: (0, 0)),
            pl.BlockSpec((KV_LORA, WIDTH_A), lambda i: (0, 0)),
            pl.BlockSpec((1, LANE), lambda i: (0, 0)),
            pl.BlockSpec((1, LANE), lambda i: (0, 0)),
            pl.BlockSpec((tm, LANE), lambda i: (i, 0)),
            pl.BlockSpec((tm, LANE), lambda i: (i, 0)),
        ],
        out_specs=(pl.BlockSpec((tm, H_A * HEAD_PAD_A), lambda i: (i, 0)),
                   pl.BlockSpec((tm, WIDTH_A), lambda i: (i, 0))),
        compiler_params=_params(("arbitrary",)),
        name="kva_proj",
    )(proj, proj, g_kv_lat.reshape(1, KV_LORA), w_uk, w_uv, gk_nope, gk_rope, cos_t, sin_t)


def _silu(g):
    return g * (1.0 / (1.0 + jnp.exp(-g)))


def _lane_tile_reduce(x, op):
    acc = x[:, :LANE]
    for t in range(1, x.shape[1] // LANE):
        acc = op(acc, x[:, t * LANE:(t + 1) * LANE])
    return acc


def _attn_a_kernel(q_ref, k_ref, v_ref, gate_ref, o_ref, *, tq, nq, heads):
    qi = pl.program_id(2)

    def branch(qv):
        n_off = qv * tq
        for g in range(heads):
            kc = slice(g * HEAD_PAD_A, (g + 1) * HEAD_PAD_A)
            vc = slice(g * V_DIM_A, (g + 1) * V_DIM_A)
            q = q_ref[:, kc]
            s_diag = _nt_dot(q, k_ref[n_off:n_off + tq, kc])
            row = lax.broadcasted_iota(I32, s_diag.shape, 0)
            col = lax.broadcasted_iota(I32, s_diag.shape, 1)
            s_diag = jnp.where(row >= col, s_diag, NEG_INF)
            m_t = _lane_tile_reduce(s_diag, jnp.maximum)
            if qv > 0:
                s_off = _nt_dot(q, k_ref[0:n_off, kc])
                m_t = jnp.maximum(m_t, _lane_tile_reduce(s_off, jnp.maximum))
            m = jnp.max(m_t, axis=-1, keepdims=True)
            p_diag = jnp.exp2(s_diag - m)
            l_t = _lane_tile_reduce(p_diag, jnp.add)
            acc = jnp.dot(p_diag.astype(MXU_DTYPE), v_ref[n_off:n_off + tq, vc], preferred_element_type=F32)
            if qv > 0:
                p_off = jnp.exp2(s_off - m)
                l_t = l_t + _lane_tile_reduce(p_off, jnp.add)
                acc = acc + jnp.dot(p_off.astype(MXU_DTYPE), v_ref[0:n_off, vc], preferred_element_type=F32)
            l = jnp.sum(l_t, axis=-1, keepdims=True)
            o = acc * (1.0 / l)
            o_ref[:, vc] = (o * _silu(gate_ref[:, vc].astype(F32))).astype(o_ref.dtype)

    for qv in range(nq):
        pl.when(qi == qv)(functools.partial(branch, qv))


def _attn_a(q_a, k_a, v_a, proj, gate_blk0, *, b, l, tq, heads):
    t = q_a.shape[0]
    nq = l // tq
    kw, vw = heads * HEAD_PAD_A, heads * V_DIM_A
    return pl.pallas_call(
        functools.partial(_attn_a_kernel, tq=tq, nq=nq, heads=heads),
        out_shape=jax.ShapeDtypeStruct((t, WIDTH_A), MXU_DTYPE),
        grid=(b, H_A // heads, nq),
        in_specs=[
            pl.BlockSpec((tq, kw), lambda bi, h, qi: (bi * nq + qi, h)),
            pl.BlockSpec((l, kw), lambda bi, h, qi: (bi, h)),
            pl.BlockSpec((l, vw), lambda bi, h, qi: (bi, h)),
            pl.BlockSpec((tq, vw), lambda bi, h, qi: (bi * nq + qi, gate_blk0 // heads + h)),
        ],
        out_specs=pl.BlockSpec((tq, vw), lambda bi, h, qi: (bi * nq + qi, h)),
        compiler_params=_params(("arbitrary", "arbitrary", "arbitrary")),
        name="attn_a",
    )(q_a, k_a, v_a, proj)


def _t5_bucket(dist):
    max_exact = N_BUCKETS // 2
    n = jnp.maximum(dist, 0)
    nf = jnp.maximum(n, 1).astype(F32)
    large = max_exact + (jnp.log(nf / max_exact) / math.log(MAX_DISTANCE / max_exact)
                         * (N_BUCKETS - max_exact)).astype(I32)
    large = jnp.minimum(large, N_BUCKETS - 1)
    return jnp.where(n < max_exact, n, large)


def _attn_b_kernel(t5_ref, qidx_ref, kidx_ref, wt_ref, qb_ref, kb_ref, vt_ref, gate_ref, gq_ref, gk_ref,
                   o_ref, sc_ref, qn_ref, acc_ref, m_ref, l_ref, bias_ref, thr_ref, *, tq, nq, topk, max_iters):
    bi = pl.program_id(0)
    qi = pl.program_id(1)
    ck = tq
    shape = (ck, tq)

    @pl.when((bi == 0) & (qi == 0))
    def _():
        s_loc = lax.broadcasted_iota(I32, shape, 0)
        t_loc = lax.broadcasted_iota(I32, shape, 1)
        for near in range(2):
            bucket = _t5_bucket(t_loc - s_loc + (1 - near) * ck)

            def per_head(h, carry, bucket=bucket, near=near):
                far = t5_ref[N_BUCKETS - 1, h]
                tab = jnp.zeros(shape, F32)
                for bk in range(N_BUCKETS - 1):
                    tab = jnp.where(bucket == bk, (t5_ref[bk, h] - far) * LOG2E, tab)
                bias_ref[h, near] = tab
                return carry

            lax.fori_loop(0, H_B, per_head, 0)

    w_all = wt_ref[...] * (IDX_HEADS ** -0.5)

    def score_chunk(c, diag):
        rows = pl.ds(pl.multiple_of(c * ck, ck), ck)
        ka = kidx_ref[rows, 0:LANE]
        kb = kidx_ref[rows, LANE:2 * LANE]
        score = jnp.zeros(shape, F32)
        for j in range(IDX_HEADS // 2):
            qp = qidx_ref[:, j * LANE:(j + 1) * LANE]
            score = score + jnp.maximum(_nt_dot(ka, qp), 0.0) * w_all[2 * j:2 * j + 1, :]
            score = score + jnp.maximum(_nt_dot(kb, qp), 0.0) * w_all[2 * j + 1:2 * j + 2, :]
        if diag:
            adm = lax.broadcasted_iota(I32, shape, 0) <= lax.broadcasted_iota(I32, shape, 1)
            lo_src = jnp.where(adm, score, POS_INF)
            score = jnp.where(adm, score, NEG_INF)
        else:
            lo_src = score
        sc_ref[c] = score
        return jnp.max(score, axis=0, keepdims=True), jnp.min(lo_src, axis=0, keepdims=True)

    def score_body(c, carry):
        mx, mn = carry
        cmx, cmn = score_chunk(c, False)
        return jnp.maximum(mx, cmx), jnp.minimum(mn, cmn)

    mx, mn = lax.fori_loop(0, qi, score_body,
                           (jnp.full((1, tq), NEG_INF, F32), jnp.full((1, tq), POS_INF, F32)))
    dmx, dmn = score_chunk(qi, True)
    mx = jnp.maximum(mx, dmx)
    mn = jnp.minimum(mn, dmn)

    rep = (8, tq)
    n_adm = qi * tq + lax.broadcasted_iota(I32, rep, 1) + 1
    kp = jnp.minimum(n_adm, topk)
    mx8 = jnp.broadcast_to(mx, rep)
    mn8 = jnp.broadcast_to(mn, rep)

    def bisect(nchunks):
        def count_ge(x):
            acc = jnp.zeros(rep, I32)
            for c in range(nchunks):
                ge = sc_ref[c].reshape(ck // 8, 8, tq) >= x[None]
                acc = acc + jnp.sum(ge.astype(I32), axis=0)
            for shift in (4, 2, 1):
                acc = acc + pltpu.roll(acc, shift, 0)
            return acc

        def bis_cond(st):
            it, lo, hi, mid, cnt_lo = st
            active = (cnt_lo != kp) & (mid > lo) & (mid < hi)
            return jnp.logical_and(it < max_iters, jnp.max(active.astype(I32)) > 0)

        def bis_body(st):
            it, lo, hi, mid, cnt_lo = st
            for _ in range(BISECT_STEPS_PER_CHECK):
                cnt = count_ge(mid)
                ge = cnt >= kp
                lo = jnp.where(ge, mid, lo)
                cnt_lo = jnp.where(ge, cnt, cnt_lo)
                hi = jnp.where(ge, hi, mid)
                mid = jnp.where(hi == POS_INF, mx8, lo + 0.5 * (hi - lo))
            return it + 1, lo, hi, mid, cnt_lo

        _, lo, hi, _, cnt_lo = lax.while_loop(
            bis_cond, bis_body, (jnp.int32(0), mn8, jnp.full(rep, POS_INF, F32), mx8, n_adm))
        thr_ref[...] = lo

        tied = cnt_lo > kp

        @pl.when(jnp.max(tied.astype(I32)) > 0)
        def _():
            n_keys = nchunks * ck
            sub = lax.broadcasted_iota(I32, (ck // 8, 8, tq), 0) * 8 + lax.broadcasted_iota(I32, (ck // 8, 8, tq), 1)

            def count_kept(j_last):
                acc = jnp.zeros(rep, I32)
                for c in range(nchunks):
                    s3 = sc_ref[c].reshape(ck // 8, 8, tq)
                    keep = (s3 >= hi[None]) | ((s3 >= lo[None]) & (sub + c * ck <= j_last[None]))
                    acc = acc + jnp.sum(keep.astype(I32), axis=0)
                for shift in (4, 2, 1):
                    acc = acc + pltpu.roll(acc, shift, 0)
                return acc

            def idx_step(_, carry):
                j_lo, j_hi = carry
                j_mid = j_lo + ((j_hi - j_lo) >> 1)
                ok = count_kept(j_mid) >= kp
                return jnp.where(ok, j_lo, j_mid), jnp.where(ok, j_mid, j_hi)

            _, j_hi = lax.fori_loop(0, max(1, (n_keys - 1).bit_length()), idx_step,
                                    (jnp.full(rep, -1, I32), jnp.full(rep, n_keys - 1, I32)))
            j_last = jnp.where(tied, j_hi, n_keys - 1)
            for c in range(nchunks):
                s3 = sc_ref[c].reshape(ck // 8, 8, tq)
                drop = (s3 >= lo[None]) & (s3 < hi[None]) & (sub + c * ck > j_last[None])
                sc_ref[c] = jnp.where(drop, NEG_INF, s3).reshape(ck, tq)

    for qv in range(nq):
        pl.when(qi == qv)(functools.partial(bisect, qv + 1))
    thr = thr_ref[0:1, :]

    for h in range(H_B):
        qh = qb_ref[:, h * LANE:(h + 1) * LANE].astype(F32)
        ms = jnp.mean(qh * qh, axis=-1, keepdims=True)
        qn_ref[h * tq:(h + 1) * tq, :] = (qh * lax.rsqrt(ms + EPS) * gq_ref[...]).astype(qn_ref.dtype)
    m_ref[...] = jnp.full(m_ref.shape, NEG_INF, F32)
    l_ref[...] = jnp.zeros(l_ref.shape, F32)
    acc_ref[...] = jnp.zeros(acc_ref.shape, F32)

    def attend_chunk(c, near):
        rows = pl.ds(pl.multiple_of(c * ck, ck), ck)
        kc = kb_ref[rows, :].astype(F32)
        ms = jnp.mean(kc * kc, axis=-1, keepdims=True)
        kc = (kc * lax.rsqrt(ms + EPS) * gk_ref[...]).astype(MXU_DTYPE)
        vt = vt_ref[c]
        sel = sc_ref[c] >= thr
        s_all = _nt_dot(kc, qn_ref[...])
        for h in range(H_B):
            s = s_all[:, h * tq:(h + 1) * tq]
            if near is not None:
                s = s + bias_ref[h, near]
            s = jnp.where(sel, s, NEG_INF)
            m_old = m_ref[h]
            m_new = jnp.maximum(m_old, jnp.max(s, axis=0, keepdims=True))
            m_safe = jnp.where(m_new == NEG_INF, 0.0, m_new)
            p = jnp.exp2(s - m_safe)
            alpha = jnp.exp2(m_old - m_safe)
            l_ref[h] = alpha * l_ref[h] + jnp.sum(p, axis=0, keepdims=True)
            acc_ref[h] = alpha * acc_ref[h] + jnp.dot(vt, p.astype(MXU_DTYPE), preferred_element_type=F32)
            m_ref[h] = m_new

    def far_body(c, carry):
        attend_chunk(c, None)
        return carry

    lax.fori_loop(0, jnp.maximum(qi - 1, 0), far_body, 0)

    @pl.when(qi >= 1)
    def _():
        attend_chunk(qi - 1, 0)

    attend_chunk(qi, 1)

    for h in range(H_B):
        o_t = acc_ref[h] * (1.0 / l_ref[h])
        g = gate_ref[:, h * LANE:(h + 1) * LANE].astype(F32)
        o_ref[:, h * LANE:(h + 1) * LANE] = (o_t.T * _silu(g)).astype(o_ref.dtype)


def _attn_b(proj, wt, vt, t5_bias, gq_b, gk_b, blk, *, b, l, tq):
    t = proj.shape[0]
    nq = l // tq
    topk = min(TOPK_MAX, l // 4)
    row = lambda bi, qi: bi * nq + qi
    return pl.pallas_call(
        functools.partial(_attn_b_kernel, tq=tq, nq=nq, topk=topk, max_iters=BISECT_MAX_CHECKS),
        out_shape=jax.ShapeDtypeStruct((t, WIDTH_B), MXU_DTYPE),
        grid=(b, nq),
        in_specs=[
            pl.BlockSpec(memory_space=pltpu.SMEM),
            pl.BlockSpec((tq, IDX_HEADS * IDX_DIM), lambda bi, qi: (row(bi, qi), blk["q_idx"])),
            pl.BlockSpec((l, 2 * LANE), lambda bi, qi: (bi, blk["k_idx"])),
            pl.BlockSpec((IDX_HEADS, tq), lambda bi, qi: (0, row(bi, qi))),
            pl.BlockSpec((tq, WIDTH_B), lambda bi, qi: (row(bi, qi), blk["q_b"])),
            pl.BlockSpec((l, LANE), lambda bi, qi: (bi, blk["k_b"])),
            pl.BlockSpec((None, nq, LANE, tq), lambda bi, qi: (bi, 0, 0, 0)),
            pl.BlockSpec((tq, WIDTH_B), lambda bi, qi: (row(bi, qi), blk["gate_b"])),
            pl.BlockSpec((1, LANE), lambda bi, qi: (0, 0)),
            pl.BlockSpec((1, LANE), lambda bi, qi: (0, 0)),
        ],
        out_specs=pl.BlockSpec((tq, WIDTH_B), lambda bi, qi: (row(bi, qi), 0)),
        scratch_shapes=[
            pltpu.VMEM((nq, tq, tq), F32),
            pltpu.VMEM((H_B * tq, LANE), MXU_DTYPE),
            pltpu.VMEM((H_B, LANE, tq), F32),
            pltpu.VMEM((H_B, 1, tq), F32),
            pltpu.VMEM((H_B, 1, tq), F32),
            pltpu.VMEM((H_B, 2, tq, tq), F32),
            pltpu.VMEM((8, tq), F32),
        ],
        compiler_params=_params(("arbitrary", "arbitrary")),
        name="attn_b",
    )(t5_bias, proj, proj, wt, proj, proj, vt, proj, gq_b, gk_b)


def _sigmoid(z):
    return 1.0 / (1.0 + jnp.exp(-z))


def _merge_kernel(oa_ref, ob_ref, pa_ref, pb_ref, ma_ref, mb_ref, o_ref):
    a = jnp.dot(oa_ref[...], pa_ref[...], preferred_element_type=F32)
    bb = jnp.dot(ob_ref[...], pb_ref[...], preferred_element_type=F32)
    o_ref[...] = (_sigmoid(ma_ref[...].astype(F32)) * a + _sigmoid(mb_ref[...].astype(F32)) * bb).astype(o_ref.dtype)


def _merge(o_a, o_b, p_a, p_b, proj, ma_off, mb_off, *, tm, tn):
    t = o_a.shape[0]
    d = p_a.shape[1]
    ma0, mb0 = ma_off // tn, mb_off // tn
    return pl.pallas_call(
        _merge_kernel,
        out_shape=jax.ShapeDtypeStruct((t, d), MXU_DTYPE),
        grid=(t // tm, d // tn),
        in_specs=[
            pl.BlockSpec((tm, WIDTH_A), lambda i, j: (i, 0)),
            pl.BlockSpec((tm, WIDTH_B), lambda i, j: (i, 0)),
            pl.BlockSpec((WIDTH_A, tn), lambda i, j: (0, j)),
            pl.BlockSpec((WIDTH_B, tn), lambda i, j: (0, j)),
            pl.BlockSpec((tm, tn), lambda i, j: (i, ma0 + j)),
            pl.BlockSpec((tm, tn), lambda i, j: (i, mb0 + j)),
        ],
        out_specs=pl.BlockSpec((tm, tn), lambda i, j: (i, j)),
        compiler_params=_params(("arbitrary", "arbitrary")),
        name="merge",
    )(o_a, o_b, p_a, p_b, proj, proj)


def _out_proj_kernel(m_ref, w_ref, x_ref, o_ref):
    o_ref[...] = x_ref[...] + jnp.dot(m_ref[...], w_ref[...], preferred_element_type=F32)


def _out_proj(merged, w_o, x2, *, tm, tn):
    t, d = x2.shape
    return pl.pallas_call(
        _out_proj_kernel,
        out_shape=jax.ShapeDtypeStruct((t, d), x2.dtype),
        grid=(t // tm, d // tn),
        in_specs=[
            pl.BlockSpec((tm, d), lambda i, j: (i, 0)),
            pl.BlockSpec((d, tn), lambda i, j: (0, j)),
            pl.BlockSpec((tm, tn), lambda i, j: (i, j)),
        ],
        out_specs=pl.BlockSpec((tm, tn), lambda i, j: (i, j)),
        compiler_params=_params(("arbitrary", "arbitrary")),
        name="out_proj",
    )(merged, w_o, x2)


def _rope_pad(a, axis):
    a1, a2 = jnp.split(a, 2, axis=axis)
    z = jnp.zeros_like(a1)
    return jnp.concatenate([a1, z, a2, z], axis=axis)


def _layout(d):
    names = [("merge_a", d), ("merge_b", d), ("gate_a", WIDTH_A), ("gate_b", WIDTH_B),
             ("q_b", WIDTH_B), ("q_idx", IDX_HEADS * IDX_DIM), ("cq", Q_LORA), ("ckv", KV_LORA),
             ("k_idx", 2 * LANE), ("k_rope", LANE), ("k_b", LANE), ("v_b", LANE), ("w_idx", LANE)]
    off, out = 0, {}
    for name, width in names:
        assert off % width == 0, (name, off, width)
        out[name] = off
        off += width
    return out, off


def kernel(x, positions, g_pre, w_in, g_q_lat, g_kv_lat, w_uq, w_ukv, g_qn_a, g_kn_a,
           g_qn_b, g_kn_b, t5_bias, p_a, p_b, w_o):
    b, l, d = x.shape
    t = b * l
    tq = 256
    tn_in = 512
    off, n_used = _layout(d)
    n_pad = -(-n_used // tn_in) * tn_in

    names = ["cq", "ckv", "k_rope", "q_b", "k_b", "v_b", "q_idx", "k_idx", "w_idx", "gate_a", "gate_b",
             "merge_a", "merge_b"]
    sizes = [Q_LORA, KV_LORA, QK_ROPE, WIDTH_B, HEAD_DIM_B, HEAD_DIM_B, IDX_HEADS * IDX_DIM, IDX_DIM,
             IDX_HEADS, WIDTH_A, WIDTH_B, d, d]
    src, acc = {}, 0
    for name, s in zip(names, sizes):
        src[name] = (acc, s)
        acc += s
    main_groups = ["merge_a", "merge_b", "gate_a", "gate_b", "q_b", "q_idx", "cq", "ckv"]
    src_starts, dst = [], 0
    for name in main_groups:
        assert off[name] == dst and src[name][1] % tn_in == 0
        src_starts += [src[name][0] + c for c in range(0, src[name][1], tn_in)]
        dst += src[name][1]
    wt = w_in.T
    rows = lambda name: wt[src[name][0]:src[name][0] + src[name][1]]
    z = lambda n: jnp.zeros((n, d), w_in.dtype)
    wt_ki = rows("k_idx")
    assert off["k_idx"] == dst
    wt_tail = jnp.concatenate(
        [wt_ki, z(LANE - IDX_DIM), z(LANE - IDX_DIM), wt_ki, _rope_pad(rows("k_rope"), 0), rows("k_b"),
         rows("v_b"), rows("w_idx"), z(LANE - IDX_HEADS), z(n_pad - n_used)], axis=0).astype(MXU_DTYPE)
    w_pad = _w_relayout(wt, src_starts, wt_tail, tn=tn_in, tc=d // 2)

    w_uq3 = w_uq.reshape(Q_LORA, H_A, QK_DIM_A)
    w_uq_pad = jnp.concatenate([w_uq3[:, :, :QK_NOPE], _rope_pad(w_uq3[:, :, QK_NOPE:], 2)], axis=2)
    w_uq_pad = w_uq_pad.reshape(Q_LORA, H_A * HEAD_PAD_A).astype(MXU_DTYPE)
    gq_head = jnp.concatenate([g_qn_a[:QK_NOPE], _rope_pad(g_qn_a[QK_NOPE:], 0)]) * (QK_DIM_A ** -0.5 * LOG2E)
    gq_pad = jnp.tile(gq_head, H_A).reshape(1, H_A * HEAD_PAD_A).astype(F32)
    w_ukv3 = w_ukv.reshape(KV_LORA, H_A, QK_NOPE + V_DIM_A)
    w_uk = w_ukv3[:, :, :QK_NOPE].reshape(KV_LORA, H_A * QK_NOPE).astype(MXU_DTYPE)
    w_uv = w_ukv3[:, :, QK_NOPE:].reshape(KV_LORA, WIDTH_A).astype(MXU_DTYPE)
    gk_nope = g_kn_a[:QK_NOPE].reshape(1, LANE).astype(F32)
    gk_rope = _rope_pad(g_kn_a[QK_NOPE:], 0).reshape(1, LANE).astype(F32)
    gq_b = (g_qn_b * (HEAD_DIM_B ** -0.5 * LOG2E)).reshape(1, LANE).astype(F32)
    gk_b = g_kn_b.reshape(1, LANE).astype(F32)

    inv = ROPE_THETA ** (-jnp.arange(HALF_ROPE, dtype=F32) / HALF_ROPE)
    ang = positions.reshape(t, 1).astype(F32) * inv
    cos, sin = jnp.cos(ang), jnp.sin(ang)
    zr = jnp.zeros_like(cos)
    cos_t = jnp.concatenate([cos, zr, cos, zr], axis=1)
    sin_t = jnp.concatenate([-sin, zr, sin, zr], axis=1)

    x2 = x.reshape(t, d)
    proj = _in_proj(x2, g_pre, w_pad, tm=1024, tn=tn_in)

    q_a = _qa_proj(proj, off["cq"] // Q_LORA, g_q_lat, w_uq_pad, gq_pad, cos_t, sin_t, tm=256)
    k_a, v_a = _kva_proj(proj, off["ckv"] // KV_LORA, off["k_rope"] // LANE, g_kv_lat, w_uk, w_uv,
                         gk_nope, gk_rope, cos_t, sin_t, tm=256)
    o_a = _attn_a(q_a, k_a, v_a, proj, off["gate_a"] // LANE, b=b, l=l, tq=tq, heads=4)

    nq = l // tq
    w_idx = proj[:, off["w_idx"]:off["w_idx"] + IDX_HEADS]
    wt = w_idx.astype(F32).T
    v_b = proj[:, off["v_b"]:off["v_b"] + HEAD_DIM_B]
    vt = v_b.reshape(b, nq, tq, HEAD_DIM_B).transpose(0, 1, 3, 2)
    blk = {"q_idx": off["q_idx"] // (IDX_HEADS * IDX_DIM), "k_idx": off["k_idx"] // (2 * LANE),
           "q_b": off["q_b"] // WIDTH_B, "k_b": off["k_b"] // LANE, "gate_b": off["gate_b"] // WIDTH_B}
    o_b = _attn_b(proj, wt, vt, t5_bias.astype(F32), gq_b, gk_b, blk, b=b, l=l, tq=tq)

    merged = _merge(o_a, o_b, p_a.astype(MXU_DTYPE), p_b.astype(MXU_DTYPE), proj,
                    off["merge_a"], off["merge_b"], tm=1024, tn=1024)
    out = _out_proj(merged, w_o.astype(MXU_DTYPE), x2, tm=1024, tn=1024)
    return out.reshape(b, l, d)
```

```python
import functools
import math

import jax
import jax.numpy as jnp
from jax import lax
from jax.experimental import pallas as pl
from jax.experimental.pallas import tpu as pltpu

F32 = jnp.float32
I32 = jnp.int32
MXU_DTYPE = jnp.bfloat16

H_A = 16
QK_NOPE = 128
QK_ROPE = 64
QK_DIM_A = QK_NOPE + QK_ROPE
V_DIM_A = 128
Q_LORA = 1024
KV_LORA = 512
ROPE_THETA = 10000.0
H_B = 16
HEAD_DIM_B = 128
IDX_HEADS = 32
IDX_DIM = 64
TOPK_MAX = 256
N_BUCKETS = 32
MAX_DISTANCE = 128
EPS = 1e-6
WIDTH_A = H_A * V_DIM_A
WIDTH_B = H_B * HEAD_DIM_B

LANE = 128
ROW_ALIGN = 32
HALF_ROPE = QK_ROPE // 2
HEAD_PAD_A = 2 * LANE
VMEM_LIMIT = 56 * 1024 * 1024

BISECT_STEPS_PER_CHECK = 4
BISECT_MAX_CHECKS = 40
LOG2E = math.log2(math.e)
NEG_INF = float("-inf")
POS_INF = float("inf")


def _nt_dot(a, b):
    return lax.dot_general(a, b, (((1,), (1,)), ((), ())), preferred_element_type=F32)


def _params(sem, vmem=VMEM_LIMIT):
    return pltpu.CompilerParams(dimension_semantics=sem, vmem_limit_bytes=vmem)


def _w_relayout_kernel(start_ref, src_ref, tail_ref, o_ref, *, n_main):
    j = pl.program_id(0)

    @pl.when(j < n_main)
    def _():
        o_ref[...] = src_ref[...].astype(o_ref.dtype)

    @pl.when(j >= n_main)
    def _():
        o_ref[...] = tail_ref[...].astype(o_ref.dtype)


def _w_relayout(wt, src_starts, wt_tail, *, tn, tc):
    d = wt.shape[1]
    n_main, n_tail = len(src_starts), wt_tail.shape[0] // tn
    assert all(s % ROW_ALIGN == 0 for s in src_starts)
    starts = jnp.array([s // ROW_ALIGN for s in src_starts] + [0] * n_tail, I32)
    return pl.pallas_call(
        functools.partial(_w_relayout_kernel, n_main=n_main),
        out_shape=jax.ShapeDtypeStruct(((n_main + n_tail) * tn, d), MXU_DTYPE),
        grid_spec=pltpu.PrefetchScalarGridSpec(
            num_scalar_prefetch=1,
            grid=(n_main + n_tail, d // tc),
            in_specs=[
                pl.BlockSpec((pl.Element(tn), pl.Element(tc)), lambda j, c, st: (st[j] * ROW_ALIGN, c * tc)),
                pl.BlockSpec((tn, tc), lambda j, c, st: (jnp.maximum(j - n_main, 0), c)),
            ],
            out_specs=pl.BlockSpec((tn, tc), lambda j, c, st: (j, c)),
        ),
        compiler_params=_params(("arbitrary", "arbitrary")),
        name="w_relayout",
    )(starts, wt, wt_tail)


def _in_proj_kernel(x_ref, g_ref, w_ref, o_ref, hn_ref, *, row_chunk):
    tm = x_ref.shape[0]

    @pl.when(pl.program_id(1) == 0)
    def _():
        def body(r, carry):
            sl = pl.ds(pl.multiple_of(r * row_chunk, row_chunk), row_chunk)
            xx = x_ref[sl, :]
            ms = jnp.mean(xx * xx, axis=-1, keepdims=True)
            hn_ref[sl, :] = (xx * lax.rsqrt(ms + EPS) * g_ref[...]).astype(hn_ref.dtype)
            return carry

        lax.fori_loop(0, tm // row_chunk, body, 0)

    o_ref[...] = _nt_dot(hn_ref[...], w_ref[...]).astype(o_ref.dtype)


def _in_proj(x2, g_pre, wt_pad, *, tm, tn):
    t, d = x2.shape
    n = wt_pad.shape[0]
    return pl.pallas_call(
        functools.partial(_in_proj_kernel, row_chunk=64),
        out_shape=jax.ShapeDtypeStruct((t, n), MXU_DTYPE),
        grid=(t // tm, n // tn),
        in_specs=[
            pl.BlockSpec((tm, d), lambda i, j: (i, 0)),
            pl.BlockSpec((1, d), lambda i, j: (0, 0)),
            pl.BlockSpec((tn, d), lambda i, j: (j, 0)),
        ],
        out_specs=pl.BlockSpec((tm, tn), lambda i, j: (i, j)),
        scratch_shapes=[pltpu.VMEM((tm, d), MXU_DTYPE)],
        compiler_params=_params(("arbitrary", "arbitrary")),
        name="in_proj",
    )(x2, g_pre.reshape(1, d), wt_pad)


def _rope_lanes(r, cos_ref, sin_ref):
    return r * cos_ref[...] + pltpu.roll(r, 2 * HALF_ROPE, 1) * sin_ref[...]


def _qa_proj_kernel(cq_ref, gl_ref, w_ref, gq_ref, cos_ref, sin_ref, o_ref):
    c = cq_ref[...].astype(F32)
    ms = jnp.mean(c * c, axis=-1, keepdims=True)
    cn = (c * lax.rsqrt(ms + EPS) * gl_ref[...]).astype(MXU_DTYPE)
    q = jnp.dot(cn, w_ref[...], preferred_element_type=F32)
    for h in range(H_A):
        lo = h * HEAD_PAD_A
        qh = q[:, lo:lo + HEAD_PAD_A]
        ss = jnp.sum(qh * qh, axis=-1, keepdims=True) * (1.0 / QK_DIM_A)
        qn = qh * lax.rsqrt(ss + EPS) * gq_ref[:, lo:lo + HEAD_PAD_A]
        o_ref[:, lo:lo + LANE] = qn[:, :LANE].astype(o_ref.dtype)
        o_ref[:, lo + LANE:lo + HEAD_PAD_A] = _rope_lanes(qn[:, LANE:], cos_ref, sin_ref).astype(o_ref.dtype)


def _qa_proj(proj, cq_blk, g_q_lat, w_uq_pad, gq_pad, cos_t, sin_t, *, tm):
    t = proj.shape[0]
    nq = H_A * HEAD_PAD_A
    return pl.pallas_call(
        _qa_proj_kernel,
        out_shape=jax.ShapeDtypeStruct((t, nq), MXU_DTYPE),
        grid=(t // tm,),
        in_specs=[
            pl.BlockSpec((tm, Q_LORA), lambda i: (i, cq_blk)),
            pl.BlockSpec((1, Q_LORA), lambda i: (0, 0)),
            pl.BlockSpec((Q_LORA, nq), lambda i: (0, 0)),
            pl.BlockSpec((1, nq), lambda i: (0, 0)),
            pl.BlockSpec((tm, LANE), lambda i: (i, 0)),
            pl.BlockSpec((tm, LANE), lambda i: (i, 0)),
        ],
        out_specs=pl.BlockSpec((tm, nq), lambda i: (i, 0)),
        compiler_params=_params(("arbitrary",)),
        name="qa_proj",
    )(proj, g_q_lat.reshape(1, Q_LORA), w_uq_pad, gq_pad, cos_t, sin_t)


def _kva_proj_kernel(ckv_ref, kr_ref, gl_ref, wk_ref, wv_ref, gkn_ref, gkr_ref, cos_ref, sin_ref,
                     k_ref, v_ref):
    c = ckv_ref[...].astype(F32)
    ms = jnp.mean(c * c, axis=-1, keepdims=True)
    cn = (c * lax.rsqrt(ms + EPS) * gl_ref[...]).astype(MXU_DTYPE)
    kn = jnp.dot(cn, wk_ref[...], preferred_element_type=F32)
    v_ref[...] = jnp.dot(cn, wv_ref[...], preferred_element_type=F32).astype(v_ref.dtype)
    kr = kr_ref[...].astype(F32)
    ss_r = jnp.sum(kr * kr, axis=-1, keepdims=True)
    krr = _rope_lanes(kr * gkr_ref[...], cos_ref, sin_ref)
    for h in range(H_A):
        kh = kn[:, h * LANE:(h + 1) * LANE]
        ss = (jnp.sum(kh * kh, axis=-1, keepdims=True) + ss_r) * (1.0 / QK_DIM_A)
        rs = lax.rsqrt(ss + EPS)
        lo = h * HEAD_PAD_A
        k_ref[:, lo:lo + LANE] = (kh * rs * gkn_ref[...]).astype(k_ref.dtype)
        k_ref[:, lo + LANE:lo + HEAD_PAD_A] = (krr * rs).astype(k_ref.dtype)


def _kva_proj(proj, ckv_blk, krope_blk, g_kv_lat, w_uk, w_uv, gk_nope, gk_rope, cos_t, sin_t, *, tm):
    t = proj.shape[0]
    return pl.pallas_call(
        _kva_proj_kernel,
        out_shape=(jax.ShapeDtypeStruct((t, H_A * HEAD_PAD_A), MXU_DTYPE),
                   jax.ShapeDtypeStruct((t, WIDTH_A), MXU_DTYPE)),
        grid=(t // tm,),
        in_specs=[
            pl.BlockSpec((tm, KV_LORA), lambda i: (i, ckv_blk)),
            pl.BlockSpec((tm, LANE), lambda i: (i, krope_blk)),
            pl.BlockSpec((1, KV_LORA), lambda i: (0, 0)),
            pl.BlockSpec((KV_LORA, H_A * QK_NOPE), lambda i: (0, 0)),
            pl.BlockSpec((KV_LORA, WIDTH_A), lambda i: (0, 0)),
            pl.BlockSpec((1, LANE), lambda i: (0, 0)),
            pl.BlockSpec((1, LANE), lambda i: (0, 0)),
            pl.BlockSpec((tm, LANE), lambda i: (i, 0)),
            pl.BlockSpec((tm, LANE), lambda i: (i, 0)),
        ],
        out_specs=(pl.BlockSpec((tm, H_A * HEAD_PAD_A), lambda i: (i, 0)),
                   pl.BlockSpec((tm, WIDTH_A), lambda i: (i, 0))),
        compiler_params=_params(("arbitrary",)),
        name="kva_proj",
    )(proj, proj, g_kv_lat.reshape(1, KV_LORA), w_uk, w_uv, gk_nope, gk_rope, cos_t, sin_t)


def _silu(g):
    return g * (1.0 / (1.0 + jnp.exp(-g)))


def _lane_tile_reduce(x, op):
    acc = x[:, :LANE]
    for t in range(1, x.shape[1] // LANE):
        acc = op(acc, x[:, t * LANE:(t + 1) * LANE])
    return acc


def _attn_a_kernel(q_ref, k_ref, v_ref, gate_ref, o_ref, *, tq, nq, heads):
    qi = pl.program_id(2)

    def branch(qv):
        n_off = qv * tq
        for g in range(heads):
            kc = slice(g * HEAD_PAD_A, (g + 1) * HEAD_PAD_A)
            vc = slice(g * V_DIM_A, (g + 1) * V_DIM_A)
            q = q_ref[:, kc]
            s_diag = _nt_dot(q, k_ref[n_off:n_off + tq, kc])
            row = lax.broadcasted_iota(I32, s_diag.shape, 0)
            col = lax.broadcasted_iota(I32, s_diag.shape, 1)
            s_diag = jnp.where(row >= col, s_diag, NEG_INF)
            m_t = _lane_tile_reduce(s_diag, jnp.maximum)
            if qv > 0:
                s_off = _nt_dot(q, k_ref[0:n_off, kc])
                m_t = jnp.maximum(m_t, _lane_tile_reduce(s_off, jnp.maximum))
            m = jnp.max(m_t, axis=-1, keepdims=True)
            p_diag = jnp.exp2(s_diag - m)
            l_t = _lane_tile_reduce(p_diag, jnp.add)
            acc = jnp.dot(p_diag.astype(MXU_DTYPE), v_ref[n_off:n_off + tq, vc], preferred_element_type=F32)
            if qv > 0:
                p_off = jnp.exp2(s_off - m)
                l_t = l_t + _lane_tile_reduce(p_off, jnp.add)
                acc = acc + jnp.dot(p_off.astype(MXU_DTYPE), v_ref[0:n_off, vc], preferred_element_type=F32)
            l = jnp.sum(l_t, axis=-1, keepdims=True)
            o = acc * (1.0 / l)
            o_ref[:, vc] = (o * _silu(gate_ref[:, vc].astype(F32))).astype(o_ref.dtype)

    for qv in range(nq):
        pl.when(qi == qv)(functools.partial(branch, qv))


def _attn_a(q_a, k_a, v_a, proj, gate_blk0, *, b, l, tq, heads):
    t = q_a.shape[0]
    nq = l // tq
    kw, vw = heads * HEAD_PAD_A, heads * V_DIM_A
    return pl.pallas_call(
        functools.partial(_attn_a_kernel, tq=tq, nq=nq, heads=heads),
        out_shape=jax.ShapeDtypeStruct((t, WIDTH_A), MXU_DTYPE),
        grid=(b, H_A // heads, nq),
        in_specs=[
            pl.BlockSpec((tq, kw), lambda bi, h, qi: (bi * nq + qi, h)),
            pl.BlockSpec((l, kw), lambda bi, h, qi: (bi, h)),
            pl.BlockSpec((l, vw), lambda bi, h, qi: (bi, h)),
            pl.BlockSpec((tq, vw), lambda bi, h, qi: (bi * nq + qi, gate_blk0 // heads + h)),
        ],
        out_specs=pl.BlockSpec((tq, vw), lambda bi, h, qi: (bi * nq + qi, h)),
        compiler_params=_params(("arbitrary", "arbitrary", "arbitrary")),
        name="attn_a",
    )(q_a, k_a, v_a, proj)


def _t5_bucket(dist):
    max_exact = N_BUCKETS // 2
    n = jnp.maximum(dist, 0)
    nf = jnp.maximum(n, 1).astype(F32)
    large = max_exact + (jnp.log(nf / max_exact) / math.log(MAX_DISTANCE / max_exact)
                         * (N_BUCKETS - max_exact)).astype(I32)
    large = jnp.minimum(large, N_BUCKETS - 1)
    return jnp.where(n < max_exact, n, large)


def _attn_b_kernel(t5_ref, qidx_ref, kidx_ref, wt_ref, qb_ref, kb_ref, vt_ref, gate_ref, gq_ref, gk_ref,
                   o_ref, sc_ref, qn_ref, acc_ref, m_ref, l_ref, bias_ref, thr_ref, *, tq, nq, topk, max_iters):
    bi = pl.program_id(0)
    qi = pl.program_id(1)
    ck = tq
    shape = (ck, tq)

    @pl.when((bi == 0) & (qi == 0))
    def _():
        s_loc = lax.broadcasted_iota(I32, shape, 0)
        t_loc = lax.broadcasted_iota(I32, shape, 1)
        for near in range(2):
            bucket = _t5_bucket(t_loc - s_loc + (1 - near) * ck)

            def per_head(h, carry, bucket=bucket, near=near):
                far = t5_ref[N_BUCKETS - 1, h]
                tab = jnp.zeros(shape, F32)
                for bk in range(N_BUCKETS - 1):
                    tab = jnp.where(bucket == bk, (t5_ref[bk, h] - far) * LOG2E, tab)
                bias_ref[h, near] = tab
                return carry

            lax.fori_loop(0, H_B, per_head, 0)

    w_all = wt_ref[...] * (IDX_HEADS ** -0.5)

    def score_chunk(c, diag):
        rows = pl.ds(pl.multiple_of(c * ck, ck), ck)
        kx = kidx_ref[rows, 0:IDX_DIM]
        zk = jnp.zeros_like(kx)
        ka = jnp.concatenate([kx, zk], axis=1)
        kb = jnp.concatenate([zk, kx], axis=1)
        score = jnp.zeros(shape, F32)
        for j in range(IDX_HEADS // 2):
            qp = qidx_ref[:, j * LANE:(j + 1) * LANE]
            score = score + jnp.maximum(_nt_dot(ka, qp), 0.0) * w_all[2 * j:2 * j + 1, :]
            score = score + jnp.maximum(_nt_dot(kb, qp), 0.0) * w_all[2 * j + 1:2 * j + 2, :]
        if diag:
            adm = lax.broadcasted_iota(I32, shape, 0) <= lax.broadcasted_iota(I32, shape, 1)
            lo_src = jnp.where(adm, score, POS_INF)
            score = jnp.where(adm, score, NEG_INF)
        else:
            lo_src = score
        sc_ref[c] = score
        return jnp.max(score, axis=0, keepdims=True), jnp.min(lo_src, axis=0, keepdims=True)

    def score_body(c, carry):
        mx, mn = carry
        cmx, cmn = score_chunk(c, False)
        return jnp.maximum(mx, cmx), jnp.minimum(mn, cmn)

    mx, mn = lax.fori_loop(0, qi, score_body,
                           (jnp.full((1, tq), NEG_INF, F32), jnp.full((1, tq), POS_INF, F32)))
    dmx, dmn = score_chunk(qi, True)
    mx = jnp.maximum(mx, dmx)
    mn = jnp.minimum(mn, dmn)

    rep = (8, tq)
    n_adm = qi * tq + lax.broadcasted_iota(I32, rep, 1) + 1
    kp = jnp.minimum(n_adm, topk)
    mx8 = jnp.broadcast_to(mx, rep)
    mn8 = jnp.broadcast_to(mn, rep)

    def bisect(nchunks):
        def count_ge(x):
            acc = jnp.zeros(rep, I32)
            for c in range(nchunks):
                ge = sc_ref[c].reshape(ck // 8, 8, tq) >= x[None]
                acc = acc + jnp.sum(ge.astype(I32), axis=0)
            for shift in (4, 2, 1):
                acc = acc + pltpu.roll(acc, shift, 0)
            return acc

        def bis_cond(st):
            it, lo, hi, mid, cnt_lo = st
            active = (cnt_lo != kp) & (mid > lo) & (mid < hi)
            return jnp.logical_and(it < max_iters, jnp.max(active.astype(I32)) > 0)

        def bis_body(st):
            it, lo, hi, mid, cnt_lo = st
            for _ in range(BISECT_STEPS_PER_CHECK):
                cnt = count_ge(mid)
                ge = cnt >= kp
                lo = jnp.where(ge, mid, lo)
                cnt_lo = jnp.where(ge, cnt, cnt_lo)
                hi = jnp.where(ge, hi, mid)
                mid = jnp.where(hi == POS_INF, mx8, lo + 0.5 * (hi - lo))
            return it + 1, lo, hi, mid, cnt_lo

        _, lo, hi, _, cnt_lo = lax.while_loop(
            bis_cond, bis_body, (jnp.int32(0), mn8, jnp.full(rep, POS_INF, F32), mx8, n_adm))
        thr_ref[...] = lo

        tied = cnt_lo > kp

        @pl.when(jnp.max(tied.astype(I32)) > 0)
        def _():
            n_keys = nchunks * ck
            sub = lax.broadcasted_iota(I32, (ck // 8, 8, tq), 0) * 8 + lax.broadcasted_iota(I32, (ck // 8, 8, tq), 1)

            def count_kept(j_last):
                acc = jnp.zeros(rep, I32)
                for c in range(nchunks):
                    s3 = sc_ref[c].reshape(ck // 8, 8, tq)
                    keep = (s3 >= hi[None]) | ((s3 >= lo[None]) & (sub + c * ck <= j_last[None]))
                    acc = acc + jnp.sum(keep.astype(I32), axis=0)
                for shift in (4, 2, 1):
                    acc = acc + pltpu.roll(acc, shift, 0)
                return acc

            def idx_step(_, carry):
                j_lo, j_hi = carry
                j_mid = j_lo + ((j_hi - j_lo) >> 1)
                ok = count_kept(j_mid) >= kp
                return jnp.where(ok, j_lo, j_mid), jnp.where(ok, j_mid, j_hi)

            _, j_hi = lax.fori_loop(0, max(1, (n_keys - 1).bit_length()), idx_step,
                                    (jnp.full(rep, -1, I32), jnp.full(rep, n_keys - 1, I32)))
            j_last = jnp.where(tied, j_hi, n_keys - 1)
            for c in range(nchunks):
                s3 = sc_ref[c].reshape(ck // 8, 8, tq)
                drop = (s3 >= lo[None]) & (s3 < hi[None]) & (sub + c * ck > j_last[None])
                sc_ref[c] = jnp.where(drop, NEG_INF, s3).reshape(ck, tq)

    for qv in range(nq):
        pl.when(qi == qv)(functools.partial(bisect, qv + 1))
    thr = thr_ref[0:1, :]

    for h in range(H_B):
        qh = qb_ref[:, h * LANE:(h + 1) * LANE].astype(F32)
        ms = jnp.mean(qh * qh, axis=-1, keepdims=True)
        qn_ref[h * tq:(h + 1) * tq, :] = (qh * lax.rsqrt(ms + EPS) * gq_ref[...]).astype(qn_ref.dtype)
    m_ref[...] = jnp.full(m_ref.shape, NEG_INF, F32)
    l_ref[...] = jnp.zeros(l_ref.shape, F32)
    acc_ref[...] = jnp.zeros(acc_ref.shape, F32)

    def attend_chunk(c, near):
        rows = pl.ds(pl.multiple_of(c * ck, ck), ck)
        kc = kb_ref[rows, :].astype(F32)
        ms = jnp.mean(kc * kc, axis=-1, keepdims=True)
        kc = (kc * lax.rsqrt(ms + EPS) * gk_ref[...]).astype(MXU_DTYPE)
        vt = vt_ref[c]
        sel = sc_ref[c] >= thr
        s_all = _nt_dot(kc, qn_ref[...])
        for h in range(H_B):
            s = s_all[:, h * tq:(h + 1) * tq]
            if near is not None:
                s = s + bias_ref[h, near]
            s = jnp.where(sel, s, NEG_INF)
            m_old = m_ref[h]
            m_new = jnp.maximum(m_old, jnp.max(s, axis=0, keepdims=True))
            m_safe = jnp.where(m_new == NEG_INF, 0.0, m_new)
            p = jnp.exp2(s - m_safe)
            alpha = jnp.exp2(m_old - m_safe)
            l_ref[h] = alpha * l_ref[h] + jnp.sum(p, axis=0, keepdims=True)
            acc_ref[h] = alpha * acc_ref[h] + jnp.dot(vt, p.astype(MXU_DTYPE), preferred_element_type=F32)
            m_ref[h] = m_new

    def far_body(c, carry):
        attend_chunk(c, None)
        return carry

    lax.fori_loop(0, jnp.maximum(qi - 1, 0), far_body, 0)

    @pl.when(qi >= 1)
    def _():
        attend_chunk(qi - 1, 0)

    attend_chunk(qi, 1)

    for h in range(H_B):
        o_t = acc_ref[h] * (1.0 / l_ref[h])
        g = gate_ref[:, h * LANE:(h + 1) * LANE].astype(F32)
        o_ref[:, h * LANE:(h + 1) * LANE] = (o_t.T * _silu(g)).astype(o_ref.dtype)


def _attn_b(proj, wt, vt, t5_bias, gq_b, gk_b, blk, *, b, l, tq):
    t = proj.shape[0]
    nq = l // tq
    topk = min(TOPK_MAX, l // 4)
    row = lambda bi, qi: bi * nq + qi
    return pl.pallas_call(
        functools.partial(_attn_b_kernel, tq=tq, nq=nq, topk=topk, max_iters=BISECT_MAX_CHECKS),
        out_shape=jax.ShapeDtypeStruct((t, WIDTH_B), MXU_DTYPE),
        grid=(b, nq),
        in_specs=[
            pl.BlockSpec(memory_space=pltpu.SMEM),
            pl.BlockSpec((tq, IDX_HEADS * IDX_DIM), lambda bi, qi: (row(bi, qi), blk["q_idx"])),
            pl.BlockSpec((l, LANE), lambda bi, qi: (bi, blk["k_idx"])),
            pl.BlockSpec((IDX_HEADS, tq), lambda bi, qi: (0, row(bi, qi))),
            pl.BlockSpec((tq, WIDTH_B), lambda bi, qi: (row(bi, qi), blk["q_b"])),
            pl.BlockSpec((l, LANE), lambda bi, qi: (bi, blk["k_b"])),
            pl.BlockSpec((None, nq, LANE, tq), lambda bi, qi: (bi, 0, 0, 0)),
            pl.BlockSpec((tq, WIDTH_B), lambda bi, qi: (row(bi, qi), blk["gate_b"])),
            pl.BlockSpec((1, LANE), lambda bi, qi: (0, 0)),
            pl.BlockSpec((1, LANE), lambda bi, qi: (0, 0)),
        ],
        out_specs=pl.BlockSpec((tq, WIDTH_B), lambda bi, qi: (row(bi, qi), 0)),
        scratch_shapes=[
            pltpu.VMEM((nq, tq, tq), F32),
            pltpu.VMEM((H_B * tq, LANE), MXU_DTYPE),
            pltpu.VMEM((H_B, LANE, tq), F32),
            pltpu.VMEM((H_B, 1, tq), F32),
            pltpu.VMEM((H_B, 1, tq), F32),
            pltpu.VMEM((H_B, 2, tq, tq), F32),
            pltpu.VMEM((8, tq), F32),
        ],
        compiler_params=_params(("arbitrary", "arbitrary")),
        name="attn_b",
    )(t5_bias, proj, proj, wt, proj, proj, vt, proj, gq_b, gk_b)


def _sigmoid(z):
    return 1.0 / (1.0 + jnp.exp(-z))


def _merge_kernel(oa_ref, ob_ref, pa_ref, pb_ref, ma_ref, mb_ref, o_ref):
    a = jnp.dot(oa_ref[...], pa_ref[...], preferred_element_type=F32)
    bb = jnp.dot(ob_ref[...], pb_ref[...], preferred_element_type=F32)
    o_ref[...] = (_sigmoid(ma_ref[...].astype(F32)) * a + _sigmoid(mb_ref[...].astype(F32)) * bb).astype(o_ref.dtype)


def _merge(o_a, o_b, p_a, p_b, proj, ma_off, mb_off, *, tm, tn):
    t = o_a.shape[0]
    d = p_a.shape[1]
    ma0, mb0 = ma_off // tn, mb_off // tn
    return pl.pallas_call(
        _merge_kernel,
        out_shape=jax.ShapeDtypeStruct((t, d), MXU_DTYPE),
        grid=(t // tm, d // tn),
        in_specs=[
            pl.BlockSpec((tm, WIDTH_A), lambda i, j: (i, 0)),
            pl.BlockSpec((tm, WIDTH_B), lambda i, j: (i, 0)),
            pl.BlockSpec((WIDTH_A, tn), lambda i, j: (0, j)),
            pl.BlockSpec((WIDTH_B, tn), lambda i, j: (0, j)),
            pl.BlockSpec((tm, tn), lambda i, j: (i, ma0 + j)),
            pl.BlockSpec((tm, tn), lambda i, j: (i, mb0 + j)),
        ],
        out_specs=pl.BlockSpec((tm, tn), lambda i, j: (i, j)),
        compiler_params=_params(("arbitrary", "arbitrary")),
        name="merge",
    )(o_a, o_b, p_a, p_b, proj, proj)


def _out_proj_kernel(m_ref, w_ref, x_ref, o_ref):
    o_ref[...] = x_ref[...] + jnp.dot(m_ref[...], w_ref[...], preferred_element_type=F32)


def _out_proj(merged, w_o, x2, *, tm, tn):
    t, d = x2.shape
    return pl.pallas_call(
        _out_proj_kernel,
        out_shape=jax.ShapeDtypeStruct((t, d), x2.dtype),
        grid=(t // tm, d // tn),
        in_specs=[
            pl.BlockSpec((tm, d), lambda i, j: (i, 0)),
            pl.BlockSpec((d, tn), lambda i, j: (0, j)),
            pl.BlockSpec((tm, tn), lambda i, j: (i, j)),
        ],
        out_specs=pl.BlockSpec((tm, tn), lambda i, j: (i, j)),
        compiler_params=_params(("arbitrary", "arbitrary")),
        name="out_proj",
    )(merged, w_o, x2)


def _rope_pad(a, axis):
    a1, a2 = jnp.split(a, 2, axis=axis)
    z = jnp.zeros_like(a1)
    return jnp.concatenate([a1, z, a2, z], axis=axis)


def _layout(d):
    names = [("merge_a", d), ("merge_b", d), ("gate_a", WIDTH_A), ("gate_b", WIDTH_B),
             ("q_b", WIDTH_B), ("q_idx", IDX_HEADS * IDX_DIM), ("cq", Q_LORA), ("ckv", KV_LORA),
             ("k_idx", LANE), ("k_rope", LANE), ("k_b", LANE), ("v_b", LANE)]
    off, out = 0, {}
    for name, width in names:
        assert off % width == 0, (name, off, width)
        out[name] = off
        off += width
    out["w_idx"] = out["k_idx"] + IDX_DIM
    return out, off


def kernel(x, positions, g_pre, w_in, g_q_lat, g_kv_lat, w_uq, w_ukv, g_qn_a, g_kn_a,
           g_qn_b, g_kn_b, t5_bias, p_a, p_b, w_o):
    b, l, d = x.shape
    t = b * l
    tq = 256
    tn_in = 512
    off, n_used = _layout(d)
    n_pad = -(-n_used // tn_in) * tn_in

    names = ["cq", "ckv", "k_rope", "q_b", "k_b", "v_b", "q_idx", "k_idx", "w_idx", "gate_a", "gate_b",
             "merge_a", "merge_b"]
    sizes = [Q_LORA, KV_LORA, QK_ROPE, WIDTH_B, HEAD_DIM_B, HEAD_DIM_B, IDX_HEADS * IDX_DIM, IDX_DIM,
             IDX_HEADS, WIDTH_A, WIDTH_B, d, d]
    src, acc = {}, 0
    for name, s in zip(names, sizes):
        src[name] = (acc, s)
        acc += s
    main_groups = ["merge_a", "merge_b", "gate_a", "gate_b", "q_b", "q_idx", "cq", "ckv"]
    src_starts, dst = [], 0
    for name in main_groups:
        assert off[name] == dst and src[name][1] % tn_in == 0
        src_starts += [src[name][0] + c for c in range(0, src[name][1], tn_in)]
        dst += src[name][1]
    wt = w_in.T
    rows = lambda name: wt[src[name][0]:src[name][0] + src[name][1]]
    z = lambda n: jnp.zeros((n, d), w_in.dtype)
    assert off["k_idx"] == dst
    wt_tail = jnp.concatenate(
        [rows("k_idx"), rows("w_idx"), z(LANE - IDX_DIM - IDX_HEADS), _rope_pad(rows("k_rope"), 0),
         rows("k_b"), rows("v_b"), z(n_pad - n_used)], axis=0)
    w_pad = _w_relayout(wt, src_starts, wt_tail, tn=tn_in, tc=d // 2)

    w_uq3 = w_uq.reshape(Q_LORA, H_A, QK_DIM_A)
    w_uq_pad = jnp.concatenate([w_uq3[:, :, :QK_NOPE], _rope_pad(w_uq3[:, :, QK_NOPE:], 2)], axis=2)
    w_uq_pad = w_uq_pad.reshape(Q_LORA, H_A * HEAD_PAD_A).astype(MXU_DTYPE)
    gq_head = jnp.concatenate([g_qn_a[:QK_NOPE], _rope_pad(g_qn_a[QK_NOPE:], 0)]) * (QK_DIM_A ** -0.5 * LOG2E)
    gq_pad = jnp.tile(gq_head, H_A).reshape(1, H_A * HEAD_PAD_A).astype(F32)
    w_ukv3 = w_ukv.reshape(KV_LORA, H_A, QK_NOPE + V_DIM_A)
    w_uk = w_ukv3[:, :, :QK_NOPE].reshape(KV_LORA, H_A * QK_NOPE).astype(MXU_DTYPE)
    w_uv = w_ukv3[:, :, QK_NOPE:].reshape(KV_LORA, WIDTH_A).astype(MXU_DTYPE)
    gk_nope = g_kn_a[:QK_NOPE].reshape(1, LANE).astype(F32)
    gk_rope = _rope_pad(g_kn_a[QK_NOPE:], 0).reshape(1, LANE).astype(F32)
    gq_b = (g_qn_b * (HEAD_DIM_B ** -0.5 * LOG2E)).reshape(1, LANE).astype(F32)
    gk_b = g_kn_b.reshape(1, LANE).astype(F32)

    inv = ROPE_THETA ** (-jnp.arange(HALF_ROPE, dtype=F32) / HALF_ROPE)
    ang = positions.reshape(t, 1).astype(F32) * inv
    cos, sin = jnp.cos(ang), jnp.sin(ang)
    zr = jnp.zeros_like(cos)
    cos_t = jnp.concatenate([cos, zr, cos, zr], axis=1)
    sin_t = jnp.concatenate([-sin, zr, sin, zr], axis=1)

    x2 = x.reshape(t, d)
    proj = _in_proj(x2, g_pre, w_pad, tm=1024, tn=tn_in)

    q_a = _qa_proj(proj, off["cq"] // Q_LORA, g_q_lat, w_uq_pad, gq_pad, cos_t, sin_t, tm=256)
    k_a, v_a = _kva_proj(proj, off["ckv"] // KV_LORA, off["k_rope"] // LANE, g_kv_lat, w_uk, w_uv,
                         gk_nope, gk_rope, cos_t, sin_t, tm=256)
    o_a = _attn_a(q_a, k_a, v_a, proj, off["gate_a"] // LANE, b=b, l=l, tq=tq, heads=4)

    nq = l // tq
    w_idx = proj[:, off["w_idx"]:off["w_idx"] + IDX_HEADS]
    wt = w_idx.astype(F32).T
    v_b = proj[:, off["v_b"]:off["v_b"] + HEAD_DIM_B]
    vt = v_b.reshape(b, nq, tq, HEAD_DIM_B).transpose(0, 1, 3, 2)
    blk = {"q_idx": off["q_idx"] // (IDX_HEADS * IDX_DIM), "k_idx": off["k_idx"] // LANE,
           "q_b": off["q_b"] // WIDTH_B, "k_b": off["k_b"] // LANE, "gate_b": off["gate_b"] // WIDTH_B}
    o_b = _attn_b(proj, wt, vt, t5_bias.astype(F32), gq_b, gk_b, blk, b=b, l=l, tq=tq)

    merged = _merge(o_a, o_b, p_a.astype(MXU_DTYPE), p_b.astype(MXU_DTYPE), proj,
                    off["merge_a"], off["merge_b"], tm=1024, tn=1024)
    out = _out_proj(merged, w_o.astype(MXU_DTYPE), x2, tm=1024, tn=1024)
    return out.reshape(b, l, d)
```

```python
import functools
import math

import jax
import jax.numpy as jnp
from jax import lax
from jax.experimental import pallas as pl
from jax.experimental.pallas import tpu as pltpu

F32 = jnp.float32
I32 = jnp.int32
MXU_DTYPE = jnp.bfloat16

H_A = 16
QK_NOPE = 128
QK_ROPE = 64
QK_DIM_A = QK_NOPE + QK_ROPE
V_DIM_A = 128
Q_LORA = 1024
KV_LORA = 512
ROPE_THETA = 10000.0
H_B = 16
HEAD_DIM_B = 128
IDX_HEADS = 32
IDX_DIM = 64
TOPK_MAX = 256
N_BUCKETS = 32
MAX_DISTANCE = 128
EPS = 1e-6
WIDTH_A = H_A * V_DIM_A
WIDTH_B = H_B * HEAD_DIM_B

LANE = 128
ROW_ALIGN = 32
HALF_ROPE = QK_ROPE // 2
HEAD_PAD_A = 2 * LANE
SHIFT_LANE = LANE + HALF_ROPE
BOUND_MARGIN = 1.0 + 2.0 ** -10
MIN_SHIFTED_SUM = 2.0 ** -64
VMEM_LIMIT = 56 * 1024 * 1024

BISECT_STEPS_PER_CHECK = 4
BISECT_MAX_CHECKS = 40
LOG2E = math.log2(math.e)
NEG_INF = float("-inf")
POS_INF = float("inf")


def _nt_dot(a, b):
    return lax.dot_general(a, b, (((1,), (1,)), ((), ())), preferred_element_type=F32)


def _params(sem, vmem=VMEM_LIMIT):
    return pltpu.CompilerParams(dimension_semantics=sem, vmem_limit_bytes=vmem)


def _w_relayout_kernel(start_ref, src_ref, tail_ref, o_ref, *, n_main):
    j = pl.program_id(0)

    @pl.when(j < n_main)
    def _():
        o_ref[...] = src_ref[...].astype(o_ref.dtype)

    @pl.when(j >= n_main)
    def _():
        o_ref[...] = tail_ref[...].astype(o_ref.dtype)


def _w_relayout(wt, src_starts, wt_tail, *, tn, tc):
    d = wt.shape[1]
    n_main, n_tail = len(src_starts), wt_tail.shape[0] // tn
    assert all(s % ROW_ALIGN == 0 for s in src_starts)
    starts = jnp.array([s // ROW_ALIGN for s in src_starts] + [0] * n_tail, I32)
    return pl.pallas_call(
        functools.partial(_w_relayout_kernel, n_main=n_main),
        out_shape=jax.ShapeDtypeStruct(((n_main + n_tail) * tn, d), MXU_DTYPE),
        grid_spec=pltpu.PrefetchScalarGridSpec(
            num_scalar_prefetch=1,
            grid=(n_main + n_tail, d // tc),
            in_specs=[
                pl.BlockSpec((pl.Element(tn), pl.Element(tc)), lambda j, c, st: (st[j] * ROW_ALIGN, c * tc)),
                pl.BlockSpec((tn, tc), lambda j, c, st: (jnp.maximum(j - n_main, 0), c)),
            ],
            out_specs=pl.BlockSpec((tn, tc), lambda j, c, st: (j, c)),
        ),
        compiler_params=_params(("arbitrary", "arbitrary")),
        name="w_relayout",
    )(starts, wt, wt_tail)


def _in_proj_kernel(x_ref, g_ref, w_ref, o_ref, hn_ref, *, row_chunk):
    tm = x_ref.shape[0]

    @pl.when(pl.program_id(1) == 0)
    def _():
        def body(r, carry):
            sl = pl.ds(pl.multiple_of(r * row_chunk, row_chunk), row_chunk)
            xx = x_ref[sl, :]
            ms = jnp.mean(xx * xx, axis=-1, keepdims=True)
            hn_ref[sl, :] = (xx * lax.rsqrt(ms + EPS) * g_ref[...]).astype(hn_ref.dtype)
            return carry

        lax.fori_loop(0, tm // row_chunk, body, 0)

    o_ref[...] = _nt_dot(hn_ref[...], w_ref[...]).astype(o_ref.dtype)


def _in_proj(x2, g_pre, wt_pad, *, tm, tn):
    t, d = x2.shape
    n = wt_pad.shape[0]
    return pl.pallas_call(
        functools.partial(_in_proj_kernel, row_chunk=64),
        out_shape=jax.ShapeDtypeStruct((t, n), MXU_DTYPE),
        grid=(t // tm, n // tn),
        in_specs=[
            pl.BlockSpec((tm, d), lambda i, j: (i, 0)),
            pl.BlockSpec((1, d), lambda i, j: (0, 0)),
            pl.BlockSpec((tn, d), lambda i, j: (j, 0)),
        ],
        out_specs=pl.BlockSpec((tm, tn), lambda i, j: (i, j)),
        scratch_shapes=[pltpu.VMEM((tm, d), MXU_DTYPE)],
        compiler_params=_params(("arbitrary", "arbitrary")),
        name="in_proj",
    )(x2, g_pre.reshape(1, d), wt_pad)


def _rope_lanes(r, cos_ref, sin_ref):
    return r * cos_ref[...] + pltpu.roll(r, 2 * HALF_ROPE, 1) * sin_ref[...]


def _qa_proj_kernel(cq_ref, gl_ref, w_ref, gq_ref, cos_ref, sin_ref, o_ref):
    c = cq_ref[...].astype(F32)
    ms = jnp.mean(c * c, axis=-1, keepdims=True)
    cn = (c * lax.rsqrt(ms + EPS) * gl_ref[...]).astype(MXU_DTYPE)
    q = jnp.dot(cn, w_ref[...], preferred_element_type=F32)
    for h in range(H_A):
        lo = h * HEAD_PAD_A
        qh = q[:, lo:lo + HEAD_PAD_A]
        ss = jnp.sum(qh * qh, axis=-1, keepdims=True) * (1.0 / QK_DIM_A)
        qn = qh * lax.rsqrt(ss + EPS) * gq_ref[:, lo:lo + HEAD_PAD_A]
        o_ref[:, lo:lo + LANE] = qn[:, :LANE].astype(o_ref.dtype)
        o_ref[:, lo + LANE:lo + HEAD_PAD_A] = _rope_lanes(qn[:, LANE:], cos_ref, sin_ref).astype(o_ref.dtype)


def _qa_proj(proj, cq_blk, g_q_lat, w_uq_pad, gq_pad, cos_t, sin_t, *, tm):
    t = proj.shape[0]
    nq = H_A * HEAD_PAD_A
    return pl.pallas_call(
        _qa_proj_kernel,
        out_shape=jax.ShapeDtypeStruct((t, nq), MXU_DTYPE),
        grid=(t // tm,),
        in_specs=[
            pl.BlockSpec((tm, Q_LORA), lambda i: (i, cq_blk)),
            pl.BlockSpec((1, Q_LORA), lambda i: (0, 0)),
            pl.BlockSpec((Q_LORA, nq), lambda i: (0, 0)),
            pl.BlockSpec((1, nq), lambda i: (0, 0)),
            pl.BlockSpec((tm, LANE), lambda i: (i, 0)),
            pl.BlockSpec((tm, LANE), lambda i: (i, 0)),
        ],
        out_specs=pl.BlockSpec((tm, nq), lambda i: (i, 0)),
        compiler_params=_params(("arbitrary",)),
        name="qa_proj",
    )(proj, g_q_lat.reshape(1, Q_LORA), w_uq_pad, gq_pad, cos_t, sin_t)


def _kva_proj_kernel(ckv_ref, kr_ref, gl_ref, wk_ref, wv_ref, gkn_ref, gkr_ref, cos_ref, sin_ref,
                     k_ref, v_ref):
    c = ckv_ref[...].astype(F32)
    ms = jnp.mean(c * c, axis=-1, keepdims=True)
    cn = (c * lax.rsqrt(ms + EPS) * gl_ref[...]).astype(MXU_DTYPE)
    kn = jnp.dot(cn, wk_ref[...], preferred_element_type=F32)
    v_ref[...] = jnp.dot(cn, wv_ref[...], preferred_element_type=F32).astype(v_ref.dtype)
    kr = kr_ref[...].astype(F32)
    ss_r = jnp.sum(kr * kr, axis=-1, keepdims=True)
    krr = _rope_lanes(kr * gkr_ref[...], cos_ref, sin_ref)
    shift_one = (lax.broadcasted_iota(I32, (1, LANE), 1) == SHIFT_LANE - LANE).astype(F32)
    for h in range(H_A):
        kh = kn[:, h * LANE:(h + 1) * LANE]
        ss = (jnp.sum(kh * kh, axis=-1, keepdims=True) + ss_r) * (1.0 / QK_DIM_A)
        rs = lax.rsqrt(ss + EPS)
        lo = h * HEAD_PAD_A
        k_ref[:, lo:lo + LANE] = (kh * rs * gkn_ref[...]).astype(k_ref.dtype)
        k_ref[:, lo + LANE:lo + HEAD_PAD_A] = (krr * rs + shift_one).astype(k_ref.dtype)


def _kva_proj(proj, ckv_blk, krope_blk, g_kv_lat, w_uk, w_uv, gk_nope, gk_rope, cos_t, sin_t, *, tm):
    t = proj.shape[0]
    return pl.pallas_call(
        _kva_proj_kernel,
        out_shape=(jax.ShapeDtypeStruct((t, H_A * HEAD_PAD_A), MXU_DTYPE),
                   jax.ShapeDtypeStruct((t, WIDTH_A), MXU_DTYPE)),
        grid=(t // tm,),
        in_specs=[
            pl.BlockSpec((tm, KV_LORA), lambda i: (i, ckv_blk)),
            pl.BlockSpec((tm, LANE), lambda i: (i, krope_blk)),
            pl.BlockSpec((1, KV_LORA), lambda i: (0, 0)),
            pl.BlockSpec((KV_LORA, H_A * QK_NOPE), lambda i: (0, 0)),
            pl.BlockSpec((KV_LORA, WIDTH_A), lambda i: (0, 0)),
            pl.BlockSpec((1, LANE), lambda i: (0, 0)),
            pl.BlockSpec((1, LANE), lambda i: (0, 0)),
            pl.BlockSpec((tm, LANE), lambda i: (i, 0)),
            pl.BlockSpec((tm, LANE), lambda i: (i, 0)),
        ],
        out_specs=(pl.BlockSpec((tm, H_A * HEAD_PAD_A), lambda i: (i, 0)),
                   pl.BlockSpec((tm, WIDTH_A), lambda i: (i, 0))),
        compiler_params=_params(("arbitrary",)),
        name="kva_proj",
    )(proj, proj, g_kv_lat.reshape(1, KV_LORA), w_uk, w_uv, gk_nope, gk_rope, cos_t, sin_t)


def _silu(g):
    return g * (1.0 / (1.0 + jnp.exp(-g)))


def _lane_tile_reduce(x, op):
    acc = x[:, :LANE]
    for t in range(1, x.shape[1] // LANE):
        acc = op(acc, x[:, t * LANE:(t + 1) * LANE])
    return acc


def _attn_a_kernel(q_ref, k_ref, v_ref, gate_ref, o_ref, knorm_ref, *, tq, nq, heads):
    qi = pl.program_id(2)

    @pl.when(qi == 0)
    def _():
        for g in range(heads):
            kk = k_ref[:, g * HEAD_PAD_A:(g + 1) * HEAD_PAD_A].astype(F32)
            lane = lax.broadcasted_iota(I32, kk.shape, 1)
            ksq = jnp.sum(jnp.where(lane == SHIFT_LANE, 0.0, kk * kk), axis=-1, keepdims=True)
            knorm_ref[g] = jnp.broadcast_to(jnp.sqrt(jnp.max(ksq, axis=0, keepdims=True)), knorm_ref.shape[1:])

    causal = lax.broadcasted_iota(I32, (tq, tq), 0) >= lax.broadcasted_iota(I32, (tq, tq), 1)

    def finish(g, l_t, acc):
        vc = slice(g * V_DIM_A, (g + 1) * V_DIM_A)
        l = jnp.sum(l_t, axis=-1, keepdims=True)
        o = acc * (1.0 / l)
        o_ref[:, vc] = (o * _silu(gate_ref[:, vc].astype(F32))).astype(o_ref.dtype)

    def branch(qv):
        n_off = qv * tq
        kcs = [slice(g * HEAD_PAD_A, (g + 1) * HEAD_PAD_A) for g in range(heads)]
        vcs = [slice(g * V_DIM_A, (g + 1) * V_DIM_A) for g in range(heads)]

        l_min = jnp.full((tq, 1), POS_INF, F32)
        for g in range(heads):
            q = q_ref[:, kcs[g]]
            q_lo, q_hi = q[:, :LANE].astype(F32), q[:, LANE:].astype(F32)
            qsq = jnp.sum(q_lo * q_lo + q_hi * q_hi, axis=-1, keepdims=True)
            hi = jnp.sqrt(qsq) * knorm_ref[g][0:1, 0:1] * BOUND_MARGIN
            lane = lax.broadcasted_iota(I32, q_hi.shape, 1)
            q2 = jnp.concatenate(
                [q[:, :LANE], jnp.where(lane == SHIFT_LANE - LANE, -hi, q_hi).astype(q.dtype)], axis=1)
            p_diag = jnp.exp2(jnp.where(causal, _nt_dot(q2, k_ref[n_off:n_off + tq, kcs[g]]), NEG_INF))
            l_t = _lane_tile_reduce(p_diag, jnp.add)
            acc = jnp.dot(p_diag.astype(MXU_DTYPE), v_ref[n_off:n_off + tq, vcs[g]], preferred_element_type=F32)
            if qv > 0:
                p_off = jnp.exp2(_nt_dot(q2, k_ref[0:n_off, kcs[g]]))
                l_t = l_t + _lane_tile_reduce(p_off, jnp.add)
                acc = acc + jnp.dot(p_off.astype(MXU_DTYPE), v_ref[0:n_off, vcs[g]], preferred_element_type=F32)
            l_min = jnp.minimum(l_min, jnp.sum(l_t, axis=-1, keepdims=True))
            finish(g, l_t, acc)

        shift_ok = jnp.min(l_min) >= MIN_SHIFTED_SUM

        @pl.when(jnp.logical_not(shift_ok))
        def _():
            for g in range(heads):
                q = q_ref[:, kcs[g]]
                s_diag = jnp.where(causal, _nt_dot(q, k_ref[n_off:n_off + tq, kcs[g]]), NEG_INF)
                m_t = _lane_tile_reduce(s_diag, jnp.maximum)
                if qv > 0:
                    s_off = _nt_dot(q, k_ref[0:n_off, kcs[g]])
                    m_t = jnp.maximum(m_t, _lane_tile_reduce(s_off, jnp.maximum))
                m = jnp.max(m_t, axis=-1, keepdims=True)
                p_diag = jnp.exp2(s_diag - m)
                l_t = _lane_tile_reduce(p_diag, jnp.add)
                acc = jnp.dot(p_diag.astype(MXU_DTYPE), v_ref[n_off:n_off + tq, vcs[g]],
                              preferred_element_type=F32)
                if qv > 0:
                    p_off = jnp.exp2(s_off - m)
                    l_t = l_t + _lane_tile_reduce(p_off, jnp.add)
                    acc = acc + jnp.dot(p_off.astype(MXU_DTYPE), v_ref[0:n_off, vcs[g]], preferred_element_type=F32)
                finish(g, l_t, acc)

    for qv in range(nq):
        pl.when(qi == qv)(functools.partial(branch, qv))


def _attn_a(q_a, k_a, v_a, proj, gate_blk0, *, b, l, tq, heads):
    t = q_a.shape[0]
    nq = l // tq
    kw, vw = heads * HEAD_PAD_A, heads * V_DIM_A
    return pl.pallas_call(
        functools.partial(_attn_a_kernel, tq=tq, nq=nq, heads=heads),
        out_shape=jax.ShapeDtypeStruct((t, WIDTH_A), MXU_DTYPE),
        grid=(b, H_A // heads, nq),
        in_specs=[
            pl.BlockSpec((tq, kw), lambda bi, h, qi: (bi * nq + qi, h)),
            pl.BlockSpec((l, kw), lambda bi, h, qi: (bi, h)),
            pl.BlockSpec((l, vw), lambda bi, h, qi: (bi, h)),
            pl.BlockSpec((tq, vw), lambda bi, h, qi: (bi * nq + qi, gate_blk0 // heads + h)),
        ],
        out_specs=pl.BlockSpec((tq, vw), lambda bi, h, qi: (bi * nq + qi, h)),
        scratch_shapes=[pltpu.VMEM((heads, 8, LANE), F32)],
        compiler_params=_params(("arbitrary", "arbitrary", "arbitrary")),
        name="attn_a",
    )(q_a, k_a, v_a, proj)


def _t5_bucket(dist):
    max_exact = N_BUCKETS // 2
    n = jnp.maximum(dist, 0)
    nf = jnp.maximum(n, 1).astype(F32)
    large = max_exact + (jnp.log(nf / max_exact) / math.log(MAX_DISTANCE / max_exact)
                         * (N_BUCKETS - max_exact)).astype(I32)
    large = jnp.minimum(large, N_BUCKETS - 1)
    return jnp.where(n < max_exact, n, large)


def _attn_b_kernel(t5_ref, qidx_ref, kidx_ref, wt_ref, qb_ref, kb_ref, vt_ref, gate_ref, gq_ref, gk_ref,
                   o_ref, sc_ref, qn_ref, acc_ref, m_ref, l_ref, bias_ref, thr_ref, kn_ref, bound_ref,
                   *, tq, nq, topk, max_iters):
    bi = pl.program_id(0)
    qi = pl.program_id(1)
    ck = tq
    shape = (ck, tq)

    @pl.when((bi == 0) & (qi == 0))
    def _():
        s_loc = lax.broadcasted_iota(I32, shape, 0)
        t_loc = lax.broadcasted_iota(I32, shape, 1)
        for near in range(2):
            bucket = _t5_bucket(t_loc - s_loc + (1 - near) * ck)

            def per_head(h, carry, bucket=bucket, near=near):
                far = t5_ref[N_BUCKETS - 1, h]
                tab = jnp.zeros(shape, F32)
                largest = jnp.float32(0.0)
                for bk in range(N_BUCKETS - 1):
                    rel = (t5_ref[bk, h] - far) * LOG2E
                    tab = jnp.where(bucket == bk, rel, tab)
                    largest = jnp.maximum(largest, rel)
                bias_ref[h, near] = tab
                bound_ref[1 + h] = largest
                return carry

            lax.fori_loop(0, H_B, per_head, 0)

    w_all = wt_ref[...] * (IDX_HEADS ** -0.5)

    def score_chunk(c, diag):
        rows = pl.ds(pl.multiple_of(c * ck, ck), ck)
        kx = kidx_ref[rows, 0:IDX_DIM]
        zk = jnp.zeros_like(kx)
        ka = jnp.concatenate([kx, zk], axis=1)
        kb = jnp.concatenate([zk, kx], axis=1)
        score = jnp.zeros(shape, F32)
        for j in range(IDX_HEADS // 2):
            qp = qidx_ref[:, j * LANE:(j + 1) * LANE]
            score = score + jnp.maximum(_nt_dot(ka, qp), 0.0) * w_all[2 * j:2 * j + 1, :]
            score = score + jnp.maximum(_nt_dot(kb, qp), 0.0) * w_all[2 * j + 1:2 * j + 2, :]
        if diag:
            adm = lax.broadcasted_iota(I32, shape, 0) <= lax.broadcasted_iota(I32, shape, 1)
            lo_src = jnp.where(adm, score, POS_INF)
            score = jnp.where(adm, score, NEG_INF)
        else:
            lo_src = score
        sc_ref[c] = score
        return jnp.max(score, axis=0, keepdims=True), jnp.min(lo_src, axis=0, keepdims=True)

    def score_body(c, carry):
        mx, mn = carry
        cmx, cmn = score_chunk(c, False)
        return jnp.maximum(mx, cmx), jnp.minimum(mn, cmn)

    mx, mn = lax.fori_loop(0, qi, score_body,
                           (jnp.full((1, tq), NEG_INF, F32), jnp.full((1, tq), POS_INF, F32)))
    dmx, dmn = score_chunk(qi, True)
    mx = jnp.maximum(mx, dmx)
    mn = jnp.minimum(mn, dmn)

    rep = (8, tq)
    n_adm = qi * tq + lax.broadcasted_iota(I32, rep, 1) + 1
    kp = jnp.minimum(n_adm, topk)
    mx8 = jnp.broadcast_to(mx, rep)
    mn8 = jnp.broadcast_to(mn, rep)

    def bisect(nchunks):
        def count_ge(x):
            acc = jnp.zeros(rep, I32)
            for c in range(nchunks):
                ge = sc_ref[c].reshape(ck // 8, 8, tq) >= x[None]
                acc = acc + jnp.sum(ge.astype(I32), axis=0)
            for shift in (4, 2, 1):
                acc = acc + pltpu.roll(acc, shift, 0)
            return acc

        def bis_cond(st):
            it, lo, hi, mid, cnt_lo = st
            active = (cnt_lo != kp) & (mid > lo) & (mid < hi)
            return jnp.logical_and(it < max_iters, jnp.max(active.astype(I32)) > 0)

        def bis_body(st):
            it, lo, hi, mid, cnt_lo = st
            for _ in range(BISECT_STEPS_PER_CHECK):
                cnt = count_ge(mid)
                ge = cnt >= kp
                lo = jnp.where(ge, mid, lo)
                cnt_lo = jnp.where(ge, cnt, cnt_lo)
                hi = jnp.where(ge, hi, mid)
                mid = jnp.where(hi == POS_INF, mx8, lo + 0.5 * (hi - lo))
            return it + 1, lo, hi, mid, cnt_lo

        _, lo, hi, _, cnt_lo = lax.while_loop(
            bis_cond, bis_body, (jnp.int32(0), mn8, jnp.full(rep, POS_INF, F32), mx8, n_adm))
        thr_ref[...] = lo

        tied = cnt_lo > kp

        @pl.when(jnp.max(tied.astype(I32)) > 0)
        def _():
            n_keys = nchunks * ck
            sub = lax.broadcasted_iota(I32, (ck // 8, 8, tq), 0) * 8 + lax.broadcasted_iota(I32, (ck // 8, 8, tq), 1)

            def count_kept(j_last):
                acc = jnp.zeros(rep, I32)
                for c in range(nchunks):
                    s3 = sc_ref[c].reshape(ck // 8, 8, tq)
                    keep = (s3 >= hi[None]) | ((s3 >= lo[None]) & (sub + c * ck <= j_last[None]))
                    acc = acc + jnp.sum(keep.astype(I32), axis=0)
                for shift in (4, 2, 1):
                    acc = acc + pltpu.roll(acc, shift, 0)
                return acc

            def idx_step(_, carry):
                j_lo, j_hi = carry
                j_mid = j_lo + ((j_hi - j_lo) >> 1)
                ok = count_kept(j_mid) >= kp
                return jnp.where(ok, j_lo, j_mid), jnp.where(ok, j_mid, j_hi)

            _, j_hi = lax.fori_loop(0, max(1, (n_keys - 1).bit_length()), idx_step,
                                    (jnp.full(rep, -1, I32), jnp.full(rep, n_keys - 1, I32)))
            j_last = jnp.where(tied, j_hi, n_keys - 1)
            for c in range(nchunks):
                s3 = sc_ref[c].reshape(ck // 8, 8, tq)
                drop = (s3 >= lo[None]) & (s3 < hi[None]) & (sub + c * ck > j_last[None])
                sc_ref[c] = jnp.where(drop, NEG_INF, s3).reshape(ck, tq)

    for qv in range(nq):
        pl.when(qi == qv)(functools.partial(bisect, qv + 1))
    thr = thr_ref[0:1, :]

    @pl.when(qi == 0)
    def _():
        kf = kb_ref[...].astype(F32)
        ms = jnp.mean(kf * kf, axis=-1, keepdims=True)
        kn = (kf * lax.rsqrt(ms + EPS) * gk_ref[...]).astype(kn_ref.dtype)
        kn_ref[...] = kn
        knf = kn.astype(F32)
        bound_ref[0] = jnp.sqrt(jnp.max(jnp.sum(knf * knf, axis=-1, keepdims=True)))

    for h in range(H_B):
        qh = qb_ref[:, h * LANE:(h + 1) * LANE].astype(F32)
        ms = jnp.mean(qh * qh, axis=-1, keepdims=True)
        qn = (qh * lax.rsqrt(ms + EPS) * gq_ref[...]).astype(qn_ref.dtype)
        qn_ref[h * tq:(h + 1) * tq, :] = qn
        qnf = qn.astype(F32)
        q_norm = jnp.sqrt(jnp.max(jnp.sum(qnf * qnf, axis=-1, keepdims=True)))
        bound_ref[1 + H_B + h] = q_norm * bound_ref[0] * BOUND_MARGIN + bound_ref[1 + h]

    def attend_chunk(c, near, exact):
        rows = pl.ds(pl.multiple_of(c * ck, ck), ck)
        vt = vt_ref[c]
        sel = sc_ref[c] >= thr
        s_all = _nt_dot(kn_ref[rows, :], qn_ref[...])
        for h in range(H_B):
            s = s_all[:, h * tq:(h + 1) * tq]
            if near is not None:
                s = s + bias_ref[h, near]
            if exact:
                s = jnp.where(sel, s, NEG_INF)
                m_old = m_ref[h]
                m_new = jnp.maximum(m_old, jnp.max(s, axis=0, keepdims=True))
                m_safe = jnp.where(m_new == NEG_INF, 0.0, m_new)
                p = jnp.exp2(s - m_safe)
                alpha = jnp.exp2(m_old - m_safe)
                l_ref[h] = alpha * l_ref[h] + jnp.sum(p, axis=0, keepdims=True)
                acc_ref[h] = alpha * acc_ref[h] + jnp.dot(vt, p.astype(MXU_DTYPE), preferred_element_type=F32)
                m_ref[h] = m_new
            else:
                p = jnp.exp2(jnp.where(sel, s - bound_ref[1 + H_B + h], NEG_INF))
                l_ref[h] = l_ref[h] + jnp.sum(p, axis=0, keepdims=True)
                acc_ref[h] = acc_ref[h] + jnp.dot(vt, p.astype(MXU_DTYPE), preferred_element_type=F32)

    def attend(exact):
        if exact:
            m_ref[...] = jnp.full(m_ref.shape, NEG_INF, F32)
        l_ref[...] = jnp.zeros(l_ref.shape, F32)
        acc_ref[...] = jnp.zeros(acc_ref.shape, F32)

        def far_body(c, carry):
            attend_chunk(c, None, exact)
            return carry

        lax.fori_loop(0, jnp.maximum(qi - 1, 0), far_body, 0)

        @pl.when(qi >= 1)
        def _():
            attend_chunk(qi - 1, 0, exact)

        attend_chunk(qi, 1, exact)

    attend(False)
    l_min = l_ref[0]
    for h in range(1, H_B):
        l_min = jnp.minimum(l_min, l_ref[h])
    shift_ok = jnp.min(l_min) >= MIN_SHIFTED_SUM
    pl.when(jnp.logical_not(shift_ok))(functools.partial(attend, True))

    for h in range(H_B):
        o_t = acc_ref[h] * (1.0 / l_ref[h])
        g = gate_ref[:, h * LANE:(h + 1) * LANE].astype(F32)
        o_ref[:, h * LANE:(h + 1) * LANE] = (o_t.T * _silu(g)).astype(o_ref.dtype)


def _attn_b(proj, wt, vt, t5_bias, gq_b, gk_b, blk, *, b, l, tq):
    t = proj.shape[0]
    nq = l // tq
    topk = min(TOPK_MAX, l // 4)
    row = lambda bi, qi: bi * nq + qi
    return pl.pallas_call(
        functools.partial(_attn_b_kernel, tq=tq, nq=nq, topk=topk, max_iters=BISECT_MAX_CHECKS),
        out_shape=jax.ShapeDtypeStruct((t, WIDTH_B), MXU_DTYPE),
        grid=(b, nq),
        in_specs=[
            pl.BlockSpec(memory_space=pltpu.SMEM),
            pl.BlockSpec((tq, IDX_HEADS * IDX_DIM), lambda bi, qi: (row(bi, qi), blk["q_idx"])),
            pl.BlockSpec((l, LANE), lambda bi, qi: (bi, blk["k_idx"])),
            pl.BlockSpec((IDX_HEADS, tq), lambda bi, qi: (0, row(bi, qi))),
            pl.BlockSpec((tq, WIDTH_B), lambda bi, qi: (row(bi, qi), blk["q_b"])),
            pl.BlockSpec((l, LANE), lambda bi, qi: (bi, blk["k_b"])),
            pl.BlockSpec((None, nq, LANE, tq), lambda bi, qi: (bi, 0, 0, 0)),
            pl.BlockSpec((tq, WIDTH_B), lambda bi, qi: (row(bi, qi), blk["gate_b"])),
            pl.BlockSpec((1, LANE), lambda bi, qi: (0, 0)),
            pl.BlockSpec((1, LANE), lambda bi, qi: (0, 0)),
        ],
        out_specs=pl.BlockSpec((tq, WIDTH_B), lambda bi, qi: (row(bi, qi), 0)),
        scratch_shapes=[
            pltpu.VMEM((nq, tq, tq), F32),
            pltpu.VMEM((H_B * tq, LANE), MXU_DTYPE),
            pltpu.VMEM((H_B, LANE, tq), F32),
            pltpu.VMEM((H_B, 1, tq), F32),
            pltpu.VMEM((H_B, 1, tq), F32),
            pltpu.VMEM((H_B, 2, tq, tq), F32),
            pltpu.VMEM((8, tq), F32),
            pltpu.VMEM((l, LANE), MXU_DTYPE),
            pltpu.SMEM((1 + 2 * H_B,), F32),
        ],
        compiler_params=_params(("arbitrary", "arbitrary")),
        name="attn_b",
    )(t5_bias, proj, proj, wt, proj, proj, vt, proj, gq_b, gk_b)


def _sigmoid(z):
    return 1.0 / (1.0 + jnp.exp(-z))


def _merge_kernel(oa_ref, ob_ref, pa_ref, pb_ref, ma_ref, mb_ref, o_ref):
    a = jnp.dot(oa_ref[...], pa_ref[...], preferred_element_type=F32)
    bb = jnp.dot(ob_ref[...], pb_ref[...], preferred_element_type=F32)
    o_ref[...] = (_sigmoid(ma_ref[...].astype(F32)) * a + _sigmoid(mb_ref[...].astype(F32)) * bb).astype(o_ref.dtype)


def _merge(o_a, o_b, p_a, p_b, proj, ma_off, mb_off, *, tm, tn):
    t = o_a.shape[0]
    d = p_a.shape[1]
    ma0, mb0 = ma_off // tn, mb_off // tn
    return pl.pallas_call(
        _merge_kernel,
        out_shape=jax.ShapeDtypeStruct((t, d), MXU_DTYPE),
        grid=(t // tm, d // tn),
        in_specs=[
            pl.BlockSpec((tm, WIDTH_A), lambda i, j: (i, 0)),
            pl.BlockSpec((tm, WIDTH_B), lambda i, j: (i, 0)),
            pl.BlockSpec((WIDTH_A, tn), lambda i, j: (0, j)),
            pl.BlockSpec((WIDTH_B, tn), lambda i, j: (0, j)),
            pl.BlockSpec((tm, tn), lambda i, j: (i, ma0 + j)),
            pl.BlockSpec((tm, tn), lambda i, j: (i, mb0 + j)),
        ],
        out_specs=pl.BlockSpec((tm, tn), lambda i, j: (i, j)),
        compiler_params=_params(("arbitrary", "arbitrary")),
        name="merge",
    )(o_a, o_b, p_a, p_b, proj, proj)


def _out_proj_kernel(m_ref, w_ref, x_ref, o_ref):
    o_ref[...] = x_ref[...] + jnp.dot(m_ref[...], w_ref[...], preferred_element_type=F32)


def _out_proj(merged, w_o, x2, *, tm, tn):
    t, d = x2.shape
    return pl.pallas_call(
        _out_proj_kernel,
        out_shape=jax.ShapeDtypeStruct((t, d), x2.dtype),
        grid=(t // tm, d // tn),
        in_specs=[
            pl.BlockSpec((tm, d), lambda i, j: (i, 0)),
            pl.BlockSpec((d, tn), lambda i, j: (0, j)),
            pl.BlockSpec((tm, tn), lambda i, j: (i, j)),
        ],
        out_specs=pl.BlockSpec((tm, tn), lambda i, j: (i, j)),
        compiler_params=_params(("arbitrary", "arbitrary")),
        name="out_proj",
    )(merged, w_o, x2)


def _rope_pad(a, axis):
    a1, a2 = jnp.split(a, 2, axis=axis)
    z = jnp.zeros_like(a1)
    return jnp.concatenate([a1, z, a2, z], axis=axis)


def _layout(d):
    names = [("merge_a", d), ("merge_b", d), ("gate_a", WIDTH_A), ("gate_b", WIDTH_B),
             ("q_b", WIDTH_B), ("q_idx", IDX_HEADS * IDX_DIM), ("cq", Q_LORA), ("ckv", KV_LORA),
             ("k_idx", LANE), ("k_rope", LANE), ("k_b", LANE), ("v_b", LANE)]
    off, out = 0, {}
    for name, width in names:
        assert off % width == 0, (name, off, width)
        out[name] = off
        off += width
    out["w_idx"] = out["k_idx"] + IDX_DIM
    return out, off


def kernel(x, positions, g_pre, w_in, g_q_lat, g_kv_lat, w_uq, w_ukv, g_qn_a, g_kn_a,
           g_qn_b, g_kn_b, t5_bias, p_a, p_b, w_o):
    b, l, d = x.shape
    t = b * l
    tq = 256
    tn_in = 512
    off, n_used = _layout(d)
    n_pad = -(-n_used // tn_in) * tn_in

    names = ["cq", "ckv", "k_rope", "q_b", "k_b", "v_b", "q_idx", "k_idx", "w_idx", "gate_a", "gate_b",
             "merge_a", "merge_b"]
    sizes = [Q_LORA, KV_LORA, QK_ROPE, WIDTH_B, HEAD_DIM_B, HEAD_DIM_B, IDX_HEADS * IDX_DIM, IDX_DIM,
             IDX_HEADS, WIDTH_A, WIDTH_B, d, d]
    src, acc = {}, 0
    for name, s in zip(names, sizes):
        src[name] = (acc, s)
        acc += s
    main_groups = ["merge_a", "merge_b", "gate_a", "gate_b", "q_b", "q_idx", "cq", "ckv"]
    src_starts, dst = [], 0
    for name in main_groups:
        assert off[name] == dst and src[name][1] % tn_in == 0
        src_starts += [src[name][0] + c for c in range(0, src[name][1], tn_in)]
        dst += src[name][1]
    wt = w_in.T
    rows = lambda name: wt[src[name][0]:src[name][0] + src[name][1]]
    z = lambda n: jnp.zeros((n, d), w_in.dtype)
    assert off["k_idx"] == dst
    wt_tail = jnp.concatenate(
        [rows("k_idx"), rows("w_idx"), z(LANE - IDX_DIM - IDX_HEADS), _rope_pad(rows("k_rope"), 0),
         rows("k_b"), rows("v_b"), z(n_pad - n_used)], axis=0)
    w_pad = _w_relayout(wt, src_starts, wt_tail, tn=tn_in, tc=d // 2)

    w_uq3 = w_uq.reshape(Q_LORA, H_A, QK_DIM_A)
    w_uq_pad = jnp.concatenate([w_uq3[:, :, :QK_NOPE], _rope_pad(w_uq3[:, :, QK_NOPE:], 2)], axis=2)
    w_uq_pad = w_uq_pad.reshape(Q_LORA, H_A * HEAD_PAD_A).astype(MXU_DTYPE)
    gq_head = jnp.concatenate([g_qn_a[:QK_NOPE], _rope_pad(g_qn_a[QK_NOPE:], 0)]) * (QK_DIM_A ** -0.5 * LOG2E)
    gq_pad = jnp.tile(gq_head, H_A).reshape(1, H_A * HEAD_PAD_A).astype(F32)
    w_ukv3 = w_ukv.reshape(KV_LORA, H_A, QK_NOPE + V_DIM_A)
    w_uk = w_ukv3[:, :, :QK_NOPE].reshape(KV_LORA, H_A * QK_NOPE).astype(MXU_DTYPE)
    w_uv = w_ukv3[:, :, QK_NOPE:].reshape(KV_LORA, WIDTH_A).astype(MXU_DTYPE)
    gk_nope = g_kn_a[:QK_NOPE].reshape(1, LANE).astype(F32)
    gk_rope = _rope_pad(g_kn_a[QK_NOPE:], 0).reshape(1, LANE).astype(F32)
    gq_b = (g_qn_b * (HEAD_DIM_B ** -0.5 * LOG2E)).reshape(1, LANE).astype(F32)
    gk_b = g_kn_b.reshape(1, LANE).astype(F32)

    inv = ROPE_THETA ** (-jnp.arange(HALF_ROPE, dtype=F32) / HALF_ROPE)
    ang = positions.reshape(t, 1).astype(F32) * inv
    cos, sin = jnp.cos(ang), jnp.sin(ang)
    zr = jnp.zeros_like(cos)
    cos_t = jnp.concatenate([cos, zr, cos, zr], axis=1)
    sin_t = jnp.concatenate([-sin, zr, sin, zr], axis=1)

    x2 = x.reshape(t, d)
    proj = _in_proj(x2, g_pre, w_pad, tm=1024, tn=tn_in)

    q_a = _qa_proj(proj, off["cq"] // Q_LORA, g_q_lat, w_uq_pad, gq_pad, cos_t, sin_t, tm=256)
    k_a, v_a = _kva_proj(proj, off["ckv"] // KV_LORA, off["k_rope"] // LANE, g_kv_lat, w_uk, w_uv,
                         gk_nope, gk_rope, cos_t, sin_t, tm=256)
    o_a = _attn_a(q_a, k_a, v_a, proj, off["gate_a"] // LANE, b=b, l=l, tq=tq, heads=4)

    nq = l // tq
    w_idx = proj[:, off["w_idx"]:off["w_idx"] + IDX_HEADS]
    wt = w_idx.astype(F32).T
    v_b = proj[:, off["v_b"]:off["v_b"] + HEAD_DIM_B]
    vt = v_b.reshape(b, nq, tq, HEAD_DIM_B).transpose(0, 1, 3, 2)
    blk = {"q_idx": off["q_idx"] // (IDX_HEADS * IDX_DIM), "k_idx": off["k_idx"] // LANE,
           "q_b": off["q_b"] // WIDTH_B, "k_b": off["k_b"] // LANE, "gate_b": off["gate_b"] // WIDTH_B}
    o_b = _attn_b(proj, wt, vt, t5_bias.astype(F32), gq_b, gk_b, blk, b=b, l=l, tq=tq)

    merged = _merge(o_a, o_b, p_a.astype(MXU_DTYPE), p_b.astype(MXU_DTYPE), proj,
                    off["merge_a"], off["merge_b"], tm=1024, tn=1024)
    out = _out_proj(merged, w_o.astype(MXU_DTYPE), x2, tm=1024, tn=1024)
    return out.reshape(b, l, d)
```

```python
import functools
import math

import jax
import jax.numpy as jnp
from jax import lax
from jax.experimental import pallas as pl
from jax.experimental.pallas import tpu as pltpu

F32 = jnp.float32
I32 = jnp.int32
MXU_DTYPE = jnp.bfloat16

H_A = 16
QK_NOPE = 128
QK_ROPE = 64
QK_DIM_A = QK_NOPE + QK_ROPE
V_DIM_A = 128
Q_LORA = 1024
KV_LORA = 512
ROPE_THETA = 10000.0
H_B = 16
HEAD_DIM_B = 128
IDX_HEADS = 32
IDX_DIM = 64
TOPK_MAX = 256
N_BUCKETS = 32
MAX_DISTANCE = 128
EPS = 1e-6
WIDTH_A = H_A * V_DIM_A
WIDTH_B = H_B * HEAD_DIM_B

LANE = 128
ROW_ALIGN = 32
HALF_ROPE = QK_ROPE // 2
HEAD_PAD_A = 2 * LANE
SHIFT_LANE = LANE + HALF_ROPE
BOUND_MARGIN = 1.0 + 2.0 ** -6
MIN_SHIFTED_SUM = 2.0 ** -64
VMEM_LIMIT = 56 * 1024 * 1024

BISECT_STEPS_PER_CHECK = 4
BISECT_MAX_CHECKS = 40
LOG2E = math.log2(math.e)
NEG_INF = float("-inf")
POS_INF = float("inf")


def _nt_dot(a, b):
    return lax.dot_general(a, b, (((1,), (1,)), ((), ())), preferred_element_type=F32)


def _params(sem, vmem=VMEM_LIMIT):
    return pltpu.CompilerParams(dimension_semantics=sem, vmem_limit_bytes=vmem)


def _w_relayout_kernel(start_ref, src_ref, tail_ref, o_ref, *, n_main):
    j = pl.program_id(0)

    @pl.when(j < n_main)
    def _():
        o_ref[...] = src_ref[...].astype(o_ref.dtype)

    @pl.when(j >= n_main)
    def _():
        o_ref[...] = tail_ref[...].astype(o_ref.dtype)


def _w_relayout(wt, src_starts, wt_tail, *, tn, tc):
    d = wt.shape[1]
    n_main, n_tail = len(src_starts), wt_tail.shape[0] // tn
    assert all(s % ROW_ALIGN == 0 for s in src_starts)
    starts = jnp.array([s // ROW_ALIGN for s in src_starts] + [0] * n_tail, I32)
    return pl.pallas_call(
        functools.partial(_w_relayout_kernel, n_main=n_main),
        out_shape=jax.ShapeDtypeStruct(((n_main + n_tail) * tn, d), MXU_DTYPE),
        grid_spec=pltpu.PrefetchScalarGridSpec(
            num_scalar_prefetch=1,
            grid=(n_main + n_tail, d // tc),
            in_specs=[
                pl.BlockSpec((pl.Element(tn), pl.Element(tc)), lambda j, c, st: (st[j] * ROW_ALIGN, c * tc)),
                pl.BlockSpec((tn, tc), lambda j, c, st: (jnp.maximum(j - n_main, 0), c)),
            ],
            out_specs=pl.BlockSpec((tn, tc), lambda j, c, st: (j, c)),
        ),
        compiler_params=_params(("arbitrary", "arbitrary")),
        name="w_relayout",
    )(starts, wt, wt_tail)


def _in_proj_kernel(x_ref, g_ref, w_ref, o_ref, hn_ref, *, row_chunk):
    tm = x_ref.shape[0]

    @pl.when(pl.program_id(1) == 0)
    def _():
        def body(r, carry):
            sl = pl.ds(pl.multiple_of(r * row_chunk, row_chunk), row_chunk)
            xx = x_ref[sl, :]
            ms = jnp.mean(xx * xx, axis=-1, keepdims=True)
            hn_ref[sl, :] = (xx * lax.rsqrt(ms + EPS) * g_ref[...]).astype(hn_ref.dtype)
            return carry

        lax.fori_loop(0, tm // row_chunk, body, 0)

    o_ref[...] = _nt_dot(hn_ref[...], w_ref[...]).astype(o_ref.dtype)


def _in_proj(x2, g_pre, wt_pad, *, tm, tn):
    t, d = x2.shape
    n = wt_pad.shape[0]
    return pl.pallas_call(
        functools.partial(_in_proj_kernel, row_chunk=64),
        out_shape=jax.ShapeDtypeStruct((t, n), MXU_DTYPE),
        grid=(t // tm, n // tn),
        in_specs=[
            pl.BlockSpec((tm, d), lambda i, j: (i, 0)),
            pl.BlockSpec((1, d), lambda i, j: (0, 0)),
            pl.BlockSpec((tn, d), lambda i, j: (j, 0)),
        ],
        out_specs=pl.BlockSpec((tm, tn), lambda i, j: (i, j)),
        scratch_shapes=[pltpu.VMEM((tm, d), MXU_DTYPE)],
        compiler_params=_params(("arbitrary", "arbitrary")),
        name="in_proj",
    )(x2, g_pre.reshape(1, d), wt_pad)


def _rope_lanes(r, cos_ref, sin_ref):
    return r * cos_ref[...] + pltpu.roll(r, 2 * HALF_ROPE, 1) * sin_ref[...]


def _qa_proj_kernel(cq_ref, gl_ref, w_ref, gq_ref, cos_ref, sin_ref, shift_ref, o_ref):
    c = cq_ref[...].astype(F32)
    ms = jnp.mean(c * c, axis=-1, keepdims=True)
    cn = (c * lax.rsqrt(ms + EPS) * gl_ref[...]).astype(MXU_DTYPE)
    q = jnp.dot(cn, w_ref[...], preferred_element_type=F32)
    for h in range(H_A):
        lo = h * HEAD_PAD_A
        qh = q[:, lo:lo + HEAD_PAD_A]
        ss = jnp.sum(qh * qh, axis=-1, keepdims=True) * (1.0 / QK_DIM_A)
        qn = qh * lax.rsqrt(ss + EPS) * gq_ref[:, lo:lo + HEAD_PAD_A]
        o_ref[:, lo:lo + LANE] = qn[:, :LANE].astype(o_ref.dtype)
        o_ref[:, lo + LANE:lo + HEAD_PAD_A] = (
            _rope_lanes(qn[:, LANE:], cos_ref, sin_ref) + shift_ref[...]).astype(o_ref.dtype)


def _qa_proj(proj, cq_blk, g_q_lat, w_uq_pad, gq_pad, cos_t, sin_t, q_shift, *, tm):
    t = proj.shape[0]
    nq = H_A * HEAD_PAD_A
    return pl.pallas_call(
        _qa_proj_kernel,
        out_shape=jax.ShapeDtypeStruct((t, nq), MXU_DTYPE),
        grid=(t // tm,),
        in_specs=[
            pl.BlockSpec((tm, Q_LORA), lambda i: (i, cq_blk)),
            pl.BlockSpec((1, Q_LORA), lambda i: (0, 0)),
            pl.BlockSpec((Q_LORA, nq), lambda i: (0, 0)),
            pl.BlockSpec((1, nq), lambda i: (0, 0)),
            pl.BlockSpec((tm, LANE), lambda i: (i, 0)),
            pl.BlockSpec((tm, LANE), lambda i: (i, 0)),
            pl.BlockSpec((1, LANE), lambda i: (0, 0)),
        ],
        out_specs=pl.BlockSpec((tm, nq), lambda i: (i, 0)),
        compiler_params=_params(("arbitrary",)),
        name="qa_proj",
    )(proj, g_q_lat.reshape(1, Q_LORA), w_uq_pad, gq_pad, cos_t, sin_t, q_shift)


def _kva_proj_kernel(ckv_ref, kr_ref, gl_ref, wk_ref, wv_ref, gkn_ref, gkr_ref, cos_ref, sin_ref,
                     k_ref, v_ref):
    c = ckv_ref[...].astype(F32)
    ms = jnp.mean(c * c, axis=-1, keepdims=True)
    cn = (c * lax.rsqrt(ms + EPS) * gl_ref[...]).astype(MXU_DTYPE)
    kn = jnp.dot(cn, wk_ref[...], preferred_element_type=F32)
    v_ref[...] = jnp.dot(cn, wv_ref[...], preferred_element_type=F32).astype(v_ref.dtype)
    kr = kr_ref[...].astype(F32)
    ss_r = jnp.sum(kr * kr, axis=-1, keepdims=True)
    krr = _rope_lanes(kr * gkr_ref[...], cos_ref, sin_ref)
    shift_one = (lax.broadcasted_iota(I32, (1, LANE), 1) == SHIFT_LANE - LANE).astype(F32)
    for h in range(H_A):
        kh = kn[:, h * LANE:(h + 1) * LANE]
        ss = (jnp.sum(kh * kh, axis=-1, keepdims=True) + ss_r) * (1.0 / QK_DIM_A)
        rs = lax.rsqrt(ss + EPS)
        lo = h * HEAD_PAD_A
        k_ref[:, lo:lo + LANE] = (kh * rs * gkn_ref[...]).astype(k_ref.dtype)
        k_ref[:, lo + LANE:lo + HEAD_PAD_A] = (krr * rs + shift_one).astype(k_ref.dtype)


def _kva_proj(proj, ckv_blk, krope_blk, g_kv_lat, w_uk, w_uv, gk_nope, gk_rope, cos_t, sin_t, *, tm):
    t = proj.shape[0]
    return pl.pallas_call(
        _kva_proj_kernel,
        out_shape=(jax.ShapeDtypeStruct((t, H_A * HEAD_PAD_A), MXU_DTYPE),
                   jax.ShapeDtypeStruct((t, WIDTH_A), MXU_DTYPE)),
        grid=(t // tm,),
        in_specs=[
            pl.BlockSpec((tm, KV_LORA), lambda i: (i, ckv_blk)),
            pl.BlockSpec((tm, LANE), lambda i: (i, krope_blk)),
            pl.BlockSpec((1, KV_LORA), lambda i: (0, 0)),
            pl.BlockSpec((KV_LORA, H_A * QK_NOPE), lambda i: (0, 0)),
            pl.BlockSpec((KV_LORA, WIDTH_A), lambda i: (0, 0)),
            pl.BlockSpec((1, LANE), lambda i: (0, 0)),
            pl.BlockSpec((1, LANE), lambda i: (0, 0)),
            pl.BlockSpec((tm, LANE), lambda i: (i, 0)),
            pl.BlockSpec((tm, LANE), lambda i: (i, 0)),
        ],
        out_specs=(pl.BlockSpec((tm, H_A * HEAD_PAD_A), lambda i: (i, 0)),
                   pl.BlockSpec((tm, WIDTH_A), lambda i: (i, 0))),
        compiler_params=_params(("arbitrary",)),
        name="kva_proj",
    )(proj, proj, g_kv_lat.reshape(1, KV_LORA), w_uk, w_uv, gk_nope, gk_rope, cos_t, sin_t)


def _silu(g):
    return g * (1.0 / (1.0 + jnp.exp(-g)))


def _lane_tile_reduce(x, op):
    acc = x[:, :LANE]
    for t in range(1, x.shape[1] // LANE):
        acc = op(acc, x[:, t * LANE:(t + 1) * LANE])
    return acc


def _attn_a_kernel(q_ref, k_ref, v_ref, gate_ref, o_ref, *, tq, nq, heads):
    qi = pl.program_id(2)
    causal = lax.broadcasted_iota(I32, (tq, tq), 0) >= lax.broadcasted_iota(I32, (tq, tq), 1)

    def finish(g, l_t, acc):
        vc = slice(g * V_DIM_A, (g + 1) * V_DIM_A)
        l = jnp.sum(l_t, axis=-1, keepdims=True)
        o = acc * (1.0 / l)
        o_ref[:, vc] = (o * _silu(gate_ref[:, vc].astype(F32))).astype(o_ref.dtype)

    def branch(qv):
        n_off = qv * tq
        kcs = [slice(g * HEAD_PAD_A, (g + 1) * HEAD_PAD_A) for g in range(heads)]
        vcs = [slice(g * V_DIM_A, (g + 1) * V_DIM_A) for g in range(heads)]

        l_min = jnp.full((tq, 1), POS_INF, F32)
        for g in range(heads):
            q = q_ref[:, kcs[g]]
            p_diag = jnp.exp2(jnp.where(causal, _nt_dot(q, k_ref[n_off:n_off + tq, kcs[g]]), NEG_INF))
            l_t = _lane_tile_reduce(p_diag, jnp.add)
            acc = jnp.dot(p_diag.astype(MXU_DTYPE), v_ref[n_off:n_off + tq, vcs[g]], preferred_element_type=F32)
            if qv > 0:
                p_off = jnp.exp2(_nt_dot(q, k_ref[0:n_off, kcs[g]]))
                l_t = l_t + _lane_tile_reduce(p_off, jnp.add)
                acc = acc + jnp.dot(p_off.astype(MXU_DTYPE), v_ref[0:n_off, vcs[g]], preferred_element_type=F32)
            l_min = jnp.minimum(l_min, jnp.sum(l_t, axis=-1, keepdims=True))
            finish(g, l_t, acc)

        shift_ok = jnp.min(l_min) >= MIN_SHIFTED_SUM

        @pl.when(jnp.logical_not(shift_ok))
        def _():
            for g in range(heads):
                q = q_ref[:, kcs[g]]
                q_hi = q[:, LANE:]
                lane = lax.broadcasted_iota(I32, q_hi.shape, 1)
                q = jnp.concatenate(
                    [q[:, :LANE], jnp.where(lane == SHIFT_LANE - LANE, 0.0, q_hi.astype(F32)).astype(q.dtype)], axis=1)
                s_diag = jnp.where(causal, _nt_dot(q, k_ref[n_off:n_off + tq, kcs[g]]), NEG_INF)
                m_t = _lane_tile_reduce(s_diag, jnp.maximum)
                if qv > 0:
                    s_off = _nt_dot(q, k_ref[0:n_off, kcs[g]])
                    m_t = jnp.maximum(m_t, _lane_tile_reduce(s_off, jnp.maximum))
                m = jnp.max(m_t, axis=-1, keepdims=True)
                p_diag = jnp.exp2(s_diag - m)
                l_t = _lane_tile_reduce(p_diag, jnp.add)
                acc = jnp.dot(p_diag.astype(MXU_DTYPE), v_ref[n_off:n_off + tq, vcs[g]],
                              preferred_element_type=F32)
                if qv > 0:
                    p_off = jnp.exp2(s_off - m)
                    l_t = l_t + _lane_tile_reduce(p_off, jnp.add)
                    acc = acc + jnp.dot(p_off.astype(MXU_DTYPE), v_ref[0:n_off, vcs[g]], preferred_element_type=F32)
                finish(g, l_t, acc)

    for qv in range(nq):
        pl.when(qi == qv)(functools.partial(branch, qv))


def _attn_a(q_a, k_a, v_a, proj, gate_blk0, *, b, l, tq, heads):
    t = q_a.shape[0]
    nq = l // tq
    kw, vw = heads * HEAD_PAD_A, heads * V_DIM_A
    return pl.pallas_call(
        functools.partial(_attn_a_kernel, tq=tq, nq=nq, heads=heads),
        out_shape=jax.ShapeDtypeStruct((t, WIDTH_A), MXU_DTYPE),
        grid=(b, H_A // heads, nq),
        in_specs=[
            pl.BlockSpec((tq, kw), lambda bi, h, qi: (bi * nq + qi, h)),
            pl.BlockSpec((l, kw), lambda bi, h, qi: (bi, h)),
            pl.BlockSpec((l, vw), lambda bi, h, qi: (bi, h)),
            pl.BlockSpec((tq, vw), lambda bi, h, qi: (bi * nq + qi, gate_blk0 // heads + h)),
        ],
        out_specs=pl.BlockSpec((tq, vw), lambda bi, h, qi: (bi * nq + qi, h)),
        compiler_params=_params(("arbitrary", "arbitrary", "arbitrary")),
        name="attn_a",
    )(q_a, k_a, v_a, proj)


def _t5_bucket(dist):
    max_exact = N_BUCKETS // 2
    n = jnp.maximum(dist, 0)
    nf = jnp.maximum(n, 1).astype(F32)
    large = max_exact + (jnp.log(nf / max_exact) / math.log(MAX_DISTANCE / max_exact)
                         * (N_BUCKETS - max_exact)).astype(I32)
    large = jnp.minimum(large, N_BUCKETS - 1)
    return jnp.where(n < max_exact, n, large)


def _attn_b_kernel(t5_ref, qidx_ref, kidx_ref, wt_ref, qb_ref, kb_ref, vt_ref, gate_ref, gq_ref, gk_ref,
                   o_ref, sc_ref, qn_ref, acc_ref, m_ref, l_ref, bias_ref, thr_ref, kn_ref, bound_ref,
                   *, tq, nq, topk, max_iters):
    bi = pl.program_id(0)
    qi = pl.program_id(1)
    ck = tq
    shape = (ck, tq)

    @pl.when((bi == 0) & (qi == 0))
    def _():
        s_loc = lax.broadcasted_iota(I32, shape, 0)
        t_loc = lax.broadcasted_iota(I32, shape, 1)
        for near in range(2):
            bucket = _t5_bucket(t_loc - s_loc + (1 - near) * ck)

            def per_head(h, carry, bucket=bucket, near=near):
                far = t5_ref[N_BUCKETS - 1, h]
                tab = jnp.zeros(shape, F32)
                largest = jnp.float32(0.0)
                for bk in range(N_BUCKETS - 1):
                    rel = (t5_ref[bk, h] - far) * LOG2E
                    tab = jnp.where(bucket == bk, rel, tab)
                    largest = jnp.maximum(largest, rel)
                bias_ref[h, near] = tab
                gains = jnp.max(jnp.abs(gq_ref[...])) * jnp.max(jnp.abs(gk_ref[...]))
                bound_ref[h] = HEAD_DIM_B * gains * BOUND_MARGIN + largest
                return carry

            lax.fori_loop(0, H_B, per_head, 0)

    w_all = wt_ref[...] * (IDX_HEADS ** -0.5)

    def score_chunk(c, diag):
        rows = pl.ds(pl.multiple_of(c * ck, ck), ck)
        kx = kidx_ref[rows, 0:IDX_DIM]
        zk = jnp.zeros_like(kx)
        kab = jnp.concatenate([jnp.concatenate([kx, zk], axis=1), jnp.concatenate([zk, kx], axis=1)], axis=0)
        score = jnp.zeros(shape, F32)
        for j in range(IDX_HEADS // 2):
            qp = qidx_ref[:, j * LANE:(j + 1) * LANE]
            logits = jnp.maximum(_nt_dot(kab, qp), 0.0)
            score = score + logits[:ck] * w_all[2 * j:2 * j + 1, :]
            score = score + logits[ck:] * w_all[2 * j + 1:2 * j + 2, :]
        if diag:
            adm = lax.broadcasted_iota(I32, shape, 0) <= lax.broadcasted_iota(I32, shape, 1)
            lo_src = jnp.where(adm, score, POS_INF)
            score = jnp.where(adm, score, NEG_INF)
        else:
            lo_src = score
        sc_ref[c] = score
        return jnp.max(score, axis=0, keepdims=True), jnp.min(lo_src, axis=0, keepdims=True)

    def score_body(c, carry):
        mx, mn = carry
        cmx, cmn = score_chunk(c, False)
        return jnp.maximum(mx, cmx), jnp.minimum(mn, cmn)

    mx, mn = lax.fori_loop(0, qi, score_body,
                           (jnp.full((1, tq), NEG_INF, F32), jnp.full((1, tq), POS_INF, F32)))
    dmx, dmn = score_chunk(qi, True)
    mx = jnp.maximum(mx, dmx)
    mn = jnp.minimum(mn, dmn)

    rep = (8, tq)
    n_adm = qi * tq + lax.broadcasted_iota(I32, rep, 1) + 1
    kp = jnp.minimum(n_adm, topk)
    mx8 = jnp.broadcast_to(mx, rep)
    mn8 = jnp.broadcast_to(mn, rep)

    def bisect(nchunks):
        def count_ge(x):
            acc = jnp.zeros(rep, I32)
            for c in range(nchunks):
                ge = sc_ref[c].reshape(ck // 8, 8, tq) >= x[None]
                acc = acc + jnp.sum(ge.astype(I32), axis=0)
            for shift in (4, 2, 1):
                acc = acc + pltpu.roll(acc, shift, 0)
            return acc

        def bis_cond(st):
            it, lo, hi, mid, cnt_lo = st
            active = (cnt_lo != kp) & (mid > lo) & (mid < hi)
            return jnp.logical_and(it < max_iters, jnp.max(active.astype(I32)) > 0)

        def bis_body(st):
            it, lo, hi, mid, cnt_lo = st
            for _ in range(BISECT_STEPS_PER_CHECK):
                cnt = count_ge(mid)
                ge = cnt >= kp
                lo = jnp.where(ge, mid, lo)
                cnt_lo = jnp.where(ge, cnt, cnt_lo)
                hi = jnp.where(ge, hi, mid)
                mid = jnp.where(hi == POS_INF, mx8, lo + 0.5 * (hi - lo))
            return it + 1, lo, hi, mid, cnt_lo

        _, lo, hi, _, cnt_lo = lax.while_loop(
            bis_cond, bis_body, (jnp.int32(0), mn8, jnp.full(rep, POS_INF, F32), mx8, n_adm))
        thr_ref[...] = lo

        tied = cnt_lo > kp

        @pl.when(jnp.max(tied.astype(I32)) > 0)
        def _():
            n_keys = nchunks * ck
            sub = lax.broadcasted_iota(I32, (ck // 8, 8, tq), 0) * 8 + lax.broadcasted_iota(I32, (ck // 8, 8, tq), 1)

            def count_kept(j_last):
                acc = jnp.zeros(rep, I32)
                for c in range(nchunks):
                    s3 = sc_ref[c].reshape(ck // 8, 8, tq)
                    keep = (s3 >= hi[None]) | ((s3 >= lo[None]) & (sub + c * ck <= j_last[None]))
                    acc = acc + jnp.sum(keep.astype(I32), axis=0)
                for shift in (4, 2, 1):
                    acc = acc + pltpu.roll(acc, shift, 0)
                return acc

            def idx_step(_, carry):
                j_lo, j_hi = carry
                j_mid = j_lo + ((j_hi - j_lo) >> 1)
                ok = count_kept(j_mid) >= kp
                return jnp.where(ok, j_lo, j_mid), jnp.where(ok, j_mid, j_hi)

            _, j_hi = lax.fori_loop(0, max(1, (n_keys - 1).bit_length()), idx_step,
                                    (jnp.full(rep, -1, I32), jnp.full(rep, n_keys - 1, I32)))
            j_last = jnp.where(tied, j_hi, n_keys - 1)
            for c in range(nchunks):
                s3 = sc_ref[c].reshape(ck // 8, 8, tq)
                drop = (s3 >= lo[None]) & (s3 < hi[None]) & (sub + c * ck > j_last[None])
                sc_ref[c] = jnp.where(drop, NEG_INF, s3).reshape(ck, tq)

    for qv in range(nq):
        pl.when(qi == qv)(functools.partial(bisect, qv + 1))
    thr = thr_ref[0:1, :]

    @pl.when(qi == 0)
    def _():
        kf = kb_ref[...].astype(F32)
        ms = jnp.mean(kf * kf, axis=-1, keepdims=True)
        kn_ref[...] = (kf * lax.rsqrt(ms + EPS) * gk_ref[...]).astype(kn_ref.dtype)

    for h in range(H_B):
        qh = qb_ref[:, h * LANE:(h + 1) * LANE].astype(F32)
        ms = jnp.mean(qh * qh, axis=-1, keepdims=True)
        qn_ref[h * tq:(h + 1) * tq, :] = (qh * lax.rsqrt(ms + EPS) * gq_ref[...]).astype(qn_ref.dtype)

    def attend_chunk(c, near, exact):
        rows = pl.ds(pl.multiple_of(c * ck, ck), ck)
        vt = vt_ref[c]
        sel = sc_ref[c] >= thr
        s_all = _nt_dot(kn_ref[rows, :], qn_ref[...])
        for h in range(H_B):
            s = s_all[:, h * tq:(h + 1) * tq]
            if near is not None:
                s = s + bias_ref[h, near]
            if exact:
                s = jnp.where(sel, s, NEG_INF)
                m_old = m_ref[h]
                m_new = jnp.maximum(m_old, jnp.max(s, axis=0, keepdims=True))
                m_safe = jnp.where(m_new == NEG_INF, 0.0, m_new)
                p = jnp.exp2(s - m_safe)
                alpha = jnp.exp2(m_old - m_safe)
                l_ref[h] = alpha * l_ref[h] + jnp.sum(p, axis=0, keepdims=True)
                acc_ref[h] = alpha * acc_ref[h] + jnp.dot(vt, p.astype(MXU_DTYPE), preferred_element_type=F32)
                m_ref[h] = m_new
            else:
                p = jnp.exp2(jnp.where(sel, s - bound_ref[h], NEG_INF))
                l_ref[h] = l_ref[h] + jnp.sum(p, axis=0, keepdims=True)
                acc_ref[h] = acc_ref[h] + jnp.dot(vt, p.astype(MXU_DTYPE), preferred_element_type=F32)

    def attend(exact):
        if exact:
            m_ref[...] = jnp.full(m_ref.shape, NEG_INF, F32)
        l_ref[...] = jnp.zeros(l_ref.shape, F32)
        acc_ref[...] = jnp.zeros(acc_ref.shape, F32)

        def far_body(c, carry):
            attend_chunk(c, None, exact)
            return carry

        lax.fori_loop(0, jnp.maximum(qi - 1, 0), far_body, 0)

        @pl.when(qi >= 1)
        def _():
            attend_chunk(qi - 1, 0, exact)

        attend_chunk(qi, 1, exact)

    attend(False)
    l_min = l_ref[0]
    for h in range(1, H_B):
        l_min = jnp.minimum(l_min, l_ref[h])
    shift_ok = jnp.min(l_min) >= MIN_SHIFTED_SUM
    pl.when(jnp.logical_not(shift_ok))(functools.partial(attend, True))

    for h in range(H_B):
        o_t = acc_ref[h] * (1.0 / l_ref[h])
        g = gate_ref[:, h * LANE:(h + 1) * LANE].astype(F32)
        o_ref[:, h * LANE:(h + 1) * LANE] = (o_t.T * _silu(g)).astype(o_ref.dtype)


def _attn_b(proj, wt, vt, t5_bias, gq_b, gk_b, blk, *, b, l, tq):
    t = proj.shape[0]
    nq = l // tq
    topk = min(TOPK_MAX, l // 4)
    row = lambda bi, qi: bi * nq + qi
    return pl.pallas_call(
        functools.partial(_attn_b_kernel, tq=tq, nq=nq, topk=topk, max_iters=BISECT_MAX_CHECKS),
        out_shape=jax.ShapeDtypeStruct((t, WIDTH_B), MXU_DTYPE),
        grid=(b, nq),
        in_specs=[
            pl.BlockSpec(memory_space=pltpu.SMEM),
            pl.BlockSpec((tq, IDX_HEADS * IDX_DIM), lambda bi, qi: (row(bi, qi), blk["q_idx"])),
            pl.BlockSpec((l, LANE), lambda bi, qi: (bi, blk["k_idx"])),
            pl.BlockSpec((IDX_HEADS, tq), lambda bi, qi: (0, row(bi, qi))),
            pl.BlockSpec((tq, WIDTH_B), lambda bi, qi: (row(bi, qi), blk["q_b"])),
            pl.BlockSpec((l, LANE), lambda bi, qi: (bi, blk["k_b"])),
            pl.BlockSpec((None, nq, LANE, tq), lambda bi, qi: (bi, 0, 0, 0)),
            pl.BlockSpec((tq, WIDTH_B), lambda bi, qi: (row(bi, qi), blk["gate_b"])),
            pl.BlockSpec((1, LANE), lambda bi, qi: (0, 0)),
            pl.BlockSpec((1, LANE), lambda bi, qi: (0, 0)),
        ],
        out_specs=pl.BlockSpec((tq, WIDTH_B), lambda bi, qi: (row(bi, qi), 0)),
        scratch_shapes=[
            pltpu.VMEM((nq, tq, tq), F32),
            pltpu.VMEM((H_B * tq, LANE), MXU_DTYPE),
            pltpu.VMEM((H_B, LANE, tq), F32),
            pltpu.VMEM((H_B, 1, tq), F32),
            pltpu.VMEM((H_B, 1, tq), F32),
            pltpu.VMEM((H_B, 2, tq, tq), F32),
            pltpu.VMEM((8, tq), F32),
            pltpu.VMEM((l, LANE), MXU_DTYPE),
            pltpu.SMEM((H_B,), F32),
        ],
        compiler_params=_params(("arbitrary", "arbitrary")),
        name="attn_b",
    )(t5_bias, proj, proj, wt, proj, proj, vt, proj, gq_b, gk_b)


def _sigmoid(z):
    return 1.0 / (1.0 + jnp.exp(-z))


def _merge_kernel(oa_ref, ob_ref, pa_ref, pb_ref, ma_ref, mb_ref, o_ref):
    a = jnp.dot(oa_ref[...], pa_ref[...], preferred_element_type=F32)
    bb = jnp.dot(ob_ref[...], pb_ref[...], preferred_element_type=F32)
    o_ref[...] = (_sigmoid(ma_ref[...].astype(F32)) * a + _sigmoid(mb_ref[...].astype(F32)) * bb).astype(o_ref.dtype)


def _merge(o_a, o_b, p_a, p_b, proj, ma_off, mb_off, *, tm, tn):
    t = o_a.shape[0]
    d = p_a.shape[1]
    ma0, mb0 = ma_off // tn, mb_off // tn
    return pl.pallas_call(
        _merge_kernel,
        out_shape=jax.ShapeDtypeStruct((t, d), MXU_DTYPE),
        grid=(t // tm, d // tn),
        in_specs=[
            pl.BlockSpec((tm, WIDTH_A), lambda i, j: (i, 0)),
            pl.BlockSpec((tm, WIDTH_B), lambda i, j: (i, 0)),
            pl.BlockSpec((WIDTH_A, tn), lambda i, j: (0, j)),
            pl.BlockSpec((WIDTH_B, tn), lambda i, j: (0, j)),
            pl.BlockSpec((tm, tn), lambda i, j: (i, ma0 + j)),
            pl.BlockSpec((tm, tn), lambda i, j: (i, mb0 + j)),
        ],
        out_specs=pl.BlockSpec((tm, tn), lambda i, j: (i, j)),
        compiler_params=_params(("arbitrary", "arbitrary")),
        name="merge",
    )(o_a, o_b, p_a, p_b, proj, proj)


def _out_proj_kernel(m_ref, w_ref, x_ref, o_ref):
    o_ref[...] = x_ref[...] + jnp.dot(m_ref[...], w_ref[...], preferred_element_type=F32)


def _out_proj(merged, w_o, x2, *, tm, tn):
    t, d = x2.shape
    return pl.pallas_call(
        _out_proj_kernel,
        out_shape=jax.ShapeDtypeStruct((t, d), x2.dtype),
        grid=(t // tm, d // tn),
        in_specs=[
            pl.BlockSpec((tm, d), lambda i, j: (i, 0)),
            pl.BlockSpec((d, tn), lambda i, j: (0, j)),
            pl.BlockSpec((tm, tn), lambda i, j: (i, j)),
        ],
        out_specs=pl.BlockSpec((tm, tn), lambda i, j: (i, j)),
        compiler_params=_params(("arbitrary", "arbitrary")),
        name="out_proj",
    )(merged, w_o, x2)


def _rope_pad(a, axis):
    a1, a2 = jnp.split(a, 2, axis=axis)
    z = jnp.zeros_like(a1)
    return jnp.concatenate([a1, z, a2, z], axis=axis)


def _layout(d):
    names = [("merge_a", d), ("merge_b", d), ("gate_a", WIDTH_A), ("gate_b", WIDTH_B),
             ("q_b", WIDTH_B), ("q_idx", IDX_HEADS * IDX_DIM), ("cq", Q_LORA), ("ckv", KV_LORA),
             ("k_idx", LANE), ("k_rope", LANE), ("k_b", LANE), ("v_b", LANE)]
    off, out = 0, {}
    for name, width in names:
        assert off % width == 0, (name, off, width)
        out[name] = off
        off += width
    out["w_idx"] = out["k_idx"] + IDX_DIM
    return out, off


def kernel(x, positions, g_pre, w_in, g_q_lat, g_kv_lat, w_uq, w_ukv, g_qn_a, g_kn_a,
           g_qn_b, g_kn_b, t5_bias, p_a, p_b, w_o):
    b, l, d = x.shape
    t = b * l
    tq = 256
    tn_in = 512
    off, n_used = _layout(d)
    n_pad = -(-n_used // tn_in) * tn_in

    names = ["cq", "ckv", "k_rope", "q_b", "k_b", "v_b", "q_idx", "k_idx", "w_idx", "gate_a", "gate_b",
             "merge_a", "merge_b"]
    sizes = [Q_LORA, KV_LORA, QK_ROPE, WIDTH_B, HEAD_DIM_B, HEAD_DIM_B, IDX_HEADS * IDX_DIM, IDX_DIM,
             IDX_HEADS, WIDTH_A, WIDTH_B, d, d]
    src, acc = {}, 0
    for name, s in zip(names, sizes):
        src[name] = (acc, s)
        acc += s
    main_groups = ["merge_a", "merge_b", "gate_a", "gate_b", "q_b", "q_idx", "cq", "ckv"]
    src_starts, dst = [], 0
    for name in main_groups:
        assert off[name] == dst and src[name][1] % tn_in == 0
        src_starts += [src[name][0] + c for c in range(0, src[name][1], tn_in)]
        dst += src[name][1]
    wt = w_in.T
    rows = lambda name: wt[src[name][0]:src[name][0] + src[name][1]]
    z = lambda n: jnp.zeros((n, d), w_in.dtype)
    assert off["k_idx"] == dst
    wt_tail = jnp.concatenate(
        [rows("k_idx"), rows("w_idx"), z(LANE - IDX_DIM - IDX_HEADS), _rope_pad(rows("k_rope"), 0),
         rows("k_b"), rows("v_b"), z(n_pad - n_used)], axis=0)
    w_pad = _w_relayout(wt, src_starts, wt_tail, tn=tn_in, tc=d)

    w_uq3 = w_uq.reshape(Q_LORA, H_A, QK_DIM_A)
    w_uq_pad = jnp.concatenate([w_uq3[:, :, :QK_NOPE], _rope_pad(w_uq3[:, :, QK_NOPE:], 2)], axis=2)
    w_uq_pad = w_uq_pad.reshape(Q_LORA, H_A * HEAD_PAD_A).astype(MXU_DTYPE)
    gq_head = jnp.concatenate([g_qn_a[:QK_NOPE], _rope_pad(g_qn_a[QK_NOPE:], 0)]) * (QK_DIM_A ** -0.5 * LOG2E)
    gq_pad = jnp.tile(gq_head, H_A).reshape(1, H_A * HEAD_PAD_A).astype(F32)
    bound_a = QK_DIM_A * jnp.max(jnp.abs(gq_head)) * jnp.max(jnp.abs(g_kn_a)) * BOUND_MARGIN
    q_shift = jnp.zeros((1, LANE), F32).at[0, SHIFT_LANE - LANE].set(-bound_a)
    w_ukv3 = w_ukv.reshape(KV_LORA, H_A, QK_NOPE + V_DIM_A)
    w_uk = w_ukv3[:, :, :QK_NOPE].reshape(KV_LORA, H_A * QK_NOPE).astype(MXU_DTYPE)
    w_uv = w_ukv3[:, :, QK_NOPE:].reshape(KV_LORA, WIDTH_A).astype(MXU_DTYPE)
    gk_nope = g_kn_a[:QK_NOPE].reshape(1, LANE).astype(F32)
    gk_rope = _rope_pad(g_kn_a[QK_NOPE:], 0).reshape(1, LANE).astype(F32)
    gq_b = (g_qn_b * (HEAD_DIM_B ** -0.5 * LOG2E)).reshape(1, LANE).astype(F32)
    gk_b = g_kn_b.reshape(1, LANE).astype(F32)

    inv = ROPE_THETA ** (-jnp.arange(HALF_ROPE, dtype=F32) / HALF_ROPE)
    ang = positions.reshape(t, 1).astype(F32) * inv
    cos, sin = jnp.cos(ang), jnp.sin(ang)
    zr = jnp.zeros_like(cos)
    cos_t = jnp.concatenate([cos, zr, cos, zr], axis=1)
    sin_t = jnp.concatenate([-sin, zr, sin, zr], axis=1)

    x2 = x.reshape(t, d)
    proj = _in_proj(x2, g_pre, w_pad, tm=1024, tn=tn_in)

    q_a = _qa_proj(proj, off["cq"] // Q_LORA, g_q_lat, w_uq_pad, gq_pad, cos_t, sin_t, q_shift, tm=256)
    k_a, v_a = _kva_proj(proj, off["ckv"] // KV_LORA, off["k_rope"] // LANE, g_kv_lat, w_uk, w_uv,
                         gk_nope, gk_rope, cos_t, sin_t, tm=256)
    o_a = _attn_a(q_a, k_a, v_a, proj, off["gate_a"] // LANE, b=b, l=l, tq=tq, heads=4)

    nq = l // tq
    w_idx = proj[:, off["w_idx"]:off["w_idx"] + IDX_HEADS]
    wt = w_idx.astype(F32).T
    v_b = proj[:, off["v_b"]:off["v_b"] + HEAD_DIM_B]
    vt = v_b.reshape(b, nq, tq, HEAD_DIM_B).transpose(0, 1, 3, 2)
    blk = {"q_idx": off["q_idx"] // (IDX_HEADS * IDX_DIM), "k_idx": off["k_idx"] // LANE,
           "q_b": off["q_b"] // WIDTH_B, "k_b": off["k_b"] // LANE, "gate_b": off["gate_b"] // WIDTH_B}
    o_b = _attn_b(proj, wt, vt, t5_bias.astype(F32), gq_b, gk_b, blk, b=b, l=l, tq=tq)

    merged = _merge(o_a, o_b, p_a.astype(MXU_DTYPE), p_b.astype(MXU_DTYPE), proj,
                    off["merge_a"], off["merge_b"], tm=1024, tn=1024)
    out = _out_proj(merged, w_o.astype(MXU_DTYPE), x2, tm=1024, tn=1024)
    return out.reshape(b, l, d)
```

```python
import functools
import math

import jax
import jax.numpy as jnp
from jax import lax
from jax.experimental import pallas as pl
from jax.experimental.pallas import tpu as pltpu

F32 = jnp.float32
I32 = jnp.int32
MXU_DTYPE = jnp.bfloat16

H_A = 16
QK_NOPE = 128
QK_ROPE = 64
QK_DIM_A = QK_NOPE + QK_ROPE
V_DIM_A = 128
Q_LORA = 1024
KV_LORA = 512
ROPE_THETA = 10000.0
H_B = 16
HEAD_DIM_B = 128
IDX_HEADS = 32
IDX_DIM = 64
TOPK_MAX = 256
N_BUCKETS = 32
MAX_DISTANCE = 128
EPS = 1e-6
WIDTH_A = H_A * V_DIM_A
WIDTH_B = H_B * HEAD_DIM_B

LANE = 128
ROW_ALIGN = 32
HALF_ROPE = QK_ROPE // 2
HEAD_PAD_A = 2 * LANE
SHIFT_LANE = LANE + HALF_ROPE
BOUND_MARGIN = 1.0 + 2.0 ** -6
MIN_SHIFTED_SUM = 2.0 ** -64
VMEM_CAP = 56 * 1024 * 1024
VMEM_HEADROOM = 4 * 1024 * 1024

BISECT_STEPS_PER_CHECK = 4
BISECT_MAX_CHECKS = 40
LOG2E = math.log2(math.e)
NEG_INF = float("-inf")
POS_INF = float("inf")


def _nt_dot(a, b):
    return lax.dot_general(a, b, (((1,), (1,)), ((), ())), preferred_element_type=F32)


def _nbytes(shape, dtype):
    return math.prod(shape) * jnp.dtype(dtype).itemsize


def _params(sem, windows, scratch=0, temps=0):
    need = 2 * sum(windows) + scratch + temps + VMEM_HEADROOM
    return pltpu.CompilerParams(dimension_semantics=sem, vmem_limit_bytes=min(need, VMEM_CAP))


def _w_relayout_kernel(start_ref, src_ref, tail_ref, o_ref, *, n_main):
    j = pl.program_id(0)

    @pl.when(j < n_main)
    def _():
        o_ref[...] = src_ref[...].astype(o_ref.dtype)

    @pl.when(j >= n_main)
    def _():
        o_ref[...] = tail_ref[...].astype(o_ref.dtype)


def _w_relayout(wt, src_starts, wt_tail, *, tn, tc):
    d = wt.shape[1]
    n_main, n_tail = len(src_starts), wt_tail.shape[0] // tn
    assert all(s % ROW_ALIGN == 0 for s in src_starts)
    starts = jnp.array([s // ROW_ALIGN for s in src_starts] + [0] * n_tail, I32)
    return pl.pallas_call(
        functools.partial(_w_relayout_kernel, n_main=n_main),
        out_shape=jax.ShapeDtypeStruct(((n_main + n_tail) * tn, d), MXU_DTYPE),
        grid_spec=pltpu.PrefetchScalarGridSpec(
            num_scalar_prefetch=1,
            grid=(n_main + n_tail, d // tc),
            in_specs=[
                pl.BlockSpec((pl.Element(tn), pl.Element(tc)), lambda j, c, st: (st[j] * ROW_ALIGN, c * tc)),
                pl.BlockSpec((tn, tc), lambda j, c, st: (jnp.maximum(j - n_main, 0), c)),
            ],
            out_specs=pl.BlockSpec((tn, tc), lambda j, c, st: (j, c)),
        ),
        compiler_params=_params(("arbitrary", "arbitrary"),
                                [_nbytes((tn, tc), wt.dtype), _nbytes((tn, tc), wt_tail.dtype),
                                 _nbytes((tn, tc), MXU_DTYPE)]),
        name="w_relayout",
    )(starts, wt, wt_tail)


def _in_proj_kernel(x_ref, g_ref, w_ref, o_ref, hn_ref, *, row_chunk):
    tm = x_ref.shape[0]

    @pl.when(pl.program_id(1) == 0)
    def _():
        def body(r, carry):
            sl = pl.ds(pl.multiple_of(r * row_chunk, row_chunk), row_chunk)
            xx = x_ref[sl, :]
            ms = jnp.mean(xx * xx, axis=-1, keepdims=True)
            hn_ref[sl, :] = (xx * lax.rsqrt(ms + EPS) * g_ref[...]).astype(hn_ref.dtype)
            return carry

        lax.fori_loop(0, tm // row_chunk, body, 0)

    o_ref[...] = _nt_dot(hn_ref[...], w_ref[...]).astype(o_ref.dtype)


def _in_proj(x2, g_pre, wt_pad, *, tm, tn):
    t, d = x2.shape
    n = wt_pad.shape[0]
    return pl.pallas_call(
        functools.partial(_in_proj_kernel, row_chunk=64),
        out_shape=jax.ShapeDtypeStruct((t, n), MXU_DTYPE),
        grid=(t // tm, n // tn),
        in_specs=[
            pl.BlockSpec((tm, d), lambda i, j: (i, 0)),
            pl.BlockSpec((1, d), lambda i, j: (0, 0)),
            pl.BlockSpec((tn, d), lambda i, j: (j, 0)),
        ],
        out_specs=pl.BlockSpec((tm, tn), lambda i, j: (i, j)),
        scratch_shapes=[pltpu.VMEM((tm, d), MXU_DTYPE)],
        compiler_params=_params(("arbitrary", "arbitrary"),
                                [_nbytes((tm, d), x2.dtype), _nbytes((1, d), F32), _nbytes((tn, d), MXU_DTYPE),
                                 _nbytes((tm, tn), MXU_DTYPE)],
                                scratch=_nbytes((tm, d), MXU_DTYPE)),
        name="in_proj",
    )(x2, g_pre.reshape(1, d), wt_pad)


def _rope_lanes(r, cos_ref, sin_ref):
    return r * cos_ref[...] + pltpu.roll(r, 2 * HALF_ROPE, 1) * sin_ref[...]


def _qa_proj_kernel(cq_ref, gl_ref, w_ref, gq_ref, cos_ref, sin_ref, shift_ref, o_ref):
    c = cq_ref[...].astype(F32)
    ms = jnp.mean(c * c, axis=-1, keepdims=True)
    cn = (c * lax.rsqrt(ms + EPS) * gl_ref[...]).astype(MXU_DTYPE)
    q = jnp.dot(cn, w_ref[...], preferred_element_type=F32)
    for h in range(H_A):
        lo = h * HEAD_PAD_A
        qh = q[:, lo:lo + HEAD_PAD_A]
        ss = jnp.sum(qh * qh, axis=-1, keepdims=True) * (1.0 / QK_DIM_A)
        qn = qh * lax.rsqrt(ss + EPS) * gq_ref[:, lo:lo + HEAD_PAD_A]
        o_ref[:, lo:lo + LANE] = qn[:, :LANE].astype(o_ref.dtype)
        o_ref[:, lo + LANE:lo + HEAD_PAD_A] = (
            _rope_lanes(qn[:, LANE:], cos_ref, sin_ref) + shift_ref[...]).astype(o_ref.dtype)


def _qa_proj(proj, cq_blk, g_q_lat, w_uq_pad, gq_pad, cos_t, sin_t, q_shift, *, tm):
    t = proj.shape[0]
    nq = H_A * HEAD_PAD_A
    return pl.pallas_call(
        _qa_proj_kernel,
        out_shape=jax.ShapeDtypeStruct((t, nq), MXU_DTYPE),
        grid=(t // tm,),
        in_specs=[
            pl.BlockSpec((tm, Q_LORA), lambda i: (i, cq_blk)),
            pl.BlockSpec((1, Q_LORA), lambda i: (0, 0)),
            pl.BlockSpec((Q_LORA, nq), lambda i: (0, 0)),
            pl.BlockSpec((1, nq), lambda i: (0, 0)),
            pl.BlockSpec((tm, LANE), lambda i: (i, 0)),
            pl.BlockSpec((tm, LANE), lambda i: (i, 0)),
            pl.BlockSpec((1, LANE), lambda i: (0, 0)),
        ],
        out_specs=pl.BlockSpec((tm, nq), lambda i: (i, 0)),
        compiler_params=_params(("arbitrary",),
                                [_nbytes((tm, Q_LORA), MXU_DTYPE), _nbytes((Q_LORA, nq), MXU_DTYPE),
                                 _nbytes((1, Q_LORA + nq + LANE), F32), 2 * _nbytes((tm, LANE), F32),
                                 _nbytes((tm, nq), MXU_DTYPE)],
                                temps=_nbytes((tm, nq), F32)),
        name="qa_proj",
    )(proj, g_q_lat.reshape(1, Q_LORA), w_uq_pad, gq_pad, cos_t, sin_t, q_shift)


def _kva_proj_kernel(ckv_ref, kr_ref, gl_ref, wk_ref, wv_ref, gkn_ref, gkr_ref, cos_ref, sin_ref,
                     k_ref, v_ref):
    c = ckv_ref[...].astype(F32)
    ms = jnp.mean(c * c, axis=-1, keepdims=True)
    cn = (c * lax.rsqrt(ms + EPS) * gl_ref[...]).astype(MXU_DTYPE)
    kn = jnp.dot(cn, wk_ref[...], preferred_element_type=F32)
    v_ref[...] = jnp.dot(cn, wv_ref[...], preferred_element_type=F32).astype(v_ref.dtype)
    kr = kr_ref[...].astype(F32)
    ss_r = jnp.sum(kr * kr, axis=-1, keepdims=True)
    krr = _rope_lanes(kr * gkr_ref[...], cos_ref, sin_ref)
    shift_one = (lax.broadcasted_iota(I32, (1, LANE), 1) == SHIFT_LANE - LANE).astype(F32)
    for h in range(H_A):
        kh = kn[:, h * LANE:(h + 1) * LANE]
        ss = (jnp.sum(kh * kh, axis=-1, keepdims=True) + ss_r) * (1.0 / QK_DIM_A)
        rs = lax.rsqrt(ss + EPS)
        lo = h * HEAD_PAD_A
        k_ref[:, lo:lo + LANE] = (kh * rs * gkn_ref[...]).astype(k_ref.dtype)
        k_ref[:, lo + LANE:lo + HEAD_PAD_A] = (krr * rs + shift_one).astype(k_ref.dtype)


def _kva_proj(proj, ckv_blk, krope_blk, g_kv_lat, w_uk, w_uv, gk_nope, gk_rope, cos_t, sin_t, *, tm):
    t = proj.shape[0]
    return pl.pallas_call(
        _kva_proj_kernel,
        out_shape=(jax.ShapeDtypeStruct((t, H_A * HEAD_PAD_A), MXU_DTYPE),
                   jax.ShapeDtypeStruct((t, WIDTH_A), MXU_DTYPE)),
        grid=(t // tm,),
        in_specs=[
            pl.BlockSpec((tm, KV_LORA), lambda i: (i, ckv_blk)),
            pl.BlockSpec((tm, LANE), lambda i: (i, krope_blk)),
            pl.BlockSpec((1, KV_LORA), lambda i: (0, 0)),
            pl.BlockSpec((KV_LORA, H_A * QK_NOPE), lambda i: (0, 0)),
            pl.BlockSpec((KV_LORA, WIDTH_A), lambda i: (0, 0)),
            pl.BlockSpec((1, LANE), lambda i: (0, 0)),
            pl.BlockSpec((1, LANE), lambda i: (0, 0)),
            pl.BlockSpec((tm, LANE), lambda i: (i, 0)),
            pl.BlockSpec((tm, LANE), lambda i: (i, 0)),
        ],
        out_specs=(pl.BlockSpec((tm, H_A * HEAD_PAD_A), lambda i: (i, 0)),
                   pl.BlockSpec((tm, WIDTH_A), lambda i: (i, 0))),
        compiler_params=_params(("arbitrary",),
                                [_nbytes((tm, KV_LORA + LANE), MXU_DTYPE), 2 * _nbytes((KV_LORA, WIDTH_A), MXU_DTYPE),
                                 _nbytes((1, KV_LORA + 2 * LANE), F32), 2 * _nbytes((tm, LANE), F32),
                                 _nbytes((tm, H_A * HEAD_PAD_A + WIDTH_A), MXU_DTYPE)],
                                temps=2 * _nbytes((tm, WIDTH_A), F32)),
        name="kva_proj",
    )(proj, proj, g_kv_lat.reshape(1, KV_LORA), w_uk, w_uv, gk_nope, gk_rope, cos_t, sin_t)


def _silu(g):
    return g * (1.0 / (1.0 + jnp.exp(-g)))


def _lane_tile_reduce(x, op):
    acc = x[:, :LANE]
    for t in range(1, x.shape[1] // LANE):
        acc = op(acc, x[:, t * LANE:(t + 1) * LANE])
    return acc


def _attn_a_kernel(q_ref, k_ref, v_ref, gate_ref, o_ref, *, tq, nq, heads):
    qi = pl.program_id(2)
    causal = lax.broadcasted_iota(I32, (tq, tq), 0) >= lax.broadcasted_iota(I32, (tq, tq), 1)

    def finish(g, l_t, acc):
        vc = slice(g * V_DIM_A, (g + 1) * V_DIM_A)
        l = jnp.sum(l_t, axis=-1, keepdims=True)
        o = acc * (1.0 / l)
        o_ref[:, vc] = (o * _silu(gate_ref[:, vc].astype(F32))).astype(o_ref.dtype)

    def branch(qv):
        n_off = qv * tq
        kcs = [slice(g * HEAD_PAD_A, (g + 1) * HEAD_PAD_A) for g in range(heads)]
        vcs = [slice(g * V_DIM_A, (g + 1) * V_DIM_A) for g in range(heads)]

        l_min = jnp.full((tq, 1), POS_INF, F32)
        for g in range(heads):
            q = q_ref[:, kcs[g]]
            p_diag = jnp.exp2(jnp.where(causal, _nt_dot(q, k_ref[n_off:n_off + tq, kcs[g]]), NEG_INF))
            l_t = _lane_tile_reduce(p_diag, jnp.add)
            acc = jnp.dot(p_diag.astype(MXU_DTYPE), v_ref[n_off:n_off + tq, vcs[g]], preferred_element_type=F32)
            if qv > 0:
                p_off = jnp.exp2(_nt_dot(q, k_ref[0:n_off, kcs[g]]))
                l_t = l_t + _lane_tile_reduce(p_off, jnp.add)
                acc = acc + jnp.dot(p_off.astype(MXU_DTYPE), v_ref[0:n_off, vcs[g]], preferred_element_type=F32)
            l_min = jnp.minimum(l_min, jnp.sum(l_t, axis=-1, keepdims=True))
            finish(g, l_t, acc)

        shift_ok = jnp.min(l_min) >= MIN_SHIFTED_SUM

        @pl.when(jnp.logical_not(shift_ok))
        def _():
            for g in range(heads):
                q = q_ref[:, kcs[g]]
                q_hi = q[:, LANE:]
                lane = lax.broadcasted_iota(I32, q_hi.shape, 1)
                q = jnp.concatenate(
                    [q[:, :LANE], jnp.where(lane == SHIFT_LANE - LANE, 0.0, q_hi.astype(F32)).astype(q.dtype)], axis=1)
                s_diag = jnp.where(causal, _nt_dot(q, k_ref[n_off:n_off + tq, kcs[g]]), NEG_INF)
                m_t = _lane_tile_reduce(s_diag, jnp.maximum)
                if qv > 0:
                    s_off = _nt_dot(q, k_ref[0:n_off, kcs[g]])
                    m_t = jnp.maximum(m_t, _lane_tile_reduce(s_off, jnp.maximum))
                m = jnp.max(m_t, axis=-1, keepdims=True)
                p_diag = jnp.exp2(s_diag - m)
                l_t = _lane_tile_reduce(p_diag, jnp.add)
                acc = jnp.dot(p_diag.astype(MXU_DTYPE), v_ref[n_off:n_off + tq, vcs[g]],
                              preferred_element_type=F32)
                if qv > 0:
                    p_off = jnp.exp2(s_off - m)
                    l_t = l_t + _lane_tile_reduce(p_off, jnp.add)
                    acc = acc + jnp.dot(p_off.astype(MXU_DTYPE), v_ref[0:n_off, vcs[g]], preferred_element_type=F32)
                finish(g, l_t, acc)

    for qv in range(nq):
        pl.when(qi == qv)(functools.partial(branch, qv))


def _attn_a(q_a, k_a, v_a, proj, gate_blk0, *, b, l, tq, heads):
    t = q_a.shape[0]
    nq = l // tq
    kw, vw = heads * HEAD_PAD_A, heads * V_DIM_A
    return pl.pallas_call(
        functools.partial(_attn_a_kernel, tq=tq, nq=nq, heads=heads),
        out_shape=jax.ShapeDtypeStruct((t, WIDTH_A), MXU_DTYPE),
        grid=(b, H_A // heads, nq),
        in_specs=[
            pl.BlockSpec((tq, kw), lambda bi, h, qi: (bi * nq + qi, h)),
            pl.BlockSpec((l, kw), lambda bi, h, qi: (bi, h)),
            pl.BlockSpec((l, vw), lambda bi, h, qi: (bi, h)),
            pl.BlockSpec((tq, vw), lambda bi, h, qi: (bi * nq + qi, gate_blk0 // heads + h)),
        ],
        out_specs=pl.BlockSpec((tq, vw), lambda bi, h, qi: (bi * nq + qi, h)),
        compiler_params=_params(("arbitrary", "arbitrary", "arbitrary"),
                                [_nbytes((tq + l, kw), MXU_DTYPE), _nbytes((l + 2 * tq, vw), MXU_DTYPE)],
                                temps=heads * (_nbytes((tq, l), F32) + _nbytes((tq, l), MXU_DTYPE))),
        name="attn_a",
    )(q_a, k_a, v_a, proj)


def _t5_bucket(dist):
    max_exact = N_BUCKETS // 2
    n = jnp.maximum(dist, 0)
    nf = jnp.maximum(n, 1).astype(F32)
    large = max_exact + (jnp.log(nf / max_exact) / math.log(MAX_DISTANCE / max_exact)
                         * (N_BUCKETS - max_exact)).astype(I32)
    large = jnp.minimum(large, N_BUCKETS - 1)
    return jnp.where(n < max_exact, n, large)


def _attn_b_kernel(t5_ref, qidx_ref, kidx_ref, wt_ref, qb_ref, kb_ref, vt_ref, gate_ref, gq_ref, gk_ref,
                   o_ref, sc_ref, qn_ref, acc_ref, m_ref, l_ref, bias_ref, thr_ref, kn_ref, bound_ref,
                   *, tq, nq, topk, max_iters):
    bi = pl.program_id(0)
    qi = pl.program_id(1)
    ck = tq
    shape = (ck, tq)

    @pl.when((bi == 0) & (qi == 0))
    def _():
        s_loc = lax.broadcasted_iota(I32, shape, 0)
        t_loc = lax.broadcasted_iota(I32, shape, 1)
        for near in range(2):
            bucket = _t5_bucket(t_loc - s_loc + (1 - near) * ck)

            def per_head(h, carry, bucket=bucket, near=near):
                far = t5_ref[N_BUCKETS - 1, h]
                tab = jnp.zeros(shape, F32)
                largest = jnp.float32(0.0)
                for bk in range(N_BUCKETS - 1):
                    rel = (t5_ref[bk, h] - far) * LOG2E
                    tab = jnp.where(bucket == bk, rel, tab)
                    largest = jnp.maximum(largest, rel)
                bias_ref[h, near] = tab
                gains = jnp.max(jnp.abs(gq_ref[...])) * jnp.max(jnp.abs(gk_ref[...]))
                bound_ref[h] = HEAD_DIM_B * gains * BOUND_MARGIN + largest
                return carry

            lax.fori_loop(0, H_B, per_head, 0)

    w_all = wt_ref[...] * (IDX_HEADS ** -0.5)

    def score_chunk(c, diag):
        rows = pl.ds(pl.multiple_of(c * ck, ck), ck)
        kx = kidx_ref[rows, 0:IDX_DIM]
        zk = jnp.zeros_like(kx)
        kab = jnp.concatenate([jnp.concatenate([kx, zk], axis=1), jnp.concatenate([zk, kx], axis=1)], axis=0)
        score = jnp.zeros(shape, F32)
        for j in range(IDX_HEADS // 2):
            qp = qidx_ref[:, j * LANE:(j + 1) * LANE]
            logits = jnp.maximum(_nt_dot(kab, qp), 0.0)
            score = score + logits[:ck] * w_all[2 * j:2 * j + 1, :]
            score = score + logits[ck:] * w_all[2 * j + 1:2 * j + 2, :]
        if diag:
            adm = lax.broadcasted_iota(I32, shape, 0) <= lax.broadcasted_iota(I32, shape, 1)
            lo_src = jnp.where(adm, score, POS_INF)
            score = jnp.where(adm, score, NEG_INF)
        else:
            lo_src = score
        sc_ref[c] = score
        return jnp.max(score, axis=0, keepdims=True), jnp.min(lo_src, axis=0, keepdims=True)

    def score_body(c, carry):
        mx, mn = carry
        cmx, cmn = score_chunk(c, False)
        return jnp.maximum(mx, cmx), jnp.minimum(mn, cmn)

    mx, mn = lax.fori_loop(0, qi, score_body,
                           (jnp.full((1, tq), NEG_INF, F32), jnp.full((1, tq), POS_INF, F32)))
    dmx, dmn = score_chunk(qi, True)
    mx = jnp.maximum(mx, dmx)
    mn = jnp.minimum(mn, dmn)

    rep = (8, tq)
    n_adm = qi * tq + lax.broadcasted_iota(I32, rep, 1) + 1
    kp = jnp.minimum(n_adm, topk)
    mx8 = jnp.broadcast_to(mx, rep)
    mn8 = jnp.broadcast_to(mn, rep)

    def bisect(nchunks):
        def count_ge(x):
            acc = jnp.zeros(rep, I32)
            for c in range(nchunks):
                ge = sc_ref[c].reshape(ck // 8, 8, tq) >= x[None]
                acc = acc + jnp.sum(ge.astype(I32), axis=0)
            for shift in (4, 2, 1):
                acc = acc + pltpu.roll(acc, shift, 0)
            return acc

        def bis_cond(st):
            it, lo, hi, mid, cnt_lo = st
            active = (cnt_lo != kp) & (mid > lo) & (mid < hi)
            return jnp.logical_and(it < max_iters, jnp.max(active.astype(I32)) > 0)

        def bis_body(st):
            it, lo, hi, mid, cnt_lo = st
            for _ in range(BISECT_STEPS_PER_CHECK):
                cnt = count_ge(mid)
                ge = cnt >= kp
                lo = jnp.where(ge, mid, lo)
                cnt_lo = jnp.where(ge, cnt, cnt_lo)
                hi = jnp.where(ge, hi, mid)
                mid = jnp.where(hi == POS_INF, mx8, lo + 0.5 * (hi - lo))
            return it + 1, lo, hi, mid, cnt_lo

        _, lo, hi, _, cnt_lo = lax.while_loop(
            bis_cond, bis_body, (jnp.int32(0), mn8, jnp.full(rep, POS_INF, F32), mx8, n_adm))
        thr_ref[...] = lo

        tied = cnt_lo > kp

        @pl.when(jnp.max(tied.astype(I32)) > 0)
        def _():
            n_keys = nchunks * ck
            sub = lax.broadcasted_iota(I32, (ck // 8, 8, tq), 0) * 8 + lax.broadcasted_iota(I32, (ck // 8, 8, tq), 1)

            def count_kept(j_last):
                acc = jnp.zeros(rep, I32)
                for c in range(nchunks):
                    s3 = sc_ref[c].reshape(ck // 8, 8, tq)
                    keep = (s3 >= hi[None]) | ((s3 >= lo[None]) & (sub + c * ck <= j_last[None]))
                    acc = acc + jnp.sum(keep.astype(I32), axis=0)
                for shift in (4, 2, 1):
                    acc = acc + pltpu.roll(acc, shift, 0)
                return acc

            def idx_step(_, carry):
                j_lo, j_hi = carry
                j_mid = j_lo + ((j_hi - j_lo) >> 1)
                ok = count_kept(j_mid) >= kp
                return jnp.where(ok, j_lo, j_mid), jnp.where(ok, j_mid, j_hi)

            _, j_hi = lax.fori_loop(0, max(1, (n_keys - 1).bit_length()), idx_step,
                                    (jnp.full(rep, -1, I32), jnp.full(rep, n_keys - 1, I32)))
            j_last = jnp.where(tied, j_hi, n_keys - 1)
            for c in range(nchunks):
                s3 = sc_ref[c].reshape(ck // 8, 8, tq)
                drop = (s3 >= lo[None]) & (s3 < hi[None]) & (sub + c * ck > j_last[None])
                sc_ref[c] = jnp.where(drop, NEG_INF, s3).reshape(ck, tq)

    for qv in range(nq):
        pl.when(qi == qv)(functools.partial(bisect, qv + 1))
    thr = thr_ref[0:1, :]

    @pl.when(qi == 0)
    def _():
        kf = kb_ref[...].astype(F32)
        ms = jnp.mean(kf * kf, axis=-1, keepdims=True)
        kn_ref[...] = (kf * lax.rsqrt(ms + EPS) * gk_ref[...]).astype(kn_ref.dtype)

    for h in range(H_B):
        qh = qb_ref[:, h * LANE:(h + 1) * LANE].astype(F32)
        ms = jnp.mean(qh * qh, axis=-1, keepdims=True)
        qn_ref[h * tq:(h + 1) * tq, :] = (qh * lax.rsqrt(ms + EPS) * gq_ref[...]).astype(qn_ref.dtype)

    def attend_chunk(c, near, exact):
        rows = pl.ds(pl.multiple_of(c * ck, ck), ck)
        vt = vt_ref[c]
        sel = sc_ref[c] >= thr
        s_all = _nt_dot(kn_ref[rows, :], qn_ref[...])
        for h in range(H_B):
            s = s_all[:, h * tq:(h + 1) * tq]
            if near is not None:
                s = s + bias_ref[h, near]
            if exact:
                s = jnp.where(sel, s, NEG_INF)
                m_old = m_ref[h]
                m_new = jnp.maximum(m_old, jnp.max(s, axis=0, keepdims=True))
                m_safe = jnp.where(m_new == NEG_INF, 0.0, m_new)
                p = jnp.exp2(s - m_safe)
                alpha = jnp.exp2(m_old - m_safe)
                l_ref[h] = alpha * l_ref[h] + jnp.sum(p, axis=0, keepdims=True)
                acc_ref[h] = alpha * acc_ref[h] + jnp.dot(vt, p.astype(MXU_DTYPE), preferred_element_type=F32)
                m_ref[h] = m_new
            else:
                p = jnp.exp2(jnp.where(sel, s - bound_ref[h], NEG_INF))
                l_ref[h] = l_ref[h] + jnp.sum(p, axis=0, keepdims=True)
                acc_ref[h] = acc_ref[h] + jnp.dot(vt, p.astype(MXU_DTYPE), preferred_element_type=F32)

    def attend(exact):
        if exact:
            m_ref[...] = jnp.full(m_ref.shape, NEG_INF, F32)
        l_ref[...] = jnp.zeros(l_ref.shape, F32)
        acc_ref[...] = jnp.zeros(acc_ref.shape, F32)

        def far_body(c, carry):
            attend_chunk(c, None, exact)
            return carry

        lax.fori_loop(0, jnp.maximum(qi - 1, 0), far_body, 0)

        @pl.when(qi >= 1)
        def _():
            attend_chunk(qi - 1, 0, exact)

        attend_chunk(qi, 1, exact)

    attend(False)
    l_min = l_ref[0]
    for h in range(1, H_B):
        l_min = jnp.minimum(l_min, l_ref[h])
    shift_ok = jnp.min(l_min) >= MIN_SHIFTED_SUM
    pl.when(jnp.logical_not(shift_ok))(functools.partial(attend, True))

    for h in range(H_B):
        o_t = acc_ref[h] * (1.0 / l_ref[h])
        g = gate_ref[:, h * LANE:(h + 1) * LANE].astype(F32)
        o_ref[:, h * LANE:(h + 1) * LANE] = (o_t.T * _silu(g)).astype(o_ref.dtype)


def _attn_b(proj, wt, vt, t5_bias, gq_b, gk_b, blk, *, b, l, tq):
    t = proj.shape[0]
    nq = l // tq
    topk = min(TOPK_MAX, l // 4)
    row = lambda bi, qi: bi * nq + qi
    return pl.pallas_call(
        functools.partial(_attn_b_kernel, tq=tq, nq=nq, topk=topk, max_iters=BISECT_MAX_CHECKS),
        out_shape=jax.ShapeDtypeStruct((t, WIDTH_B), MXU_DTYPE),
        grid=(b, nq),
        in_specs=[
            pl.BlockSpec(memory_space=pltpu.SMEM),
            pl.BlockSpec((tq, IDX_HEADS * IDX_DIM), lambda bi, qi: (row(bi, qi), blk["q_idx"])),
            pl.BlockSpec((l, LANE), lambda bi, qi: (bi, blk["k_idx"])),
            pl.BlockSpec((IDX_HEADS, tq), lambda bi, qi: (0, row(bi, qi))),
            pl.BlockSpec((tq, WIDTH_B), lambda bi, qi: (row(bi, qi), blk["q_b"])),
            pl.BlockSpec((l, LANE), lambda bi, qi: (bi, blk["k_b"])),
            pl.BlockSpec((None, nq, LANE, tq), lambda bi, qi: (bi, 0, 0, 0)),
            pl.BlockSpec((tq, WIDTH_B), lambda bi, qi: (row(bi, qi), blk["gate_b"])),
            pl.BlockSpec((1, LANE), lambda bi, qi: (0, 0)),
            pl.BlockSpec((1, LANE), lambda bi, qi: (0, 0)),
        ],
        out_specs=pl.BlockSpec((tq, WIDTH_B), lambda bi, qi: (row(bi, qi), 0)),
        scratch_shapes=[
            pltpu.VMEM((nq, tq, tq), F32),
            pltpu.VMEM((H_B * tq, LANE), MXU_DTYPE),
            pltpu.VMEM((H_B, LANE, tq), F32),
            pltpu.VMEM((H_B, 1, tq), F32),
            pltpu.VMEM((H_B, 1, tq), F32),
            pltpu.VMEM((H_B, 2, tq, tq), F32),
            pltpu.VMEM((8, tq), F32),
            pltpu.VMEM((l, LANE), MXU_DTYPE),
            pltpu.SMEM((H_B,), F32),
        ],
        compiler_params=_params(
            ("arbitrary", "arbitrary"),
            [_nbytes((tq, IDX_HEADS * IDX_DIM + 2 * WIDTH_B), MXU_DTYPE), _nbytes((l, 2 * LANE), MXU_DTYPE),
             _nbytes((IDX_HEADS, tq), F32), _nbytes((nq, LANE, tq), MXU_DTYPE), _nbytes((tq, WIDTH_B), MXU_DTYPE)],
            scratch=(_nbytes((nq, tq, tq), F32) + _nbytes((H_B * tq + l, LANE), MXU_DTYPE)
                     + _nbytes((H_B, LANE + 2 * 8, tq), F32) + _nbytes((H_B, 2, tq, tq), F32) + _nbytes((8, tq), F32)),
            temps=2 * _nbytes((tq, H_B * tq), F32)),
        name="attn_b",
    )(t5_bias, proj, proj, wt, proj, proj, vt, proj, gq_b, gk_b)


def _sigmoid(z):
    return 1.0 / (1.0 + jnp.exp(-z))


def _merge_kernel(oa_ref, ob_ref, pa_ref, pb_ref, ma_ref, mb_ref, o_ref):
    a = jnp.dot(oa_ref[...], pa_ref[...], preferred_element_type=F32)
    bb = jnp.dot(ob_ref[...], pb_ref[...], preferred_element_type=F32)
    o_ref[...] = (_sigmoid(ma_ref[...].astype(F32)) * a + _sigmoid(mb_ref[...].astype(F32)) * bb).astype(o_ref.dtype)


def _merge(o_a, o_b, p_a, p_b, proj, ma_off, mb_off, *, tm, tn):
    t = o_a.shape[0]
    d = p_a.shape[1]
    ma0, mb0 = ma_off // tn, mb_off // tn
    return pl.pallas_call(
        _merge_kernel,
        out_shape=jax.ShapeDtypeStruct((t, d), MXU_DTYPE),
        grid=(t // tm, d // tn),
        in_specs=[
            pl.BlockSpec((tm, WIDTH_A), lambda i, j: (i, 0)),
            pl.BlockSpec((tm, WIDTH_B), lambda i, j: (i, 0)),
            pl.BlockSpec((WIDTH_A, tn), lambda i, j: (0, j)),
            pl.BlockSpec((WIDTH_B, tn), lambda i, j: (0, j)),
            pl.BlockSpec((tm, tn), lambda i, j: (i, ma0 + j)),
            pl.BlockSpec((tm, tn), lambda i, j: (i, mb0 + j)),
        ],
        out_specs=pl.BlockSpec((tm, tn), lambda i, j: (i, j)),
        compiler_params=_params(("arbitrary", "arbitrary"),
                                [_nbytes((tm, WIDTH_A + WIDTH_B), MXU_DTYPE), _nbytes((WIDTH_A + WIDTH_B, tn), MXU_DTYPE),
                                 3 * _nbytes((tm, tn), MXU_DTYPE)],
                                temps=2 * _nbytes((tm, tn), F32)),
        name="merge",
    )(o_a, o_b, p_a, p_b, proj, proj)


def _out_proj_kernel(m_ref, w_ref, x_ref, o_ref):
    o_ref[...] = x_ref[...] + jnp.dot(m_ref[...], w_ref[...], preferred_element_type=F32)


def _out_proj(merged, w_o, x2, *, tm, tn):
    t, d = x2.shape
    return pl.pallas_call(
        _out_proj_kernel,
        out_shape=jax.ShapeDtypeStruct((t, d), x2.dtype),
        grid=(t // tm, d // tn),
        in_specs=[
            pl.BlockSpec((tm, d), lambda i, j: (i, 0)),
            pl.BlockSpec((d, tn), lambda i, j: (0, j)),
            pl.BlockSpec((tm, tn), lambda i, j: (i, j)),
        ],
        out_specs=pl.BlockSpec((tm, tn), lambda i, j: (i, j)),
        compiler_params=_params(("arbitrary", "arbitrary"),
                                [_nbytes((tm, d), MXU_DTYPE), _nbytes((d, tn), MXU_DTYPE), 2 * _nbytes((tm, tn), x2.dtype)],
                                temps=_nbytes((tm, tn), F32)),
        name="out_proj",
    )(merged, w_o, x2)


def _rope_pad(a, axis):
    a1, a2 = jnp.split(a, 2, axis=axis)
    z = jnp.zeros_like(a1)
    return jnp.concatenate([a1, z, a2, z], axis=axis)


def _layout(d):
    names = [("merge_a", d), ("merge_b", d), ("gate_a", WIDTH_A), ("gate_b", WIDTH_B),
             ("q_b", WIDTH_B), ("q_idx", IDX_HEADS * IDX_DIM), ("cq", Q_LORA), ("ckv", KV_LORA),
             ("k_idx", LANE), ("k_rope", LANE), ("k_b", LANE), ("v_b", LANE)]
    off, out = 0, {}
    for name, width in names:
        assert off % width == 0, (name, off, width)
        out[name] = off
        off += width
    out["w_idx"] = out["k_idx"] + IDX_DIM
    return out, off


def kernel(x, positions, g_pre, w_in, g_q_lat, g_kv_lat, w_uq, w_ukv, g_qn_a, g_kn_a,
           g_qn_b, g_kn_b, t5_bias, p_a, p_b, w_o):
    b, l, d = x.shape
    t = b * l
    tq = 256
    tn_in = 512
    off, n_used = _layout(d)
    n_pad = -(-n_used // tn_in) * tn_in

    names = ["cq", "ckv", "k_rope", "q_b", "k_b", "v_b", "q_idx", "k_idx", "w_idx", "gate_a", "gate_b",
             "merge_a", "merge_b"]
    sizes = [Q_LORA, KV_LORA, QK_ROPE, WIDTH_B, HEAD_DIM_B, HEAD_DIM_B, IDX_HEADS * IDX_DIM, IDX_DIM,
             IDX_HEADS, WIDTH_A, WIDTH_B, d, d]
    src, acc = {}, 0
    for name, s in zip(names, sizes):
        src[name] = (acc, s)
        acc += s
    main_groups = ["merge_a", "merge_b", "gate_a", "gate_b", "q_b", "q_idx", "cq", "ckv"]
    src_starts, dst = [], 0
    for name in main_groups:
        assert off[name] == dst and src[name][1] % tn_in == 0
        src_starts += [src[name][0] + c for c in range(0, src[name][1], tn_in)]
        dst += src[name][1]
    wt = w_in.T
    rows = lambda name: wt[src[name][0]:src[name][0] + src[name][1]]
    z = lambda n: jnp.zeros((n, d), w_in.dtype)
    assert off["k_idx"] == dst
    wt_tail = jnp.concatenate(
        [rows("k_idx"), rows("w_idx"), z(LANE - IDX_DIM - IDX_HEADS), _rope_pad(rows("k_rope"), 0),
         rows("k_b"), rows("v_b"), z(n_pad - n_used)], axis=0)
    w_pad = _w_relayout(wt, src_starts, wt_tail, tn=tn_in, tc=d)

    w_uq3 = w_uq.reshape(Q_LORA, H_A, QK_DIM_A)
    w_uq_pad = jnp.concatenate([w_uq3[:, :, :QK_NOPE], _rope_pad(w_uq3[:, :, QK_NOPE:], 2)], axis=2)
    w_uq_pad = w_uq_pad.reshape(Q_LORA, H_A * HEAD_PAD_A).astype(MXU_DTYPE)
    gq_head = jnp.concatenate([g_qn_a[:QK_NOPE], _rope_pad(g_qn_a[QK_NOPE:], 0)]) * (QK_DIM_A ** -0.5 * LOG2E)
    gq_pad = jnp.tile(gq_head, H_A).reshape(1, H_A * HEAD_PAD_A).astype(F32)
    bound_a = QK_DIM_A * jnp.max(jnp.abs(gq_head)) * jnp.max(jnp.abs(g_kn_a)) * BOUND_MARGIN
    q_shift = jnp.zeros((1, LANE), F32).at[0, SHIFT_LANE - LANE].set(-bound_a)
    w_ukv3 = w_ukv.reshape(KV_LORA, H_A, QK_NOPE + V_DIM_A)
    w_uk = w_ukv3[:, :, :QK_NOPE].reshape(KV_LORA, H_A * QK_NOPE).astype(MXU_DTYPE)
    w_uv = w_ukv3[:, :, QK_NOPE:].reshape(KV_LORA, WIDTH_A).astype(MXU_DTYPE)
    gk_nope = g_kn_a[:QK_NOPE].reshape(1, LANE).astype(F32)
    gk_rope = _rope_pad(g_kn_a[QK_NOPE:], 0).reshape(1, LANE).astype(F32)
    gq_b = (g_qn_b * (HEAD_DIM_B ** -0.5 * LOG2E)).reshape(1, LANE).astype(F32)
    gk_b = g_kn_b.reshape(1, LANE).astype(F32)

    inv = ROPE_THETA ** (-jnp.arange(HALF_ROPE, dtype=F32) / HALF_ROPE)
    ang = positions.reshape(t, 1).astype(F32) * inv
    cos, sin = jnp.cos(ang), jnp.sin(ang)
    zr = jnp.zeros_like(cos)
    cos_t = jnp.concatenate([cos, zr, cos, zr], axis=1)
    sin_t = jnp.concatenate([-sin, zr, sin, zr], axis=1)

    x2 = x.reshape(t, d)
    proj = _in_proj(x2, g_pre, w_pad, tm=1024, tn=tn_in)

    q_a = _qa_proj(proj, off["cq"] // Q_LORA, g_q_lat, w_uq_pad, gq_pad, cos_t, sin_t, q_shift, tm=256)
    k_a, v_a = _kva_proj(proj, off["ckv"] // KV_LORA, off["k_rope"] // LANE, g_kv_lat, w_uk, w_uv,
                         gk_nope, gk_rope, cos_t, sin_t, tm=256)
    o_a = _attn_a(q_a, k_a, v_a, proj, off["gate_a"] // LANE, b=b, l=l, tq=tq, heads=4)

    nq = l // tq
    w_idx = proj[:, off["w_idx"]:off["w_idx"] + IDX_HEADS]
    w_idx_t = w_idx.astype(F32).T
    v_b = proj[:, off["v_b"]:off["v_b"] + HEAD_DIM_B]
    vt = v_b.reshape(b, nq, tq, HEAD_DIM_B).transpose(0, 1, 3, 2)
    blk = {"q_idx": off["q_idx"] // (IDX_HEADS * IDX_DIM), "k_idx": off["k_idx"] // LANE,
           "q_b": off["q_b"] // WIDTH_B, "k_b": off["k_b"] // LANE, "gate_b": off["gate_b"] // WIDTH_B}
    o_b = _attn_b(proj, w_idx_t, vt, t5_bias.astype(F32), gq_b, gk_b, blk, b=b, l=l, tq=tq)

    merged = _merge(o_a, o_b, p_a.astype(MXU_DTYPE), p_b.astype(MXU_DTYPE), proj,
                    off["merge_a"], off["merge_b"], tm=1024, tn=1024)
    out = _out_proj(merged, w_o.astype(MXU_DTYPE), x2, tm=1024, tn=1024)
    return out.reshape(b, l, d)
```

```python
import functools
import math

import jax
import jax.numpy as jnp
from jax import lax
from jax.experimental import pallas as pl
from jax.experimental.pallas import tpu as pltpu

F32 = jnp.float32
I32 = jnp.int32
MXU_DTYPE = jnp.bfloat16

H_A = 16
QK_NOPE = 128
QK_ROPE = 64
QK_DIM_A = QK_NOPE + QK_ROPE
V_DIM_A = 128
Q_LORA = 1024
KV_LORA = 512
ROPE_THETA = 10000.0
H_B = 16
HEAD_DIM_B = 128
IDX_HEADS = 32
IDX_DIM = 64
TOPK_MAX = 256
N_BUCKETS = 32
MAX_DISTANCE = 128
EPS = 1e-6
WIDTH_A = H_A * V_DIM_A
WIDTH_B = H_B * HEAD_DIM_B

LANE = 128
ROW_ALIGN = 32
HALF_ROPE = QK_ROPE // 2
HEAD_PAD_A = 2 * LANE
SHIFT_LANE = LANE + HALF_ROPE
BOUND_MARGIN = 1.0 + 2.0 ** -6
MIN_SHIFTED_SUM = 2.0 ** -64
VMEM_CAP = 56 * 1024 * 1024
VMEM_HEADROOM = 4 * 1024 * 1024

BISECT_STEPS_PER_CHECK = 4
BISECT_MAX_CHECKS = 40
LOG2E = math.log2(math.e)
NEG_INF = float("-inf")
POS_INF = float("inf")


def _nt_dot(a, b):
    return lax.dot_general(a, b, (((1,), (1,)), ((), ())), preferred_element_type=F32)


def _nbytes(shape, dtype):
    return math.prod(shape) * jnp.dtype(dtype).itemsize


def _params(sem, windows, scratch=0, temps=0):
    need = 2 * sum(windows) + scratch + temps + VMEM_HEADROOM
    return pltpu.CompilerParams(dimension_semantics=sem, vmem_limit_bytes=min(need, VMEM_CAP))


def _w_relayout_kernel(start_ref, src_ref, tail_ref, o_ref, *, n_main):
    j = pl.program_id(0)

    @pl.when(j < n_main)
    def _():
        o_ref[...] = src_ref[...].astype(o_ref.dtype)

    @pl.when(j >= n_main)
    def _():
        o_ref[...] = tail_ref[...].astype(o_ref.dtype)


def _w_relayout(wt, src_starts, wt_tail, *, tn, tc):
    d = wt.shape[1]
    n_main, n_tail = len(src_starts), wt_tail.shape[0] // tn
    assert all(s % ROW_ALIGN == 0 for s in src_starts)
    starts = jnp.array([s // ROW_ALIGN for s in src_starts] + [0] * n_tail, I32)
    return pl.pallas_call(
        functools.partial(_w_relayout_kernel, n_main=n_main),
        out_shape=jax.ShapeDtypeStruct(((n_main + n_tail) * tn, d), MXU_DTYPE),
        grid_spec=pltpu.PrefetchScalarGridSpec(
            num_scalar_prefetch=1,
            grid=(n_main + n_tail, d // tc),
            in_specs=[
                pl.BlockSpec((pl.Element(tn), pl.Element(tc)), lambda j, c, st: (st[j] * ROW_ALIGN, c * tc)),
                pl.BlockSpec((tn, tc), lambda j, c, st: (jnp.maximum(j - n_main, 0), c)),
            ],
            out_specs=pl.BlockSpec((tn, tc), lambda j, c, st: (j, c)),
        ),
        compiler_params=_params(("arbitrary", "arbitrary"),
                                [_nbytes((tn, tc), wt.dtype), _nbytes((tn, tc), wt_tail.dtype),
                                 _nbytes((tn, tc), MXU_DTYPE)]),
        name="w_relayout",
    )(starts, wt, wt_tail)


def _in_proj_kernel(x_hbm, g_ref, w_ref, o_ref, x_buf, hn_ref, x_sem, *, row_chunk):
    tm = x_buf.shape[0]
    i, j = pl.program_id(0), pl.program_id(1)

    def x_copy(row_block):
        rows = pl.ds(pl.multiple_of(row_block * tm, tm), tm)
        return pltpu.make_async_copy(x_hbm.at[rows, :], x_buf, x_sem)

    @pl.when((i == 0) & (j == 0))
    def _():
        x_copy(0).start()

    @pl.when(j == 0)
    def _():
        x_copy(i).wait()

        def body(r, carry):
            sl = pl.ds(pl.multiple_of(r * row_chunk, row_chunk), row_chunk)
            xx = x_buf[sl, :]
            ms = jnp.mean(xx * xx, axis=-1, keepdims=True)
            hn_ref[sl, :] = (xx * lax.rsqrt(ms + EPS) * g_ref[...]).astype(hn_ref.dtype)
            return carry

        lax.fori_loop(0, tm // row_chunk, body, 0)

    @pl.when((j == 1) & (i + 1 < pl.num_programs(0)))
    def _():
        x_copy(i + 1).start()

    o_ref[...] = _nt_dot(hn_ref[...], w_ref[...]).astype(o_ref.dtype)


def _in_proj(x2, g_pre, wt_pad, *, tm, tn):
    t, d = x2.shape
    n = wt_pad.shape[0]
    assert n // tn >= 2
    return pl.pallas_call(
        functools.partial(_in_proj_kernel, row_chunk=64),
        out_shape=jax.ShapeDtypeStruct((t, n), MXU_DTYPE),
        grid=(t // tm, n // tn),
        in_specs=[
            pl.BlockSpec(memory_space=pl.ANY),
            pl.BlockSpec((1, d), lambda i, j: (0, 0)),
            pl.BlockSpec((tn, d), lambda i, j: (j, 0)),
        ],
        out_specs=pl.BlockSpec((tm, tn), lambda i, j: (i, j)),
        scratch_shapes=[pltpu.VMEM((tm, d), x2.dtype), pltpu.VMEM((tm, d), MXU_DTYPE),
                        pltpu.SemaphoreType.DMA(())],
        compiler_params=_params(("arbitrary", "arbitrary"),
                                [_nbytes((1, d), F32), _nbytes((tn, d), MXU_DTYPE), _nbytes((tm, tn), MXU_DTYPE)],
                                scratch=_nbytes((tm, d), x2.dtype) + _nbytes((tm, d), MXU_DTYPE),
                                temps=2 * _nbytes((tm, tn), F32)),
        name="in_proj",
    )(x2, g_pre.reshape(1, d), wt_pad)


def _rope_lanes(r, cos_ref, sin_ref):
    return r * cos_ref[...] + pltpu.roll(r, 2 * HALF_ROPE, 1) * sin_ref[...]


def _qa_proj_kernel(cq_ref, gl_ref, w_ref, gq_ref, cos_ref, sin_ref, shift_ref, o_ref):
    c = cq_ref[...].astype(F32)
    ms = jnp.mean(c * c, axis=-1, keepdims=True)
    cn = (c * lax.rsqrt(ms + EPS) * gl_ref[...]).astype(MXU_DTYPE)
    q = jnp.dot(cn, w_ref[...], preferred_element_type=F32)
    for h in range(H_A):
        lo = h * HEAD_PAD_A
        qh = q[:, lo:lo + HEAD_PAD_A]
        ss = jnp.sum(qh * qh, axis=-1, keepdims=True) * (1.0 / QK_DIM_A)
        qn = qh * lax.rsqrt(ss + EPS) * gq_ref[:, lo:lo + HEAD_PAD_A]
        o_ref[:, lo:lo + LANE] = qn[:, :LANE].astype(o_ref.dtype)
        o_ref[:, lo + LANE:lo + HEAD_PAD_A] = (
            _rope_lanes(qn[:, LANE:], cos_ref, sin_ref) + shift_ref[...]).astype(o_ref.dtype)


def _qa_proj(proj, cq_blk, g_q_lat, w_uq_pad, gq_pad, cos_t, sin_t, q_shift, *, tm):
    t = proj.shape[0]
    nq = H_A * HEAD_PAD_A
    return pl.pallas_call(
        _qa_proj_kernel,
        out_shape=jax.ShapeDtypeStruct((t, nq), MXU_DTYPE),
        grid=(t // tm,),
        in_specs=[
            pl.BlockSpec((tm, Q_LORA), lambda i: (i, cq_blk)),
            pl.BlockSpec((1, Q_LORA), lambda i: (0, 0)),
            pl.BlockSpec((Q_LORA, nq), lambda i: (0, 0)),
            pl.BlockSpec((1, nq), lambda i: (0, 0)),
            pl.BlockSpec((tm, LANE), lambda i: (i, 0)),
            pl.BlockSpec((tm, LANE), lambda i: (i, 0)),
            pl.BlockSpec((1, LANE), lambda i: (0, 0)),
        ],
        out_specs=pl.BlockSpec((tm, nq), lambda i: (i, 0)),
        compiler_params=_params(("arbitrary",),
                                [_nbytes((tm, Q_LORA), MXU_DTYPE), _nbytes((Q_LORA, nq), MXU_DTYPE),
                                 _nbytes((1, Q_LORA + nq + LANE), F32), 2 * _nbytes((tm, LANE), F32),
                                 _nbytes((tm, nq), MXU_DTYPE)],
                                temps=_nbytes((tm, nq), F32)),
        name="qa_proj",
    )(proj, g_q_lat.reshape(1, Q_LORA), w_uq_pad, gq_pad, cos_t, sin_t, q_shift)


def _kva_proj_kernel(ckv_ref, kr_ref, gl_ref, wk_ref, wv_ref, gkn_ref, gkr_ref, cos_ref, sin_ref,
                     k_ref, v_ref):
    c = ckv_ref[...].astype(F32)
    ms = jnp.mean(c * c, axis=-1, keepdims=True)
    cn = (c * lax.rsqrt(ms + EPS) * gl_ref[...]).astype(MXU_DTYPE)
    kn = jnp.dot(cn, wk_ref[...], preferred_element_type=F32)
    v_ref[...] = jnp.dot(cn, wv_ref[...], preferred_element_type=F32).astype(v_ref.dtype)
    kr = kr_ref[...].astype(F32)
    ss_r = jnp.sum(kr * kr, axis=-1, keepdims=True)
    krr = _rope_lanes(kr * gkr_ref[...], cos_ref, sin_ref)
    shift_one = (lax.broadcasted_iota(I32, (1, LANE), 1) == SHIFT_LANE - LANE).astype(F32)
    for h in range(H_A):
        kh = kn[:, h * LANE:(h + 1) * LANE]
        ss = (jnp.sum(kh * kh, axis=-1, keepdims=True) + ss_r) * (1.0 / QK_DIM_A)
        rs = lax.rsqrt(ss + EPS)
        lo = h * HEAD_PAD_A
        k_ref[:, lo:lo + LANE] = (kh * rs * gkn_ref[...]).astype(k_ref.dtype)
        k_ref[:, lo + LANE:lo + HEAD_PAD_A] = (krr * rs + shift_one).astype(k_ref.dtype)


def _kva_proj(proj, ckv_blk, krope_blk, g_kv_lat, w_uk, w_uv, gk_nope, gk_rope, cos_t, sin_t, *, tm):
    t = proj.shape[0]
    return pl.pallas_call(
        _kva_proj_kernel,
        out_shape=(jax.ShapeDtypeStruct((t, H_A * HEAD_PAD_A), MXU_DTYPE),
                   jax.ShapeDtypeStruct((t, WIDTH_A), MXU_DTYPE)),
        grid=(t // tm,),
        in_specs=[
            pl.BlockSpec((tm, KV_LORA), lambda i: (i, ckv_blk)),
            pl.BlockSpec((tm, LANE), lambda i: (i, krope_blk)),
            pl.BlockSpec((1, KV_LORA), lambda i: (0, 0)),
            pl.BlockSpec((KV_LORA, H_A * QK_NOPE), lambda i: (0, 0)),
            pl.BlockSpec((KV_LORA, WIDTH_A), lambda i: (0, 0)),
            pl.BlockSpec((1, LANE), lambda i: (0, 0)),
            pl.BlockSpec((1, LANE), lambda i: (0, 0)),
            pl.BlockSpec((tm, LANE), lambda i: (i, 0)),
            pl.BlockSpec((tm, LANE), lambda i: (i, 0)),
        ],
        out_specs=(pl.BlockSpec((tm, H_A * HEAD_PAD_A), lambda i: (i, 0)),
                   pl.BlockSpec((tm, WIDTH_A), lambda i: (i, 0))),
        compiler_params=_params(("arbitrary",),
                                [_nbytes((tm, KV_LORA + LANE), MXU_DTYPE), 2 * _nbytes((KV_LORA, WIDTH_A), MXU_DTYPE),
                                 _nbytes((1, KV_LORA + 2 * LANE), F32), 2 * _nbytes((tm, LANE), F32),
                                 _nbytes((tm, H_A * HEAD_PAD_A + WIDTH_A), MXU_DTYPE)],
                                temps=2 * _nbytes((tm, WIDTH_A), F32)),
        name="kva_proj",
    )(proj, proj, g_kv_lat.reshape(1, KV_LORA), w_uk, w_uv, gk_nope, gk_rope, cos_t, sin_t)


def _silu(g):
    return g * (1.0 / (1.0 + jnp.exp(-g)))


def _lane_tile_reduce(x, op):
    acc = x[:, :LANE]
    for t in range(1, x.shape[1] // LANE):
        acc = op(acc, x[:, t * LANE:(t + 1) * LANE])
    return acc


def _attn_a_kernel(q_ref, k_ref, v_ref, gate_ref, o_ref, *, tq, nq, heads):
    qi = pl.program_id(2)
    causal = lax.broadcasted_iota(I32, (tq, tq), 0) >= lax.broadcasted_iota(I32, (tq, tq), 1)

    def finish(g, l_t, acc):
        vc = slice(g * V_DIM_A, (g + 1) * V_DIM_A)
        l = jnp.sum(l_t, axis=-1, keepdims=True)
        o = acc * (1.0 / l)
        o_ref[:, vc] = (o * _silu(gate_ref[:, vc].astype(F32))).astype(o_ref.dtype)

    def branch(qv):
        n_off = qv * tq
        kcs = [slice(g * HEAD_PAD_A, (g + 1) * HEAD_PAD_A) for g in range(heads)]
        vcs = [slice(g * V_DIM_A, (g + 1) * V_DIM_A) for g in range(heads)]

        l_min = jnp.full((tq, 1), POS_INF, F32)
        for g in range(heads):
            q = q_ref[:, kcs[g]]
            p_diag = jnp.exp2(jnp.where(causal, _nt_dot(q, k_ref[n_off:n_off + tq, kcs[g]]), NEG_INF))
            l_t = _lane_tile_reduce(p_diag, jnp.add)
            acc = jnp.dot(p_diag.astype(MXU_DTYPE), v_ref[n_off:n_off + tq, vcs[g]], preferred_element_type=F32)
            if qv > 0:
                p_off = jnp.exp2(_nt_dot(q, k_ref[0:n_off, kcs[g]]))
                l_t = l_t + _lane_tile_reduce(p_off, jnp.add)
                acc = acc + jnp.dot(p_off.astype(MXU_DTYPE), v_ref[0:n_off, vcs[g]], preferred_element_type=F32)
            l_min = jnp.minimum(l_min, jnp.sum(l_t, axis=-1, keepdims=True))
            finish(g, l_t, acc)

        shift_ok = jnp.min(l_min) >= MIN_SHIFTED_SUM

        @pl.when(jnp.logical_not(shift_ok))
        def _():
            for g in range(heads):
                q = q_ref[:, kcs[g]]
                q_hi = q[:, LANE:]
                lane = lax.broadcasted_iota(I32, q_hi.shape, 1)
                q = jnp.concatenate(
                    [q[:, :LANE], jnp.where(lane == SHIFT_LANE - LANE, 0.0, q_hi.astype(F32)).astype(q.dtype)], axis=1)
                s_diag = jnp.where(causal, _nt_dot(q, k_ref[n_off:n_off + tq, kcs[g]]), NEG_INF)
                m_t = _lane_tile_reduce(s_diag, jnp.maximum)
                if qv > 0:
                    s_off = _nt_dot(q, k_ref[0:n_off, kcs[g]])
                    m_t = jnp.maximum(m_t, _lane_tile_reduce(s_off, jnp.maximum))
                m = jnp.max(m_t, axis=-1, keepdims=True)
                p_diag = jnp.exp2(s_diag - m)
                l_t = _lane_tile_reduce(p_diag, jnp.add)
                acc = jnp.dot(p_diag.astype(MXU_DTYPE), v_ref[n_off:n_off + tq, vcs[g]],
                              preferred_element_type=F32)
                if qv > 0:
                    p_off = jnp.exp2(s_off - m)
                    l_t = l_t + _lane_tile_reduce(p_off, jnp.add)
                    acc = acc + jnp.dot(p_off.astype(MXU_DTYPE), v_ref[0:n_off, vcs[g]], preferred_element_type=F32)
                finish(g, l_t, acc)

    for qv in range(nq):
        pl.when(qi == qv)(functools.partial(branch, qv))


def _attn_a(q_a, k_a, v_a, proj, gate_blk0, *, b, l, tq, heads):
    t = q_a.shape[0]
    nq = l // tq
    kw, vw = heads * HEAD_PAD_A, heads * V_DIM_A
    return pl.pallas_call(
        functools.partial(_attn_a_kernel, tq=tq, nq=nq, heads=heads),
        out_shape=jax.ShapeDtypeStruct((t, WIDTH_A), MXU_DTYPE),
        grid=(b, H_A // heads, nq),
        in_specs=[
            pl.BlockSpec((tq, kw), lambda bi, h, qi: (bi * nq + qi, h)),
            pl.BlockSpec((l, kw), lambda bi, h, qi: (bi, h)),
            pl.BlockSpec((l, vw), lambda bi, h, qi: (bi, h)),
            pl.BlockSpec((tq, vw), lambda bi, h, qi: (bi * nq + qi, gate_blk0 // heads + h)),
        ],
        out_specs=pl.BlockSpec((tq, vw), lambda bi, h, qi: (bi * nq + qi, h)),
        compiler_params=_params(("arbitrary", "arbitrary", "arbitrary"),
                                [_nbytes((tq + l, kw), MXU_DTYPE), _nbytes((l + 2 * tq, vw), MXU_DTYPE)],
                                temps=heads * (_nbytes((tq, l), F32) + _nbytes((tq, l), MXU_DTYPE))),
        name="attn_a",
    )(q_a, k_a, v_a, proj)


def _t5_bucket(dist):
    max_exact = N_BUCKETS // 2
    n = jnp.maximum(dist, 0)
    nf = jnp.maximum(n, 1).astype(F32)
    large = max_exact + (jnp.log(nf / max_exact) / math.log(MAX_DISTANCE / max_exact)
                         * (N_BUCKETS - max_exact)).astype(I32)
    large = jnp.minimum(large, N_BUCKETS - 1)
    return jnp.where(n < max_exact, n, large)


def _attn_b_kernel(t5_ref, qidx_ref, kidx_ref, wt_ref, qb_ref, kb_ref, vt_ref, gate_ref, gq_ref, gk_ref,
                   o_ref, sc_ref, qn_ref, acc_ref, m_ref, l_ref, bias_ref, thr_ref, kn_ref, bound_ref,
                   *, tq, nq, topk, max_iters):
    bi = pl.program_id(0)
    qi = pl.program_id(1)
    ck = tq
    shape = (ck, tq)

    @pl.when((bi == 0) & (qi == 0))
    def _():
        s_loc = lax.broadcasted_iota(I32, shape, 0)
        t_loc = lax.broadcasted_iota(I32, shape, 1)
        for near in range(2):
            bucket = _t5_bucket(t_loc - s_loc + (1 - near) * ck)

            def per_head(h, carry, bucket=bucket, near=near):
                far = t5_ref[N_BUCKETS - 1, h]
                tab = jnp.zeros(shape, F32)
                largest = jnp.float32(0.0)
                for bk in range(N_BUCKETS - 1):
                    rel = (t5_ref[bk, h] - far) * LOG2E
                    tab = jnp.where(bucket == bk, rel, tab)
                    largest = jnp.maximum(largest, rel)
                bias_ref[h, near] = tab
                gains = jnp.max(jnp.abs(gq_ref[...])) * jnp.max(jnp.abs(gk_ref[...]))
                bound_ref[h] = HEAD_DIM_B * gains * BOUND_MARGIN + largest
                return carry

            lax.fori_loop(0, H_B, per_head, 0)

    w_all = wt_ref[...] * (IDX_HEADS ** -0.5)

    def score_chunk(c, diag):
        rows = pl.ds(pl.multiple_of(c * ck, ck), ck)
        kx = kidx_ref[rows, 0:IDX_DIM]
        zk = jnp.zeros_like(kx)
        kab = jnp.concatenate([jnp.concatenate([kx, zk], axis=1), jnp.concatenate([zk, kx], axis=1)], axis=0)
        score = jnp.zeros(shape, F32)
        for j in range(IDX_HEADS // 2):
            qp = qidx_ref[:, j * LANE:(j + 1) * LANE]
            logits = jnp.maximum(_nt_dot(kab, qp), 0.0)
            score = score + logits[:ck] * w_all[2 * j:2 * j + 1, :]
            score = score + logits[ck:] * w_all[2 * j + 1:2 * j + 2, :]
        if diag:
            adm = lax.broadcasted_iota(I32, shape, 0) <= lax.broadcasted_iota(I32, shape, 1)
            lo_src = jnp.where(adm, score, POS_INF)
            score = jnp.where(adm, score, NEG_INF)
        else:
            lo_src = score
        sc_ref[c] = score
        return jnp.max(score, axis=0, keepdims=True), jnp.min(lo_src, axis=0, keepdims=True)

    def score_body(c, carry):
        mx, mn = carry
        cmx, cmn = score_chunk(c, False)
        return jnp.maximum(mx, cmx), jnp.minimum(mn, cmn)

    mx, mn = lax.fori_loop(0, qi, score_body,
                           (jnp.full((1, tq), NEG_INF, F32), jnp.full((1, tq), POS_INF, F32)))
    dmx, dmn = score_chunk(qi, True)
    mx = jnp.maximum(mx, dmx)
    mn = jnp.minimum(mn, dmn)

    rep = (8, tq)
    n_adm = qi * tq + lax.broadcasted_iota(I32, rep, 1) + 1
    kp = jnp.minimum(n_adm, topk)
    mx8 = jnp.broadcast_to(mx, rep)
    mn8 = jnp.broadcast_to(mn, rep)

    def bisect(nchunks):
        def count_ge(x):
            acc = jnp.zeros(rep, I32)
            for c in range(nchunks):
                ge = sc_ref[c].reshape(ck // 8, 8, tq) >= x[None]
                acc = acc + jnp.sum(ge.astype(I32), axis=0)
            for shift in (4, 2, 1):
                acc = acc + pltpu.roll(acc, shift, 0)
            return acc

        def bis_cond(st):
            it, lo, hi, mid, cnt_lo = st
            active = (cnt_lo != kp) & (mid > lo) & (mid < hi)
            return jnp.logical_and(it < max_iters, jnp.max(active.astype(I32)) > 0)

        def bis_body(st):
            it, lo, hi, mid, cnt_lo = st
            for _ in range(BISECT_STEPS_PER_CHECK):
                cnt = count_ge(mid)
                ge = cnt >= kp
                lo = jnp.where(ge, mid, lo)
                cnt_lo = jnp.where(ge, cnt, cnt_lo)
                hi = jnp.where(ge, hi, mid)
                mid = jnp.where(hi == POS_INF, mx8, lo + 0.5 * (hi - lo))
            return it + 1, lo, hi, mid, cnt_lo

        _, lo, hi, _, cnt_lo = lax.while_loop(
            bis_cond, bis_body, (jnp.int32(0), mn8, jnp.full(rep, POS_INF, F32), mx8, n_adm))
        thr_ref[...] = lo

        tied = cnt_lo > kp

        @pl.when(jnp.max(tied.astype(I32)) > 0)
        def _():
            n_keys = nchunks * ck
            sub = lax.broadcasted_iota(I32, (ck // 8, 8, tq), 0) * 8 + lax.broadcasted_iota(I32, (ck // 8, 8, tq), 1)

            def count_kept(j_last):
                acc = jnp.zeros(rep, I32)
                for c in range(nchunks):
                    s3 = sc_ref[c].reshape(ck // 8, 8, tq)
                    keep = (s3 >= hi[None]) | ((s3 >= lo[None]) & (sub + c * ck <= j_last[None]))
                    acc = acc + jnp.sum(keep.astype(I32), axis=0)
                for shift in (4, 2, 1):
                    acc = acc + pltpu.roll(acc, shift, 0)
                return acc

            def idx_step(_, carry):
                j_lo, j_hi = carry
                j_mid = j_lo + ((j_hi - j_lo) >> 1)
                ok = count_kept(j_mid) >= kp
                return jnp.where(ok, j_lo, j_mid), jnp.where(ok, j_mid, j_hi)

            _, j_hi = lax.fori_loop(0, max(1, (n_keys - 1).bit_length()), idx_step,
                                    (jnp.full(rep, -1, I32), jnp.full(rep, n_keys - 1, I32)))
            j_last = jnp.where(tied, j_hi, n_keys - 1)
            for c in range(nchunks):
                s3 = sc_ref[c].reshape(ck // 8, 8, tq)
                drop = (s3 >= lo[None]) & (s3 < hi[None]) & (sub + c * ck > j_last[None])
                sc_ref[c] = jnp.where(drop, NEG_INF, s3).reshape(ck, tq)

    for qv in range(nq):
        pl.when(qi == qv)(functools.partial(bisect, qv + 1))
    thr = thr_ref[0:1, :]

    @pl.when(qi == 0)
    def _():
        kf = kb_ref[...].astype(F32)
        ms = jnp.mean(kf * kf, axis=-1, keepdims=True)
        kn_ref[...] = (kf * lax.rsqrt(ms + EPS) * gk_ref[...]).astype(kn_ref.dtype)

    for h in range(H_B):
        qh = qb_ref[:, h * LANE:(h + 1) * LANE].astype(F32)
        ms = jnp.mean(qh * qh, axis=-1, keepdims=True)
        qn_ref[h * tq:(h + 1) * tq, :] = (qh * lax.rsqrt(ms + EPS) * gq_ref[...]).astype(qn_ref.dtype)

    def attend_chunk(c, near, exact):
        rows = pl.ds(pl.multiple_of(c * ck, ck), ck)
        vt = vt_ref[c]
        sel = sc_ref[c] >= thr
        s_all = _nt_dot(kn_ref[rows, :], qn_ref[...])
        for h in range(H_B):
            s = s_all[:, h * tq:(h + 1) * tq]
            if near is not None:
                s = s + bias_ref[h, near]
            if exact:
                s = jnp.where(sel, s, NEG_INF)
                m_old = m_ref[h]
                m_new = jnp.maximum(m_old, jnp.max(s, axis=0, keepdims=True))
                m_safe = jnp.where(m_new == NEG_INF, 0.0, m_new)
                p = jnp.exp2(s - m_safe)
                alpha = jnp.exp2(m_old - m_safe)
                l_ref[h] = alpha * l_ref[h] + jnp.sum(p, axis=0, keepdims=True)
                acc_ref[h] = alpha * acc_ref[h] + jnp.dot(vt, p.astype(MXU_DTYPE), preferred_element_type=F32)
                m_ref[h] = m_new
            else:
                p = jnp.exp2(jnp.where(sel, s - bound_ref[h], NEG_INF))
                l_ref[h] = l_ref[h] + jnp.sum(p, axis=0, keepdims=True)
                acc_ref[h] = acc_ref[h] + jnp.dot(vt, p.astype(MXU_DTYPE), preferred_element_type=F32)

    def attend(exact):
        if exact:
            m_ref[...] = jnp.full(m_ref.shape, NEG_INF, F32)
        l_ref[...] = jnp.zeros(l_ref.shape, F32)
        acc_ref[...] = jnp.zeros(acc_ref.shape, F32)

        def far_body(c, carry):
            attend_chunk(c, None, exact)
            return carry

        lax.fori_loop(0, jnp.maximum(qi - 1, 0), far_body, 0)

        @pl.when(qi >= 1)
        def _():
            attend_chunk(qi - 1, 0, exact)

        attend_chunk(qi, 1, exact)

    attend(False)
    l_min = l_ref[0]
    for h in range(1, H_B):
        l_min = jnp.minimum(l_min, l_ref[h])
    shift_ok = jnp.min(l_min) >= MIN_SHIFTED_SUM
    pl.when(jnp.logical_not(shift_ok))(functools.partial(attend, True))

    for h in range(H_B):
        o_t = acc_ref[h] * (1.0 / l_ref[h])
        g = gate_ref[:, h * LANE:(h + 1) * LANE].astype(F32)
        o_ref[:, h * LANE:(h + 1) * LANE] = (o_t.T * _silu(g)).astype(o_ref.dtype)


def _attn_b(proj, wt, vt, t5_bias, gq_b, gk_b, blk, *, b, l, tq):
    t = proj.shape[0]
    nq = l // tq
    topk = min(TOPK_MAX, l // 4)
    row = lambda bi, qi: bi * nq + qi
    return pl.pallas_call(
        functools.partial(_attn_b_kernel, tq=tq, nq=nq, topk=topk, max_iters=BISECT_MAX_CHECKS),
        out_shape=jax.ShapeDtypeStruct((t, WIDTH_B), MXU_DTYPE),
        grid=(b, nq),
        in_specs=[
            pl.BlockSpec(memory_space=pltpu.SMEM),
            pl.BlockSpec((tq, IDX_HEADS * IDX_DIM), lambda bi, qi: (row(bi, qi), blk["q_idx"])),
            pl.BlockSpec((l, LANE), lambda bi, qi: (bi, blk["k_idx"])),
            pl.BlockSpec((IDX_HEADS, tq), lambda bi, qi: (0, row(bi, qi))),
            pl.BlockSpec((tq, WIDTH_B), lambda bi, qi: (row(bi, qi), blk["q_b"])),
            pl.BlockSpec((l, LANE), lambda bi, qi: (bi, blk["k_b"])),
            pl.BlockSpec((None, nq, LANE, tq), lambda bi, qi: (bi, 0, 0, 0)),
            pl.BlockSpec((tq, WIDTH_B), lambda bi, qi: (row(bi, qi), blk["gate_b"])),
            pl.BlockSpec((1, LANE), lambda bi, qi: (0, 0)),
            pl.BlockSpec((1, LANE), lambda bi, qi: (0, 0)),
        ],
        out_specs=pl.BlockSpec((tq, WIDTH_B), lambda bi, qi: (row(bi, qi), 0)),
        scratch_shapes=[
            pltpu.VMEM((nq, tq, tq), F32),
            pltpu.VMEM((H_B * tq, LANE), MXU_DTYPE),
            pltpu.VMEM((H_B, LANE, tq), F32),
            pltpu.VMEM((H_B, 1, tq), F32),
            pltpu.VMEM((H_B, 1, tq), F32),
            pltpu.VMEM((H_B, 2, tq, tq), F32),
            pltpu.VMEM((8, tq), F32),
            pltpu.VMEM((l, LANE), MXU_DTYPE),
            pltpu.SMEM((H_B,), F32),
        ],
        compiler_params=_params(
            ("arbitrary", "arbitrary"),
            [_nbytes((tq, IDX_HEADS * IDX_DIM + 2 * WIDTH_B), MXU_DTYPE), _nbytes((l, 2 * LANE), MXU_DTYPE),
             _nbytes((IDX_HEADS, tq), F32), _nbytes((nq, LANE, tq), MXU_DTYPE), _nbytes((tq, WIDTH_B), MXU_DTYPE)],
            scratch=(_nbytes((nq, tq, tq), F32) + _nbytes((H_B * tq + l, LANE), MXU_DTYPE)
                     + _nbytes((H_B, LANE + 2 * 8, tq), F32) + _nbytes((H_B, 2, tq, tq), F32) + _nbytes((8, tq), F32)),
            temps=2 * _nbytes((tq, H_B * tq), F32)),
        name="attn_b",
    )(t5_bias, proj, proj, wt, proj, proj, vt, proj, gq_b, gk_b)


def _sigmoid(z):
    return 1.0 / (1.0 + jnp.exp(-z))


def _merge_kernel(oa_ref, ob_ref, pa_ref, pb_ref, ma_ref, mb_ref, o_ref):
    a = jnp.dot(oa_ref[...], pa_ref[...], preferred_element_type=F32)
    bb = jnp.dot(ob_ref[...], pb_ref[...], preferred_element_type=F32)
    o_ref[...] = (_sigmoid(ma_ref[...].astype(F32)) * a + _sigmoid(mb_ref[...].astype(F32)) * bb).astype(o_ref.dtype)


def _merge(o_a, o_b, p_a, p_b, proj, ma_off, mb_off, *, tm, tn):
    t = o_a.shape[0]
    d = p_a.shape[1]
    ma0, mb0 = ma_off // tn, mb_off // tn
    return pl.pallas_call(
        _merge_kernel,
        out_shape=jax.ShapeDtypeStruct((t, d), MXU_DTYPE),
        grid=(t // tm, d // tn),
        in_specs=[
            pl.BlockSpec((tm, WIDTH_A), lambda i, j: (i, 0)),
            pl.BlockSpec((tm, WIDTH_B), lambda i, j: (i, 0)),
            pl.BlockSpec((WIDTH_A, tn), lambda i, j: (0, j)),
            pl.BlockSpec((WIDTH_B, tn), lambda i, j: (0, j)),
            pl.BlockSpec((tm, tn), lambda i, j: (i, ma0 + j)),
            pl.BlockSpec((tm, tn), lambda i, j: (i, mb0 + j)),
        ],
        out_specs=pl.BlockSpec((tm, tn), lambda i, j: (i, j)),
        compiler_params=_params(("arbitrary", "arbitrary"),
                                [_nbytes((tm, WIDTH_A + WIDTH_B), MXU_DTYPE), _nbytes((WIDTH_A + WIDTH_B, tn), MXU_DTYPE),
                                 3 * _nbytes((tm, tn), MXU_DTYPE)],
                                temps=2 * _nbytes((tm, tn), F32)),
        name="merge",
    )(o_a, o_b, p_a, p_b, proj, proj)


def _out_proj_kernel(m_ref, w_ref, x_ref, o_ref):
    o_ref[...] = x_ref[...] + jnp.dot(m_ref[...], w_ref[...], preferred_element_type=F32)


def _out_proj(merged, w_o, x2, *, tm, tn):
    t, d = x2.shape
    return pl.pallas_call(
        _out_proj_kernel,
        out_shape=jax.ShapeDtypeStruct((t, d), x2.dtype),
        grid=(t // tm, d // tn),
        in_specs=[
            pl.BlockSpec((tm, d), lambda i, j: (i, 0)),
            pl.BlockSpec((d, tn), lambda i, j: (0, j)),
            pl.BlockSpec((tm, tn), lambda i, j: (i, j)),
        ],
        out_specs=pl.BlockSpec((tm, tn), lambda i, j: (i, j)),
        compiler_params=_params(("arbitrary", "arbitrary"),
                                [_nbytes((tm, d), MXU_DTYPE), _nbytes((d, tn), MXU_DTYPE), 2 * _nbytes((tm, tn), x2.dtype)],
                                temps=_nbytes((tm, tn), F32)),
        name="out_proj",
    )(merged, w_o, x2)


def _rope_pad(a, axis):
    a1, a2 = jnp.split(a, 2, axis=axis)
    z = jnp.zeros_like(a1)
    return jnp.concatenate([a1, z, a2, z], axis=axis)


def _layout(d):
    names = [("merge_a", d), ("merge_b", d), ("gate_a", WIDTH_A), ("gate_b", WIDTH_B),
             ("q_b", WIDTH_B), ("q_idx", IDX_HEADS * IDX_DIM), ("cq", Q_LORA), ("ckv", KV_LORA),
             ("k_idx", LANE), ("k_rope", LANE), ("k_b", LANE), ("v_b", LANE)]
    off, out = 0, {}
    for name, width in names:
        assert off % width == 0, (name, off, width)
        out[name] = off
        off += width
    out["w_idx"] = out["k_idx"] + IDX_DIM
    return out, off


def kernel(x, positions, g_pre, w_in, g_q_lat, g_kv_lat, w_uq, w_ukv, g_qn_a, g_kn_a,
           g_qn_b, g_kn_b, t5_bias, p_a, p_b, w_o):
    b, l, d = x.shape
    t = b * l
    tq = 256
    tn_in = 512
    off, n_used = _layout(d)
    n_pad = -(-n_used // tn_in) * tn_in

    names = ["cq", "ckv", "k_rope", "q_b", "k_b", "v_b", "q_idx", "k_idx", "w_idx", "gate_a", "gate_b",
             "merge_a", "merge_b"]
    sizes = [Q_LORA, KV_LORA, QK_ROPE, WIDTH_B, HEAD_DIM_B, HEAD_DIM_B, IDX_HEADS * IDX_DIM, IDX_DIM,
             IDX_HEADS, WIDTH_A, WIDTH_B, d, d]
    src, acc = {}, 0
    for name, s in zip(names, sizes):
        src[name] = (acc, s)
        acc += s
    main_groups = ["merge_a", "merge_b", "gate_a", "gate_b", "q_b", "q_idx", "cq", "ckv"]
    src_starts, dst = [], 0
    for name in main_groups:
        assert off[name] == dst and src[name][1] % tn_in == 0
        src_starts += [src[name][0] + c for c in range(0, src[name][1], tn_in)]
        dst += src[name][1]
    wt = w_in.T
    rows = lambda name: wt[src[name][0]:src[name][0] + src[name][1]]
    z = lambda n: jnp.zeros((n, d), w_in.dtype)
    assert off["k_idx"] == dst
    wt_tail = jnp.concatenate(
        [rows("k_idx"), rows("w_idx"), z(LANE - IDX_DIM - IDX_HEADS), _rope_pad(rows("k_rope"), 0),
         rows("k_b"), rows("v_b"), z(n_pad - n_used)], axis=0)
    w_pad = _w_relayout(wt, src_starts, wt_tail, tn=tn_in, tc=d)

    w_uq3 = w_uq.reshape(Q_LORA, H_A, QK_DIM_A)
    w_uq_pad = jnp.concatenate([w_uq3[:, :, :QK_NOPE], _rope_pad(w_uq3[:, :, QK_NOPE:], 2)], axis=2)
    w_uq_pad = w_uq_pad.reshape(Q_LORA, H_A * HEAD_PAD_A).astype(MXU_DTYPE)
    gq_head = jnp.concatenate([g_qn_a[:QK_NOPE], _rope_pad(g_qn_a[QK_NOPE:], 0)]) * (QK_DIM_A ** -0.5 * LOG2E)
    gq_pad = jnp.tile(gq_head, H_A).reshape(1, H_A * HEAD_PAD_A).astype(F32)
    bound_a = QK_DIM_A * jnp.max(jnp.abs(gq_head)) * jnp.max(jnp.abs(g_kn_a)) * BOUND_MARGIN
    q_shift = jnp.zeros((1, LANE), F32).at[0, SHIFT_LANE - LANE].set(-bound_a)
    w_ukv3 = w_ukv.reshape(KV_LORA, H_A, QK_NOPE + V_DIM_A)
    w_uk = w_ukv3[:, :, :QK_NOPE].reshape(KV_LORA, H_A * QK_NOPE).astype(MXU_DTYPE)
    w_uv = w_ukv3[:, :, QK_NOPE:].reshape(KV_LORA, WIDTH_A).astype(MXU_DTYPE)
    gk_nope = g_kn_a[:QK_NOPE].reshape(1, LANE).astype(F32)
    gk_rope = _rope_pad(g_kn_a[QK_NOPE:], 0).reshape(1, LANE).astype(F32)
    gq_b = (g_qn_b * (HEAD_DIM_B ** -0.5 * LOG2E)).reshape(1, LANE).astype(F32)
    gk_b = g_kn_b.reshape(1, LANE).astype(F32)

    inv = ROPE_THETA ** (-jnp.arange(HALF_ROPE, dtype=F32) / HALF_ROPE)
    ang = positions.reshape(t, 1).astype(F32) * inv
    cos, sin = jnp.cos(ang), jnp.sin(ang)
    zr = jnp.zeros_like(cos)
    cos_t = jnp.concatenate([cos, zr, cos, zr], axis=1)
    sin_t = jnp.concatenate([-sin, zr, sin, zr], axis=1)

    x2 = x.reshape(t, d)
    proj = _in_proj(x2, g_pre, w_pad, tm=1024, tn=2 * tn_in if n_pad % (2 * tn_in) == 0 else tn_in)

    q_a = _qa_proj(proj, off["cq"] // Q_LORA, g_q_lat, w_uq_pad, gq_pad, cos_t, sin_t, q_shift, tm=256)
    k_a, v_a = _kva_proj(proj, off["ckv"] // KV_LORA, off["k_rope"] // LANE, g_kv_lat, w_uk, w_uv,
                         gk_nope, gk_rope, cos_t, sin_t, tm=256)
    o_a = _attn_a(q_a, k_a, v_a, proj, off["gate_a"] // LANE, b=b, l=l, tq=tq, heads=4)

    nq = l // tq
    w_idx = proj[:, off["w_idx"]:off["w_idx"] + IDX_HEADS]
    w_idx_t = w_idx.astype(F32).T
    v_b = proj[:, off["v_b"]:off["v_b"] + HEAD_DIM_B]
    vt = v_b.reshape(b, nq, tq, HEAD_DIM_B).transpose(0, 1, 3, 2)
    blk = {"q_idx": off["q_idx"] // (IDX_HEADS * IDX_DIM), "k_idx": off["k_idx"] // LANE,
           "q_b": off["q_b"] // WIDTH_B, "k_b": off["k_b"] // LANE, "gate_b": off["gate_b"] // WIDTH_B}
    o_b = _attn_b(proj, w_idx_t, vt, t5_bias.astype(F32), gq_b, gk_b, blk, b=b, l=l, tq=tq)

    merged = _merge(o_a, o_b, p_a.astype(MXU_DTYPE), p_b.astype(MXU_DTYPE), proj,
                    off["merge_a"], off["merge_b"], tm=1024, tn=1024)
    out = _out_proj(merged, w_o.astype(MXU_DTYPE), x2, tm=1024, tn=1024)
    return out.reshape(b, l, d)
```

```python
import functools
import math

import jax
import jax.numpy as jnp
from jax import lax
from jax.experimental import pallas as pl
from jax.experimental.pallas import tpu as pltpu

F32 = jnp.float32
I32 = jnp.int32
MXU_DTYPE = jnp.bfloat16

H_A = 16
QK_NOPE = 128
QK_ROPE = 64
QK_DIM_A = QK_NOPE + QK_ROPE
V_DIM_A = 128
Q_LORA = 1024
KV_LORA = 512
ROPE_THETA = 10000.0
H_B = 16
HEAD_DIM_B = 128
IDX_HEADS = 32
IDX_DIM = 64
TOPK_MAX = 256
N_BUCKETS = 32
MAX_DISTANCE = 128
EPS = 1e-6
WIDTH_A = H_A * V_DIM_A
WIDTH_B = H_B * HEAD_DIM_B

LANE = 128
ROW_ALIGN = 32
HALF_ROPE = QK_ROPE // 2
HEAD_PAD_A = 2 * LANE
SHIFT_LANE = LANE + HALF_ROPE
BOUND_MARGIN = 1.0 + 2.0 ** -6
MIN_SHIFTED_SUM = 2.0 ** -64
VMEM_CAP = 56 * 1024 * 1024
VMEM_HEADROOM = 4 * 1024 * 1024

BISECT_STEPS_PER_CHECK = 4
BISECT_MAX_CHECKS = 40
LOG2E = math.log2(math.e)
NEG_INF = float("-inf")
POS_INF = float("inf")


def _nt_dot(a, b):
    return lax.dot_general(a, b, (((1,), (1,)), ((), ())), preferred_element_type=F32)


def _nbytes(shape, dtype):
    return math.prod(shape) * jnp.dtype(dtype).itemsize


def _params(sem, windows, scratch=0, temps=0):
    need = 2 * sum(windows) + scratch + temps + VMEM_HEADROOM
    return pltpu.CompilerParams(dimension_semantics=sem, vmem_limit_bytes=min(need, VMEM_CAP))


def _w_relayout_kernel(start_ref, src_ref, tail_ref, o_ref, *, n_main):
    j = pl.program_id(0)

    @pl.when(j < n_main)
    def _():
        o_ref[...] = src_ref[...].astype(o_ref.dtype)

    @pl.when(j >= n_main)
    def _():
        o_ref[...] = tail_ref[...].astype(o_ref.dtype)


def _w_relayout(wt, src_starts, wt_tail, *, tn, tc):
    d = wt.shape[1]
    n_main, n_tail = len(src_starts), wt_tail.shape[0] // tn
    assert all(s % ROW_ALIGN == 0 for s in src_starts)
    starts = jnp.array([s // ROW_ALIGN for s in src_starts] + [0] * n_tail, I32)
    return pl.pallas_call(
        functools.partial(_w_relayout_kernel, n_main=n_main),
        out_shape=jax.ShapeDtypeStruct(((n_main + n_tail) * tn, d), MXU_DTYPE),
        grid_spec=pltpu.PrefetchScalarGridSpec(
            num_scalar_prefetch=1,
            grid=(n_main + n_tail, d // tc),
            in_specs=[
                pl.BlockSpec((pl.Element(tn), pl.Element(tc)), lambda j, c, st: (st[j] * ROW_ALIGN, c * tc)),
                pl.BlockSpec((tn, tc), lambda j, c, st: (jnp.maximum(j - n_main, 0), c)),
            ],
            out_specs=pl.BlockSpec((tn, tc), lambda j, c, st: (j, c)),
        ),
        compiler_params=_params(("arbitrary", "arbitrary"),
                                [_nbytes((tn, tc), wt.dtype), _nbytes((tn, tc), wt_tail.dtype),
                                 _nbytes((tn, tc), MXU_DTYPE)]),
        name="w_relayout",
    )(starts, wt, wt_tail)


def _in_proj_kernel(x_hbm, g_ref, w_ref, gqb_ref, o_ref, x_buf, hn_ref, x_sem, *, row_chunk, qb_blocks):
    tm = x_buf.shape[0]
    i, j = pl.program_id(0), pl.program_id(1)

    def x_copy(row_block):
        rows = pl.ds(pl.multiple_of(row_block * tm, tm), tm)
        return pltpu.make_async_copy(x_hbm.at[rows, :], x_buf, x_sem)

    @pl.when((i == 0) & (j == 0))
    def _():
        x_copy(0).start()

    @pl.when(j == 0)
    def _():
        x_copy(i).wait()

        def body(r, carry):
            sl = pl.ds(pl.multiple_of(r * row_chunk, row_chunk), row_chunk)
            xx = x_buf[sl, :]
            ms = jnp.mean(xx * xx, axis=-1, keepdims=True)
            hn_ref[sl, :] = (xx * lax.rsqrt(ms + EPS) * g_ref[...]).astype(hn_ref.dtype)
            return carry

        lax.fori_loop(0, tm // row_chunk, body, 0)

    @pl.when((j == 1) & (i + 1 < pl.num_programs(0)))
    def _():
        x_copy(i + 1).start()

    is_qb = (j >= qb_blocks[0]) & (j < qb_blocks[1])

    @pl.when(is_qb)
    def _():
        acc = _nt_dot(hn_ref[...], w_ref[...])
        for h in range(acc.shape[1] // HEAD_DIM_B):
            qh = acc[:, h * HEAD_DIM_B:(h + 1) * HEAD_DIM_B]
            ms = jnp.mean(qh * qh, axis=-1, keepdims=True)
            o_ref[:, h * HEAD_DIM_B:(h + 1) * HEAD_DIM_B] = (
                qh * lax.rsqrt(ms + EPS) * gqb_ref[...]).astype(o_ref.dtype)

    @pl.when(jnp.logical_not(is_qb))
    def _():
        o_ref[...] = _nt_dot(hn_ref[...], w_ref[...]).astype(o_ref.dtype)


def _in_proj(x2, g_pre, wt_pad, gq_b, qb_off, *, tm, tn):
    t, d = x2.shape
    n = wt_pad.shape[0]
    assert n // tn >= 2
    assert qb_off % tn == 0 and WIDTH_B % tn == 0 and tn % HEAD_DIM_B == 0
    qb_blocks = (qb_off // tn, (qb_off + WIDTH_B) // tn)
    return pl.pallas_call(
        functools.partial(_in_proj_kernel, row_chunk=64, qb_blocks=qb_blocks),
        out_shape=jax.ShapeDtypeStruct((t, n), MXU_DTYPE),
        grid=(t // tm, n // tn),
        in_specs=[
            pl.BlockSpec(memory_space=pl.ANY),
            pl.BlockSpec((1, d), lambda i, j: (0, 0)),
            pl.BlockSpec((tn, d), lambda i, j: (j, 0)),
            pl.BlockSpec((1, HEAD_DIM_B), lambda i, j: (0, 0)),
        ],
        out_specs=pl.BlockSpec((tm, tn), lambda i, j: (i, j)),
        scratch_shapes=[pltpu.VMEM((tm, d), x2.dtype), pltpu.VMEM((tm, d), MXU_DTYPE),
                        pltpu.SemaphoreType.DMA(())],
        compiler_params=_params(("arbitrary", "arbitrary"),
                                [_nbytes((1, d), F32), _nbytes((tn, d), MXU_DTYPE), _nbytes((tm, tn), MXU_DTYPE)],
                                scratch=_nbytes((tm, d), x2.dtype) + _nbytes((tm, d), MXU_DTYPE),
                                temps=2 * _nbytes((tm, tn), F32)),
        name="in_proj",
    )(x2, g_pre.reshape(1, d), wt_pad, gq_b)


def _rope_lanes(r, cos_ref, sin_ref):
    return r * cos_ref[...] + pltpu.roll(r, 2 * HALF_ROPE, 1) * sin_ref[...]


def _qa_proj_kernel(cq_ref, gl_ref, w_ref, gq_ref, cos_ref, sin_ref, shift_ref, o_ref):
    c = cq_ref[...].astype(F32)
    ms = jnp.mean(c * c, axis=-1, keepdims=True)
    cn = (c * lax.rsqrt(ms + EPS) * gl_ref[...]).astype(MXU_DTYPE)
    q = jnp.dot(cn, w_ref[...], preferred_element_type=F32)
    for h in range(H_A):
        lo = h * HEAD_PAD_A
        qh = q[:, lo:lo + HEAD_PAD_A]
        ss = jnp.sum(qh * qh, axis=-1, keepdims=True) * (1.0 / QK_DIM_A)
        qn = qh * lax.rsqrt(ss + EPS) * gq_ref[:, lo:lo + HEAD_PAD_A]
        o_ref[:, lo:lo + LANE] = qn[:, :LANE].astype(o_ref.dtype)
        o_ref[:, lo + LANE:lo + HEAD_PAD_A] = (
            _rope_lanes(qn[:, LANE:], cos_ref, sin_ref) + shift_ref[...]).astype(o_ref.dtype)


def _qa_proj(proj, cq_blk, g_q_lat, w_uq_pad, gq_pad, cos_t, sin_t, q_shift, *, tm):
    t = proj.shape[0]
    nq = H_A * HEAD_PAD_A
    return pl.pallas_call(
        _qa_proj_kernel,
        out_shape=jax.ShapeDtypeStruct((t, nq), MXU_DTYPE),
        grid=(t // tm,),
        in_specs=[
            pl.BlockSpec((tm, Q_LORA), lambda i: (i, cq_blk)),
            pl.BlockSpec((1, Q_LORA), lambda i: (0, 0)),
            pl.BlockSpec((Q_LORA, nq), lambda i: (0, 0)),
            pl.BlockSpec((1, nq), lambda i: (0, 0)),
            pl.BlockSpec((tm, LANE), lambda i: (i, 0)),
            pl.BlockSpec((tm, LANE), lambda i: (i, 0)),
            pl.BlockSpec((1, LANE), lambda i: (0, 0)),
        ],
        out_specs=pl.BlockSpec((tm, nq), lambda i: (i, 0)),
        compiler_params=_params(("arbitrary",),
                                [_nbytes((tm, Q_LORA), MXU_DTYPE), _nbytes((Q_LORA, nq), MXU_DTYPE),
                                 _nbytes((1, Q_LORA + nq + LANE), F32), 2 * _nbytes((tm, LANE), F32),
                                 _nbytes((tm, nq), MXU_DTYPE)],
                                temps=_nbytes((tm, nq), F32)),
        name="qa_proj",
    )(proj, g_q_lat.reshape(1, Q_LORA), w_uq_pad, gq_pad, cos_t, sin_t, q_shift)


def _kva_proj_kernel(ckv_ref, kr_ref, gl_ref, wk_ref, wv_ref, gkn_ref, gkr_ref, cos_ref, sin_ref,
                     k_ref, v_ref):
    c = ckv_ref[...].astype(F32)
    ms = jnp.mean(c * c, axis=-1, keepdims=True)
    cn = (c * lax.rsqrt(ms + EPS) * gl_ref[...]).astype(MXU_DTYPE)
    kn = jnp.dot(cn, wk_ref[...], preferred_element_type=F32)
    v_ref[...] = jnp.dot(cn, wv_ref[...], preferred_element_type=F32).astype(v_ref.dtype)
    kr = kr_ref[...].astype(F32)
    ss_r = jnp.sum(kr * kr, axis=-1, keepdims=True)
    krr = _rope_lanes(kr * gkr_ref[...], cos_ref, sin_ref)
    shift_one = (lax.broadcasted_iota(I32, (1, LANE), 1) == SHIFT_LANE - LANE).astype(F32)
    for h in range(H_A):
        kh = kn[:, h * LANE:(h + 1) * LANE]
        ss = (jnp.sum(kh * kh, axis=-1, keepdims=True) + ss_r) * (1.0 / QK_DIM_A)
        rs = lax.rsqrt(ss + EPS)
        lo = h * HEAD_PAD_A
        k_ref[:, lo:lo + LANE] = (kh * rs * gkn_ref[...]).astype(k_ref.dtype)
        k_ref[:, lo + LANE:lo + HEAD_PAD_A] = (krr * rs + shift_one).astype(k_ref.dtype)


def _kva_proj(proj, ckv_blk, krope_blk, g_kv_lat, w_uk, w_uv, gk_nope, gk_rope, cos_t, sin_t, *, tm):
    t = proj.shape[0]
    return pl.pallas_call(
        _kva_proj_kernel,
        out_shape=(jax.ShapeDtypeStruct((t, H_A * HEAD_PAD_A), MXU_DTYPE),
                   jax.ShapeDtypeStruct((t, WIDTH_A), MXU_DTYPE)),
        grid=(t // tm,),
        in_specs=[
            pl.BlockSpec((tm, KV_LORA), lambda i: (i, ckv_blk)),
            pl.BlockSpec((tm, LANE), lambda i: (i, krope_blk)),
            pl.BlockSpec((1, KV_LORA), lambda i: (0, 0)),
            pl.BlockSpec((KV_LORA, H_A * QK_NOPE), lambda i: (0, 0)),
            pl.BlockSpec((KV_LORA, WIDTH_A), lambda i: (0, 0)),
            pl.BlockSpec((1, LANE), lambda i: (0, 0)),
            pl.BlockSpec((1, LANE), lambda i: (0, 0)),
            pl.BlockSpec((tm, LANE), lambda i: (i, 0)),
            pl.BlockSpec((tm, LANE), lambda i: (i, 0)),
        ],
        out_specs=(pl.BlockSpec((tm, H_A * HEAD_PAD_A), lambda i: (i, 0)),
                   pl.BlockSpec((tm, WIDTH_A), lambda i: (i, 0))),
        compiler_params=_params(("arbitrary",),
                                [_nbytes((tm, KV_LORA + LANE), MXU_DTYPE), 2 * _nbytes((KV_LORA, WIDTH_A), MXU_DTYPE),
                                 _nbytes((1, KV_LORA + 2 * LANE), F32), 2 * _nbytes((tm, LANE), F32),
                                 _nbytes((tm, H_A * HEAD_PAD_A + WIDTH_A), MXU_DTYPE)],
                                temps=2 * _nbytes((tm, WIDTH_A), F32)),
        name="kva_proj",
    )(proj, proj, g_kv_lat.reshape(1, KV_LORA), w_uk, w_uv, gk_nope, gk_rope, cos_t, sin_t)


def _silu(g):
    return g * (1.0 / (1.0 + jnp.exp(-g)))


def _lane_tile_reduce(x, op):
    acc = x[:, :LANE]
    for t in range(1, x.shape[1] // LANE):
        acc = op(acc, x[:, t * LANE:(t + 1) * LANE])
    return acc


def _attn_a_kernel(q_ref, k_ref, v_ref, gate_ref, o_ref, *, tq, nq, heads):
    qi = pl.program_id(2)
    causal = lax.broadcasted_iota(I32, (tq, tq), 0) >= lax.broadcasted_iota(I32, (tq, tq), 1)

    def finish(g, l_t, acc):
        vc = slice(g * V_DIM_A, (g + 1) * V_DIM_A)
        l = jnp.sum(l_t, axis=-1, keepdims=True)
        o = acc * (1.0 / l)
        o_ref[:, vc] = (o * _silu(gate_ref[:, vc].astype(F32))).astype(o_ref.dtype)

    def branch(qv):
        n_off = qv * tq
        kcs = [slice(g * HEAD_PAD_A, (g + 1) * HEAD_PAD_A) for g in range(heads)]
        vcs = [slice(g * V_DIM_A, (g + 1) * V_DIM_A) for g in range(heads)]

        l_min = jnp.full((tq, 1), POS_INF, F32)
        for g in range(heads):
            q = q_ref[:, kcs[g]]
            p_diag = jnp.exp2(jnp.where(causal, _nt_dot(q, k_ref[n_off:n_off + tq, kcs[g]]), NEG_INF))
            l_t = _lane_tile_reduce(p_diag, jnp.add)
            acc = jnp.dot(p_diag.astype(MXU_DTYPE), v_ref[n_off:n_off + tq, vcs[g]], preferred_element_type=F32)
            if qv > 0:
                p_off = jnp.exp2(_nt_dot(q, k_ref[0:n_off, kcs[g]]))
                l_t = l_t + _lane_tile_reduce(p_off, jnp.add)
                acc = acc + jnp.dot(p_off.astype(MXU_DTYPE), v_ref[0:n_off, vcs[g]], preferred_element_type=F32)
            l_min = jnp.minimum(l_min, jnp.sum(l_t, axis=-1, keepdims=True))
            finish(g, l_t, acc)

        shift_ok = jnp.min(l_min) >= MIN_SHIFTED_SUM

        @pl.when(jnp.logical_not(shift_ok))
        def _():
            for g in range(heads):
                q = q_ref[:, kcs[g]]
                q_hi = q[:, LANE:]
                lane = lax.broadcasted_iota(I32, q_hi.shape, 1)
                q = jnp.concatenate(
                    [q[:, :LANE], jnp.where(lane == SHIFT_LANE - LANE, 0.0, q_hi.astype(F32)).astype(q.dtype)], axis=1)
                s_diag = jnp.where(causal, _nt_dot(q, k_ref[n_off:n_off + tq, kcs[g]]), NEG_INF)
                m_t = _lane_tile_reduce(s_diag, jnp.maximum)
                if qv > 0:
                    s_off = _nt_dot(q, k_ref[0:n_off, kcs[g]])
                    m_t = jnp.maximum(m_t, _lane_tile_reduce(s_off, jnp.maximum))
                m = jnp.max(m_t, axis=-1, keepdims=True)
                p_diag = jnp.exp2(s_diag - m)
                l_t = _lane_tile_reduce(p_diag, jnp.add)
                acc = jnp.dot(p_diag.astype(MXU_DTYPE), v_ref[n_off:n_off + tq, vcs[g]],
                              preferred_element_type=F32)
                if qv > 0:
                    p_off = jnp.exp2(s_off - m)
                    l_t = l_t + _lane_tile_reduce(p_off, jnp.add)
                    acc = acc + jnp.dot(p_off.astype(MXU_DTYPE), v_ref[0:n_off, vcs[g]], preferred_element_type=F32)
                finish(g, l_t, acc)

    for qv in range(nq):
        pl.when(qi == qv)(functools.partial(branch, qv))


def _attn_a(q_a, k_a, v_a, proj, gate_blk0, *, b, l, tq, heads):
    t = q_a.shape[0]
    nq = l // tq
    kw, vw = heads * HEAD_PAD_A, heads * V_DIM_A
    return pl.pallas_call(
        functools.partial(_attn_a_kernel, tq=tq, nq=nq, heads=heads),
        out_shape=jax.ShapeDtypeStruct((t, WIDTH_A), MXU_DTYPE),
        grid=(b, H_A // heads, nq),
        in_specs=[
            pl.BlockSpec((tq, kw), lambda bi, h, qi: (bi * nq + qi, h)),
            pl.BlockSpec((l, kw), lambda bi, h, qi: (bi, h)),
            pl.BlockSpec((l, vw), lambda bi, h, qi: (bi, h)),
            pl.BlockSpec((tq, vw), lambda bi, h, qi: (bi * nq + qi, gate_blk0 // heads + h)),
        ],
        out_specs=pl.BlockSpec((tq, vw), lambda bi, h, qi: (bi * nq + qi, h)),
        compiler_params=_params(("arbitrary", "arbitrary", "arbitrary"),
                                [_nbytes((tq + l, kw), MXU_DTYPE), _nbytes((l + 2 * tq, vw), MXU_DTYPE)],
                                temps=heads * (_nbytes((tq, l), F32) + _nbytes((tq, l), MXU_DTYPE))),
        name="attn_a",
    )(q_a, k_a, v_a, proj)


def _t5_bucket(dist):
    max_exact = N_BUCKETS // 2
    n = jnp.maximum(dist, 0)
    nf = jnp.maximum(n, 1).astype(F32)
    large = max_exact + (jnp.log(nf / max_exact) / math.log(MAX_DISTANCE / max_exact)
                         * (N_BUCKETS - max_exact)).astype(I32)
    large = jnp.minimum(large, N_BUCKETS - 1)
    return jnp.where(n < max_exact, n, large)


def _attn_b_kernel(t5_ref, qidx_ref, kidx_ref, wt_ref, qb_ref, kb_ref, vt_ref, gate_ref, gq_ref, gk_ref,
                   o_ref, sc_ref, qn_ref, acc_ref, m_ref, l_ref, bias_ref, thr_ref, kn_ref, bound_ref,
                   *, tq, nq, topk, max_iters):
    bi = pl.program_id(0)
    qi = pl.program_id(1)
    ck = tq
    shape = (ck, tq)

    @pl.when((bi == 0) & (qi == 0))
    def _():
        s_loc = lax.broadcasted_iota(I32, shape, 0)
        t_loc = lax.broadcasted_iota(I32, shape, 1)
        for near in range(2):
            bucket = _t5_bucket(t_loc - s_loc + (1 - near) * ck)

            def per_head(h, carry, bucket=bucket, near=near):
                far = t5_ref[N_BUCKETS - 1, h]
                tab = jnp.zeros(shape, F32)
                largest = jnp.float32(0.0)
                for bk in range(N_BUCKETS - 1):
                    rel = (t5_ref[bk, h] - far) * LOG2E
                    tab = jnp.where(bucket == bk, rel, tab)
                    largest = jnp.maximum(largest, rel)
                bias_ref[h, near] = tab
                gains = jnp.max(jnp.abs(gq_ref[...])) * jnp.max(jnp.abs(gk_ref[...]))
                bound_ref[h] = HEAD_DIM_B * gains * BOUND_MARGIN + largest
                return carry

            lax.fori_loop(0, H_B, per_head, 0)

    w_all = wt_ref[...] * (IDX_HEADS ** -0.5)

    def score_chunk(c, diag):
        rows = pl.ds(pl.multiple_of(c * ck, ck), ck)
        kx = kidx_ref[rows, 0:IDX_DIM]
        zk = jnp.zeros_like(kx)
        kab = jnp.concatenate([jnp.concatenate([kx, zk], axis=1), jnp.concatenate([zk, kx], axis=1)], axis=0)
        score = jnp.zeros(shape, F32)
        for j in range(IDX_HEADS // 2):
            qp = qidx_ref[:, j * LANE:(j + 1) * LANE]
            logits = jnp.maximum(_nt_dot(kab, qp), 0.0)
            score = score + logits[:ck] * w_all[2 * j:2 * j + 1, :]
            score = score + logits[ck:] * w_all[2 * j + 1:2 * j + 2, :]
        if diag:
            adm = lax.broadcasted_iota(I32, shape, 0) <= lax.broadcasted_iota(I32, shape, 1)
            lo_src = jnp.where(adm, score, POS_INF)
            score = jnp.where(adm, score, NEG_INF)
        else:
            lo_src = score
        sc_ref[c] = score
        return jnp.max(score, axis=0, keepdims=True), jnp.min(lo_src, axis=0, keepdims=True)

    def score_body(c, carry):
        mx, mn = carry
        cmx, cmn = score_chunk(c, False)
        return jnp.maximum(mx, cmx), jnp.minimum(mn, cmn)

    mx, mn = lax.fori_loop(0, qi, score_body,
                           (jnp.full((1, tq), NEG_INF, F32), jnp.full((1, tq), POS_INF, F32)))
    dmx, dmn = score_chunk(qi, True)
    mx = jnp.maximum(mx, dmx)
    mn = jnp.minimum(mn, dmn)

    rep = (8, tq)
    n_adm = qi * tq + lax.broadcasted_iota(I32, rep, 1) + 1
    kp = jnp.minimum(n_adm, topk)
    mx8 = jnp.broadcast_to(mx, rep)
    mn8 = jnp.broadcast_to(mn, rep)

    def bisect(nchunks):
        def count_ge(x):
            acc = jnp.zeros(rep, I32)
            for c in range(nchunks):
                ge = sc_ref[c].reshape(ck // 8, 8, tq) >= x[None]
                acc = acc + jnp.sum(ge.astype(I32), axis=0)
            for shift in (4, 2, 1):
                acc = acc + pltpu.roll(acc, shift, 0)
            return acc

        def bis_cond(st):
            it, lo, hi, mid, cnt_lo = st
            active = (cnt_lo != kp) & (mid > lo) & (mid < hi)
            return jnp.logical_and(it < max_iters, jnp.max(active.astype(I32)) > 0)

        def bis_body(st):
            it, lo, hi, mid, cnt_lo = st
            for _ in range(BISECT_STEPS_PER_CHECK):
                cnt = count_ge(mid)
                ge = cnt >= kp
                lo = jnp.where(ge, mid, lo)
                cnt_lo = jnp.where(ge, cnt, cnt_lo)
                hi = jnp.where(ge, hi, mid)
                mid = jnp.where(hi == POS_INF, mx8, lo + 0.5 * (hi - lo))
            return it + 1, lo, hi, mid, cnt_lo

        _, lo, hi, _, cnt_lo = lax.while_loop(
            bis_cond, bis_body, (jnp.int32(0), mn8, jnp.full(rep, POS_INF, F32), mx8, n_adm))
        thr_ref[...] = lo

        tied = cnt_lo > kp

        @pl.when(jnp.max(tied.astype(I32)) > 0)
        def _():
            n_keys = nchunks * ck
            sub = lax.broadcasted_iota(I32, (ck // 8, 8, tq), 0) * 8 + lax.broadcasted_iota(I32, (ck // 8, 8, tq), 1)

            def count_kept(j_last):
                acc = jnp.zeros(rep, I32)
                for c in range(nchunks):
                    s3 = sc_ref[c].reshape(ck // 8, 8, tq)
                    keep = (s3 >= hi[None]) | ((s3 >= lo[None]) & (sub + c * ck <= j_last[None]))
                    acc = acc + jnp.sum(keep.astype(I32), axis=0)
                for shift in (4, 2, 1):
                    acc = acc + pltpu.roll(acc, shift, 0)
                return acc

            def idx_step(_, carry):
                j_lo, j_hi = carry
                j_mid = j_lo + ((j_hi - j_lo) >> 1)
                ok = count_kept(j_mid) >= kp
                return jnp.where(ok, j_lo, j_mid), jnp.where(ok, j_mid, j_hi)

            _, j_hi = lax.fori_loop(0, max(1, (n_keys - 1).bit_length()), idx_step,
                                    (jnp.full(rep, -1, I32), jnp.full(rep, n_keys - 1, I32)))
            j_last = jnp.where(tied, j_hi, n_keys - 1)
            for c in range(nchunks):
                s3 = sc_ref[c].reshape(ck // 8, 8, tq)
                drop = (s3 >= lo[None]) & (s3 < hi[None]) & (sub + c * ck > j_last[None])
                sc_ref[c] = jnp.where(drop, NEG_INF, s3).reshape(ck, tq)

    for qv in range(nq):
        pl.when(qi == qv)(functools.partial(bisect, qv + 1))
    thr = thr_ref[0:1, :]

    @pl.when(qi == 0)
    def _():
        kf = kb_ref[...].astype(F32)
        ms = jnp.mean(kf * kf, axis=-1, keepdims=True)
        kn_ref[...] = (kf * lax.rsqrt(ms + EPS) * gk_ref[...]).astype(kn_ref.dtype)

    for h in range(H_B):
        qn_ref[h * tq:(h + 1) * tq, :] = qb_ref[:, h * LANE:(h + 1) * LANE]

    def attend_chunk(c, near, exact):
        rows = pl.ds(pl.multiple_of(c * ck, ck), ck)
        vt = vt_ref[c]
        sel = sc_ref[c] >= thr
        s_all = _nt_dot(kn_ref[rows, :], qn_ref[...])
        for h in range(H_B):
            s = s_all[:, h * tq:(h + 1) * tq]
            if near is not None:
                s = s + bias_ref[h, near]
            if exact:
                s = jnp.where(sel, s, NEG_INF)
                m_old = m_ref[h]
                m_new = jnp.maximum(m_old, jnp.max(s, axis=0, keepdims=True))
                m_safe = jnp.where(m_new == NEG_INF, 0.0, m_new)
                p = jnp.exp2(s - m_safe)
                alpha = jnp.exp2(m_old - m_safe)
                l_ref[h] = alpha * l_ref[h] + jnp.sum(p, axis=0, keepdims=True)
                acc_ref[h] = alpha * acc_ref[h] + jnp.dot(vt, p.astype(MXU_DTYPE), preferred_element_type=F32)
                m_ref[h] = m_new
            else:
                p = jnp.exp2(jnp.where(sel, s - bound_ref[h], NEG_INF))
                l_ref[h] = l_ref[h] + jnp.sum(p, axis=0, keepdims=True)
                acc_ref[h] = acc_ref[h] + jnp.dot(vt, p.astype(MXU_DTYPE), preferred_element_type=F32)

    def attend(exact):
        if exact:
            m_ref[...] = jnp.full(m_ref.shape, NEG_INF, F32)
        l_ref[...] = jnp.zeros(l_ref.shape, F32)
        acc_ref[...] = jnp.zeros(acc_ref.shape, F32)

        def far_body(c, carry):
            attend_chunk(c, None, exact)
            return carry

        lax.fori_loop(0, jnp.maximum(qi - 1, 0), far_body, 0)

        @pl.when(qi >= 1)
        def _():
            attend_chunk(qi - 1, 0, exact)

        attend_chunk(qi, 1, exact)

    attend(False)
    l_min = l_ref[0]
    for h in range(1, H_B):
        l_min = jnp.minimum(l_min, l_ref[h])
    shift_ok = jnp.min(l_min) >= MIN_SHIFTED_SUM
    pl.when(jnp.logical_not(shift_ok))(functools.partial(attend, True))

    for h in range(H_B):
        o_t = acc_ref[h] * (1.0 / l_ref[h])
        g = gate_ref[:, h * LANE:(h + 1) * LANE].astype(F32)
        o_ref[:, h * LANE:(h + 1) * LANE] = (o_t.T * _silu(g)).astype(o_ref.dtype)


def _attn_b(proj, wt, vt, t5_bias, gq_b, gk_b, blk, *, b, l, tq):
    t = proj.shape[0]
    nq = l // tq
    topk = min(TOPK_MAX, l // 4)
    row = lambda bi, qi: bi * nq + qi
    return pl.pallas_call(
        functools.partial(_attn_b_kernel, tq=tq, nq=nq, topk=topk, max_iters=BISECT_MAX_CHECKS),
        out_shape=jax.ShapeDtypeStruct((t, WIDTH_B), MXU_DTYPE),
        grid=(b, nq),
        in_specs=[
            pl.BlockSpec(memory_space=pltpu.SMEM),
            pl.BlockSpec((tq, IDX_HEADS * IDX_DIM), lambda bi, qi: (row(bi, qi), blk["q_idx"])),
            pl.BlockSpec((l, LANE), lambda bi, qi: (bi, blk["k_idx"])),
            pl.BlockSpec((IDX_HEADS, tq), lambda bi, qi: (0, row(bi, qi))),
            pl.BlockSpec((tq, WIDTH_B), lambda bi, qi: (row(bi, qi), blk["q_b"])),
            pl.BlockSpec((l, LANE), lambda bi, qi: (bi, blk["k_b"])),
            pl.BlockSpec((None, nq, LANE, tq), lambda bi, qi: (bi, 0, 0, 0)),
            pl.BlockSpec((tq, WIDTH_B), lambda bi, qi: (row(bi, qi), blk["gate_b"])),
            pl.BlockSpec((1, LANE), lambda bi, qi: (0, 0)),
            pl.BlockSpec((1, LANE), lambda bi, qi: (0, 0)),
        ],
        out_specs=pl.BlockSpec((tq, WIDTH_B), lambda bi, qi: (row(bi, qi), 0)),
        scratch_shapes=[
            pltpu.VMEM((nq, tq, tq), F32),
            pltpu.VMEM((H_B * tq, LANE), MXU_DTYPE),
            pltpu.VMEM((H_B, LANE, tq), F32),
            pltpu.VMEM((H_B, 1, tq), F32),
            pltpu.VMEM((H_B, 1, tq), F32),
            pltpu.VMEM((H_B, 2, tq, tq), F32),
            pltpu.VMEM((8, tq), F32),
            pltpu.VMEM((l, LANE), MXU_DTYPE),
            pltpu.SMEM((H_B,), F32),
        ],
        compiler_params=_params(
            ("arbitrary", "arbitrary"),
            [_nbytes((tq, IDX_HEADS * IDX_DIM + 2 * WIDTH_B), MXU_DTYPE), _nbytes((l, 2 * LANE), MXU_DTYPE),
             _nbytes((IDX_HEADS, tq), F32), _nbytes((nq, LANE, tq), MXU_DTYPE), _nbytes((tq, WIDTH_B), MXU_DTYPE)],
            scratch=(_nbytes((nq, tq, tq), F32) + _nbytes((H_B * tq + l, LANE), MXU_DTYPE)
                     + _nbytes((H_B, LANE + 2 * 8, tq), F32) + _nbytes((H_B, 2, tq, tq), F32) + _nbytes((8, tq), F32)),
            temps=2 * _nbytes((tq, H_B * tq), F32)),
        name="attn_b",
    )(t5_bias, proj, proj, wt, proj, proj, vt, proj, gq_b, gk_b)


def _sigmoid(z):
    return 1.0 / (1.0 + jnp.exp(-z))


def _merge_kernel(oa_ref, ob_ref, pa_ref, pb_ref, ma_ref, mb_ref, o_ref):
    a = jnp.dot(oa_ref[...], pa_ref[...], preferred_element_type=F32)
    bb = jnp.dot(ob_ref[...], pb_ref[...], preferred_element_type=F32)
    o_ref[...] = (_sigmoid(ma_ref[...].astype(F32)) * a + _sigmoid(mb_ref[...].astype(F32)) * bb).astype(o_ref.dtype)


def _merge(o_a, o_b, p_a, p_b, proj, ma_off, mb_off, *, tm, tn):
    t = o_a.shape[0]
    d = p_a.shape[1]
    ma0, mb0 = ma_off // tn, mb_off // tn
    return pl.pallas_call(
        _merge_kernel,
        out_shape=jax.ShapeDtypeStruct((t, d), MXU_DTYPE),
        grid=(t // tm, d // tn),
        in_specs=[
            pl.BlockSpec((tm, WIDTH_A), lambda i, j: (i, 0)),
            pl.BlockSpec((tm, WIDTH_B), lambda i, j: (i, 0)),
            pl.BlockSpec((WIDTH_A, tn), lambda i, j: (0, j)),
            pl.BlockSpec((WIDTH_B, tn), lambda i, j: (0, j)),
            pl.BlockSpec((tm, tn), lambda i, j: (i, ma0 + j)),
            pl.BlockSpec((tm, tn), lambda i, j: (i, mb0 + j)),
        ],
        out_specs=pl.BlockSpec((tm, tn), lambda i, j: (i, j)),
        compiler_params=_params(("arbitrary", "arbitrary"),
                                [_nbytes((tm, WIDTH_A + WIDTH_B), MXU_DTYPE), _nbytes((WIDTH_A + WIDTH_B, tn), MXU_DTYPE),
                                 3 * _nbytes((tm, tn), MXU_DTYPE)],
                                temps=2 * _nbytes((tm, tn), F32)),
        name="merge",
    )(o_a, o_b, p_a, p_b, proj, proj)


def _out_proj_kernel(m_ref, w_ref, x_ref, o_ref):
    o_ref[...] = x_ref[...] + jnp.dot(m_ref[...], w_ref[...], preferred_element_type=F32)


def _out_proj(merged, w_o, x2, *, tm, tn):
    t, d = x2.shape
    return pl.pallas_call(
        _out_proj_kernel,
        out_shape=jax.ShapeDtypeStruct((t, d), x2.dtype),
        grid=(t // tm, d // tn),
        in_specs=[
            pl.BlockSpec((tm, d), lambda i, j: (i, 0)),
            pl.BlockSpec((d, tn), lambda i, j: (0, j)),
            pl.BlockSpec((tm, tn), lambda i, j: (i, j)),
        ],
        out_specs=pl.BlockSpec((tm, tn), lambda i, j: (i, j)),
        compiler_params=_params(("arbitrary", "arbitrary"),
                                [_nbytes((tm, d), MXU_DTYPE), _nbytes((d, tn), MXU_DTYPE), 2 * _nbytes((tm, tn), x2.dtype)],
                                temps=_nbytes((tm, tn), F32)),
        name="out_proj",
    )(merged, w_o, x2)


def _rope_pad(a, axis):
    a1, a2 = jnp.split(a, 2, axis=axis)
    z = jnp.zeros_like(a1)
    return jnp.concatenate([a1, z, a2, z], axis=axis)


def _layout(d):
    names = [("merge_a", d), ("merge_b", d), ("gate_a", WIDTH_A), ("gate_b", WIDTH_B),
             ("q_b", WIDTH_B), ("q_idx", IDX_HEADS * IDX_DIM), ("cq", Q_LORA), ("ckv", KV_LORA),
             ("k_idx", LANE), ("k_rope", LANE), ("k_b", LANE), ("v_b", LANE)]
    off, out = 0, {}
    for name, width in names:
        assert off % width == 0, (name, off, width)
        out[name] = off
        off += width
    out["w_idx"] = out["k_idx"] + IDX_DIM
    return out, off


def kernel(x, positions, g_pre, w_in, g_q_lat, g_kv_lat, w_uq, w_ukv, g_qn_a, g_kn_a,
           g_qn_b, g_kn_b, t5_bias, p_a, p_b, w_o):
    b, l, d = x.shape
    t = b * l
    tq = 256
    tn_in = 512
    off, n_used = _layout(d)
    n_pad = -(-n_used // tn_in) * tn_in

    names = ["cq", "ckv", "k_rope", "q_b", "k_b", "v_b", "q_idx", "k_idx", "w_idx", "gate_a", "gate_b",
             "merge_a", "merge_b"]
    sizes = [Q_LORA, KV_LORA, QK_ROPE, WIDTH_B, HEAD_DIM_B, HEAD_DIM_B, IDX_HEADS * IDX_DIM, IDX_DIM,
             IDX_HEADS, WIDTH_A, WIDTH_B, d, d]
    src, acc = {}, 0
    for name, s in zip(names, sizes):
        src[name] = (acc, s)
        acc += s
    main_groups = ["merge_a", "merge_b", "gate_a", "gate_b", "q_b", "q_idx", "cq", "ckv"]
    src_starts, dst = [], 0
    for name in main_groups:
        assert off[name] == dst and src[name][1] % tn_in == 0
        src_starts += [src[name][0] + c for c in range(0, src[name][1], tn_in)]
        dst += src[name][1]
    wt = w_in.T
    rows = lambda name: wt[src[name][0]:src[name][0] + src[name][1]]
    z = lambda n: jnp.zeros((n, d), w_in.dtype)
    assert off["k_idx"] == dst
    wt_tail = jnp.concatenate(
        [rows("k_idx"), rows("w_idx"), z(LANE - IDX_DIM - IDX_HEADS), _rope_pad(rows("k_rope"), 0),
         rows("k_b"), rows("v_b"), z(n_pad - n_used)], axis=0)
    w_pad = _w_relayout(wt, src_starts, wt_tail, tn=tn_in, tc=d)

    w_uq3 = w_uq.reshape(Q_LORA, H_A, QK_DIM_A)
    w_uq_pad = jnp.concatenate([w_uq3[:, :, :QK_NOPE], _rope_pad(w_uq3[:, :, QK_NOPE:], 2)], axis=2)
    w_uq_pad = w_uq_pad.reshape(Q_LORA, H_A * HEAD_PAD_A).astype(MXU_DTYPE)
    gq_head = jnp.concatenate([g_qn_a[:QK_NOPE], _rope_pad(g_qn_a[QK_NOPE:], 0)]) * (QK_DIM_A ** -0.5 * LOG2E)
    gq_pad = jnp.tile(gq_head, H_A).reshape(1, H_A * HEAD_PAD_A).astype(F32)
    bound_a = QK_DIM_A * jnp.max(jnp.abs(gq_head)) * jnp.max(jnp.abs(g_kn_a)) * BOUND_MARGIN
    q_shift = jnp.zeros((1, LANE), F32).at[0, SHIFT_LANE - LANE].set(-bound_a)
    w_ukv3 = w_ukv.reshape(KV_LORA, H_A, QK_NOPE + V_DIM_A)
    w_uk = w_ukv3[:, :, :QK_NOPE].reshape(KV_LORA, H_A * QK_NOPE).astype(MXU_DTYPE)
    w_uv = w_ukv3[:, :, QK_NOPE:].reshape(KV_LORA, WIDTH_A).astype(MXU_DTYPE)
    gk_nope = g_kn_a[:QK_NOPE].reshape(1, LANE).astype(F32)
    gk_rope = _rope_pad(g_kn_a[QK_NOPE:], 0).reshape(1, LANE).astype(F32)
    gq_b = (g_qn_b * (HEAD_DIM_B ** -0.5 * LOG2E)).reshape(1, LANE).astype(F32)
    gk_b = g_kn_b.reshape(1, LANE).astype(F32)

    inv = ROPE_THETA ** (-jnp.arange(HALF_ROPE, dtype=F32) / HALF_ROPE)
    ang = positions.reshape(t, 1).astype(F32) * inv
    cos, sin = jnp.cos(ang), jnp.sin(ang)
    zr = jnp.zeros_like(cos)
    cos_t = jnp.concatenate([cos, zr, cos, zr], axis=1)
    sin_t = jnp.concatenate([-sin, zr, sin, zr], axis=1)

    x2 = x.reshape(t, d)
    proj = _in_proj(x2, g_pre, w_pad, gq_b, off["q_b"], tm=1024,
                    tn=2 * tn_in if n_pad % (2 * tn_in) == 0 else tn_in)

    q_a = _qa_proj(proj, off["cq"] // Q_LORA, g_q_lat, w_uq_pad, gq_pad, cos_t, sin_t, q_shift, tm=256)
    k_a, v_a = _kva_proj(proj, off["ckv"] // KV_LORA, off["k_rope"] // LANE, g_kv_lat, w_uk, w_uv,
                         gk_nope, gk_rope, cos_t, sin_t, tm=256)
    o_a = _attn_a(q_a, k_a, v_a, proj, off["gate_a"] // LANE, b=b, l=l, tq=tq, heads=4)

    nq = l // tq
    w_idx = proj[:, off["w_idx"]:off["w_idx"] + IDX_HEADS]
    w_idx_t = w_idx.astype(F32).T
    v_b = proj[:, off["v_b"]:off["v_b"] + HEAD_DIM_B]
    vt = v_b.reshape(b, nq, tq, HEAD_DIM_B).transpose(0, 1, 3, 2)
    blk = {"q_idx": off["q_idx"] // (IDX_HEADS * IDX_DIM), "k_idx": off["k_idx"] // LANE,
           "q_b": off["q_b"] // WIDTH_B, "k_b": off["k_b"] // LANE, "gate_b": off["gate_b"] // WIDTH_B}
    o_b = _attn_b(proj, w_idx_t, vt, t5_bias.astype(F32), gq_b, gk_b, blk, b=b, l=l, tq=tq)

    merged = _merge(o_a, o_b, p_a.astype(MXU_DTYPE), p_b.astype(MXU_DTYPE), proj,
                    off["merge_a"], off["merge_b"], tm=1024, tn=1024)
    out = _out_proj(merged, w_o.astype(MXU_DTYPE), x2, tm=1024, tn=1024)
    return out.reshape(b, l, d)
```

```python
import functools
import math

import jax
import jax.numpy as jnp
from jax import lax
from jax.experimental import pallas as pl
from jax.experimental.pallas import tpu as pltpu

F32 = jnp.float32
I32 = jnp.int32
MXU_DTYPE = jnp.bfloat16

H_A = 16
QK_NOPE = 128
QK_ROPE = 64
QK_DIM_A = QK_NOPE + QK_ROPE
V_DIM_A = 128
Q_LORA = 1024
KV_LORA = 512
ROPE_THETA = 10000.0
H_B = 16
HEAD_DIM_B = 128
IDX_HEADS = 32
IDX_DIM = 64
TOPK_MAX = 256
N_BUCKETS = 32
MAX_DISTANCE = 128
EPS = 1e-6
WIDTH_A = H_A * V_DIM_A
WIDTH_B = H_B * HEAD_DIM_B

LANE = 128
ROW_ALIGN = 32
HALF_ROPE = QK_ROPE // 2
HEAD_PAD_A = 2 * LANE
SHIFT_LANE = LANE + HALF_ROPE
BOUND_MARGIN = 1.0 + 2.0 ** -6
MIN_SHIFTED_SUM = 2.0 ** -64
VMEM_CAP = 56 * 1024 * 1024
VMEM_HEADROOM = 4 * 1024 * 1024

BISECT_STEPS_PER_CHECK = 5
BISECT_MAX_CHECKS = 32
LOG2E = math.log2(math.e)
NEG_INF = float("-inf")
POS_INF = float("inf")


def _nt_dot(a, b):
    return lax.dot_general(a, b, (((1,), (1,)), ((), ())), preferred_element_type=F32)


def _nbytes(shape, dtype):
    return math.prod(shape) * jnp.dtype(dtype).itemsize


def _params(sem, windows, scratch=0, temps=0):
    need = 2 * sum(windows) + scratch + temps + VMEM_HEADROOM
    return pltpu.CompilerParams(dimension_semantics=sem, vmem_limit_bytes=min(need, VMEM_CAP))


def _w_relayout_kernel(start_ref, src_ref, tail_ref, o_ref, *, n_main):
    j = pl.program_id(0)

    @pl.when(j < n_main)
    def _():
        o_ref[...] = src_ref[...].astype(o_ref.dtype)

    @pl.when(j >= n_main)
    def _():
        o_ref[...] = tail_ref[...].astype(o_ref.dtype)


def _w_relayout(wt, src_starts, wt_tail, *, tn, tc):
    d = wt.shape[1]
    n_main, n_tail = len(src_starts), wt_tail.shape[0] // tn
    assert all(s % ROW_ALIGN == 0 for s in src_starts)
    starts = jnp.array([s // ROW_ALIGN for s in src_starts] + [0] * n_tail, I32)
    return pl.pallas_call(
        functools.partial(_w_relayout_kernel, n_main=n_main),
        out_shape=jax.ShapeDtypeStruct(((n_main + n_tail) * tn, d), MXU_DTYPE),
        grid_spec=pltpu.PrefetchScalarGridSpec(
            num_scalar_prefetch=1,
            grid=(n_main + n_tail, d // tc),
            in_specs=[
                pl.BlockSpec((pl.Element(tn), pl.Element(tc)), lambda j, c, st: (st[j] * ROW_ALIGN, c * tc)),
                pl.BlockSpec((tn, tc), lambda j, c, st: (jnp.maximum(j - n_main, 0), c)),
            ],
            out_specs=pl.BlockSpec((tn, tc), lambda j, c, st: (j, c)),
        ),
        compiler_params=_params(("arbitrary", "arbitrary"),
                                [_nbytes((tn, tc), wt.dtype), _nbytes((tn, tc), wt_tail.dtype),
                                 _nbytes((tn, tc), MXU_DTYPE)]),
        name="w_relayout",
    )(starts, wt, wt_tail)


def _in_proj_kernel(x_hbm, g_ref, w_ref, o_ref, x_buf, hn_ref, x_sem, *, row_chunk):
    tm = x_buf.shape[0]
    i, j = pl.program_id(0), pl.program_id(1)

    def x_copy(row_block):
        rows = pl.ds(pl.multiple_of(row_block * tm, tm), tm)
        return pltpu.make_async_copy(x_hbm.at[rows, :], x_buf, x_sem)

    @pl.when((i == 0) & (j == 0))
    def _():
        x_copy(0).start()

    @pl.when(j == 0)
    def _():
        x_copy(i).wait()

        def body(r, carry):
            sl = pl.ds(pl.multiple_of(r * row_chunk, row_chunk), row_chunk)
            xx = x_buf[sl, :]
            ms = jnp.mean(xx * xx, axis=-1, keepdims=True)
            hn_ref[sl, :] = (xx * lax.rsqrt(ms + EPS) * g_ref[...]).astype(hn_ref.dtype)
            return carry

        lax.fori_loop(0, tm // row_chunk, body, 0)

    @pl.when((j == 1) & (i + 1 < pl.num_programs(0)))
    def _():
        x_copy(i + 1).start()

    o_ref[...] = _nt_dot(hn_ref[...], w_ref[...]).astype(o_ref.dtype)


def _in_proj(x2, g_pre, wt_pad, *, tm, tn):
    t, d = x2.shape
    n = wt_pad.shape[0]
    assert n // tn >= 2
    return pl.pallas_call(
        functools.partial(_in_proj_kernel, row_chunk=64),
        out_shape=jax.ShapeDtypeStruct((t, n), MXU_DTYPE),
        grid=(t // tm, n // tn),
        in_specs=[
            pl.BlockSpec(memory_space=pl.ANY),
            pl.BlockSpec((1, d), lambda i, j: (0, 0)),
            pl.BlockSpec((tn, d), lambda i, j: (j, 0)),
        ],
        out_specs=pl.BlockSpec((tm, tn), lambda i, j: (i, j)),
        scratch_shapes=[pltpu.VMEM((tm, d), x2.dtype), pltpu.VMEM((tm, d), MXU_DTYPE),
                        pltpu.SemaphoreType.DMA(())],
        compiler_params=_params(("arbitrary", "arbitrary"),
                                [_nbytes((1, d), F32), _nbytes((tn, d), MXU_DTYPE), _nbytes((tm, tn), MXU_DTYPE)],
                                scratch=_nbytes((tm, d), x2.dtype) + _nbytes((tm, d), MXU_DTYPE),
                                temps=2 * _nbytes((tm, tn), F32)),
        name="in_proj",
    )(x2, g_pre.reshape(1, d), wt_pad)


def _rope_lanes(r, cos_ref, sin_ref):
    return r * cos_ref[...] + pltpu.roll(r, 2 * HALF_ROPE, 1) * sin_ref[...]


def _qa_proj_kernel(cq_ref, gl_ref, w_ref, gq_ref, cos_ref, sin_ref, shift_ref, o_ref):
    c = cq_ref[...].astype(F32)
    ms = jnp.mean(c * c, axis=-1, keepdims=True)
    cn = (c * lax.rsqrt(ms + EPS) * gl_ref[...]).astype(MXU_DTYPE)
    q = jnp.dot(cn, w_ref[...], preferred_element_type=F32)
    for h in range(H_A):
        lo = h * HEAD_PAD_A
        qh = q[:, lo:lo + HEAD_PAD_A]
        ss = jnp.sum(qh * qh, axis=-1, keepdims=True) * (1.0 / QK_DIM_A)
        qn = qh * lax.rsqrt(ss + EPS) * gq_ref[:, lo:lo + HEAD_PAD_A]
        o_ref[:, lo:lo + LANE] = qn[:, :LANE].astype(o_ref.dtype)
        o_ref[:, lo + LANE:lo + HEAD_PAD_A] = (
            _rope_lanes(qn[:, LANE:], cos_ref, sin_ref) + shift_ref[...]).astype(o_ref.dtype)


def _qa_proj(proj, cq_blk, g_q_lat, w_uq_pad, gq_pad, cos_t, sin_t, q_shift, *, tm):
    t = proj.shape[0]
    nq = H_A * HEAD_PAD_A
    return pl.pallas_call(
        _qa_proj_kernel,
        out_shape=jax.ShapeDtypeStruct((t, nq), MXU_DTYPE),
        grid=(t // tm,),
        in_specs=[
            pl.BlockSpec((tm, Q_LORA), lambda i: (i, cq_blk)),
            pl.BlockSpec((1, Q_LORA), lambda i: (0, 0)),
            pl.BlockSpec((Q_LORA, nq), lambda i: (0, 0)),
            pl.BlockSpec((1, nq), lambda i: (0, 0)),
            pl.BlockSpec((tm, LANE), lambda i: (i, 0)),
            pl.BlockSpec((tm, LANE), lambda i: (i, 0)),
            pl.BlockSpec((1, LANE), lambda i: (0, 0)),
        ],
        out_specs=pl.BlockSpec((tm, nq), lambda i: (i, 0)),
        compiler_params=_params(("arbitrary",),
                                [_nbytes((tm, Q_LORA), MXU_DTYPE), _nbytes((Q_LORA, nq), MXU_DTYPE),
                                 _nbytes((1, Q_LORA + nq + LANE), F32), 2 * _nbytes((tm, LANE), F32),
                                 _nbytes((tm, nq), MXU_DTYPE)],
                                temps=_nbytes((tm, nq), F32)),
        name="qa_proj",
    )(proj, g_q_lat.reshape(1, Q_LORA), w_uq_pad, gq_pad, cos_t, sin_t, q_shift)


def _kva_proj_kernel(ckv_ref, kr_ref, gl_ref, wk_ref, wv_ref, gkn_ref, gkr_ref, cos_ref, sin_ref,
                     k_ref, v_ref):
    c = ckv_ref[...].astype(F32)
    ms = jnp.mean(c * c, axis=-1, keepdims=True)
    cn = (c * lax.rsqrt(ms + EPS) * gl_ref[...]).astype(MXU_DTYPE)
    kn = jnp.dot(cn, wk_ref[...], preferred_element_type=F32)
    v_ref[...] = jnp.dot(cn, wv_ref[...], preferred_element_type=F32).astype(v_ref.dtype)
    kr = kr_ref[...].astype(F32)
    ss_r = jnp.sum(kr * kr, axis=-1, keepdims=True)
    krr = _rope_lanes(kr * gkr_ref[...], cos_ref, sin_ref)
    shift_one = (lax.broadcasted_iota(I32, (1, LANE), 1) == SHIFT_LANE - LANE).astype(F32)
    for h in range(H_A):
        kh = kn[:, h * LANE:(h + 1) * LANE]
        ss = (jnp.sum(kh * kh, axis=-1, keepdims=True) + ss_r) * (1.0 / QK_DIM_A)
        rs = lax.rsqrt(ss + EPS)
        lo = h * HEAD_PAD_A
        k_ref[:, lo:lo + LANE] = (kh * rs * gkn_ref[...]).astype(k_ref.dtype)
        k_ref[:, lo + LANE:lo + HEAD_PAD_A] = (krr * rs + shift_one).astype(k_ref.dtype)


def _kva_proj(proj, ckv_blk, krope_blk, g_kv_lat, w_uk, w_uv, gk_nope, gk_rope, cos_t, sin_t, *, tm):
    t = proj.shape[0]
    return pl.pallas_call(
        _kva_proj_kernel,
        out_shape=(jax.ShapeDtypeStruct((t, H_A * HEAD_PAD_A), MXU_DTYPE),
                   jax.ShapeDtypeStruct((t, WIDTH_A), MXU_DTYPE)),
        grid=(t // tm,),
        in_specs=[
            pl.BlockSpec((tm, KV_LORA), lambda i: (i, ckv_blk)),
            pl.BlockSpec((tm, LANE), lambda i: (i, krope_blk)),
            pl.BlockSpec((1, KV_LORA), lambda i: (0, 0)),
            pl.BlockSpec((KV_LORA, H_A * QK_NOPE), lambda i: (0, 0)),
            pl.BlockSpec((KV_LORA, WIDTH_A), lambda i: (0, 0)),
            pl.BlockSpec((1, LANE), lambda i: (0, 0)),
            pl.BlockSpec((1, LANE), lambda i: (0, 0)),
            pl.BlockSpec((tm, LANE), lambda i: (i, 0)),
            pl.BlockSpec((tm, LANE), lambda i: (i, 0)),
        ],
        out_specs=(pl.BlockSpec((tm, H_A * HEAD_PAD_A), lambda i: (i, 0)),
                   pl.BlockSpec((tm, WIDTH_A), lambda i: (i, 0))),
        compiler_params=_params(("arbitrary",),
                                [_nbytes((tm, KV_LORA + LANE), MXU_DTYPE), 2 * _nbytes((KV_LORA, WIDTH_A), MXU_DTYPE),
                                 _nbytes((1, KV_LORA + 2 * LANE), F32), 2 * _nbytes((tm, LANE), F32),
                                 _nbytes((tm, H_A * HEAD_PAD_A + WIDTH_A), MXU_DTYPE)],
                                temps=2 * _nbytes((tm, WIDTH_A), F32)),
        name="kva_proj",
    )(proj, proj, g_kv_lat.reshape(1, KV_LORA), w_uk, w_uv, gk_nope, gk_rope, cos_t, sin_t)


def _silu(g):
    return g * (1.0 / (1.0 + jnp.exp(-g)))


def _lane_tile_reduce(x, op):
    acc = x[:, :LANE]
    for t in range(1, x.shape[1] // LANE):
        acc = op(acc, x[:, t * LANE:(t + 1) * LANE])
    return acc


def _attn_a_kernel(q_ref, k_ref, v_ref, gate_ref, o_ref, *, tq, nq, heads):
    qi = pl.program_id(2)
    causal = lax.broadcasted_iota(I32, (tq, tq), 0) >= lax.broadcasted_iota(I32, (tq, tq), 1)

    def finish(g, l_t, acc):
        vc = slice(g * V_DIM_A, (g + 1) * V_DIM_A)
        l = jnp.sum(l_t, axis=-1, keepdims=True)
        o = acc * (1.0 / l)
        o_ref[:, vc] = (o * _silu(gate_ref[:, vc].astype(F32))).astype(o_ref.dtype)

    def branch(qv):
        n_off = qv * tq
        kcs = [slice(g * HEAD_PAD_A, (g + 1) * HEAD_PAD_A) for g in range(heads)]
        vcs = [slice(g * V_DIM_A, (g + 1) * V_DIM_A) for g in range(heads)]

        l_min = jnp.full((tq, 1), POS_INF, F32)
        for g in range(heads):
            q = q_ref[:, kcs[g]]
            p_diag = jnp.exp2(jnp.where(causal, _nt_dot(q, k_ref[n_off:n_off + tq, kcs[g]]), NEG_INF))
            l_t = _lane_tile_reduce(p_diag, jnp.add)
            acc = jnp.dot(p_diag.astype(MXU_DTYPE), v_ref[n_off:n_off + tq, vcs[g]], preferred_element_type=F32)
            if qv > 0:
                p_off = jnp.exp2(_nt_dot(q, k_ref[0:n_off, kcs[g]]))
                l_t = l_t + _lane_tile_reduce(p_off, jnp.add)
                acc = acc + jnp.dot(p_off.astype(MXU_DTYPE), v_ref[0:n_off, vcs[g]], preferred_element_type=F32)
            l_min = jnp.minimum(l_min, jnp.sum(l_t, axis=-1, keepdims=True))
            finish(g, l_t, acc)

        shift_ok = jnp.min(l_min) >= MIN_SHIFTED_SUM

        @pl.when(jnp.logical_not(shift_ok))
        def _():
            for g in range(heads):
                q = q_ref[:, kcs[g]]
                q_hi = q[:, LANE:]
                lane = lax.broadcasted_iota(I32, q_hi.shape, 1)
                q = jnp.concatenate(
                    [q[:, :LANE], jnp.where(lane == SHIFT_LANE - LANE, 0.0, q_hi.astype(F32)).astype(q.dtype)], axis=1)
                s_diag = jnp.where(causal, _nt_dot(q, k_ref[n_off:n_off + tq, kcs[g]]), NEG_INF)
                m_t = _lane_tile_reduce(s_diag, jnp.maximum)
                if qv > 0:
                    s_off = _nt_dot(q, k_ref[0:n_off, kcs[g]])
                    m_t = jnp.maximum(m_t, _lane_tile_reduce(s_off, jnp.maximum))
                m = jnp.max(m_t, axis=-1, keepdims=True)
                p_diag = jnp.exp2(s_diag - m)
                l_t = _lane_tile_reduce(p_diag, jnp.add)
                acc = jnp.dot(p_diag.astype(MXU_DTYPE), v_ref[n_off:n_off + tq, vcs[g]],
                              preferred_element_type=F32)
                if qv > 0:
                    p_off = jnp.exp2(s_off - m)
                    l_t = l_t + _lane_tile_reduce(p_off, jnp.add)
                    acc = acc + jnp.dot(p_off.astype(MXU_DTYPE), v_ref[0:n_off, vcs[g]], preferred_element_type=F32)
                finish(g, l_t, acc)

    for qv in range(nq):
        pl.when(qi == qv)(functools.partial(branch, qv))


def _attn_a(q_a, k_a, v_a, proj, gate_blk0, *, b, l, tq, heads):
    t = q_a.shape[0]
    nq = l // tq
    kw, vw = heads * HEAD_PAD_A, heads * V_DIM_A
    return pl.pallas_call(
        functools.partial(_attn_a_kernel, tq=tq, nq=nq, heads=heads),
        out_shape=jax.ShapeDtypeStruct((t, WIDTH_A), MXU_DTYPE),
        grid=(b, H_A // heads, nq),
        in_specs=[
            pl.BlockSpec((tq, kw), lambda bi, h, qi: (bi * nq + qi, h)),
            pl.BlockSpec((l, kw), lambda bi, h, qi: (bi, h)),
            pl.BlockSpec((l, vw), lambda bi, h, qi: (bi, h)),
            pl.BlockSpec((tq, vw), lambda bi, h, qi: (bi * nq + qi, gate_blk0 // heads + h)),
        ],
        out_specs=pl.BlockSpec((tq, vw), lambda bi, h, qi: (bi * nq + qi, h)),
        compiler_params=_params(("arbitrary", "arbitrary", "arbitrary"),
                                [_nbytes((tq + l, kw), MXU_DTYPE), _nbytes((l + 2 * tq, vw), MXU_DTYPE)],
                                temps=heads * (_nbytes((tq, l), F32) + _nbytes((tq, l), MXU_DTYPE))),
        name="attn_a",
    )(q_a, k_a, v_a, proj)


def _t5_bucket(dist):
    max_exact = N_BUCKETS // 2
    n = jnp.maximum(dist, 0)
    nf = jnp.maximum(n, 1).astype(F32)
    large = max_exact + (jnp.log(nf / max_exact) / math.log(MAX_DISTANCE / max_exact)
                         * (N_BUCKETS - max_exact)).astype(I32)
    large = jnp.minimum(large, N_BUCKETS - 1)
    return jnp.where(n < max_exact, n, large)


def _attn_b_kernel(t5_ref, qidx_ref, kidx_ref, wt_ref, qb_ref, kb_ref, vt_ref, gate_ref, gq_ref, gk_ref,
                   o_ref, sc_ref, qn_ref, acc_ref, m_ref, l_ref, bias_ref, thr_ref, kn_ref, bound_ref,
                   *, tq, nq, topk, max_iters):
    bi = pl.program_id(0)
    qi = pl.program_id(1)
    ck = tq
    shape = (ck, tq)

    @pl.when((bi == 0) & (qi == 0))
    def _():
        s_loc = lax.broadcasted_iota(I32, shape, 0)
        t_loc = lax.broadcasted_iota(I32, shape, 1)
        for near in range(2):
            bucket = _t5_bucket(t_loc - s_loc + (1 - near) * ck)

            def per_head(h, carry, bucket=bucket, near=near):
                far = t5_ref[N_BUCKETS - 1, h]
                tab = jnp.zeros(shape, F32)
                largest = jnp.float32(0.0)
                for bk in range(N_BUCKETS - 1):
                    rel = (t5_ref[bk, h] - far) * LOG2E
                    tab = jnp.where(bucket == bk, rel, tab)
                    largest = jnp.maximum(largest, rel)
                bias_ref[h, near] = tab
                gains = jnp.max(jnp.abs(gq_ref[...])) * jnp.max(jnp.abs(gk_ref[...]))
                bound_ref[h] = HEAD_DIM_B * gains * BOUND_MARGIN + largest
                return carry

            lax.fori_loop(0, H_B, per_head, 0)

    w_all = wt_ref[...] * (IDX_HEADS ** -0.5)

    def score_chunk(c, diag):
        rows = pl.ds(pl.multiple_of(c * ck, ck), ck)
        kx = kidx_ref[rows, 0:IDX_DIM]
        zk = jnp.zeros_like(kx)
        kab = jnp.concatenate([jnp.concatenate([kx, zk], axis=1), jnp.concatenate([zk, kx], axis=1)], axis=0)
        score = jnp.zeros(shape, F32)
        for j in range(IDX_HEADS // 2):
            qp = qidx_ref[:, j * LANE:(j + 1) * LANE]
            logits = jnp.maximum(_nt_dot(kab, qp), 0.0)
            score = score + logits[:ck] * w_all[2 * j:2 * j + 1, :]
            score = score + logits[ck:] * w_all[2 * j + 1:2 * j + 2, :]
        if diag:
            adm = lax.broadcasted_iota(I32, shape, 0) <= lax.broadcasted_iota(I32, shape, 1)
            lo_src = jnp.where(adm, score, POS_INF)
            score = jnp.where(adm, score, NEG_INF)
        else:
            lo_src = score
        sc_ref[c] = score
        return jnp.max(score, axis=0, keepdims=True), jnp.min(lo_src, axis=0, keepdims=True)

    def score_body(c, carry):
        mx, mn = carry
        cmx, cmn = score_chunk(c, False)
        return jnp.maximum(mx, cmx), jnp.minimum(mn, cmn)

    mx, mn = lax.fori_loop(0, qi, score_body,
                           (jnp.full((1, tq), NEG_INF, F32), jnp.full((1, tq), POS_INF, F32)))
    dmx, dmn = score_chunk(qi, True)
    mx = jnp.maximum(mx, dmx)
    mn = jnp.minimum(mn, dmn)

    rep = (8, tq)
    n_adm = qi * tq + lax.broadcasted_iota(I32, rep, 1) + 1
    kp = jnp.minimum(n_adm, topk)
    mx8 = jnp.broadcast_to(mx, rep)
    mn8 = jnp.broadcast_to(mn, rep)

    def bisect(nchunks):
        groups = range(tq // LANE)
        split = lambda a: tuple(a[:, g * LANE:(g + 1) * LANE] for g in groups)
        join = lambda parts: jnp.concatenate(parts, axis=1)
        kp_g, mx_g = split(kp), split(mx8)

        def count_ge(x, g):
            acc = jnp.zeros((8, LANE), I32)
            for c in range(nchunks):
                ge = sc_ref[c, :, g * LANE:(g + 1) * LANE].reshape(ck // 8, 8, LANE) >= x[None]
                acc = acc + jnp.sum(ge.astype(I32), axis=0)
            for shift in (4, 2, 1):
                acc = acc + pltpu.roll(acc, shift, 0)
            return acc

        def bis_cond(st):
            it, lo, hi, mid, cnt_lo = st
            active = [(cnt_lo[g] != kp_g[g]) & (mid[g] > lo[g]) & (mid[g] < hi[g]) for g in groups]
            return jnp.logical_and(it < max_iters, jnp.max(join(active).astype(I32)) > 0)

        def bis_body(st):
            it, lo, hi, mid, cnt_lo = st
            lo, hi, mid, cnt_lo = list(lo), list(hi), list(mid), list(cnt_lo)
            for _ in range(BISECT_STEPS_PER_CHECK):
                for g in groups:
                    cnt = count_ge(mid[g], g)
                    ge = cnt >= kp_g[g]
                    lo[g] = jnp.where(ge, mid[g], lo[g])
                    cnt_lo[g] = jnp.where(ge, cnt, cnt_lo[g])
                    hi[g] = jnp.where(ge, hi[g], mid[g])
                    mid[g] = jnp.where(hi[g] == POS_INF, mx_g[g], lo[g] + 0.5 * (hi[g] - lo[g]))
            return it + 1, tuple(lo), tuple(hi), tuple(mid), tuple(cnt_lo)

        _, lo, hi, _, cnt_lo = lax.while_loop(
            bis_cond, bis_body,
            (jnp.int32(0), split(mn8), split(jnp.full(rep, POS_INF, F32)), split(mx8), split(n_adm)))
        lo, hi, cnt_lo = join(lo), join(hi), join(cnt_lo)
        thr_ref[...] = lo

        tied = cnt_lo > kp

        @pl.when(jnp.max(tied.astype(I32)) > 0)
        def _():
            n_keys = nchunks * ck
            sub = lax.broadcasted_iota(I32, (ck // 8, 8, tq), 0) * 8 + lax.broadcasted_iota(I32, (ck // 8, 8, tq), 1)

            def count_kept(j_last):
                acc = jnp.zeros(rep, I32)
                for c in range(nchunks):
                    s3 = sc_ref[c].reshape(ck // 8, 8, tq)
                    keep = (s3 >= hi[None]) | ((s3 >= lo[None]) & (sub + c * ck <= j_last[None]))
                    acc = acc + jnp.sum(keep.astype(I32), axis=0)
                for shift in (4, 2, 1):
                    acc = acc + pltpu.roll(acc, shift, 0)
                return acc

            def idx_step(_, carry):
                j_lo, j_hi = carry
                j_mid = j_lo + ((j_hi - j_lo) >> 1)
                ok = count_kept(j_mid) >= kp
                return jnp.where(ok, j_lo, j_mid), jnp.where(ok, j_mid, j_hi)

            _, j_hi = lax.fori_loop(0, max(1, (n_keys - 1).bit_length()), idx_step,
                                    (jnp.full(rep, -1, I32), jnp.full(rep, n_keys - 1, I32)))
            j_last = jnp.where(tied, j_hi, n_keys - 1)
            for c in range(nchunks):
                s3 = sc_ref[c].reshape(ck // 8, 8, tq)
                drop = (s3 >= lo[None]) & (s3 < hi[None]) & (sub + c * ck > j_last[None])
                sc_ref[c] = jnp.where(drop, NEG_INF, s3).reshape(ck, tq)

    for qv in range(nq):
        pl.when(qi == qv)(functools.partial(bisect, qv + 1))
    thr = thr_ref[0:1, :]

    @pl.when(qi == 0)
    def _():
        kf = kb_ref[...].astype(F32)
        ms = jnp.mean(kf * kf, axis=-1, keepdims=True)
        kn_ref[...] = (kf * lax.rsqrt(ms + EPS) * gk_ref[...]).astype(kn_ref.dtype)

    for h in range(H_B):
        qh = qb_ref[:, h * LANE:(h + 1) * LANE].astype(F32)
        ms = jnp.mean(qh * qh, axis=-1, keepdims=True)
        qn_ref[h * tq:(h + 1) * tq, :] = (qh * lax.rsqrt(ms + EPS) * gq_ref[...]).astype(qn_ref.dtype)

    def attend_chunk(c, near, exact):
        rows = pl.ds(pl.multiple_of(c * ck, ck), ck)
        vt = vt_ref[c]
        sel = sc_ref[c] >= thr
        s_all = _nt_dot(kn_ref[rows, :], qn_ref[...])
        for h in range(H_B):
            s = s_all[:, h * tq:(h + 1) * tq]
            if near is not None:
                s = s + bias_ref[h, near]
            if exact:
                s = jnp.where(sel, s, NEG_INF)
                m_old = m_ref[h]
                m_new = jnp.maximum(m_old, jnp.max(s, axis=0, keepdims=True))
                m_safe = jnp.where(m_new == NEG_INF, 0.0, m_new)
                p = jnp.exp2(s - m_safe)
                alpha = jnp.exp2(m_old - m_safe)
                l_ref[h] = alpha * l_ref[h] + jnp.sum(p, axis=0, keepdims=True)
                acc_ref[h] = alpha * acc_ref[h] + jnp.dot(vt, p.astype(MXU_DTYPE), preferred_element_type=F32)
                m_ref[h] = m_new
            else:
                p = jnp.exp2(jnp.where(sel, s - bound_ref[h], NEG_INF))
                l_ref[h] = l_ref[h] + jnp.sum(p, axis=0, keepdims=True)
                acc_ref[h] = acc_ref[h] + jnp.dot(vt, p.astype(MXU_DTYPE), preferred_element_type=F32)

    def attend(exact):
        if exact:
            m_ref[...] = jnp.full(m_ref.shape, NEG_INF, F32)
        l_ref[...] = jnp.zeros(l_ref.shape, F32)
        acc_ref[...] = jnp.zeros(acc_ref.shape, F32)

        def far_body(c, carry):
            attend_chunk(c, None, exact)
            return carry

        lax.fori_loop(0, jnp.maximum(qi - 1, 0), far_body, 0)

        @pl.when(qi >= 1)
        def _():
            attend_chunk(qi - 1, 0, exact)

        attend_chunk(qi, 1, exact)

    attend(False)
    l_min = l_ref[0]
    for h in range(1, H_B):
        l_min = jnp.minimum(l_min, l_ref[h])
    shift_ok = jnp.min(l_min) >= MIN_SHIFTED_SUM
    pl.when(jnp.logical_not(shift_ok))(functools.partial(attend, True))

    for h in range(H_B):
        o_t = acc_ref[h] * (1.0 / l_ref[h])
        g = gate_ref[:, h * LANE:(h + 1) * LANE].astype(F32)
        o_ref[:, h * LANE:(h + 1) * LANE] = (o_t.T * _silu(g)).astype(o_ref.dtype)


def _attn_b(proj, wt, vt, t5_bias, gq_b, gk_b, blk, *, b, l, tq):
    t = proj.shape[0]
    nq = l // tq
    topk = min(TOPK_MAX, l // 4)
    row = lambda bi, qi: bi * nq + qi
    return pl.pallas_call(
        functools.partial(_attn_b_kernel, tq=tq, nq=nq, topk=topk, max_iters=BISECT_MAX_CHECKS),
        out_shape=jax.ShapeDtypeStruct((t, WIDTH_B), MXU_DTYPE),
        grid=(b, nq),
        in_specs=[
            pl.BlockSpec(memory_space=pltpu.SMEM),
            pl.BlockSpec((tq, IDX_HEADS * IDX_DIM), lambda bi, qi: (row(bi, qi), blk["q_idx"])),
            pl.BlockSpec((l, LANE), lambda bi, qi: (bi, blk["k_idx"])),
            pl.BlockSpec((IDX_HEADS, tq), lambda bi, qi: (0, row(bi, qi))),
            pl.BlockSpec((tq, WIDTH_B), lambda bi, qi: (row(bi, qi), blk["q_b"])),
            pl.BlockSpec((l, LANE), lambda bi, qi: (bi, blk["k_b"])),
            pl.BlockSpec((None, nq, LANE, tq), lambda bi, qi: (bi, 0, 0, 0)),
            pl.BlockSpec((tq, WIDTH_B), lambda bi, qi: (row(bi, qi), blk["gate_b"])),
            pl.BlockSpec((1, LANE), lambda bi, qi: (0, 0)),
            pl.BlockSpec((1, LANE), lambda bi, qi: (0, 0)),
        ],
        out_specs=pl.BlockSpec((tq, WIDTH_B), lambda bi, qi: (row(bi, qi), 0)),
        scratch_shapes=[
            pltpu.VMEM((nq, tq, tq), F32),
            pltpu.VMEM((H_B * tq, LANE), MXU_DTYPE),
            pltpu.VMEM((H_B, LANE, tq), F32),
            pltpu.VMEM((H_B, 1, tq), F32),
            pltpu.VMEM((H_B, 1, tq), F32),
            pltpu.VMEM((H_B, 2, tq, tq), F32),
            pltpu.VMEM((8, tq), F32),
            pltpu.VMEM((l, LANE), MXU_DTYPE),
            pltpu.SMEM((H_B,), F32),
        ],
        compiler_params=_params(
            ("arbitrary", "arbitrary"),
            [_nbytes((tq, IDX_HEADS * IDX_DIM + 2 * WIDTH_B), MXU_DTYPE), _nbytes((l, 2 * LANE), MXU_DTYPE),
             _nbytes((IDX_HEADS, tq), F32), _nbytes((nq, LANE, tq), MXU_DTYPE), _nbytes((tq, WIDTH_B), MXU_DTYPE)],
            scratch=(_nbytes((nq, tq, tq), F32) + _nbytes((H_B * tq + l, LANE), MXU_DTYPE)
                     + _nbytes((H_B, LANE + 2 * 8, tq), F32) + _nbytes((H_B, 2, tq, tq), F32) + _nbytes((8, tq), F32)),
            temps=2 * _nbytes((tq, H_B * tq), F32)),
        name="attn_b",
    )(t5_bias, proj, proj, wt, proj, proj, vt, proj, gq_b, gk_b)


def _sigmoid(z):
    return 1.0 / (1.0 + jnp.exp(-z))


def _merge_kernel(oa_ref, ob_ref, pa_ref, pb_ref, ma_ref, mb_ref, o_ref):
    a = jnp.dot(oa_ref[...], pa_ref[...], preferred_element_type=F32)
    bb = jnp.dot(ob_ref[...], pb_ref[...], preferred_element_type=F32)
    o_ref[...] = (_sigmoid(ma_ref[...].astype(F32)) * a + _sigmoid(mb_ref[...].astype(F32)) * bb).astype(o_ref.dtype)


def _merge(o_a, o_b, p_a, p_b, proj, ma_off, mb_off, *, tm, tn):
    t = o_a.shape[0]
    d = p_a.shape[1]
    ma0, mb0 = ma_off // tn, mb_off // tn
    return pl.pallas_call(
        _merge_kernel,
        out_shape=jax.ShapeDtypeStruct((t, d), MXU_DTYPE),
        grid=(t // tm, d // tn),
        in_specs=[
            pl.BlockSpec((tm, WIDTH_A), lambda i, j: (i, 0)),
            pl.BlockSpec((tm, WIDTH_B), lambda i, j: (i, 0)),
            pl.BlockSpec((WIDTH_A, tn), lambda i, j: (0, j)),
            pl.BlockSpec((WIDTH_B, tn), lambda i, j: (0, j)),
            pl.BlockSpec((tm, tn), lambda i, j: (i, ma0 + j)),
            pl.BlockSpec((tm, tn), lambda i, j: (i, mb0 + j)),
        ],
        out_specs=pl.BlockSpec((tm, tn), lambda i, j: (i, j)),
        compiler_params=_params(("arbitrary", "arbitrary"),
                                [_nbytes((tm, WIDTH_A + WIDTH_B), MXU_DTYPE), _nbytes((WIDTH_A + WIDTH_B, tn), MXU_DTYPE),
                                 3 * _nbytes((tm, tn), MXU_DTYPE)],
                                temps=2 * _nbytes((tm, tn), F32)),
        name="merge",
    )(o_a, o_b, p_a, p_b, proj, proj)


def _out_proj_kernel(m_ref, w_ref, x_ref, o_ref):
    o_ref[...] = x_ref[...] + jnp.dot(m_ref[...], w_ref[...], preferred_element_type=F32)


def _out_proj(merged, w_o, x2, *, tm, tn):
    t, d = x2.shape
    return pl.pallas_call(
        _out_proj_kernel,
        out_shape=jax.ShapeDtypeStruct((t, d), x2.dtype),
        grid=(t // tm, d // tn),
        in_specs=[
            pl.BlockSpec((tm, d), lambda i, j: (i, 0)),
            pl.BlockSpec((d, tn), lambda i, j: (0, j)),
            pl.BlockSpec((tm, tn), lambda i, j: (i, j)),
        ],
        out_specs=pl.BlockSpec((tm, tn), lambda i, j: (i, j)),
        compiler_params=_params(("arbitrary", "arbitrary"),
                                [_nbytes((tm, d), MXU_DTYPE), _nbytes((d, tn), MXU_DTYPE), 2 * _nbytes((tm, tn), x2.dtype)],
                                temps=_nbytes((tm, tn), F32)),
        name="out_proj",
    )(merged, w_o, x2)


def _rope_pad(a, axis):
    a1, a2 = jnp.split(a, 2, axis=axis)
    z = jnp.zeros_like(a1)
    return jnp.concatenate([a1, z, a2, z], axis=axis)


def _layout(d):
    names = [("merge_a", d), ("merge_b", d), ("gate_a", WIDTH_A), ("gate_b", WIDTH_B),
             ("q_b", WIDTH_B), ("q_idx", IDX_HEADS * IDX_DIM), ("cq", Q_LORA), ("ckv", KV_LORA),
             ("k_idx", LANE), ("k_rope", LANE), ("k_b", LANE), ("v_b", LANE)]
    off, out = 0, {}
    for name, width in names:
        assert off % width == 0, (name, off, width)
        out[name] = off
        off += width
    out["w_idx"] = out["k_idx"] + IDX_DIM
    return out, off


def kernel(x, positions, g_pre, w_in, g_q_lat, g_kv_lat, w_uq, w_ukv, g_qn_a, g_kn_a,
           g_qn_b, g_kn_b, t5_bias, p_a, p_b, w_o):
    b, l, d = x.shape
    t = b * l
    tq = 256
    tn_in = 512
    off, n_used = _layout(d)
    n_pad = -(-n_used // tn_in) * tn_in

    names = ["cq", "ckv", "k_rope", "q_b", "k_b", "v_b", "q_idx", "k_idx", "w_idx", "gate_a", "gate_b",
             "merge_a", "merge_b"]
    sizes = [Q_LORA, KV_LORA, QK_ROPE, WIDTH_B, HEAD_DIM_B, HEAD_DIM_B, IDX_HEADS * IDX_DIM, IDX_DIM,
             IDX_HEADS, WIDTH_A, WIDTH_B, d, d]
    src, acc = {}, 0
    for name, s in zip(names, sizes):
        src[name] = (acc, s)
        acc += s
    main_groups = ["merge_a", "merge_b", "gate_a", "gate_b", "q_b", "q_idx", "cq", "ckv"]
    src_starts, dst = [], 0
    for name in main_groups:
        assert off[name] == dst and src[name][1] % tn_in == 0
        src_starts += [src[name][0] + c for c in range(0, src[name][1], tn_in)]
        dst += src[name][1]
    wt = w_in.T
    rows = lambda name: wt[src[name][0]:src[name][0] + src[name][1]]
    z = lambda n: jnp.zeros((n, d), w_in.dtype)
    assert off["k_idx"] == dst
    wt_tail = jnp.concatenate(
        [rows("k_idx"), rows("w_idx"), z(LANE - IDX_DIM - IDX_HEADS), _rope_pad(rows("k_rope"), 0),
         rows("k_b"), rows("v_b"), z(n_pad - n_used)], axis=0)
    w_pad = _w_relayout(wt, src_starts, wt_tail, tn=tn_in, tc=d)

    w_uq3 = w_uq.reshape(Q_LORA, H_A, QK_DIM_A)
    w_uq_pad = jnp.concatenate([w_uq3[:, :, :QK_NOPE], _rope_pad(w_uq3[:, :, QK_NOPE:], 2)], axis=2)
    w_uq_pad = w_uq_pad.reshape(Q_LORA, H_A * HEAD_PAD_A).astype(MXU_DTYPE)
    gq_head = jnp.concatenate([g_qn_a[:QK_NOPE], _rope_pad(g_qn_a[QK_NOPE:], 0)]) * (QK_DIM_A ** -0.5 * LOG2E)
    gq_pad = jnp.tile(gq_head, H_A).reshape(1, H_A * HEAD_PAD_A).astype(F32)
    bound_a = QK_DIM_A * jnp.max(jnp.abs(gq_head)) * jnp.max(jnp.abs(g_kn_a)) * BOUND_MARGIN
    q_shift = jnp.zeros((1, LANE), F32).at[0, SHIFT_LANE - LANE].set(-bound_a)
    w_ukv3 = w_ukv.reshape(KV_LORA, H_A, QK_NOPE + V_DIM_A)
    w_uk = w_ukv3[:, :, :QK_NOPE].reshape(KV_LORA, H_A * QK_NOPE).astype(MXU_DTYPE)
    w_uv = w_ukv3[:, :, QK_NOPE:].reshape(KV_LORA, WIDTH_A).astype(MXU_DTYPE)
    gk_nope = g_kn_a[:QK_NOPE].reshape(1, LANE).astype(F32)
    gk_rope = _rope_pad(g_kn_a[QK_NOPE:], 0).reshape(1, LANE).astype(F32)
    gq_b = (g_qn_b * (HEAD_DIM_B ** -0.5 * LOG2E)).reshape(1, LANE).astype(F32)
    gk_b = g_kn_b.reshape(1, LANE).astype(F32)

    inv = ROPE_THETA ** (-jnp.arange(HALF_ROPE, dtype=F32) / HALF_ROPE)
    ang_off = positions[:, :1, None].astype(F32) * inv
    ang_rel = jnp.arange(l, dtype=F32)[None, :, None] * inv
    cos = (jnp.cos(ang_off) * jnp.cos(ang_rel) - jnp.sin(ang_off) * jnp.sin(ang_rel)).reshape(t, HALF_ROPE)
    sin = (jnp.sin(ang_off) * jnp.cos(ang_rel) + jnp.cos(ang_off) * jnp.sin(ang_rel)).reshape(t, HALF_ROPE)
    zr = jnp.zeros_like(cos)
    cos_t = jnp.concatenate([cos, zr, cos, zr], axis=1)
    sin_t = jnp.concatenate([-sin, zr, sin, zr], axis=1)

    x2 = x.reshape(t, d)
    proj = _in_proj(x2, g_pre, w_pad, tm=1024, tn=2 * tn_in if n_pad % (2 * tn_in) == 0 else tn_in)

    q_a = _qa_proj(proj, off["cq"] // Q_LORA, g_q_lat, w_uq_pad, gq_pad, cos_t, sin_t, q_shift, tm=256)
    k_a, v_a = _kva_proj(proj, off["ckv"] // KV_LORA, off["k_rope"] // LANE, g_kv_lat, w_uk, w_uv,
                         gk_nope, gk_rope, cos_t, sin_t, tm=256)
    o_a = _attn_a(q_a, k_a, v_a, proj, off["gate_a"] // LANE, b=b, l=l, tq=tq, heads=4)

    nq = l // tq
    w_idx = proj[:, off["w_idx"]:off["w_idx"] + IDX_HEADS]
    w_idx_t = w_idx.astype(F32).T
    v_b = proj[:, off["v_b"]:off["v_b"] + HEAD_DIM_B]
    vt = v_b.reshape(b, nq, tq, HEAD_DIM_B).transpose(0, 1, 3, 2)
    blk = {"q_idx": off["q_idx"] // (IDX_HEADS * IDX_DIM), "k_idx": off["k_idx"] // LANE,
           "q_b": off["q_b"] // WIDTH_B, "k_b": off["k_b"] // LANE, "gate_b": off["gate_b"] // WIDTH_B}
    o_b = _attn_b(proj, w_idx_t, vt, t5_bias.astype(F32), gq_b, gk_b, blk, b=b, l=l, tq=tq)

    merged = _merge(o_a, o_b, p_a.astype(MXU_DTYPE), p_b.astype(MXU_DTYPE), proj,
                    off["merge_a"], off["merge_b"], tm=1024, tn=1024)
    out = _out_proj(merged, w_o.astype(MXU_DTYPE), x2, tm=1024, tn=1024)
    return out.reshape(b, l, d)
```

```python
import functools
import math

import jax
import jax.numpy as jnp
from jax import lax
from jax.experimental import pallas as pl
from jax.experimental.pallas import tpu as pltpu

F32 = jnp.float32
I32 = jnp.int32
MXU_DTYPE = jnp.bfloat16

H_A = 16
QK_NOPE = 128
QK_ROPE = 64
QK_DIM_A = QK_NOPE + QK_ROPE
V_DIM_A = 128
Q_LORA = 1024
KV_LORA = 512
ROPE_THETA = 10000.0
H_B = 16
HEAD_DIM_B = 128
IDX_HEADS = 32
IDX_DIM = 64
TOPK_MAX = 256
N_BUCKETS = 32
MAX_DISTANCE = 128
EPS = 1e-6
WIDTH_A = H_A * V_DIM_A
WIDTH_B = H_B * HEAD_DIM_B

LANE = 128
ROW_ALIGN = 32
HALF_ROPE = QK_ROPE // 2
HEAD_PAD_A = 2 * LANE
SHIFT_LANE = LANE + HALF_ROPE
BOUND_MARGIN = 1.0 + 2.0 ** -6
MIN_SHIFTED_SUM = 2.0 ** -64
VMEM_CAP = 56 * 1024 * 1024
VMEM_HEADROOM = 4 * 1024 * 1024

BISECT_STEPS_PER_CHECK = 5
BISECT_MAX_CHECKS = 32
LOG2E = math.log2(math.e)
NEG_INF = float("-inf")
POS_INF = float("inf")


def _nt_dot(a, b):
    return lax.dot_general(a, b, (((1,), (1,)), ((), ())), preferred_element_type=F32)


def _nbytes(shape, dtype):
    return math.prod(shape) * jnp.dtype(dtype).itemsize


def _params(sem, windows, scratch=0, temps=0):
    need = 2 * sum(windows) + scratch + temps + VMEM_HEADROOM
    return pltpu.CompilerParams(dimension_semantics=sem, vmem_limit_bytes=min(need, VMEM_CAP))


def _w_relayout_kernel(start_ref, src_ref, tail_ref, o_ref, *, n_main):
    j = pl.program_id(0)

    @pl.when(j < n_main)
    def _():
        o_ref[...] = src_ref[...].astype(o_ref.dtype)

    @pl.when(j >= n_main)
    def _():
        o_ref[...] = tail_ref[...].astype(o_ref.dtype)


def _w_relayout(wt, src_starts, wt_tail, *, tn, tc):
    d = wt.shape[1]
    n_main, n_tail = len(src_starts), wt_tail.shape[0] // tn
    assert all(s % ROW_ALIGN == 0 for s in src_starts)
    starts = jnp.array([s // ROW_ALIGN for s in src_starts] + [0] * n_tail, I32)
    return pl.pallas_call(
        functools.partial(_w_relayout_kernel, n_main=n_main),
        out_shape=jax.ShapeDtypeStruct(((n_main + n_tail) * tn, d), MXU_DTYPE),
        grid_spec=pltpu.PrefetchScalarGridSpec(
            num_scalar_prefetch=1,
            grid=(n_main + n_tail, d // tc),
            in_specs=[
                pl.BlockSpec((pl.Element(tn), pl.Element(tc)), lambda j, c, st: (st[j] * ROW_ALIGN, c * tc)),
                pl.BlockSpec((tn, tc), lambda j, c, st: (jnp.maximum(j - n_main, 0), c)),
            ],
            out_specs=pl.BlockSpec((tn, tc), lambda j, c, st: (j, c)),
        ),
        compiler_params=_params(("arbitrary", "arbitrary"),
                                [_nbytes((tn, tc), wt.dtype), _nbytes((tn, tc), wt_tail.dtype),
                                 _nbytes((tn, tc), MXU_DTYPE)]),
        name="w_relayout",
    )(starts, wt, wt_tail)


def _in_proj_kernel(x_hbm, g_ref, w_ref, o_ref, x_buf, hn_ref, x_sem, *, row_chunk):
    tm = x_buf.shape[0]
    i, j = pl.program_id(0), pl.program_id(1)

    def x_copy(row_block):
        rows = pl.ds(pl.multiple_of(row_block * tm, tm), tm)
        return pltpu.make_async_copy(x_hbm.at[rows, :], x_buf, x_sem)

    @pl.when((i == 0) & (j == 0))
    def _():
        x_copy(0).start()

    @pl.when(j == 0)
    def _():
        x_copy(i).wait()

        def body(r, carry):
            sl = pl.ds(pl.multiple_of(r * row_chunk, row_chunk), row_chunk)
            xx = x_buf[sl, :]
            ms = jnp.mean(xx * xx, axis=-1, keepdims=True)
            hn_ref[sl, :] = (xx * lax.rsqrt(ms + EPS) * g_ref[...]).astype(hn_ref.dtype)
            return carry

        lax.fori_loop(0, tm // row_chunk, body, 0)

    @pl.when((j == 1) & (i + 1 < pl.num_programs(0)))
    def _():
        x_copy(i + 1).start()

    o_ref[...] = _nt_dot(hn_ref[...], w_ref[...]).astype(o_ref.dtype)


def _in_proj(x2, g_pre, wt_pad, *, tm, tn):
    t, d = x2.shape
    n = wt_pad.shape[0]
    assert n // tn >= 2
    return pl.pallas_call(
        functools.partial(_in_proj_kernel, row_chunk=64),
        out_shape=jax.ShapeDtypeStruct((t, n), MXU_DTYPE),
        grid=(t // tm, n // tn),
        in_specs=[
            pl.BlockSpec(memory_space=pl.ANY),
            pl.BlockSpec((1, d), lambda i, j: (0, 0)),
            pl.BlockSpec((tn, d), lambda i, j: (j, 0)),
        ],
        out_specs=pl.BlockSpec((tm, tn), lambda i, j: (i, j)),
        scratch_shapes=[pltpu.VMEM((tm, d), x2.dtype), pltpu.VMEM((tm, d), MXU_DTYPE),
                        pltpu.SemaphoreType.DMA(())],
        compiler_params=_params(("arbitrary", "arbitrary"),
                                [_nbytes((1, d), F32), _nbytes((tn, d), MXU_DTYPE), _nbytes((tm, tn), MXU_DTYPE)],
                                scratch=_nbytes((tm, d), x2.dtype) + _nbytes((tm, d), MXU_DTYPE),
                                temps=2 * _nbytes((tm, tn), F32)),
        name="in_proj",
    )(x2, g_pre.reshape(1, d), wt_pad)


def _rope_lanes(r, cos_ref, sin_ref):
    return r * cos_ref[...] + pltpu.roll(r, 2 * HALF_ROPE, 1) * sin_ref[...]


def _qa_proj_kernel(cq_ref, gl_ref, w_ref, gq_ref, cos_ref, sin_ref, shift_ref, o_ref):
    c = cq_ref[...].astype(F32)
    ms = jnp.mean(c * c, axis=-1, keepdims=True)
    cn = (c * lax.rsqrt(ms + EPS) * gl_ref[...]).astype(MXU_DTYPE)
    q = jnp.dot(cn, w_ref[...], preferred_element_type=F32)
    for h in range(H_A):
        lo = h * HEAD_PAD_A
        qh = q[:, lo:lo + HEAD_PAD_A]
        ss = jnp.sum(qh * qh, axis=-1, keepdims=True) * (1.0 / QK_DIM_A)
        qn = qh * lax.rsqrt(ss + EPS) * gq_ref[:, lo:lo + HEAD_PAD_A]
        o_ref[:, lo:lo + LANE] = qn[:, :LANE].astype(o_ref.dtype)
        o_ref[:, lo + LANE:lo + HEAD_PAD_A] = (
            _rope_lanes(qn[:, LANE:], cos_ref, sin_ref) + shift_ref[...]).astype(o_ref.dtype)


def _qa_proj(proj, cq_blk, g_q_lat, w_uq_pad, gq_pad, cos_t, sin_t, q_shift, *, tm):
    t = proj.shape[0]
    nq = H_A * HEAD_PAD_A
    return pl.pallas_call(
        _qa_proj_kernel,
        out_shape=jax.ShapeDtypeStruct((t, nq), MXU_DTYPE),
        grid=(t // tm,),
        in_specs=[
            pl.BlockSpec((tm, Q_LORA), lambda i: (i, cq_blk)),
            pl.BlockSpec((1, Q_LORA), lambda i: (0, 0)),
            pl.BlockSpec((Q_LORA, nq), lambda i: (0, 0)),
            pl.BlockSpec((1, nq), lambda i: (0, 0)),
            pl.BlockSpec((tm, LANE), lambda i: (i, 0)),
            pl.BlockSpec((tm, LANE), lambda i: (i, 0)),
            pl.BlockSpec((1, LANE), lambda i: (0, 0)),
        ],
        out_specs=pl.BlockSpec((tm, nq), lambda i: (i, 0)),
        compiler_params=_params(("arbitrary",),
                                [_nbytes((tm, Q_LORA), MXU_DTYPE), _nbytes((Q_LORA, nq), MXU_DTYPE),
                                 _nbytes((1, Q_LORA + nq + LANE), F32), 2 * _nbytes((tm, LANE), F32),
                                 _nbytes((tm, nq), MXU_DTYPE)],
                                temps=_nbytes((tm, nq), F32)),
        name="qa_proj",
    )(proj, g_q_lat.reshape(1, Q_LORA), w_uq_pad, gq_pad, cos_t, sin_t, q_shift)


def _kva_proj_kernel(ckv_ref, kr_ref, gl_ref, wk_ref, wv_ref, gkn_ref, gkr_ref, cos_ref, sin_ref,
                     k_ref, v_ref):
    c = ckv_ref[...].astype(F32)
    ms = jnp.mean(c * c, axis=-1, keepdims=True)
    cn = (c * lax.rsqrt(ms + EPS) * gl_ref[...]).astype(MXU_DTYPE)
    kn = jnp.dot(cn, wk_ref[...], preferred_element_type=F32)
    v_ref[...] = jnp.dot(cn, wv_ref[...], preferred_element_type=F32).astype(v_ref.dtype)
    kr = kr_ref[...].astype(F32)
    ss_r = jnp.sum(kr * kr, axis=-1, keepdims=True)
    krr = _rope_lanes(kr * gkr_ref[...], cos_ref, sin_ref)
    shift_one = (lax.broadcasted_iota(I32, (1, LANE), 1) == SHIFT_LANE - LANE).astype(F32)
    for h in range(H_A):
        kh = kn[:, h * LANE:(h + 1) * LANE]
        ss = (jnp.sum(kh * kh, axis=-1, keepdims=True) + ss_r) * (1.0 / QK_DIM_A)
        rs = lax.rsqrt(ss + EPS)
        lo = h * HEAD_PAD_A
        k_ref[:, lo:lo + LANE] = (kh * rs * gkn_ref[...]).astype(k_ref.dtype)
        k_ref[:, lo + LANE:lo + HEAD_PAD_A] = (krr * rs + shift_one).astype(k_ref.dtype)


def _kva_proj(proj, ckv_blk, krope_blk, g_kv_lat, w_uk, w_uv, gk_nope, gk_rope, cos_t, sin_t, *, tm):
    t = proj.shape[0]
    return pl.pallas_call(
        _kva_proj_kernel,
        out_shape=(jax.ShapeDtypeStruct((t, H_A * HEAD_PAD_A), MXU_DTYPE),
                   jax.ShapeDtypeStruct((t, WIDTH_A), MXU_DTYPE)),
        grid=(t // tm,),
        in_specs=[
            pl.BlockSpec((tm, KV_LORA), lambda i: (i, ckv_blk)),
            pl.BlockSpec((tm, LANE), lambda i: (i, krope_blk)),
            pl.BlockSpec((1, KV_LORA), lambda i: (0, 0)),
            pl.BlockSpec((KV_LORA, H_A * QK_NOPE), lambda i: (0, 0)),
            pl.BlockSpec((KV_LORA, WIDTH_A), lambda i: (0, 0)),
            pl.BlockSpec((1, LANE), lambda i: (0, 0)),
            pl.BlockSpec((1, LANE), lambda i: (0, 0)),
            pl.BlockSpec((tm, LANE), lambda i: (i, 0)),
            pl.BlockSpec((tm, LANE), lambda i: (i, 0)),
        ],
        out_specs=(pl.BlockSpec((tm, H_A * HEAD_PAD_A), lambda i: (i, 0)),
                   pl.BlockSpec((tm, WIDTH_A), lambda i: (i, 0))),
        compiler_params=_params(("arbitrary",),
                                [_nbytes((tm, KV_LORA + LANE), MXU_DTYPE), 2 * _nbytes((KV_LORA, WIDTH_A), MXU_DTYPE),
                                 _nbytes((1, KV_LORA + 2 * LANE), F32), 2 * _nbytes((tm, LANE), F32),
                                 _nbytes((tm, H_A * HEAD_PAD_A + WIDTH_A), MXU_DTYPE)],
                                temps=2 * _nbytes((tm, WIDTH_A), F32)),
        name="kva_proj",
    )(proj, proj, g_kv_lat.reshape(1, KV_LORA), w_uk, w_uv, gk_nope, gk_rope, cos_t, sin_t)


def _silu(g):
    return g * (1.0 / (1.0 + jnp.exp(-g)))


def _lane_tile_reduce(x, op):
    acc = x[:, :LANE]
    for t in range(1, x.shape[1] // LANE):
        acc = op(acc, x[:, t * LANE:(t + 1) * LANE])
    return acc


def _attn_a_kernel(q_ref, k_ref, v_ref, gate_ref, o_ref, *, tq, nq, heads):
    causal = lax.broadcasted_iota(I32, (tq, tq), 0) >= lax.broadcasted_iota(I32, (tq, tq), 1)

    def query_block(qv):
        n_off = qv * tq
        rows = slice(n_off, n_off + tq)

        def finish(g, l_t, acc):
            vc = slice(g * V_DIM_A, (g + 1) * V_DIM_A)
            l = jnp.sum(l_t, axis=-1, keepdims=True)
            o = acc * (1.0 / l)
            o_ref[rows, vc] = (o * _silu(gate_ref[rows, vc].astype(F32))).astype(o_ref.dtype)

        kcs = [slice(g * HEAD_PAD_A, (g + 1) * HEAD_PAD_A) for g in range(heads)]
        vcs = [slice(g * V_DIM_A, (g + 1) * V_DIM_A) for g in range(heads)]

        l_min = jnp.full((tq, 1), POS_INF, F32)
        for g in range(heads):
            q = q_ref[rows, kcs[g]]
            p_diag = jnp.exp2(jnp.where(causal, _nt_dot(q, k_ref[n_off:n_off + tq, kcs[g]]), NEG_INF))
            l_t = _lane_tile_reduce(p_diag, jnp.add)
            acc = jnp.dot(p_diag.astype(MXU_DTYPE), v_ref[n_off:n_off + tq, vcs[g]], preferred_element_type=F32)
            if qv > 0:
                p_off = jnp.exp2(_nt_dot(q, k_ref[0:n_off, kcs[g]]))
                l_t = l_t + _lane_tile_reduce(p_off, jnp.add)
                acc = acc + jnp.dot(p_off.astype(MXU_DTYPE), v_ref[0:n_off, vcs[g]], preferred_element_type=F32)
            l_min = jnp.minimum(l_min, jnp.sum(l_t, axis=-1, keepdims=True))
            finish(g, l_t, acc)

        shift_ok = jnp.min(l_min) >= MIN_SHIFTED_SUM

        @pl.when(jnp.logical_not(shift_ok))
        def _():
            for g in range(heads):
                q = q_ref[rows, kcs[g]]
                q_hi = q[:, LANE:]
                lane = lax.broadcasted_iota(I32, q_hi.shape, 1)
                q = jnp.concatenate(
                    [q[:, :LANE], jnp.where(lane == SHIFT_LANE - LANE, 0.0, q_hi.astype(F32)).astype(q.dtype)], axis=1)
                s_diag = jnp.where(causal, _nt_dot(q, k_ref[n_off:n_off + tq, kcs[g]]), NEG_INF)
                m_t = _lane_tile_reduce(s_diag, jnp.maximum)
                if qv > 0:
                    s_off = _nt_dot(q, k_ref[0:n_off, kcs[g]])
                    m_t = jnp.maximum(m_t, _lane_tile_reduce(s_off, jnp.maximum))
                m = jnp.max(m_t, axis=-1, keepdims=True)
                p_diag = jnp.exp2(s_diag - m)
                l_t = _lane_tile_reduce(p_diag, jnp.add)
                acc = jnp.dot(p_diag.astype(MXU_DTYPE), v_ref[n_off:n_off + tq, vcs[g]],
                              preferred_element_type=F32)
                if qv > 0:
                    p_off = jnp.exp2(s_off - m)
                    l_t = l_t + _lane_tile_reduce(p_off, jnp.add)
                    acc = acc + jnp.dot(p_off.astype(MXU_DTYPE), v_ref[0:n_off, vcs[g]], preferred_element_type=F32)
                finish(g, l_t, acc)

    for qv in range(nq):
        query_block(qv)


def _attn_a(q_a, k_a, v_a, proj, gate_blk0, *, b, l, tq, heads):
    t = q_a.shape[0]
    nq = l // tq
    kw, vw = heads * HEAD_PAD_A, heads * V_DIM_A
    return pl.pallas_call(
        functools.partial(_attn_a_kernel, tq=tq, nq=nq, heads=heads),
        out_shape=jax.ShapeDtypeStruct((t, WIDTH_A), MXU_DTYPE),
        grid=(b, H_A // heads),
        in_specs=[
            pl.BlockSpec((l, kw), lambda bi, h: (bi, h)),
            pl.BlockSpec((l, kw), lambda bi, h: (bi, h)),
            pl.BlockSpec((l, vw), lambda bi, h: (bi, h)),
            pl.BlockSpec((l, vw), lambda bi, h: (bi, gate_blk0 // heads + h)),
        ],
        out_specs=pl.BlockSpec((l, vw), lambda bi, h: (bi, h)),
        compiler_params=_params(("arbitrary", "arbitrary"),
                                [2 * _nbytes((l, kw), MXU_DTYPE), 3 * _nbytes((l, vw), MXU_DTYPE)],
                                temps=heads * (_nbytes((tq, l), F32) + _nbytes((tq, l), MXU_DTYPE))),
        name="attn_a",
    )(q_a, k_a, v_a, proj)


def _t5_bucket(dist):
    max_exact = N_BUCKETS // 2
    n = jnp.maximum(dist, 0)
    nf = jnp.maximum(n, 1).astype(F32)
    large = max_exact + (jnp.log(nf / max_exact) / math.log(MAX_DISTANCE / max_exact)
                         * (N_BUCKETS - max_exact)).astype(I32)
    large = jnp.minimum(large, N_BUCKETS - 1)
    return jnp.where(n < max_exact, n, large)


def _attn_b_kernel(t5_ref, qidx_ref, kidx_ref, wt_ref, qb_ref, kb_ref, vt_ref, gate_ref, gq_ref, gk_ref,
                   o_ref, sc_ref, qn_ref, acc_ref, m_ref, l_ref, bias_ref, thr_ref, kn_ref, bound_ref,
                   *, tq, nq, topk, max_iters):
    bi = pl.program_id(0)
    qi = pl.program_id(1)
    ck = tq
    shape = (ck, tq)

    @pl.when((bi == 0) & (qi == 0))
    def _():
        s_loc = lax.broadcasted_iota(I32, shape, 0)
        t_loc = lax.broadcasted_iota(I32, shape, 1)
        for near in range(2):
            bucket = _t5_bucket(t_loc - s_loc + (1 - near) * ck)

            def per_head(h, carry, bucket=bucket, near=near):
                far = t5_ref[N_BUCKETS - 1, h]
                tab = jnp.zeros(shape, F32)
                largest = jnp.float32(0.0)
                for bk in range(N_BUCKETS - 1):
                    rel = (t5_ref[bk, h] - far) * LOG2E
                    tab = jnp.where(bucket == bk, rel, tab)
                    largest = jnp.maximum(largest, rel)
                bias_ref[h, near] = tab
                gains = jnp.max(jnp.abs(gq_ref[...])) * jnp.max(jnp.abs(gk_ref[...]))
                bound_ref[h] = HEAD_DIM_B * gains * BOUND_MARGIN + largest
                return carry

            lax.fori_loop(0, H_B, per_head, 0)

    w_all = wt_ref[...] * (IDX_HEADS ** -0.5)

    def score_chunk(c, diag):
        rows = pl.ds(pl.multiple_of(c * ck, ck), ck)
        kx = kidx_ref[rows, 0:IDX_DIM]
        zk = jnp.zeros_like(kx)
        kab = jnp.concatenate([jnp.concatenate([kx, zk], axis=1), jnp.concatenate([zk, kx], axis=1)], axis=0)
        score = jnp.zeros(shape, F32)
        for j in range(IDX_HEADS // 2):
            qp = qidx_ref[:, j * LANE:(j + 1) * LANE]
            logits = jnp.maximum(_nt_dot(kab, qp), 0.0)
            score = score + logits[:ck] * w_all[2 * j:2 * j + 1, :]
            score = score + logits[ck:] * w_all[2 * j + 1:2 * j + 2, :]
        if diag:
            adm = lax.broadcasted_iota(I32, shape, 0) <= lax.broadcasted_iota(I32, shape, 1)
            lo_src = jnp.where(adm, score, POS_INF)
            score = jnp.where(adm, score, NEG_INF)
        else:
            lo_src = score
        sc_ref[c] = score
        return jnp.max(score, axis=0, keepdims=True), jnp.min(lo_src, axis=0, keepdims=True)

    def score_body(c, carry):
        mx, mn = carry
        cmx, cmn = score_chunk(c, False)
        return jnp.maximum(mx, cmx), jnp.minimum(mn, cmn)

    mx, mn = lax.fori_loop(0, qi, score_body,
                           (jnp.full((1, tq), NEG_INF, F32), jnp.full((1, tq), POS_INF, F32)))
    dmx, dmn = score_chunk(qi, True)
    mx = jnp.maximum(mx, dmx)
    mn = jnp.minimum(mn, dmn)

    rep = (8, tq)
    n_adm = qi * tq + lax.broadcasted_iota(I32, rep, 1) + 1
    kp = jnp.minimum(n_adm, topk)
    mx8 = jnp.broadcast_to(mx, rep)
    mn8 = jnp.broadcast_to(mn, rep)

    def bisect(nchunks):
        groups = range(tq // LANE)
        split = lambda a: tuple(a[:, g * LANE:(g + 1) * LANE] for g in groups)
        join = lambda parts: jnp.concatenate(parts, axis=1)
        kp_g, mx_g = split(kp), split(mx8)

        def count_ge(x, g):
            acc = jnp.zeros((8, LANE), I32)
            for c in range(nchunks):
                ge = sc_ref[c, :, g * LANE:(g + 1) * LANE].reshape(ck // 8, 8, LANE) >= x[None]
                acc = acc + jnp.sum(ge.astype(I32), axis=0)
            for shift in (4, 2, 1):
                acc = acc + pltpu.roll(acc, shift, 0)
            return acc

        def bis_cond(st):
            it, lo, hi, mid, cnt_lo = st
            active = [(cnt_lo[g] != kp_g[g]) & (mid[g] > lo[g]) & (mid[g] < hi[g]) for g in groups]
            return jnp.logical_and(it < max_iters, jnp.max(join(active).astype(I32)) > 0)

        def bis_body(st):
            it, lo, hi, mid, cnt_lo = st
            lo, hi, mid, cnt_lo = list(lo), list(hi), list(mid), list(cnt_lo)
            for _ in range(BISECT_STEPS_PER_CHECK):
                for g in groups:
                    cnt = count_ge(mid[g], g)
                    ge = cnt >= kp_g[g]
                    lo[g] = jnp.where(ge, mid[g], lo[g])
                    cnt_lo[g] = jnp.where(ge, cnt, cnt_lo[g])
                    hi[g] = jnp.where(ge, hi[g], mid[g])
                    mid[g] = jnp.where(hi[g] == POS_INF, mx_g[g], lo[g] + 0.5 * (hi[g] - lo[g]))
            return it + 1, tuple(lo), tuple(hi), tuple(mid), tuple(cnt_lo)

        _, lo, hi, _, cnt_lo = lax.while_loop(
            bis_cond, bis_body,
            (jnp.int32(0), split(mn8), split(jnp.full(rep, POS_INF, F32)), split(mx8), split(n_adm)))
        lo, hi, cnt_lo = join(lo), join(hi), join(cnt_lo)
        thr_ref[...] = lo

        tied = cnt_lo > kp

        @pl.when(jnp.max(tied.astype(I32)) > 0)
        def _():
            n_keys = nchunks * ck
            sub = lax.broadcasted_iota(I32, (ck // 8, 8, tq), 0) * 8 + lax.broadcasted_iota(I32, (ck // 8, 8, tq), 1)

            def count_kept(j_last):
                acc = jnp.zeros(rep, I32)
                for c in range(nchunks):
                    s3 = sc_ref[c].reshape(ck // 8, 8, tq)
                    keep = (s3 >= hi[None]) | ((s3 >= lo[None]) & (sub + c * ck <= j_last[None]))
                    acc = acc + jnp.sum(keep.astype(I32), axis=0)
                for shift in (4, 2, 1):
                    acc = acc + pltpu.roll(acc, shift, 0)
                return acc

            def idx_step(_, carry):
                j_lo, j_hi = carry
                j_mid = j_lo + ((j_hi - j_lo) >> 1)
                ok = count_kept(j_mid) >= kp
                return jnp.where(ok, j_lo, j_mid), jnp.where(ok, j_mid, j_hi)

            _, j_hi = lax.fori_loop(0, max(1, (n_keys - 1).bit_length()), idx_step,
                                    (jnp.full(rep, -1, I32), jnp.full(rep, n_keys - 1, I32)))
            j_last = jnp.where(tied, j_hi, n_keys - 1)
            for c in range(nchunks):
                s3 = sc_ref[c].reshape(ck // 8, 8, tq)
                drop = (s3 >= lo[None]) & (s3 < hi[None]) & (sub + c * ck > j_last[None])
                sc_ref[c] = jnp.where(drop, NEG_INF, s3).reshape(ck, tq)

    for qv in range(nq):
        pl.when(qi == qv)(functools.partial(bisect, qv + 1))
    thr = thr_ref[0:1, :]

    @pl.when(qi == 0)
    def _():
        kf = kb_ref[...].astype(F32)
        ms = jnp.mean(kf * kf, axis=-1, keepdims=True)
        kn_ref[...] = (kf * lax.rsqrt(ms + EPS) * gk_ref[...]).astype(kn_ref.dtype)

    for h in range(H_B):
        qh = qb_ref[:, h * LANE:(h + 1) * LANE].astype(F32)
        ms = jnp.mean(qh * qh, axis=-1, keepdims=True)
        qn_ref[h * tq:(h + 1) * tq, :] = (qh * lax.rsqrt(ms + EPS) * gq_ref[...]).astype(qn_ref.dtype)

    def attend_chunk(c, near, exact):
        rows = pl.ds(pl.multiple_of(c * ck, ck), ck)
        vt = vt_ref[c]
        sel = sc_ref[c] >= thr
        s_all = _nt_dot(kn_ref[rows, :], qn_ref[...])
        for h in range(H_B):
            s = s_all[:, h * tq:(h + 1) * tq]
            if near is not None:
                s = s + bias_ref[h, near]
            if exact:
                s = jnp.where(sel, s, NEG_INF)
                m_old = m_ref[h]
                m_new = jnp.maximum(m_old, jnp.max(s, axis=0, keepdims=True))
                m_safe = jnp.where(m_new == NEG_INF, 0.0, m_new)
                p = jnp.exp2(s - m_safe)
                alpha = jnp.exp2(m_old - m_safe)
                l_ref[h] = alpha * l_ref[h] + jnp.sum(p, axis=0, keepdims=True)
                acc_ref[h] = alpha * acc_ref[h] + jnp.dot(vt, p.astype(MXU_DTYPE), preferred_element_type=F32)
                m_ref[h] = m_new
            else:
                p = jnp.exp2(jnp.where(sel, s - bound_ref[h], NEG_INF))
                l_ref[h] = l_ref[h] + jnp.sum(p, axis=0, keepdims=True)
                acc_ref[h] = acc_ref[h] + jnp.dot(vt, p.astype(MXU_DTYPE), preferred_element_type=F32)

    def attend(exact):
        if exact:
            m_ref[...] = jnp.full(m_ref.shape, NEG_INF, F32)
        l_ref[...] = jnp.zeros(l_ref.shape, F32)
        acc_ref[...] = jnp.zeros(acc_ref.shape, F32)

        def far_body(c, carry):
            attend_chunk(c, None, exact)
            return carry

        lax.fori_loop(0, jnp.maximum(qi - 1, 0), far_body, 0)

        @pl.when(qi >= 1)
        def _():
            attend_chunk(qi - 1, 0, exact)

        attend_chunk(qi, 1, exact)

    attend(False)
    l_min = l_ref[0]
    for h in range(1, H_B):
        l_min = jnp.minimum(l_min, l_ref[h])
    shift_ok = jnp.min(l_min) >= MIN_SHIFTED_SUM
    pl.when(jnp.logical_not(shift_ok))(functools.partial(attend, True))

    for h in range(H_B):
        o_t = acc_ref[h] * (1.0 / l_ref[h])
        g = gate_ref[:, h * LANE:(h + 1) * LANE].astype(F32)
        o_ref[:, h * LANE:(h + 1) * LANE] = (o_t.T * _silu(g)).astype(o_ref.dtype)


def _attn_b(proj, wt, vt, t5_bias, gq_b, gk_b, blk, *, b, l, tq):
    t = proj.shape[0]
    nq = l // tq
    topk = min(TOPK_MAX, l // 4)
    row = lambda bi, qi: bi * nq + qi
    return pl.pallas_call(
        functools.partial(_attn_b_kernel, tq=tq, nq=nq, topk=topk, max_iters=BISECT_MAX_CHECKS),
        out_shape=jax.ShapeDtypeStruct((t, WIDTH_B), MXU_DTYPE),
        grid=(b, nq),
        in_specs=[
            pl.BlockSpec(memory_space=pltpu.SMEM),
            pl.BlockSpec((tq, IDX_HEADS * IDX_DIM), lambda bi, qi: (row(bi, qi), blk["q_idx"])),
            pl.BlockSpec((l, LANE), lambda bi, qi: (bi, blk["k_idx"])),
            pl.BlockSpec((IDX_HEADS, tq), lambda bi, qi: (0, row(bi, qi))),
            pl.BlockSpec((tq, WIDTH_B), lambda bi, qi: (row(bi, qi), blk["q_b"])),
            pl.BlockSpec((l, LANE), lambda bi, qi: (bi, blk["k_b"])),
            pl.BlockSpec((None, nq, LANE, tq), lambda bi, qi: (bi, 0, 0, 0)),
            pl.BlockSpec((tq, WIDTH_B), lambda bi, qi: (row(bi, qi), blk["gate_b"])),
            pl.BlockSpec((1, LANE), lambda bi, qi: (0, 0)),
            pl.BlockSpec((1, LANE), lambda bi, qi: (0, 0)),
        ],
        out_specs=pl.BlockSpec((tq, WIDTH_B), lambda bi, qi: (row(bi, qi), 0)),
        scratch_shapes=[
            pltpu.VMEM((nq, tq, tq), F32),
            pltpu.VMEM((H_B * tq, LANE), MXU_DTYPE),
            pltpu.VMEM((H_B, LANE, tq), F32),
            pltpu.VMEM((H_B, 1, tq), F32),
            pltpu.VMEM((H_B, 1, tq), F32),
            pltpu.VMEM((H_B, 2, tq, tq), F32),
            pltpu.VMEM((8, tq), F32),
            pltpu.VMEM((l, LANE), MXU_DTYPE),
            pltpu.SMEM((H_B,), F32),
        ],
        compiler_params=_params(
            ("arbitrary", "arbitrary"),
            [_nbytes((tq, IDX_HEADS * IDX_DIM + 2 * WIDTH_B), MXU_DTYPE), _nbytes((l, 2 * LANE), MXU_DTYPE),
             _nbytes((IDX_HEADS, tq), F32), _nbytes((nq, LANE, tq), MXU_DTYPE), _nbytes((tq, WIDTH_B), MXU_DTYPE)],
            scratch=(_nbytes((nq, tq, tq), F32) + _nbytes((H_B * tq + l, LANE), MXU_DTYPE)
                     + _nbytes((H_B, LANE + 2 * 8, tq), F32) + _nbytes((H_B, 2, tq, tq), F32) + _nbytes((8, tq), F32)),
            temps=2 * _nbytes((tq, H_B * tq), F32)),
        name="attn_b",
    )(t5_bias, proj, proj, wt, proj, proj, vt, proj, gq_b, gk_b)


def _sigmoid(z):
    return 1.0 / (1.0 + jnp.exp(-z))


def _merge_kernel(oa_ref, ob_ref, pa_ref, pb_ref, ma_ref, mb_ref, o_ref):
    a = jnp.dot(oa_ref[...], pa_ref[...], preferred_element_type=F32)
    bb = jnp.dot(ob_ref[...], pb_ref[...], preferred_element_type=F32)
    o_ref[...] = (_sigmoid(ma_ref[...].astype(F32)) * a + _sigmoid(mb_ref[...].astype(F32)) * bb).astype(o_ref.dtype)


def _merge(o_a, o_b, p_a, p_b, proj, ma_off, mb_off, *, tm, tn):
    t = o_a.shape[0]
    d = p_a.shape[1]
    ma0, mb0 = ma_off // tn, mb_off // tn
    return pl.pallas_call(
        _merge_kernel,
        out_shape=jax.ShapeDtypeStruct((t, d), MXU_DTYPE),
        grid=(t // tm, d // tn),
        in_specs=[
            pl.BlockSpec((tm, WIDTH_A), lambda i, j: (i, 0)),
            pl.BlockSpec((tm, WIDTH_B), lambda i, j: (i, 0)),
            pl.BlockSpec((WIDTH_A, tn), lambda i, j: (0, j)),
            pl.BlockSpec((WIDTH_B, tn), lambda i, j: (0, j)),
            pl.BlockSpec((tm, tn), lambda i, j: (i, ma0 + j)),
            pl.BlockSpec((tm, tn), lambda i, j: (i, mb0 + j)),
        ],
        out_specs=pl.BlockSpec((tm, tn), lambda i, j: (i, j)),
        compiler_params=_params(("arbitrary", "arbitrary"),
                                [_nbytes((tm, WIDTH_A + WIDTH_B), MXU_DTYPE), _nbytes((WIDTH_A + WIDTH_B, tn), MXU_DTYPE),
                                 3 * _nbytes((tm, tn), MXU_DTYPE)],
                                temps=2 * _nbytes((tm, tn), F32)),
        name="merge",
    )(o_a, o_b, p_a, p_b, proj, proj)


def _out_proj_kernel(m_ref, w_ref, x_ref, o_ref):
    o_ref[...] = x_ref[...] + jnp.dot(m_ref[...], w_ref[...], preferred_element_type=F32)


def _out_proj(merged, w_o, x2, *, tm, tn):
    t, d = x2.shape
    return pl.pallas_call(
        _out_proj_kernel,
        out_shape=jax.ShapeDtypeStruct((t, d), x2.dtype),
        grid=(t // tm, d // tn),
        in_specs=[
            pl.BlockSpec((tm, d), lambda i, j: (i, 0)),
            pl.BlockSpec((d, tn), lambda i, j: (0, j)),
            pl.BlockSpec((tm, tn), lambda i, j: (i, j)),
        ],
        out_specs=pl.BlockSpec((tm, tn), lambda i, j: (i, j)),
        compiler_params=_params(("arbitrary", "arbitrary"),
                                [_nbytes((tm, d), MXU_DTYPE), _nbytes((d, tn), MXU_DTYPE), 2 * _nbytes((tm, tn), x2.dtype)],
                                temps=_nbytes((tm, tn), F32)),
        name="out_proj",
    )(merged, w_o, x2)


def _rope_pad(a, axis):
    a1, a2 = jnp.split(a, 2, axis=axis)
    z = jnp.zeros_like(a1)
    return jnp.concatenate([a1, z, a2, z], axis=axis)


def _layout(d):
    names = [("merge_a", d), ("merge_b", d), ("gate_a", WIDTH_A), ("gate_b", WIDTH_B),
             ("q_b", WIDTH_B), ("q_idx", IDX_HEADS * IDX_DIM), ("cq", Q_LORA), ("ckv", KV_LORA),
             ("k_idx", LANE), ("k_rope", LANE), ("k_b", LANE), ("v_b", LANE)]
    off, out = 0, {}
    for name, width in names:
        assert off % width == 0, (name, off, width)
        out[name] = off
        off += width
    out["w_idx"] = out["k_idx"] + IDX_DIM
    return out, off


def kernel(x, positions, g_pre, w_in, g_q_lat, g_kv_lat, w_uq, w_ukv, g_qn_a, g_kn_a,
           g_qn_b, g_kn_b, t5_bias, p_a, p_b, w_o):
    b, l, d = x.shape
    t = b * l
    tq = 256
    tn_in = 512
    off, n_used = _layout(d)
    n_pad = -(-n_used // tn_in) * tn_in

    names = ["cq", "ckv", "k_rope", "q_b", "k_b", "v_b", "q_idx", "k_idx", "w_idx", "gate_a", "gate_b",
             "merge_a", "merge_b"]
    sizes = [Q_LORA, KV_LORA, QK_ROPE, WIDTH_B, HEAD_DIM_B, HEAD_DIM_B, IDX_HEADS * IDX_DIM, IDX_DIM,
             IDX_HEADS, WIDTH_A, WIDTH_B, d, d]
    src, acc = {}, 0
    for name, s in zip(names, sizes):
        src[name] = (acc, s)
        acc += s
    main_groups = ["merge_a", "merge_b", "gate_a", "gate_b", "q_b", "q_idx", "cq", "ckv"]
    src_starts, dst = [], 0
    for name in main_groups:
        assert off[name] == dst and src[name][1] % tn_in == 0
        src_starts += [src[name][0] + c for c in range(0, src[name][1], tn_in)]
        dst += src[name][1]
    wt = w_in.T
    rows = lambda name: wt[src[name][0]:src[name][0] + src[name][1]]
    z = lambda n: jnp.zeros((n, d), w_in.dtype)
    assert off["k_idx"] == dst
    wt_tail = jnp.concatenate(
        [rows("k_idx"), rows("w_idx"), z(LANE - IDX_DIM - IDX_HEADS), _rope_pad(rows("k_rope"), 0),
         rows("k_b"), rows("v_b"), z(n_pad - n_used)], axis=0)
    w_pad = _w_relayout(wt, src_starts, wt_tail, tn=tn_in, tc=d)

    w_uq3 = w_uq.reshape(Q_LORA, H_A, QK_DIM_A)
    w_uq_pad = jnp.concatenate([w_uq3[:, :, :QK_NOPE], _rope_pad(w_uq3[:, :, QK_NOPE:], 2)], axis=2)
    w_uq_pad = w_uq_pad.reshape(Q_LORA, H_A * HEAD_PAD_A).astype(MXU_DTYPE)
    gq_head = jnp.concatenate([g_qn_a[:QK_NOPE], _rope_pad(g_qn_a[QK_NOPE:], 0)]) * (QK_DIM_A ** -0.5 * LOG2E)
    gq_pad = jnp.tile(gq_head, H_A).reshape(1, H_A * HEAD_PAD_A).astype(F32)
    bound_a = QK_DIM_A * jnp.max(jnp.abs(gq_head)) * jnp.max(jnp.abs(g_kn_a)) * BOUND_MARGIN
    q_shift = jnp.zeros((1, LANE), F32).at[0, SHIFT_LANE - LANE].set(-bound_a)
    w_ukv3 = w_ukv.reshape(KV_LORA, H_A, QK_NOPE + V_DIM_A)
    w_uk = w_ukv3[:, :, :QK_NOPE].reshape(KV_LORA, H_A * QK_NOPE).astype(MXU_DTYPE)
    w_uv = w_ukv3[:, :, QK_NOPE:].reshape(KV_LORA, WIDTH_A).astype(MXU_DTYPE)
    gk_nope = g_kn_a[:QK_NOPE].reshape(1, LANE).astype(F32)
    gk_rope = _rope_pad(g_kn_a[QK_NOPE:], 0).reshape(1, LANE).astype(F32)
    gq_b = (g_qn_b * (HEAD_DIM_B ** -0.5 * LOG2E)).reshape(1, LANE).astype(F32)
    gk_b = g_kn_b.reshape(1, LANE).astype(F32)

    inv = ROPE_THETA ** (-jnp.arange(HALF_ROPE, dtype=F32) / HALF_ROPE)
    ang_off = positions[:, :1, None].astype(F32) * inv
    ang_rel = jnp.arange(l, dtype=F32)[None, :, None] * inv
    cos = (jnp.cos(ang_off) * jnp.cos(ang_rel) - jnp.sin(ang_off) * jnp.sin(ang_rel)).reshape(t, HALF_ROPE)
    sin = (jnp.sin(ang_off) * jnp.cos(ang_rel) + jnp.cos(ang_off) * jnp.sin(ang_rel)).reshape(t, HALF_ROPE)
    zr = jnp.zeros_like(cos)
    cos_t = jnp.concatenate([cos, zr, cos, zr], axis=1)
    sin_t = jnp.concatenate([-sin, zr, sin, zr], axis=1)

    x2 = x.reshape(t, d)
    proj = _in_proj(x2, g_pre, w_pad, tm=1024, tn=2 * tn_in if n_pad % (2 * tn_in) == 0 else tn_in)

    q_a = _qa_proj(proj, off["cq"] // Q_LORA, g_q_lat, w_uq_pad, gq_pad, cos_t, sin_t, q_shift, tm=256)
    k_a, v_a = _kva_proj(proj, off["ckv"] // KV_LORA, off["k_rope"] // LANE, g_kv_lat, w_uk, w_uv,
                         gk_nope, gk_rope, cos_t, sin_t, tm=256)
    o_a = _attn_a(q_a, k_a, v_a, proj, off["gate_a"] // LANE, b=b, l=l, tq=tq, heads=4)

    nq = l // tq
    w_idx = proj[:, off["w_idx"]:off["w_idx"] + IDX_HEADS]
    w_idx_t = w_idx.astype(F32).T
    v_b = proj[:, off["v_b"]:off["v_b"] + HEAD_DIM_B]
    vt = v_b.reshape(b, nq, tq, HEAD_DIM_B).transpose(0, 1, 3, 2)
    blk = {"q_idx": off["q_idx"] // (IDX_HEADS * IDX_DIM), "k_idx": off["k_idx"] // LANE,
           "q_b": off["q_b"] // WIDTH_B, "k_b": off["k_b"] // LANE, "gate_b": off["gate_b"] // WIDTH_B}
    o_b = _attn_b(proj, w_idx_t, vt, t5_bias.astype(F32), gq_b, gk_b, blk, b=b, l=l, tq=tq)

    merged = _merge(o_a, o_b, p_a.astype(MXU_DTYPE), p_b.astype(MXU_DTYPE), proj,
                    off["merge_a"], off["merge_b"], tm=1024, tn=1024)
    out = _out_proj(merged, w_o.astype(MXU_DTYPE), x2, tm=1024, tn=1024)
    return out.reshape(b, l, d)
```

```python
import functools
import math

import jax
import jax.numpy as jnp
from jax import lax
from jax.experimental import pallas as pl
from jax.experimental.pallas import tpu as pltpu

F32 = jnp.float32
I32 = jnp.int32
MXU_DTYPE = jnp.bfloat16

H_A = 16
QK_NOPE = 128
QK_ROPE = 64
QK_DIM_A = QK_NOPE + QK_ROPE
V_DIM_A = 128
Q_LORA = 1024
KV_LORA = 512
ROPE_THETA = 10000.0
H_B = 16
HEAD_DIM_B = 128
IDX_HEADS = 32
IDX_DIM = 64
TOPK_MAX = 256
N_BUCKETS = 32
MAX_DISTANCE = 128
EPS = 1e-6
WIDTH_A = H_A * V_DIM_A
WIDTH_B = H_B * HEAD_DIM_B

LANE = 128
ROW_ALIGN = 32
HALF_ROPE = QK_ROPE // 2
HEAD_PAD_A = 2 * LANE
SHIFT_LANE = LANE + HALF_ROPE
BOUND_MARGIN = 1.0 + 2.0 ** -6
MIN_SHIFTED_SUM = 2.0 ** -64
VMEM_CAP = 56 * 1024 * 1024
VMEM_HEADROOM = 4 * 1024 * 1024

BISECT_STEPS_PER_CHECK = 5
BISECT_MAX_CHECKS = 32
LOG2E = math.log2(math.e)
NEG_INF = float("-inf")
POS_INF = float("inf")


def _nt_dot(a, b):
    return lax.dot_general(a, b, (((1,), (1,)), ((), ())), preferred_element_type=F32)


def _nbytes(shape, dtype):
    return math.prod(shape) * jnp.dtype(dtype).itemsize


def _params(sem, windows, scratch=0, temps=0):
    need = 2 * sum(windows) + scratch + temps + VMEM_HEADROOM
    return pltpu.CompilerParams(dimension_semantics=sem, vmem_limit_bytes=min(need, VMEM_CAP))


def _w_relayout_kernel(start_ref, src_ref, tail_ref, o_ref, *, n_main):
    j = pl.program_id(0)

    @pl.when(j < n_main)
    def _():
        o_ref[...] = src_ref[...].astype(o_ref.dtype)

    @pl.when(j >= n_main)
    def _():
        o_ref[...] = tail_ref[...].astype(o_ref.dtype)


def _w_relayout(wt, src_starts, wt_tail, *, tn, tc):
    d = wt.shape[1]
    n_main, n_tail = len(src_starts), wt_tail.shape[0] // tn
    assert all(s % ROW_ALIGN == 0 for s in src_starts)
    starts = jnp.array([s // ROW_ALIGN for s in src_starts] + [0] * n_tail, I32)
    return pl.pallas_call(
        functools.partial(_w_relayout_kernel, n_main=n_main),
        out_shape=jax.ShapeDtypeStruct(((n_main + n_tail) * tn, d), MXU_DTYPE),
        grid_spec=pltpu.PrefetchScalarGridSpec(
            num_scalar_prefetch=1,
            grid=(n_main + n_tail, d // tc),
            in_specs=[
                pl.BlockSpec((pl.Element(tn), pl.Element(tc)), lambda j, c, st: (st[j] * ROW_ALIGN, c * tc)),
                pl.BlockSpec((tn, tc), lambda j, c, st: (jnp.maximum(j - n_main, 0), c)),
            ],
            out_specs=pl.BlockSpec((tn, tc), lambda j, c, st: (j, c)),
        ),
        compiler_params=_params(("arbitrary", "arbitrary"),
                                [_nbytes((tn, tc), wt.dtype), _nbytes((tn, tc), wt_tail.dtype),
                                 _nbytes((tn, tc), MXU_DTYPE)]),
        name="w_relayout",
    )(starts, wt, wt_tail)


def _in_proj_kernel(x_hbm, g_ref, w_ref, o_ref, x_buf, hn_ref, x_sem, *, row_chunk):
    tm = x_buf.shape[0]
    i, j = pl.program_id(0), pl.program_id(1)

    def x_copy(row_block):
        rows = pl.ds(pl.multiple_of(row_block * tm, tm), tm)
        return pltpu.make_async_copy(x_hbm.at[rows, :], x_buf, x_sem)

    @pl.when((i == 0) & (j == 0))
    def _():
        x_copy(0).start()

    @pl.when(j == 0)
    def _():
        x_copy(i).wait()

        def body(r, carry):
            sl = pl.ds(pl.multiple_of(r * row_chunk, row_chunk), row_chunk)
            xx = x_buf[sl, :]
            ms = jnp.mean(xx * xx, axis=-1, keepdims=True)
            hn_ref[sl, :] = (xx * lax.rsqrt(ms + EPS) * g_ref[...]).astype(hn_ref.dtype)
            return carry

        lax.fori_loop(0, tm // row_chunk, body, 0)

    @pl.when((j == 1) & (i + 1 < pl.num_programs(0)))
    def _():
        x_copy(i + 1).start()

    o_ref[...] = _nt_dot(hn_ref[...], w_ref[...]).astype(o_ref.dtype)


def _in_proj(x2, g_pre, wt_pad, *, tm, tn):
    t, d = x2.shape
    n = wt_pad.shape[0]
    assert n // tn >= 2
    return pl.pallas_call(
        functools.partial(_in_proj_kernel, row_chunk=64),
        out_shape=jax.ShapeDtypeStruct((t, n), MXU_DTYPE),
        grid=(t // tm, n // tn),
        in_specs=[
            pl.BlockSpec(memory_space=pl.ANY),
            pl.BlockSpec((1, d), lambda i, j: (0, 0)),
            pl.BlockSpec((tn, d), lambda i, j: (j, 0)),
        ],
        out_specs=pl.BlockSpec((tm, tn), lambda i, j: (i, j)),
        scratch_shapes=[pltpu.VMEM((tm, d), x2.dtype), pltpu.VMEM((tm, d), MXU_DTYPE),
                        pltpu.SemaphoreType.DMA(())],
        compiler_params=_params(("arbitrary", "arbitrary"),
                                [_nbytes((1, d), F32), _nbytes((tn, d), MXU_DTYPE), _nbytes((tm, tn), MXU_DTYPE)],
                                scratch=_nbytes((tm, d), x2.dtype) + _nbytes((tm, d), MXU_DTYPE),
                                temps=2 * _nbytes((tm, tn), F32)),
        name="in_proj",
    )(x2, g_pre.reshape(1, d), wt_pad)


def _rope_lanes(r, cos_ref, sin_ref):
    return r * cos_ref[...] + pltpu.roll(r, 2 * HALF_ROPE, 1) * sin_ref[...]


def _qa_proj_kernel(cq_ref, gl_ref, w_ref, gq_ref, cos_ref, sin_ref, shift_ref, o_ref):
    c = cq_ref[...].astype(F32)
    ms = jnp.mean(c * c, axis=-1, keepdims=True)
    cn = (c * lax.rsqrt(ms + EPS) * gl_ref[...]).astype(MXU_DTYPE)
    q = jnp.dot(cn, w_ref[...], preferred_element_type=F32)
    for h in range(H_A):
        lo = h * HEAD_PAD_A
        qh = q[:, lo:lo + HEAD_PAD_A]
        ss = jnp.sum(qh * qh, axis=-1, keepdims=True) * (1.0 / QK_DIM_A)
        qn = qh * lax.rsqrt(ss + EPS) * gq_ref[:, lo:lo + HEAD_PAD_A]
        o_ref[:, lo:lo + LANE] = qn[:, :LANE].astype(o_ref.dtype)
        o_ref[:, lo + LANE:lo + HEAD_PAD_A] = (
            _rope_lanes(qn[:, LANE:], cos_ref, sin_ref) + shift_ref[...]).astype(o_ref.dtype)


def _qa_proj(proj, cq_blk, g_q_lat, w_uq_pad, gq_pad, cos_t, sin_t, q_shift, *, tm):
    t = proj.shape[0]
    nq = H_A * HEAD_PAD_A
    return pl.pallas_call(
        _qa_proj_kernel,
        out_shape=jax.ShapeDtypeStruct((t, nq), MXU_DTYPE),
        grid=(t // tm,),
        in_specs=[
            pl.BlockSpec((tm, Q_LORA), lambda i: (i, cq_blk)),
            pl.BlockSpec((1, Q_LORA), lambda i: (0, 0)),
            pl.BlockSpec((Q_LORA, nq), lambda i: (0, 0)),
            pl.BlockSpec((1, nq), lambda i: (0, 0)),
            pl.BlockSpec((tm, LANE), lambda i: (i, 0)),
            pl.BlockSpec((tm, LANE), lambda i: (i, 0)),
            pl.BlockSpec((1, LANE), lambda i: (0, 0)),
        ],
        out_specs=pl.BlockSpec((tm, nq), lambda i: (i, 0)),
        compiler_params=_params(("arbitrary",),
                                [_nbytes((tm, Q_LORA), MXU_DTYPE), _nbytes((Q_LORA, nq), MXU_DTYPE),
                                 _nbytes((1, Q_LORA + nq + LANE), F32), 2 * _nbytes((tm, LANE), F32),
                                 _nbytes((tm, nq), MXU_DTYPE)],
                                temps=_nbytes((tm, nq), F32)),
        name="qa_proj",
    )(proj, g_q_lat.reshape(1, Q_LORA), w_uq_pad, gq_pad, cos_t, sin_t, q_shift)


def _kva_proj_kernel(ckv_ref, kr_ref, gl_ref, wk_ref, wv_ref, gkn_ref, gkr_ref, cos_ref, sin_ref,
                     k_ref, v_ref):
    c = ckv_ref[...].astype(F32)
    ms = jnp.mean(c * c, axis=-1, keepdims=True)
    cn = (c * lax.rsqrt(ms + EPS) * gl_ref[...]).astype(MXU_DTYPE)
    kn = jnp.dot(cn, wk_ref[...], preferred_element_type=F32)
    v_ref[...] = jnp.dot(cn, wv_ref[...], preferred_element_type=F32).astype(v_ref.dtype)
    kr = kr_ref[...].astype(F32)
    ss_r = jnp.sum(kr * kr, axis=-1, keepdims=True)
    krr = _rope_lanes(kr * gkr_ref[...], cos_ref, sin_ref)
    shift_one = (lax.broadcasted_iota(I32, (1, LANE), 1) == SHIFT_LANE - LANE).astype(F32)
    for h in range(H_A):
        kh = kn[:, h * LANE:(h + 1) * LANE]
        ss = (jnp.sum(kh * kh, axis=-1, keepdims=True) + ss_r) * (1.0 / QK_DIM_A)
        rs = lax.rsqrt(ss + EPS)
        lo = h * HEAD_PAD_A
        k_ref[:, lo:lo + LANE] = (kh * rs * gkn_ref[...]).astype(k_ref.dtype)
        k_ref[:, lo + LANE:lo + HEAD_PAD_A] = (krr * rs + shift_one).astype(k_ref.dtype)


def _kva_proj(proj, ckv_blk, krope_blk, g_kv_lat, w_uk, w_uv, gk_nope, gk_rope, cos_t, sin_t, *, tm):
    t = proj.shape[0]
    return pl.pallas_call(
        _kva_proj_kernel,
        out_shape=(jax.ShapeDtypeStruct((t, H_A * HEAD_PAD_A), MXU_DTYPE),
                   jax.ShapeDtypeStruct((t, WIDTH_A), MXU_DTYPE)),
        grid=(t // tm,),
        in_specs=[
            pl.BlockSpec((tm, KV_LORA), lambda i: (i, ckv_blk)),
            pl.BlockSpec((tm, LANE), lambda i: (i, krope_blk)),
            pl.BlockSpec((1, KV_LORA), lambda i: (0, 0)),
            pl.BlockSpec((KV_LORA, H_A * QK_NOPE), lambda i: (0, 0)),
            pl.BlockSpec((KV_LORA, WIDTH_A), lambda i: (0, 0)),
            pl.BlockSpec((1, LANE), lambda i: (0, 0)),
            pl.BlockSpec((1, LANE), lambda i: (0, 0)),
            pl.BlockSpec((tm, LANE), lambda i: (i, 0)),
            pl.BlockSpec((tm, LANE), lambda i: (i, 0)),
        ],
        out_specs=(pl.BlockSpec((tm, H_A * HEAD_PAD_A), lambda i: (i, 0)),
                   pl.BlockSpec((tm, WIDTH_A), lambda i: (i, 0))),
        compiler_params=_params(("arbitrary",),
                                [_nbytes((tm, KV_LORA + LANE), MXU_DTYPE), 2 * _nbytes((KV_LORA, WIDTH_A), MXU_DTYPE),
                                 _nbytes((1, KV_LORA + 2 * LANE), F32), 2 * _nbytes((tm, LANE), F32),
                                 _nbytes((tm, H_A * HEAD_PAD_A + WIDTH_A), MXU_DTYPE)],
                                temps=2 * _nbytes((tm, WIDTH_A), F32)),
        name="kva_proj",
    )(proj, proj, g_kv_lat.reshape(1, KV_LORA), w_uk, w_uv, gk_nope, gk_rope, cos_t, sin_t)


def _silu(g):
    return g * (1.0 / (1.0 + jnp.exp(-g)))


def _lane_tile_reduce(x, op):
    acc = x[:, :LANE]
    for t in range(1, x.shape[1] // LANE):
        acc = op(acc, x[:, t * LANE:(t + 1) * LANE])
    return acc


def _attn_a_kernel(q_ref, k_ref, v_ref, gate_ref, o_ref, *, tq, nq, heads):
    causal = lax.broadcasted_iota(I32, (tq, tq), 0) >= lax.broadcasted_iota(I32, (tq, tq), 1)

    def query_block(qv):
        n_off = qv * tq
        rows = slice(n_off, n_off + tq)

        def finish(g, l_t, acc):
            vc = slice(g * V_DIM_A, (g + 1) * V_DIM_A)
            l = jnp.sum(l_t, axis=-1, keepdims=True)
            o = acc * (1.0 / l)
            o_ref[rows, vc] = (o * _silu(gate_ref[rows, vc].astype(F32))).astype(o_ref.dtype)

        kcs = [slice(g * HEAD_PAD_A, (g + 1) * HEAD_PAD_A) for g in range(heads)]
        vcs = [slice(g * V_DIM_A, (g + 1) * V_DIM_A) for g in range(heads)]

        l_min = jnp.full((tq, 1), POS_INF, F32)
        for g in range(heads):
            q = q_ref[rows, kcs[g]]
            p_diag = jnp.exp2(jnp.where(causal, _nt_dot(q, k_ref[n_off:n_off + tq, kcs[g]]), NEG_INF))
            l_t = _lane_tile_reduce(p_diag, jnp.add)
            acc = jnp.dot(p_diag.astype(MXU_DTYPE), v_ref[n_off:n_off + tq, vcs[g]], preferred_element_type=F32)
            if qv > 0:
                p_off = jnp.exp2(_nt_dot(q, k_ref[0:n_off, kcs[g]]))
                l_t = l_t + _lane_tile_reduce(p_off, jnp.add)
                acc = acc + jnp.dot(p_off.astype(MXU_DTYPE), v_ref[0:n_off, vcs[g]], preferred_element_type=F32)
            l_min = jnp.minimum(l_min, jnp.sum(l_t, axis=-1, keepdims=True))
            finish(g, l_t, acc)

        shift_ok = jnp.min(l_min) >= MIN_SHIFTED_SUM

        @pl.when(jnp.logical_not(shift_ok))
        def _():
            for g in range(heads):
                q = q_ref[rows, kcs[g]]
                q_hi = q[:, LANE:]
                lane = lax.broadcasted_iota(I32, q_hi.shape, 1)
                q = jnp.concatenate(
                    [q[:, :LANE], jnp.where(lane == SHIFT_LANE - LANE, 0.0, q_hi.astype(F32)).astype(q.dtype)], axis=1)
                s_diag = jnp.where(causal, _nt_dot(q, k_ref[n_off:n_off + tq, kcs[g]]), NEG_INF)
                m_t = _lane_tile_reduce(s_diag, jnp.maximum)
                if qv > 0:
                    s_off = _nt_dot(q, k_ref[0:n_off, kcs[g]])
                    m_t = jnp.maximum(m_t, _lane_tile_reduce(s_off, jnp.maximum))
                m = jnp.max(m_t, axis=-1, keepdims=True)
                p_diag = jnp.exp2(s_diag - m)
                l_t = _lane_tile_reduce(p_diag, jnp.add)
                acc = jnp.dot(p_diag.astype(MXU_DTYPE), v_ref[n_off:n_off + tq, vcs[g]],
                              preferred_element_type=F32)
                if qv > 0:
                    p_off = jnp.exp2(s_off - m)
                    l_t = l_t + _lane_tile_reduce(p_off, jnp.add)
                    acc = acc + jnp.dot(p_off.astype(MXU_DTYPE), v_ref[0:n_off, vcs[g]], preferred_element_type=F32)
                finish(g, l_t, acc)

    for qv in range(nq):
        query_block(qv)


def _attn_a(q_a, k_a, v_a, proj, gate_blk0, *, b, l, tq, heads):
    t = q_a.shape[0]
    nq = l // tq
    kw, vw = heads * HEAD_PAD_A, heads * V_DIM_A
    return pl.pallas_call(
        functools.partial(_attn_a_kernel, tq=tq, nq=nq, heads=heads),
        out_shape=jax.ShapeDtypeStruct((t, WIDTH_A), MXU_DTYPE),
        grid=(b, H_A // heads),
        in_specs=[
            pl.BlockSpec((l, kw), lambda bi, h: (bi, h)),
            pl.BlockSpec((l, kw), lambda bi, h: (bi, h)),
            pl.BlockSpec((l, vw), lambda bi, h: (bi, h)),
            pl.BlockSpec((l, vw), lambda bi, h: (bi, gate_blk0 // heads + h)),
        ],
        out_specs=pl.BlockSpec((l, vw), lambda bi, h: (bi, h)),
        compiler_params=_params(("arbitrary", "arbitrary"),
                                [2 * _nbytes((l, kw), MXU_DTYPE), 3 * _nbytes((l, vw), MXU_DTYPE)],
                                temps=heads * (_nbytes((tq, l), F32) + _nbytes((tq, l), MXU_DTYPE))),
        name="attn_a",
    )(q_a, k_a, v_a, proj)


def _t5_bucket(dist):
    max_exact = N_BUCKETS // 2
    n = jnp.maximum(dist, 0)
    nf = jnp.maximum(n, 1).astype(F32)
    large = max_exact + (jnp.log(nf / max_exact) / math.log(MAX_DISTANCE / max_exact)
                         * (N_BUCKETS - max_exact)).astype(I32)
    large = jnp.minimum(large, N_BUCKETS - 1)
    return jnp.where(n < max_exact, n, large)


def _attn_b_kernel(t5_ref, qidx_ref, kidx_ref, wt_ref, qb_ref, kb_ref, vt_ref, gate_ref, gq_ref, gk_ref,
                   o_ref, sc_ref, qn_ref, acc_ref, m_ref, l_ref, bias_ref, thr_ref, kn_ref, bound_ref,
                   *, tq, nq, topk, max_iters):
    bi = pl.program_id(0)
    qi = pl.program_id(1)
    ck = tq
    shape = (ck, tq)

    @pl.when((bi == 0) & (qi == 0))
    def _():
        s_loc = lax.broadcasted_iota(I32, shape, 0)
        t_loc = lax.broadcasted_iota(I32, shape, 1)
        for near in range(2):
            bucket = _t5_bucket(t_loc - s_loc + (1 - near) * ck)

            def per_head(h, carry, bucket=bucket, near=near):
                far = t5_ref[N_BUCKETS - 1, h]
                tab = jnp.zeros(shape, F32)
                largest = jnp.float32(0.0)
                for bk in range(N_BUCKETS - 1):
                    rel = (t5_ref[bk, h] - far) * LOG2E
                    tab = jnp.where(bucket == bk, rel, tab)
                    largest = jnp.maximum(largest, rel)
                bias_ref[h, near] = tab
                gains = jnp.max(jnp.abs(gq_ref[...])) * jnp.max(jnp.abs(gk_ref[...]))
                bound_ref[h] = HEAD_DIM_B * gains * BOUND_MARGIN + largest
                return carry

            lax.fori_loop(0, H_B, per_head, 0)

    w_all = wt_ref[...] * (IDX_HEADS ** -0.5)

    def score_chunk(c, diag):
        rows = pl.ds(pl.multiple_of(c * ck, ck), ck)
        kx = kidx_ref[rows, 0:IDX_DIM]
        zk = jnp.zeros_like(kx)
        kab = jnp.concatenate([jnp.concatenate([kx, zk], axis=1), jnp.concatenate([zk, kx], axis=1)], axis=0)
        score = jnp.zeros(shape, F32)
        for j in range(IDX_HEADS // 2):
            qp = qidx_ref[:, j * LANE:(j + 1) * LANE]
            logits = jnp.maximum(_nt_dot(kab, qp), 0.0)
            score = score + logits[:ck] * w_all[2 * j:2 * j + 1, :]
            score = score + logits[ck:] * w_all[2 * j + 1:2 * j + 2, :]
        if diag:
            adm = lax.broadcasted_iota(I32, shape, 0) <= lax.broadcasted_iota(I32, shape, 1)
            lo_src = jnp.where(adm, score, POS_INF)
            score = jnp.where(adm, score, NEG_INF)
        else:
            lo_src = score
        sc_ref[c] = score
        return jnp.max(score, axis=0, keepdims=True), jnp.min(lo_src, axis=0, keepdims=True)

    def score_body(c, carry):
        mx, mn = carry
        cmx, cmn = score_chunk(c, False)
        return jnp.maximum(mx, cmx), jnp.minimum(mn, cmn)

    mx, mn = lax.fori_loop(0, qi, score_body,
                           (jnp.full((1, tq), NEG_INF, F32), jnp.full((1, tq), POS_INF, F32)))
    dmx, dmn = score_chunk(qi, True)
    mx = jnp.maximum(mx, dmx)
    mn = jnp.minimum(mn, dmn)

    rep = (8, tq)
    n_adm = qi * tq + lax.broadcasted_iota(I32, rep, 1) + 1
    kp = jnp.minimum(n_adm, topk)
    mx8 = jnp.broadcast_to(mx, rep)
    mn8 = jnp.broadcast_to(mn, rep)

    def bisect(nchunks):
        groups = range(tq // LANE)
        split = lambda a: tuple(a[:, g * LANE:(g + 1) * LANE] for g in groups)
        join = lambda parts: jnp.concatenate(parts, axis=1)
        kp_g, mx_g = split(kp), split(mx8)

        def count_ge(x, g):
            acc = jnp.zeros((8, LANE), I32)
            for c in range(nchunks):
                ge = sc_ref[c, :, g * LANE:(g + 1) * LANE].reshape(ck // 8, 8, LANE) >= x[None]
                acc = acc + jnp.sum(ge.astype(I32), axis=0)
            for shift in (4, 2, 1):
                acc = acc + pltpu.roll(acc, shift, 0)
            return acc

        def bis_cond(st):
            it, lo, hi, mid, cnt_lo = st
            active = [(cnt_lo[g] != kp_g[g]) & (mid[g] > lo[g]) & (mid[g] < hi[g]) for g in groups]
            return jnp.logical_and(it < max_iters, jnp.max(join(active).astype(I32)) > 0)

        def bis_body(st):
            it, lo, hi, mid, cnt_lo = st
            lo, hi, mid, cnt_lo = list(lo), list(hi), list(mid), list(cnt_lo)
            for _ in range(BISECT_STEPS_PER_CHECK):
                for g in groups:
                    cnt = count_ge(mid[g], g)
                    ge = cnt >= kp_g[g]
                    lo[g] = jnp.where(ge, mid[g], lo[g])
                    cnt_lo[g] = jnp.where(ge, cnt, cnt_lo[g])
                    hi[g] = jnp.where(ge, hi[g], mid[g])
                    mid[g] = jnp.where(hi[g] == POS_INF, mx_g[g], lo[g] + 0.5 * (hi[g] - lo[g]))
            return it + 1, tuple(lo), tuple(hi), tuple(mid), tuple(cnt_lo)

        _, lo, hi, _, cnt_lo = lax.while_loop(
            bis_cond, bis_body,
            (jnp.int32(0), split(mn8), split(jnp.full(rep, POS_INF, F32)), split(mx8), split(n_adm)))
        lo, hi, cnt_lo = join(lo), join(hi), join(cnt_lo)
        thr_ref[...] = lo

        tied = cnt_lo > kp

        @pl.when(jnp.max(tied.astype(I32)) > 0)
        def _():
            n_keys = nchunks * ck
            sub = lax.broadcasted_iota(I32, (ck // 8, 8, tq), 0) * 8 + lax.broadcasted_iota(I32, (ck // 8, 8, tq), 1)

            def count_kept(j_last):
                acc = jnp.zeros(rep, I32)
                for c in range(nchunks):
                    s3 = sc_ref[c].reshape(ck // 8, 8, tq)
                    keep = (s3 >= hi[None]) | ((s3 >= lo[None]) & (sub + c * ck <= j_last[None]))
                    acc = acc + jnp.sum(keep.astype(I32), axis=0)
                for shift in (4, 2, 1):
                    acc = acc + pltpu.roll(acc, shift, 0)
                return acc

            def idx_step(_, carry):
                j_lo, j_hi = carry
                j_mid = j_lo + ((j_hi - j_lo) >> 1)
                ok = count_kept(j_mid) >= kp
                return jnp.where(ok, j_lo, j_mid), jnp.where(ok, j_mid, j_hi)

            _, j_hi = lax.fori_loop(0, max(1, (n_keys - 1).bit_length()), idx_step,
                                    (jnp.full(rep, -1, I32), jnp.full(rep, n_keys - 1, I32)))
            j_last = jnp.where(tied, j_hi, n_keys - 1)
            for c in range(nchunks):
                s3 = sc_ref[c].reshape(ck // 8, 8, tq)
                drop = (s3 >= lo[None]) & (s3 < hi[None]) & (sub + c * ck > j_last[None])
                sc_ref[c] = jnp.where(drop, NEG_INF, s3).reshape(ck, tq)

    for qv in range(nq):
        pl.when(qi == qv)(functools.partial(bisect, qv + 1))
    thr = thr_ref[0:1, :]

    @pl.when(qi == 0)
    def _():
        kf = kb_ref[...].astype(F32)
        ms = jnp.mean(kf * kf, axis=-1, keepdims=True)
        kn_ref[...] = (kf * lax.rsqrt(ms + EPS) * gk_ref[...]).astype(kn_ref.dtype)

    for h in range(H_B):
        qh = qb_ref[:, h * LANE:(h + 1) * LANE].astype(F32)
        ms = jnp.mean(qh * qh, axis=-1, keepdims=True)
        qn_ref[h * tq:(h + 1) * tq, :] = (qh * lax.rsqrt(ms + EPS) * gq_ref[...]).astype(qn_ref.dtype)

    def attend_chunk(c, near, exact):
        rows = pl.ds(pl.multiple_of(c * ck, ck), ck)
        vt = vt_ref[c]
        sel = sc_ref[c] >= thr
        s_all = _nt_dot(kn_ref[rows, :], qn_ref[...])
        for h in range(H_B):
            s = s_all[:, h * tq:(h + 1) * tq]
            if near is not None:
                s = s + bias_ref[h, near]
            if exact:
                s = jnp.where(sel, s, NEG_INF)
                m_old = m_ref[h]
                m_new = jnp.maximum(m_old, jnp.max(s, axis=0, keepdims=True))
                m_safe = jnp.where(m_new == NEG_INF, 0.0, m_new)
                p = jnp.exp2(s - m_safe)
                alpha = jnp.exp2(m_old - m_safe)
                l_ref[h] = alpha * l_ref[h] + jnp.sum(p, axis=0, keepdims=True)
                acc_ref[h] = alpha * acc_ref[h] + jnp.dot(vt, p.astype(MXU_DTYPE), preferred_element_type=F32)
                m_ref[h] = m_new
            else:
                p = jnp.exp2(jnp.where(sel, s - bound_ref[h], NEG_INF))
                l_ref[h] = l_ref[h] + jnp.sum(p, axis=0, keepdims=True)
                acc_ref[h] = acc_ref[h] + jnp.dot(vt, p.astype(MXU_DTYPE), preferred_element_type=F32)

    def attend(exact):
        if exact:
            m_ref[...] = jnp.full(m_ref.shape, NEG_INF, F32)
        l_ref[...] = jnp.zeros(l_ref.shape, F32)
        acc_ref[...] = jnp.zeros(acc_ref.shape, F32)

        def far_body(c, carry):
            attend_chunk(c, None, exact)
            return carry

        lax.fori_loop(0, jnp.maximum(qi - 1, 0), far_body, 0)

        @pl.when(qi >= 1)
        def _():
            attend_chunk(qi - 1, 0, exact)

        attend_chunk(qi, 1, exact)

    attend(False)
    l_min = l_ref[0]
    for h in range(1, H_B):
        l_min = jnp.minimum(l_min, l_ref[h])
    shift_ok = jnp.min(l_min) >= MIN_SHIFTED_SUM
    pl.when(jnp.logical_not(shift_ok))(functools.partial(attend, True))

    for h in range(H_B):
        o_t = acc_ref[h] * (1.0 / l_ref[h])
        g = gate_ref[:, h * LANE:(h + 1) * LANE].astype(F32)
        o_ref[:, h * LANE:(h + 1) * LANE] = (o_t.T * _silu(g)).astype(o_ref.dtype)


def _attn_b(proj, wt, vt, t5_bias, gq_b, gk_b, blk, *, b, l, tq):
    t = proj.shape[0]
    nq = l // tq
    topk = min(TOPK_MAX, l // 4)
    row = lambda bi, qi: bi * nq + qi
    return pl.pallas_call(
        functools.partial(_attn_b_kernel, tq=tq, nq=nq, topk=topk, max_iters=BISECT_MAX_CHECKS),
        out_shape=jax.ShapeDtypeStruct((t, WIDTH_B), MXU_DTYPE),
        grid=(b, nq),
        in_specs=[
            pl.BlockSpec(memory_space=pltpu.SMEM),
            pl.BlockSpec((tq, IDX_HEADS * IDX_DIM), lambda bi, qi: (row(bi, qi), blk["q_idx"])),
            pl.BlockSpec((l, LANE), lambda bi, qi: (bi, blk["k_idx"])),
            pl.BlockSpec((IDX_HEADS, tq), lambda bi, qi: (0, row(bi, qi))),
            pl.BlockSpec((tq, WIDTH_B), lambda bi, qi: (row(bi, qi), blk["q_b"])),
            pl.BlockSpec((l, LANE), lambda bi, qi: (bi, blk["k_b"])),
            pl.BlockSpec((None, nq, LANE, tq), lambda bi, qi: (bi, 0, 0, 0)),
            pl.BlockSpec((tq, WIDTH_B), lambda bi, qi: (row(bi, qi), blk["gate_b"])),
            pl.BlockSpec((1, LANE), lambda bi, qi: (0, 0)),
            pl.BlockSpec((1, LANE), lambda bi, qi: (0, 0)),
        ],
        out_specs=pl.BlockSpec((tq, WIDTH_B), lambda bi, qi: (row(bi, qi), 0)),
        scratch_shapes=[
            pltpu.VMEM((nq, tq, tq), F32),
            pltpu.VMEM((H_B * tq, LANE), MXU_DTYPE),
            pltpu.VMEM((H_B, LANE, tq), F32),
            pltpu.VMEM((H_B, 1, tq), F32),
            pltpu.VMEM((H_B, 1, tq), F32),
            pltpu.VMEM((H_B, 2, tq, tq), F32),
            pltpu.VMEM((8, tq), F32),
            pltpu.VMEM((l, LANE), MXU_DTYPE),
            pltpu.SMEM((H_B,), F32),
        ],
        compiler_params=_params(
            ("arbitrary", "arbitrary"),
            [_nbytes((tq, IDX_HEADS * IDX_DIM + 2 * WIDTH_B), MXU_DTYPE), _nbytes((l, 2 * LANE), MXU_DTYPE),
             _nbytes((IDX_HEADS, tq), F32), _nbytes((nq, LANE, tq), MXU_DTYPE), _nbytes((tq, WIDTH_B), MXU_DTYPE)],
            scratch=(_nbytes((nq, tq, tq), F32) + _nbytes((H_B * tq + l, LANE), MXU_DTYPE)
                     + _nbytes((H_B, LANE + 2 * 8, tq), F32) + _nbytes((H_B, 2, tq, tq), F32) + _nbytes((8, tq), F32)),
            temps=2 * _nbytes((tq, H_B * tq), F32)),
        name="attn_b",
    )(t5_bias, proj, proj, wt, proj, proj, vt, proj, gq_b, gk_b)


def _sigmoid(z):
    return 1.0 / (1.0 + jnp.exp(-z))


def _merge_kernel(oa_ref, ob_ref, pa_ref, pb_ref, ma_ref, mb_ref, o_ref):
    a = jnp.dot(oa_ref[...], pa_ref[...], preferred_element_type=F32)
    bb = jnp.dot(ob_ref[...], pb_ref[...], preferred_element_type=F32)
    o_ref[...] = (_sigmoid(ma_ref[...].astype(F32)) * a + _sigmoid(mb_ref[...].astype(F32)) * bb).astype(o_ref.dtype)


def _merge(o_a, o_b, p_a, p_b, proj, ma_off, mb_off, *, tm, tn):
    t = o_a.shape[0]
    d = p_a.shape[1]
    ma0, mb0 = ma_off // tn, mb_off // tn
    return pl.pallas_call(
        _merge_kernel,
        out_shape=jax.ShapeDtypeStruct((t, d), MXU_DTYPE),
        grid=(t // tm, d // tn),
        in_specs=[
            pl.BlockSpec((tm, WIDTH_A), lambda i, j: (i, 0)),
            pl.BlockSpec((tm, WIDTH_B), lambda i, j: (i, 0)),
            pl.BlockSpec((WIDTH_A, tn), lambda i, j: (0, j)),
            pl.BlockSpec((WIDTH_B, tn), lambda i, j: (0, j)),
            pl.BlockSpec((tm, tn), lambda i, j: (i, ma0 + j)),
            pl.BlockSpec((tm, tn), lambda i, j: (i, mb0 + j)),
        ],
        out_specs=pl.BlockSpec((tm, tn), lambda i, j: (i, j)),
        compiler_params=_params(("arbitrary", "arbitrary"),
                                [_nbytes((tm, WIDTH_A + WIDTH_B), MXU_DTYPE), _nbytes((WIDTH_A + WIDTH_B, tn), MXU_DTYPE),
                                 3 * _nbytes((tm, tn), MXU_DTYPE)],
                                temps=2 * _nbytes((tm, tn), F32)),
        name="merge",
    )(o_a, o_b, p_a, p_b, proj, proj)


def _out_proj_kernel(m_ref, w_ref, x_ref, o_ref):
    o_ref[...] = x_ref[...] + jnp.dot(m_ref[...], w_ref[...], preferred_element_type=F32)


def _out_proj(merged, w_o, x2, *, tm, tn):
    t, d = x2.shape
    return pl.pallas_call(
        _out_proj_kernel,
        out_shape=jax.ShapeDtypeStruct((t, d), x2.dtype),
        grid=(t // tm, d // tn),
        in_specs=[
            pl.BlockSpec((tm, d), lambda i, j: (i, 0)),
            pl.BlockSpec((d, tn), lambda i, j: (0, j)),
            pl.BlockSpec((tm, tn), lambda i, j: (i, j)),
        ],
        out_specs=pl.BlockSpec((tm, tn), lambda i, j: (i, j)),
        compiler_params=_params(("arbitrary", "arbitrary"),
                                [_nbytes((tm, d), MXU_DTYPE), _nbytes((d, tn), MXU_DTYPE), 2 * _nbytes((tm, tn), x2.dtype)],
                                temps=_nbytes((tm, tn), F32)),
        name="out_proj",
    )(merged, w_o, x2)


def _rope_pad(a, axis):
    a1, a2 = jnp.split(a, 2, axis=axis)
    z = jnp.zeros_like(a1)
    return jnp.concatenate([a1, z, a2, z], axis=axis)


def _layout(d):
    names = [("merge_a", d), ("merge_b", d), ("gate_a", WIDTH_A), ("gate_b", WIDTH_B),
             ("q_b", WIDTH_B), ("q_idx", IDX_HEADS * IDX_DIM), ("cq", Q_LORA), ("ckv", KV_LORA),
             ("k_idx", LANE), ("k_rope", LANE), ("k_b", LANE), ("v_b", LANE)]
    off, out = 0, {}
    for name, width in names:
        assert off % width == 0, (name, off, width)
        out[name] = off
        off += width
    out["w_idx"] = out["k_idx"] + IDX_DIM
    return out, off


def kernel(x, positions, g_pre, w_in, g_q_lat, g_kv_lat, w_uq, w_ukv, g_qn_a, g_kn_a,
           g_qn_b, g_kn_b, t5_bias, p_a, p_b, w_o):
    b, l, d = x.shape
    t = b * l
    tq = 256
    tn_in = 512
    off, n_used = _layout(d)
    n_pad = -(-n_used // tn_in) * tn_in

    names = ["cq", "ckv", "k_rope", "q_b", "k_b", "v_b", "q_idx", "k_idx", "w_idx", "gate_a", "gate_b",
             "merge_a", "merge_b"]
    sizes = [Q_LORA, KV_LORA, QK_ROPE, WIDTH_B, HEAD_DIM_B, HEAD_DIM_B, IDX_HEADS * IDX_DIM, IDX_DIM,
             IDX_HEADS, WIDTH_A, WIDTH_B, d, d]
    src, acc = {}, 0
    for name, s in zip(names, sizes):
        src[name] = (acc, s)
        acc += s
    main_groups = ["merge_a", "merge_b", "gate_a", "gate_b", "q_b", "q_idx", "cq", "ckv"]
    src_starts, dst = [], 0
    for name in main_groups:
        assert off[name] == dst and src[name][1] % tn_in == 0
        src_starts += [src[name][0] + c for c in range(0, src[name][1], tn_in)]
        dst += src[name][1]
    wt = w_in.T
    rows = lambda name: wt[src[name][0]:src[name][0] + src[name][1]]
    z = lambda n: jnp.zeros((n, d), w_in.dtype)
    assert off["k_idx"] == dst
    wt_tail = jnp.concatenate(
        [rows("k_idx"), rows("w_idx"), z(LANE - IDX_DIM - IDX_HEADS), _rope_pad(rows("k_rope"), 0),
         rows("k_b"), rows("v_b"), z(n_pad - n_used)], axis=0)
    w_pad = _w_relayout(wt, src_starts, wt_tail, tn=tn_in, tc=d)

    w_uq3 = w_uq.reshape(Q_LORA, H_A, QK_DIM_A)
    w_uq_pad = jnp.concatenate([w_uq3[:, :, :QK_NOPE], _rope_pad(w_uq3[:, :, QK_NOPE:], 2)], axis=2)
    w_uq_pad = w_uq_pad.reshape(Q_LORA, H_A * HEAD_PAD_A).astype(MXU_DTYPE)
    gq_head = jnp.concatenate([g_qn_a[:QK_NOPE], _rope_pad(g_qn_a[QK_NOPE:], 0)]) * (QK_DIM_A ** -0.5 * LOG2E)
    gq_pad = jnp.tile(gq_head, H_A).reshape(1, H_A * HEAD_PAD_A).astype(F32)
    bound_a = QK_DIM_A * jnp.max(jnp.abs(gq_head)) * jnp.max(jnp.abs(g_kn_a)) * BOUND_MARGIN
    q_shift = jnp.zeros((1, LANE), F32).at[0, SHIFT_LANE - LANE].set(-bound_a)
    w_ukv3 = w_ukv.reshape(KV_LORA, H_A, QK_NOPE + V_DIM_A)
    w_uk = w_ukv3[:, :, :QK_NOPE].reshape(KV_LORA, H_A * QK_NOPE).astype(MXU_DTYPE)
    w_uv = w_ukv3[:, :, QK_NOPE:].reshape(KV_LORA, WIDTH_A).astype(MXU_DTYPE)
    gk_nope = g_kn_a[:QK_NOPE].reshape(1, LANE).astype(F32)
    gk_rope = _rope_pad(g_kn_a[QK_NOPE:], 0).reshape(1, LANE).astype(F32)
    gq_b = (g_qn_b * (HEAD_DIM_B ** -0.5 * LOG2E)).reshape(1, LANE).astype(F32)
    gk_b = g_kn_b.reshape(1, LANE).astype(F32)

    inv = ROPE_THETA ** (-jnp.arange(HALF_ROPE, dtype=F32) / HALF_ROPE)
    ang_off = positions[:, :1, None].astype(F32) * inv
    ang_rel = jnp.arange(l, dtype=F32)[None, :, None] * inv
    cos = (jnp.cos(ang_off) * jnp.cos(ang_rel) - jnp.sin(ang_off) * jnp.sin(ang_rel)).reshape(t, HALF_ROPE)
    sin = (jnp.sin(ang_off) * jnp.cos(ang_rel) + jnp.cos(ang_off) * jnp.sin(ang_rel)).reshape(t, HALF_ROPE)
    zr = jnp.zeros_like(cos)
    cos_t = jnp.concatenate([cos, zr, cos, zr], axis=1)
    sin_t = jnp.concatenate([-sin, zr, sin, zr], axis=1)

    x2 = x.reshape(t, d)
    proj = _in_proj(x2, g_pre, w_pad, tm=1024, tn=2 * tn_in if n_pad % (2 * tn_in) == 0 else tn_in)

    q_a = _qa_proj(proj, off["cq"] // Q_LORA, g_q_lat, w_uq_pad, gq_pad, cos_t, sin_t, q_shift, tm=512)
    k_a, v_a = _kva_proj(proj, off["ckv"] // KV_LORA, off["k_rope"] // LANE, g_kv_lat, w_uk, w_uv,
                         gk_nope, gk_rope, cos_t, sin_t, tm=512)
    o_a = _attn_a(q_a, k_a, v_a, proj, off["gate_a"] // LANE, b=b, l=l, tq=tq, heads=4)

    nq = l // tq
    w_idx = proj[:, off["w_idx"]:off["w_idx"] + IDX_HEADS]
    w_idx_t = w_idx.astype(F32).T
    v_b = proj[:, off["v_b"]:off["v_b"] + HEAD_DIM_B]
    vt = v_b.reshape(b, nq, tq, HEAD_DIM_B).transpose(0, 1, 3, 2)
    blk = {"q_idx": off["q_idx"] // (IDX_HEADS * IDX_DIM), "k_idx": off["k_idx"] // LANE,
           "q_b": off["q_b"] // WIDTH_B, "k_b": off["k_b"] // LANE, "gate_b": off["gate_b"] // WIDTH_B}
    o_b = _attn_b(proj, w_idx_t, vt, t5_bias.astype(F32), gq_b, gk_b, blk, b=b, l=l, tq=tq)

    merged = _merge(o_a, o_b, p_a.astype(MXU_DTYPE), p_b.astype(MXU_DTYPE), proj,
                    off["merge_a"], off["merge_b"], tm=1024, tn=1024)
    out = _out_proj(merged, w_o.astype(MXU_DTYPE), x2, tm=1024, tn=1024)
    return out.reshape(b, l, d)
```

```python
import functools
import math

import jax
import jax.numpy as jnp
from jax import lax
from jax.experimental import pallas as pl
from jax.experimental.pallas import tpu as pltpu

F32 = jnp.float32
I32 = jnp.int32
MXU_DTYPE = jnp.bfloat16

H_A = 16
QK_NOPE = 128
QK_ROPE = 64
QK_DIM_A = QK_NOPE + QK_ROPE
V_DIM_A = 128
Q_LORA = 1024
KV_LORA = 512
ROPE_THETA = 10000.0
H_B = 16
HEAD_DIM_B = 128
IDX_HEADS = 32
IDX_DIM = 64
TOPK_MAX = 256
N_BUCKETS = 32
MAX_DISTANCE = 128
EPS = 1e-6
WIDTH_A = H_A * V_DIM_A
WIDTH_B = H_B * HEAD_DIM_B

LANE = 128
ROW_ALIGN = 32
HALF_ROPE = QK_ROPE // 2
HEAD_PAD_A = 2 * LANE
SHIFT_LANE = LANE + HALF_ROPE
BOUND_MARGIN = 1.0 + 2.0 ** -6
MIN_SHIFTED_SUM = 2.0 ** -64
VMEM_CAP = 56 * 1024 * 1024
VMEM_HEADROOM = 4 * 1024 * 1024

BISECT_STEPS_PER_CHECK = 5
BISECT_MAX_CHECKS = 32
LOG2E = math.log2(math.e)
NEG_INF = float("-inf")
POS_INF = float("inf")


def _nt_dot(a, b):
    return lax.dot_general(a, b, (((1,), (1,)), ((), ())), preferred_element_type=F32)


def _nbytes(shape, dtype):
    return math.prod(shape) * jnp.dtype(dtype).itemsize


def _params(sem, windows, scratch=0, temps=0):
    need = 2 * sum(windows) + scratch + temps + VMEM_HEADROOM
    return pltpu.CompilerParams(dimension_semantics=sem, vmem_limit_bytes=min(need, VMEM_CAP))


def _w_relayout_kernel(start_ref, src_ref, tail_ref, o_ref, *, n_main):
    j = pl.program_id(0)

    @pl.when(j < n_main)
    def _():
        o_ref[...] = src_ref[...].astype(o_ref.dtype)

    @pl.when(j >= n_main)
    def _():
        o_ref[...] = tail_ref[...].astype(o_ref.dtype)


def _w_relayout(wt, src_starts, wt_tail, *, tn, tc):
    d = wt.shape[1]
    n_main, n_tail = len(src_starts), wt_tail.shape[0] // tn
    assert all(s % ROW_ALIGN == 0 for s in src_starts)
    starts = jnp.array([s // ROW_ALIGN for s in src_starts] + [0] * n_tail, I32)
    return pl.pallas_call(
        functools.partial(_w_relayout_kernel, n_main=n_main),
        out_shape=jax.ShapeDtypeStruct(((n_main + n_tail) * tn, d), MXU_DTYPE),
        grid_spec=pltpu.PrefetchScalarGridSpec(
            num_scalar_prefetch=1,
            grid=(n_main + n_tail, d // tc),
            in_specs=[
                pl.BlockSpec((pl.Element(tn), pl.Element(tc)), lambda j, c, st: (st[j] * ROW_ALIGN, c * tc)),
                pl.BlockSpec((tn, tc), lambda j, c, st: (jnp.maximum(j - n_main, 0), c)),
            ],
            out_specs=pl.BlockSpec((tn, tc), lambda j, c, st: (j, c)),
        ),
        compiler_params=_params(("arbitrary", "arbitrary"),
                                [_nbytes((tn, tc), wt.dtype), _nbytes((tn, tc), wt_tail.dtype),
                                 _nbytes((tn, tc), MXU_DTYPE)]),
        name="w_relayout",
    )(starts, wt, wt_tail)


def _in_proj_kernel(x_hbm, g_ref, w_ref, o_ref, x_buf, hn_ref, x_sem, *, row_chunk):
    tm = x_buf.shape[0]
    i, j = pl.program_id(0), pl.program_id(1)

    def x_copy(row_block):
        rows = pl.ds(pl.multiple_of(row_block * tm, tm), tm)
        return pltpu.make_async_copy(x_hbm.at[rows, :], x_buf, x_sem)

    @pl.when((i == 0) & (j == 0))
    def _():
        x_copy(0).start()

    @pl.when(j == 0)
    def _():
        x_copy(i).wait()

        def body(r, carry):
            sl = pl.ds(pl.multiple_of(r * row_chunk, row_chunk), row_chunk)
            xx = x_buf[sl, :]
            ms = jnp.mean(xx * xx, axis=-1, keepdims=True)
            hn_ref[sl, :] = (xx * lax.rsqrt(ms + EPS) * g_ref[...]).astype(hn_ref.dtype)
            return carry

        lax.fori_loop(0, tm // row_chunk, body, 0)

    @pl.when((j == 1) & (i + 1 < pl.num_programs(0)))
    def _():
        x_copy(i + 1).start()

    o_ref[...] = _nt_dot(hn_ref[...], w_ref[...]).astype(o_ref.dtype)


def _in_proj(x2, g_pre, wt_pad, *, tm, tn):
    t, d = x2.shape
    n = wt_pad.shape[0]
    assert n // tn >= 2
    return pl.pallas_call(
        functools.partial(_in_proj_kernel, row_chunk=64),
        out_shape=jax.ShapeDtypeStruct((t, n), MXU_DTYPE),
        grid=(t // tm, n // tn),
        in_specs=[
            pl.BlockSpec(memory_space=pl.ANY),
            pl.BlockSpec((1, d), lambda i, j: (0, 0)),
            pl.BlockSpec((tn, d), lambda i, j: (j, 0)),
        ],
        out_specs=pl.BlockSpec((tm, tn), lambda i, j: (i, j)),
        scratch_shapes=[pltpu.VMEM((tm, d), x2.dtype), pltpu.VMEM((tm, d), MXU_DTYPE),
                        pltpu.SemaphoreType.DMA(())],
        compiler_params=_params(("arbitrary", "arbitrary"),
                                [_nbytes((1, d), F32), _nbytes((tn, d), MXU_DTYPE), _nbytes((tm, tn), MXU_DTYPE)],
                                scratch=_nbytes((tm, d), x2.dtype) + _nbytes((tm, d), MXU_DTYPE),
                                temps=2 * _nbytes((tm, tn), F32)),
        name="in_proj",
    )(x2, g_pre.reshape(1, d), wt_pad)


def _rope_lanes(r, cos_ref, sin_ref):
    return r * cos_ref[...] + pltpu.roll(r, 2 * HALF_ROPE, 1) * sin_ref[...]


def _qa_proj_kernel(cq_ref, gl_ref, w_ref, gq_ref, cos_ref, sin_ref, shift_ref, o_ref):
    c = cq_ref[...].astype(F32)
    ms = jnp.mean(c * c, axis=-1, keepdims=True)
    cn = (c * lax.rsqrt(ms + EPS) * gl_ref[...]).astype(MXU_DTYPE)
    q = jnp.dot(cn, w_ref[...], preferred_element_type=F32)
    for h in range(H_A):
        lo = h * HEAD_PAD_A
        qh = q[:, lo:lo + HEAD_PAD_A]
        ss = jnp.sum(qh * qh, axis=-1, keepdims=True) * (1.0 / QK_DIM_A)
        qn = qh * lax.rsqrt(ss + EPS) * gq_ref[:, lo:lo + HEAD_PAD_A]
        o_ref[:, lo:lo + LANE] = qn[:, :LANE].astype(o_ref.dtype)
        o_ref[:, lo + LANE:lo + HEAD_PAD_A] = (
            _rope_lanes(qn[:, LANE:], cos_ref, sin_ref) + shift_ref[...]).astype(o_ref.dtype)


def _qa_proj(proj, cq_blk, g_q_lat, w_uq_pad, gq_pad, cos_t, sin_t, q_shift, *, tm):
    t = proj.shape[0]
    nq = H_A * HEAD_PAD_A
    return pl.pallas_call(
        _qa_proj_kernel,
        out_shape=jax.ShapeDtypeStruct((t, nq), MXU_DTYPE),
        grid=(t // tm,),
        in_specs=[
            pl.BlockSpec((tm, Q_LORA), lambda i: (i, cq_blk)),
            pl.BlockSpec((1, Q_LORA), lambda i: (0, 0)),
            pl.BlockSpec((Q_LORA, nq), lambda i: (0, 0)),
            pl.BlockSpec((1, nq), lambda i: (0, 0)),
            pl.BlockSpec((tm, LANE), lambda i: (i, 0)),
            pl.BlockSpec((tm, LANE), lambda i: (i, 0)),
            pl.BlockSpec((1, LANE), lambda i: (0, 0)),
        ],
        out_specs=pl.BlockSpec((tm, nq), lambda i: (i, 0)),
        compiler_params=_params(("arbitrary",),
                                [_nbytes((tm, Q_LORA), MXU_DTYPE), _nbytes((Q_LORA, nq), MXU_DTYPE),
                                 _nbytes((1, Q_LORA + nq + LANE), F32), 2 * _nbytes((tm, LANE), F32),
                                 _nbytes((tm, nq), MXU_DTYPE)],
                                temps=_nbytes((tm, nq), F32)),
        name="qa_proj",
    )(proj, g_q_lat.reshape(1, Q_LORA), w_uq_pad, gq_pad, cos_t, sin_t, q_shift)


def _kva_proj_kernel(ckv_ref, kr_ref, gl_ref, wk_ref, wv_ref, gkn_ref, gkr_ref, cos_ref, sin_ref,
                     k_ref, v_ref):
    c = ckv_ref[...].astype(F32)
    ms = jnp.mean(c * c, axis=-1, keepdims=True)
    cn = (c * lax.rsqrt(ms + EPS) * gl_ref[...]).astype(MXU_DTYPE)
    kn = jnp.dot(cn, wk_ref[...], preferred_element_type=F32)
    v_ref[...] = jnp.dot(cn, wv_ref[...], preferred_element_type=F32).astype(v_ref.dtype)
    kr = kr_ref[...].astype(F32)
    ss_r = jnp.sum(kr * kr, axis=-1, keepdims=True)
    krr = _rope_lanes(kr * gkr_ref[...], cos_ref, sin_ref)
    shift_one = (lax.broadcasted_iota(I32, (1, LANE), 1) == SHIFT_LANE - LANE).astype(F32)
    for h in range(H_A):
        kh = kn[:, h * LANE:(h + 1) * LANE]
        ss = (jnp.sum(kh * kh, axis=-1, keepdims=True) + ss_r) * (1.0 / QK_DIM_A)
        rs = lax.rsqrt(ss + EPS)
        lo = h * HEAD_PAD_A
        k_ref[:, lo:lo + LANE] = (kh * rs * gkn_ref[...]).astype(k_ref.dtype)
        k_ref[:, lo + LANE:lo + HEAD_PAD_A] = (krr * rs + shift_one).astype(k_ref.dtype)


def _kva_proj(proj, ckv_blk, krope_blk, g_kv_lat, w_uk, w_uv, gk_nope, gk_rope, cos_t, sin_t, *, tm):
    t = proj.shape[0]
    return pl.pallas_call(
        _kva_proj_kernel,
        out_shape=(jax.ShapeDtypeStruct((t, H_A * HEAD_PAD_A), MXU_DTYPE),
                   jax.ShapeDtypeStruct((t, WIDTH_A), MXU_DTYPE)),
        grid=(t // tm,),
        in_specs=[
            pl.BlockSpec((tm, KV_LORA), lambda i: (i, ckv_blk)),
            pl.BlockSpec((tm, LANE), lambda i: (i, krope_blk)),
            pl.BlockSpec((1, KV_LORA), lambda i: (0, 0)),
            pl.BlockSpec((KV_LORA, H_A * QK_NOPE), lambda i: (0, 0)),
            pl.BlockSpec((KV_LORA, WIDTH_A), lambda i: (0, 0)),
            pl.BlockSpec((1, LANE), lambda i: (0, 0)),
            pl.BlockSpec((1, LANE), lambda i: (0, 0)),
            pl.BlockSpec((tm, LANE), lambda i: (i, 0)),
            pl.BlockSpec((tm, LANE), lambda i: (i, 0)),
        ],
        out_specs=(pl.BlockSpec((tm, H_A * HEAD_PAD_A), lambda i: (i, 0)),
                   pl.BlockSpec((tm, WIDTH_A), lambda i: (i, 0))),
        compiler_params=_params(("arbitrary",),
                                [_nbytes((tm, KV_LORA + LANE), MXU_DTYPE), 2 * _nbytes((KV_LORA, WIDTH_A), MXU_DTYPE),
                                 _nbytes((1, KV_LORA + 2 * LANE), F32), 2 * _nbytes((tm, LANE), F32),
                                 _nbytes((tm, H_A * HEAD_PAD_A + WIDTH_A), MXU_DTYPE)],
                                temps=2 * _nbytes((tm, WIDTH_A), F32)),
        name="kva_proj",
    )(proj, proj, g_kv_lat.reshape(1, KV_LORA), w_uk, w_uv, gk_nope, gk_rope, cos_t, sin_t)


def _silu(g):
    return g * (1.0 / (1.0 + jnp.exp(-g)))


def _lane_tile_reduce(x, op):
    acc = x[:, :LANE]
    for t in range(1, x.shape[1] // LANE):
        acc = op(acc, x[:, t * LANE:(t + 1) * LANE])
    return acc


def _attn_a_kernel(anchor_ref, q_ref, k_ref, v_ref, gate_ref, o_ref, lmin_ref, *, tq, nq, heads):
    causal = lax.broadcasted_iota(I32, (tq, tq), 0) >= lax.broadcasted_iota(I32, (tq, tq), 1)
    kcs = [slice(g * HEAD_PAD_A, (g + 1) * HEAD_PAD_A) for g in range(heads)]
    vcs = [slice(g * V_DIM_A, (g + 1) * V_DIM_A) for g in range(heads)]

    def query_block(qv, exact):
        n_off = qv * tq
        rows = slice(n_off, n_off + tq)
        l_min = lmin_ref[...]
        for g in range(heads):
            q = q_ref[rows, kcs[g]]
            if exact:
                q_hi = q[:, LANE:]
                lane = lax.broadcasted_iota(I32, q_hi.shape, 1)
                q = jnp.concatenate(
                    [q[:, :LANE], jnp.where(lane == SHIFT_LANE - LANE, 0.0, q_hi.astype(F32)).astype(q.dtype)], axis=1)
                s_diag = jnp.where(causal, _nt_dot(q, k_ref[rows, kcs[g]]), NEG_INF)
                m_t = _lane_tile_reduce(s_diag, jnp.maximum)
                if qv > 0:
                    s_off = _nt_dot(q, k_ref[0:n_off, kcs[g]])
                    m_t = jnp.maximum(m_t, _lane_tile_reduce(s_off, jnp.maximum))
                m = jnp.max(m_t, axis=-1, keepdims=True)
                p_diag = jnp.exp2(s_diag - m)
                p_off = jnp.exp2(s_off - m) if qv > 0 else None
            else:
                p_diag = jnp.exp2(jnp.where(causal, _nt_dot(q, k_ref[rows, kcs[g]]), NEG_INF))
                p_off = jnp.exp2(_nt_dot(q, k_ref[0:n_off, kcs[g]])) if qv > 0 else None
            l_t = _lane_tile_reduce(p_diag, jnp.add)
            acc = jnp.dot(p_diag.astype(MXU_DTYPE), v_ref[rows, vcs[g]], preferred_element_type=F32)
            if qv > 0:
                l_t = l_t + _lane_tile_reduce(p_off, jnp.add)
                acc = acc + jnp.dot(p_off.astype(MXU_DTYPE), v_ref[0:n_off, vcs[g]], preferred_element_type=F32)
            l = jnp.sum(l_t, axis=-1, keepdims=True)
            o_ref[rows, vcs[g]] = (acc * (1.0 / l) * _silu(gate_ref[rows, vcs[g]].astype(F32))).astype(o_ref.dtype)
            l_min = jnp.minimum(l_min, l)
        lmin_ref[...] = l_min

    def all_blocks(exact):
        for qv in range(nq):
            pl.when(anchor_ref[0] == 1)(functools.partial(query_block, qv, exact))

    lmin_ref[...] = jnp.full(lmin_ref.shape, POS_INF, F32)
    all_blocks(False)
    shift_ok = jnp.min(lmin_ref[...]) >= MIN_SHIFTED_SUM
    pl.when(jnp.logical_not(shift_ok))(functools.partial(all_blocks, True))


def _attn_a(q_a, k_a, v_a, proj, gate_blk0, *, b, l, tq, heads):
    t = q_a.shape[0]
    nq = l // tq
    kw, vw = heads * HEAD_PAD_A, heads * V_DIM_A
    return pl.pallas_call(
        functools.partial(_attn_a_kernel, tq=tq, nq=nq, heads=heads),
        out_shape=jax.ShapeDtypeStruct((t, WIDTH_A), MXU_DTYPE),
        grid_spec=pltpu.PrefetchScalarGridSpec(
            num_scalar_prefetch=1,
            grid=(b, H_A // heads),
            in_specs=[
                pl.BlockSpec((l, kw), lambda bi, h, a: (bi, h)),
                pl.BlockSpec((l, kw), lambda bi, h, a: (bi, h)),
                pl.BlockSpec((l, vw), lambda bi, h, a: (bi, h)),
                pl.BlockSpec((l, vw), lambda bi, h, a: (bi, gate_blk0 // heads + h)),
            ],
            out_specs=pl.BlockSpec((l, vw), lambda bi, h, a: (bi, h)),
            scratch_shapes=[pltpu.VMEM((tq, 1), F32)],
        ),
        compiler_params=_params(("arbitrary", "arbitrary"),
                                [2 * _nbytes((l, kw), MXU_DTYPE), 3 * _nbytes((l, vw), MXU_DTYPE)],
                                scratch=_nbytes((tq, LANE), F32),
                                temps=heads * (_nbytes((tq, l), F32) + _nbytes((tq, l), MXU_DTYPE))),
        name="attn_a",
    )(jnp.ones((1,), I32), q_a, k_a, v_a, proj)


def _t5_bucket(dist):
    max_exact = N_BUCKETS // 2
    n = jnp.maximum(dist, 0)
    nf = jnp.maximum(n, 1).astype(F32)
    large = max_exact + (jnp.log(nf / max_exact) / math.log(MAX_DISTANCE / max_exact)
                         * (N_BUCKETS - max_exact)).astype(I32)
    large = jnp.minimum(large, N_BUCKETS - 1)
    return jnp.where(n < max_exact, n, large)


def _attn_b_kernel(t5_ref, qidx_ref, kidx_ref, wt_ref, qb_ref, kb_ref, vt_ref, gate_ref, gq_ref, gk_ref,
                   o_ref, sc_ref, qn_ref, acc_ref, m_ref, l_ref, bias_ref, thr_ref, kn_ref, bound_ref,
                   *, tq, nq, topk, max_iters):
    bi = pl.program_id(0)
    qi = pl.program_id(1)
    ck = tq
    shape = (ck, tq)

    @pl.when((bi == 0) & (qi == 0))
    def _():
        s_loc = lax.broadcasted_iota(I32, shape, 0)
        t_loc = lax.broadcasted_iota(I32, shape, 1)
        for near in range(2):
            bucket = _t5_bucket(t_loc - s_loc + (1 - near) * ck)

            def per_head(h, carry, bucket=bucket, near=near):
                far = t5_ref[N_BUCKETS - 1, h]
                tab = jnp.zeros(shape, F32)
                largest = jnp.float32(0.0)
                for bk in range(N_BUCKETS - 1):
                    rel = (t5_ref[bk, h] - far) * LOG2E
                    tab = jnp.where(bucket == bk, rel, tab)
                    largest = jnp.maximum(largest, rel)
                bias_ref[h, near] = tab
                gains = jnp.max(jnp.abs(gq_ref[...])) * jnp.max(jnp.abs(gk_ref[...]))
                bound_ref[h] = HEAD_DIM_B * gains * BOUND_MARGIN + largest
                return carry

            lax.fori_loop(0, H_B, per_head, 0)

    w_all = wt_ref[...] * (IDX_HEADS ** -0.5)

    def score_chunk(c, diag):
        rows = pl.ds(pl.multiple_of(c * ck, ck), ck)
        kx = kidx_ref[rows, 0:IDX_DIM]
        zk = jnp.zeros_like(kx)
        kab = jnp.concatenate([jnp.concatenate([kx, zk], axis=1), jnp.concatenate([zk, kx], axis=1)], axis=0)
        score = jnp.zeros(shape, F32)
        for j in range(IDX_HEADS // 2):
            qp = qidx_ref[:, j * LANE:(j + 1) * LANE]
            logits = jnp.maximum(_nt_dot(kab, qp), 0.0)
            score = score + logits[:ck] * w_all[2 * j:2 * j + 1, :]
            score = score + logits[ck:] * w_all[2 * j + 1:2 * j + 2, :]
        if diag:
            adm = lax.broadcasted_iota(I32, shape, 0) <= lax.broadcasted_iota(I32, shape, 1)
            lo_src = jnp.where(adm, score, POS_INF)
            score = jnp.where(adm, score, NEG_INF)
        else:
            lo_src = score
        sc_ref[c] = score
        return jnp.max(score, axis=0, keepdims=True), jnp.min(lo_src, axis=0, keepdims=True)

    def score_body(c, carry):
        mx, mn = carry
        cmx, cmn = score_chunk(c, False)
        return jnp.maximum(mx, cmx), jnp.minimum(mn, cmn)

    mx, mn = lax.fori_loop(0, qi, score_body,
                           (jnp.full((1, tq), NEG_INF, F32), jnp.full((1, tq), POS_INF, F32)))
    dmx, dmn = score_chunk(qi, True)
    mx = jnp.maximum(mx, dmx)
    mn = jnp.minimum(mn, dmn)

    rep = (8, tq)
    n_adm = qi * tq + lax.broadcasted_iota(I32, rep, 1) + 1
    kp = jnp.minimum(n_adm, topk)
    mx8 = jnp.broadcast_to(mx, rep)
    mn8 = jnp.broadcast_to(mn, rep)

    def bisect(nchunks):
        groups = range(tq // LANE)
        split = lambda a: tuple(a[:, g * LANE:(g + 1) * LANE] for g in groups)
        join = lambda parts: jnp.concatenate(parts, axis=1)
        kp_g, mx_g = split(kp), split(mx8)

        def count_ge(x, g):
            acc = jnp.zeros((8, LANE), I32)
            for c in range(nchunks):
                ge = sc_ref[c, :, g * LANE:(g + 1) * LANE].reshape(ck // 8, 8, LANE) >= x[None]
                acc = acc + jnp.sum(ge.astype(I32), axis=0)
            for shift in (4, 2, 1):
                acc = acc + pltpu.roll(acc, shift, 0)
            return acc

        def bis_cond(st):
            it, lo, hi, mid, cnt_lo = st
            active = [(cnt_lo[g] != kp_g[g]) & (mid[g] > lo[g]) & (mid[g] < hi[g]) for g in groups]
            return jnp.logical_and(it < max_iters, jnp.max(join(active).astype(I32)) > 0)

        def bis_body(st):
            it, lo, hi, mid, cnt_lo = st
            lo, hi, mid, cnt_lo = list(lo), list(hi), list(mid), list(cnt_lo)
            for _ in range(BISECT_STEPS_PER_CHECK):
                for g in groups:
                    cnt = count_ge(mid[g], g)
                    ge = cnt >= kp_g[g]
                    lo[g] = jnp.where(ge, mid[g], lo[g])
                    cnt_lo[g] = jnp.where(ge, cnt, cnt_lo[g])
                    hi[g] = jnp.where(ge, hi[g], mid[g])
                    mid[g] = jnp.where(hi[g] == POS_INF, mx_g[g], lo[g] + 0.5 * (hi[g] - lo[g]))
            return it + 1, tuple(lo), tuple(hi), tuple(mid), tuple(cnt_lo)

        _, lo, hi, _, cnt_lo = lax.while_loop(
            bis_cond, bis_body,
            (jnp.int32(0), split(mn8), split(jnp.full(rep, POS_INF, F32)), split(mx8), split(n_adm)))
        lo, hi, cnt_lo = join(lo), join(hi), join(cnt_lo)
        thr_ref[...] = lo

        tied = cnt_lo > kp

        @pl.when(jnp.max(tied.astype(I32)) > 0)
        def _():
            n_keys = nchunks * ck
            sub = lax.broadcasted_iota(I32, (ck // 8, 8, tq), 0) * 8 + lax.broadcasted_iota(I32, (ck // 8, 8, tq), 1)

            def count_kept(j_last):
                acc = jnp.zeros(rep, I32)
                for c in range(nchunks):
                    s3 = sc_ref[c].reshape(ck // 8, 8, tq)
                    keep = (s3 >= hi[None]) | ((s3 >= lo[None]) & (sub + c * ck <= j_last[None]))
                    acc = acc + jnp.sum(keep.astype(I32), axis=0)
                for shift in (4, 2, 1):
                    acc = acc + pltpu.roll(acc, shift, 0)
                return acc

            def idx_step(_, carry):
                j_lo, j_hi = carry
                j_mid = j_lo + ((j_hi - j_lo) >> 1)
                ok = count_kept(j_mid) >= kp
                return jnp.where(ok, j_lo, j_mid), jnp.where(ok, j_mid, j_hi)

            _, j_hi = lax.fori_loop(0, max(1, (n_keys - 1).bit_length()), idx_step,
                                    (jnp.full(rep, -1, I32), jnp.full(rep, n_keys - 1, I32)))
            j_last = jnp.where(tied, j_hi, n_keys - 1)
            for c in range(nchunks):
                s3 = sc_ref[c].reshape(ck // 8, 8, tq)
                drop = (s3 >= lo[None]) & (s3 < hi[None]) & (sub + c * ck > j_last[None])
                sc_ref[c] = jnp.where(drop, NEG_INF, s3).reshape(ck, tq)

    for qv in range(nq):
        pl.when(qi == qv)(functools.partial(bisect, qv + 1))
    thr = thr_ref[0:1, :]

    @pl.when(qi == 0)
    def _():
        kf = kb_ref[...].astype(F32)
        ms = jnp.mean(kf * kf, axis=-1, keepdims=True)
        kn_ref[...] = (kf * lax.rsqrt(ms + EPS) * gk_ref[...]).astype(kn_ref.dtype)

    for h in range(H_B):
        qh = qb_ref[:, h * LANE:(h + 1) * LANE].astype(F32)
        ms = jnp.mean(qh * qh, axis=-1, keepdims=True)
        qn_ref[h * tq:(h + 1) * tq, :] = (qh * lax.rsqrt(ms + EPS) * gq_ref[...]).astype(qn_ref.dtype)

    def attend_chunk(c, near, exact):
        rows = pl.ds(pl.multiple_of(c * ck, ck), ck)
        vt = vt_ref[c]
        sel = sc_ref[c] >= thr
        s_all = _nt_dot(kn_ref[rows, :], qn_ref[...])
        for h in range(H_B):
            s = s_all[:, h * tq:(h + 1) * tq]
            if near is not None:
                s = s + bias_ref[h, near]
            if exact:
                s = jnp.where(sel, s, NEG_INF)
                m_old = m_ref[h]
                m_new = jnp.maximum(m_old, jnp.max(s, axis=0, keepdims=True))
                m_safe = jnp.where(m_new == NEG_INF, 0.0, m_new)
                p = jnp.exp2(s - m_safe)
                alpha = jnp.exp2(m_old - m_safe)
                l_ref[h] = alpha * l_ref[h] + jnp.sum(p, axis=0, keepdims=True)
                acc_ref[h] = alpha * acc_ref[h] + jnp.dot(vt, p.astype(MXU_DTYPE), preferred_element_type=F32)
                m_ref[h] = m_new
            else:
                p = jnp.exp2(jnp.where(sel, s - bound_ref[h], NEG_INF))
                l_ref[h] = l_ref[h] + jnp.sum(p, axis=0, keepdims=True)
                acc_ref[h] = acc_ref[h] + jnp.dot(vt, p.astype(MXU_DTYPE), preferred_element_type=F32)

    def attend(exact):
        if exact:
            m_ref[...] = jnp.full(m_ref.shape, NEG_INF, F32)
        l_ref[...] = jnp.zeros(l_ref.shape, F32)
        acc_ref[...] = jnp.zeros(acc_ref.shape, F32)

        def far_body(c, carry):
            attend_chunk(c, None, exact)
            return carry

        lax.fori_loop(0, jnp.maximum(qi - 1, 0), far_body, 0)

        @pl.when(qi >= 1)
        def _():
            attend_chunk(qi - 1, 0, exact)

        attend_chunk(qi, 1, exact)

    attend(False)
    l_min = l_ref[0]
    for h in range(1, H_B):
        l_min = jnp.minimum(l_min, l_ref[h])
    shift_ok = jnp.min(l_min) >= MIN_SHIFTED_SUM
    pl.when(jnp.logical_not(shift_ok))(functools.partial(attend, True))

    for h in range(H_B):
        o_t = acc_ref[h] * (1.0 / l_ref[h])
        g = gate_ref[:, h * LANE:(h + 1) * LANE].astype(F32)
        o_ref[:, h * LANE:(h + 1) * LANE] = (o_t.T * _silu(g)).astype(o_ref.dtype)


def _attn_b(proj, wt, vt, t5_bias, gq_b, gk_b, blk, *, b, l, tq):
    t = proj.shape[0]
    nq = l // tq
    topk = min(TOPK_MAX, l // 4)
    row = lambda bi, qi: bi * nq + qi
    return pl.pallas_call(
        functools.partial(_attn_b_kernel, tq=tq, nq=nq, topk=topk, max_iters=BISECT_MAX_CHECKS),
        out_shape=jax.ShapeDtypeStruct((t, WIDTH_B), MXU_DTYPE),
        grid=(b, nq),
        in_specs=[
            pl.BlockSpec(memory_space=pltpu.SMEM),
            pl.BlockSpec((tq, IDX_HEADS * IDX_DIM), lambda bi, qi: (row(bi, qi), blk["q_idx"])),
            pl.BlockSpec((l, LANE), lambda bi, qi: (bi, blk["k_idx"])),
            pl.BlockSpec((IDX_HEADS, tq), lambda bi, qi: (0, row(bi, qi))),
            pl.BlockSpec((tq, WIDTH_B), lambda bi, qi: (row(bi, qi), blk["q_b"])),
            pl.BlockSpec((l, LANE), lambda bi, qi: (bi, blk["k_b"])),
            pl.BlockSpec((None, nq, LANE, tq), lambda bi, qi: (bi, 0, 0, 0)),
            pl.BlockSpec((tq, WIDTH_B), lambda bi, qi: (row(bi, qi), blk["gate_b"])),
            pl.BlockSpec((1, LANE), lambda bi, qi: (0, 0)),
            pl.BlockSpec((1, LANE), lambda bi, qi: (0, 0)),
        ],
        out_specs=pl.BlockSpec((tq, WIDTH_B), lambda bi, qi: (row(bi, qi), 0)),
        scratch_shapes=[
            pltpu.VMEM((nq, tq, tq), F32),
            pltpu.VMEM((H_B * tq, LANE), MXU_DTYPE),
            pltpu.VMEM((H_B, LANE, tq), F32),
            pltpu.VMEM((H_B, 1, tq), F32),
            pltpu.VMEM((H_B, 1, tq), F32),
            pltpu.VMEM((H_B, 2, tq, tq), F32),
            pltpu.VMEM((8, tq), F32),
            pltpu.VMEM((l, LANE), MXU_DTYPE),
            pltpu.SMEM((H_B,), F32),
        ],
        compiler_params=_params(
            ("arbitrary", "arbitrary"),
            [_nbytes((tq, IDX_HEADS * IDX_DIM + 2 * WIDTH_B), MXU_DTYPE), _nbytes((l, 2 * LANE), MXU_DTYPE),
             _nbytes((IDX_HEADS, tq), F32), _nbytes((nq, LANE, tq), MXU_DTYPE), _nbytes((tq, WIDTH_B), MXU_DTYPE)],
            scratch=(_nbytes((nq, tq, tq), F32) + _nbytes((H_B * tq + l, LANE), MXU_DTYPE)
                     + _nbytes((H_B, LANE + 2 * 8, tq), F32) + _nbytes((H_B, 2, tq, tq), F32) + _nbytes((8, tq), F32)),
            temps=2 * _nbytes((tq, H_B * tq), F32)),
        name="attn_b",
    )(t5_bias, proj, proj, wt, proj, proj, vt, proj, gq_b, gk_b)


def _sigmoid(z):
    return 1.0 / (1.0 + jnp.exp(-z))


def _merge_kernel(oa_ref, ob_ref, pa_ref, pb_ref, ma_ref, mb_ref, o_ref):
    a = jnp.dot(oa_ref[...], pa_ref[...], preferred_element_type=F32)
    bb = jnp.dot(ob_ref[...], pb_ref[...], preferred_element_type=F32)
    o_ref[...] = (_sigmoid(ma_ref[...].astype(F32)) * a + _sigmoid(mb_ref[...].astype(F32)) * bb).astype(o_ref.dtype)


def _merge(o_a, o_b, p_a, p_b, proj, ma_off, mb_off, *, tm, tn):
    t = o_a.shape[0]
    d = p_a.shape[1]
    ma0, mb0 = ma_off // tn, mb_off // tn
    return pl.pallas_call(
        _merge_kernel,
        out_shape=jax.ShapeDtypeStruct((t, d), MXU_DTYPE),
        grid=(t // tm, d // tn),
        in_specs=[
            pl.BlockSpec((tm, WIDTH_A), lambda i, j: (i, 0)),
            pl.BlockSpec((tm, WIDTH_B), lambda i, j: (i, 0)),
            pl.BlockSpec((WIDTH_A, tn), lambda i, j: (0, j)),
            pl.BlockSpec((WIDTH_B, tn), lambda i, j: (0, j)),
            pl.BlockSpec((tm, tn), lambda i, j: (i, ma0 + j)),
            pl.BlockSpec((tm, tn), lambda i, j: (i, mb0 + j)),
        ],
        out_specs=pl.BlockSpec((tm, tn), lambda i, j: (i, j)),
        compiler_params=_params(("arbitrary", "arbitrary"),
                                [_nbytes((tm, WIDTH_A + WIDTH_B), MXU_DTYPE), _nbytes((WIDTH_A + WIDTH_B, tn), MXU_DTYPE),
                                 3 * _nbytes((tm, tn), MXU_DTYPE)],
                                temps=2 * _nbytes((tm, tn), F32)),
        name="merge",
    )(o_a, o_b, p_a, p_b, proj, proj)


def _out_proj_kernel(m_ref, w_ref, x_ref, o_ref):
    o_ref[...] = x_ref[...] + jnp.dot(m_ref[...], w_ref[...], preferred_element_type=F32)


def _out_proj(merged, w_o, x2, *, tm, tn):
    t, d = x2.shape
    return pl.pallas_call(
        _out_proj_kernel,
        out_shape=jax.ShapeDtypeStruct((t, d), x2.dtype),
        grid=(t // tm, d // tn),
        in_specs=[
            pl.BlockSpec((tm, d), lambda i, j: (i, 0)),
            pl.BlockSpec((d, tn), lambda i, j: (0, j)),
            pl.BlockSpec((tm, tn), lambda i, j: (i, j)),
        ],
        out_specs=pl.BlockSpec((tm, tn), lambda i, j: (i, j)),
        compiler_params=_params(("arbitrary", "arbitrary"),
                                [_nbytes((tm, d), MXU_DTYPE), _nbytes((d, tn), MXU_DTYPE), 2 * _nbytes((tm, tn), x2.dtype)],
                                temps=_nbytes((tm, tn), F32)),
        name="out_proj",
    )(merged, w_o, x2)


def _rope_pad(a, axis):
    a1, a2 = jnp.split(a, 2, axis=axis)
    z = jnp.zeros_like(a1)
    return jnp.concatenate([a1, z, a2, z], axis=axis)


def _layout(d):
    names = [("merge_a", d), ("merge_b", d), ("gate_a", WIDTH_A), ("gate_b", WIDTH_B),
             ("q_b", WIDTH_B), ("q_idx", IDX_HEADS * IDX_DIM), ("cq", Q_LORA), ("ckv", KV_LORA),
             ("k_idx", LANE), ("k_rope", LANE), ("k_b", LANE), ("v_b", LANE)]
    off, out = 0, {}
    for name, width in names:
        assert off % width == 0, (name, off, width)
        out[name] = off
        off += width
    out["w_idx"] = out["k_idx"] + IDX_DIM
    return out, off


def kernel(x, positions, g_pre, w_in, g_q_lat, g_kv_lat, w_uq, w_ukv, g_qn_a, g_kn_a,
           g_qn_b, g_kn_b, t5_bias, p_a, p_b, w_o):
    b, l, d = x.shape
    t = b * l
    tq = 256
    tn_in = 512
    off, n_used = _layout(d)
    n_pad = -(-n_used // tn_in) * tn_in

    names = ["cq", "ckv", "k_rope", "q_b", "k_b", "v_b", "q_idx", "k_idx", "w_idx", "gate_a", "gate_b",
             "merge_a", "merge_b"]
    sizes = [Q_LORA, KV_LORA, QK_ROPE, WIDTH_B, HEAD_DIM_B, HEAD_DIM_B, IDX_HEADS * IDX_DIM, IDX_DIM,
             IDX_HEADS, WIDTH_A, WIDTH_B, d, d]
    src, acc = {}, 0
    for name, s in zip(names, sizes):
        src[name] = (acc, s)
        acc += s
    main_groups = ["merge_a", "merge_b", "gate_a", "gate_b", "q_b", "q_idx", "cq", "ckv"]
    src_starts, dst = [], 0
    for name in main_groups:
        assert off[name] == dst and src[name][1] % tn_in == 0
        src_starts += [src[name][0] + c for c in range(0, src[name][1], tn_in)]
        dst += src[name][1]
    wt = w_in.T
    rows = lambda name: wt[src[name][0]:src[name][0] + src[name][1]]
    z = lambda n: jnp.zeros((n, d), w_in.dtype)
    assert off["k_idx"] == dst
    wt_tail = jnp.concatenate(
        [rows("k_idx"), rows("w_idx"), z(LANE - IDX_DIM - IDX_HEADS), _rope_pad(rows("k_rope"), 0),
         rows("k_b"), rows("v_b"), z(n_pad - n_used)], axis=0)
    w_pad = _w_relayout(wt, src_starts, wt_tail, tn=tn_in, tc=d)

    w_uq3 = w_uq.reshape(Q_LORA, H_A, QK_DIM_A)
    w_uq_pad = jnp.concatenate([w_uq3[:, :, :QK_NOPE], _rope_pad(w_uq3[:, :, QK_NOPE:], 2)], axis=2)
    w_uq_pad = w_uq_pad.reshape(Q_LORA, H_A * HEAD_PAD_A).astype(MXU_DTYPE)
    gq_head = jnp.concatenate([g_qn_a[:QK_NOPE], _rope_pad(g_qn_a[QK_NOPE:], 0)]) * (QK_DIM_A ** -0.5 * LOG2E)
    gq_pad = jnp.tile(gq_head, H_A).reshape(1, H_A * HEAD_PAD_A).astype(F32)
    bound_a = QK_DIM_A * jnp.max(jnp.abs(gq_head)) * jnp.max(jnp.abs(g_kn_a)) * BOUND_MARGIN
    q_shift = jnp.zeros((1, LANE), F32).at[0, SHIFT_LANE - LANE].set(-bound_a)
    w_ukv3 = w_ukv.reshape(KV_LORA, H_A, QK_NOPE + V_DIM_A)
    w_uk = w_ukv3[:, :, :QK_NOPE].reshape(KV_LORA, H_A * QK_NOPE).astype(MXU_DTYPE)
    w_uv = w_ukv3[:, :, QK_NOPE:].reshape(KV_LORA, WIDTH_A).astype(MXU_DTYPE)
    gk_nope = g_kn_a[:QK_NOPE].reshape(1, LANE).astype(F32)
    gk_rope = _rope_pad(g_kn_a[QK_NOPE:], 0).reshape(1, LANE).astype(F32)
    gq_b = (g_qn_b * (HEAD_DIM_B ** -0.5 * LOG2E)).reshape(1, LANE).astype(F32)
    gk_b = g_kn_b.reshape(1, LANE).astype(F32)

    inv = ROPE_THETA ** (-jnp.arange(HALF_ROPE, dtype=F32) / HALF_ROPE)
    ang_off = positions[:, :1, None].astype(F32) * inv
    ang_rel = jnp.arange(l, dtype=F32)[None, :, None] * inv
    cos = (jnp.cos(ang_off) * jnp.cos(ang_rel) - jnp.sin(ang_off) * jnp.sin(ang_rel)).reshape(t, HALF_ROPE)
    sin = (jnp.sin(ang_off) * jnp.cos(ang_rel) + jnp.cos(ang_off) * jnp.sin(ang_rel)).reshape(t, HALF_ROPE)
    zr = jnp.zeros_like(cos)
    cos_t = jnp.concatenate([cos, zr, cos, zr], axis=1)
    sin_t = jnp.concatenate([-sin, zr, sin, zr], axis=1)

    x2 = x.reshape(t, d)
    proj = _in_proj(x2, g_pre, w_pad, tm=1024, tn=2 * tn_in if n_pad % (2 * tn_in) == 0 else tn_in)

    q_a = _qa_proj(proj, off["cq"] // Q_LORA, g_q_lat, w_uq_pad, gq_pad, cos_t, sin_t, q_shift, tm=512)
    k_a, v_a = _kva_proj(proj, off["ckv"] // KV_LORA, off["k_rope"] // LANE, g_kv_lat, w_uk, w_uv,
                         gk_nope, gk_rope, cos_t, sin_t, tm=512)
    o_a = _attn_a(q_a, k_a, v_a, proj, off["gate_a"] // LANE, b=b, l=l, tq=tq, heads=4)

    nq = l // tq
    w_idx = proj[:, off["w_idx"]:off["w_idx"] + IDX_HEADS]
    w_idx_t = w_idx.astype(F32).T
    v_b = proj[:, off["v_b"]:off["v_b"] + HEAD_DIM_B]
    vt = v_b.reshape(b, nq, tq, HEAD_DIM_B).transpose(0, 1, 3, 2)
    blk = {"q_idx": off["q_idx"] // (IDX_HEADS * IDX_DIM), "k_idx": off["k_idx"] // LANE,
           "q_b": off["q_b"] // WIDTH_B, "k_b": off["k_b"] // LANE, "gate_b": off["gate_b"] // WIDTH_B}
    o_b = _attn_b(proj, w_idx_t, vt, t5_bias.astype(F32), gq_b, gk_b, blk, b=b, l=l, tq=tq)

    merged = _merge(o_a, o_b, p_a.astype(MXU_DTYPE), p_b.astype(MXU_DTYPE), proj,
                    off["merge_a"], off["merge_b"], tm=1024, tn=1024)
    out = _out_proj(merged, w_o.astype(MXU_DTYPE), x2, tm=1024, tn=1024)
    return out.reshape(b, l, d)
```

```python
import functools
import math

import jax
import jax.numpy as jnp
from jax import lax
from jax.experimental import pallas as pl
from jax.experimental.pallas import tpu as pltpu

F32 = jnp.float32
I32 = jnp.int32
MXU_DTYPE = jnp.bfloat16

H_A = 16
QK_NOPE = 128
QK_ROPE = 64
QK_DIM_A = QK_NOPE + QK_ROPE
V_DIM_A = 128
Q_LORA = 1024
KV_LORA = 512
ROPE_THETA = 10000.0
H_B = 16
HEAD_DIM_B = 128
IDX_HEADS = 32
IDX_DIM = 64
TOPK_MAX = 256
N_BUCKETS = 32
MAX_DISTANCE = 128
EPS = 1e-6
WIDTH_A = H_A * V_DIM_A
WIDTH_B = H_B * HEAD_DIM_B

LANE = 128
ROW_ALIGN = 32
HALF_ROPE = QK_ROPE // 2
HEAD_PAD_A = 2 * LANE
SHIFT_LANE = LANE + HALF_ROPE
BOUND_MARGIN = 1.0 + 2.0 ** -6
MIN_SHIFTED_SUM = 2.0 ** -64
VMEM_CAP = 56 * 1024 * 1024
VMEM_HEADROOM = 4 * 1024 * 1024

BISECT_STEPS_PER_CHECK = 5
BISECT_MAX_CHECKS = 32
LOG2E = math.log2(math.e)
NEG_INF = float("-inf")
POS_INF = float("inf")


def _nt_dot(a, b):
    return lax.dot_general(a, b, (((1,), (1,)), ((), ())), preferred_element_type=F32)


def _nbytes(shape, dtype):
    return math.prod(shape) * jnp.dtype(dtype).itemsize


def _params(sem, windows, scratch=0, temps=0):
    need = 2 * sum(windows) + scratch + temps + VMEM_HEADROOM
    return pltpu.CompilerParams(dimension_semantics=sem, vmem_limit_bytes=min(need, VMEM_CAP))


def _w_relayout_kernel(start_ref, src_ref, tail_ref, o_ref, *, n_main):
    j = pl.program_id(0)

    @pl.when(j < n_main)
    def _():
        o_ref[...] = src_ref[...].astype(o_ref.dtype)

    @pl.when(j >= n_main)
    def _():
        o_ref[...] = tail_ref[...].astype(o_ref.dtype)


def _w_relayout(wt, src_starts, wt_tail, *, tn, tc):
    d = wt.shape[1]
    n_main, n_tail = len(src_starts), wt_tail.shape[0] // tn
    assert all(s % ROW_ALIGN == 0 for s in src_starts)
    starts = jnp.array([s // ROW_ALIGN for s in src_starts] + [0] * n_tail, I32)
    return pl.pallas_call(
        functools.partial(_w_relayout_kernel, n_main=n_main),
        out_shape=jax.ShapeDtypeStruct(((n_main + n_tail) * tn, d), MXU_DTYPE),
        grid_spec=pltpu.PrefetchScalarGridSpec(
            num_scalar_prefetch=1,
            grid=(n_main + n_tail, d // tc),
            in_specs=[
                pl.BlockSpec((pl.Element(tn), pl.Element(tc)), lambda j, c, st: (st[j] * ROW_ALIGN, c * tc)),
                pl.BlockSpec((tn, tc), lambda j, c, st: (jnp.maximum(j - n_main, 0), c)),
            ],
            out_specs=pl.BlockSpec((tn, tc), lambda j, c, st: (j, c)),
        ),
        compiler_params=_params(("arbitrary", "arbitrary"),
                                [_nbytes((tn, tc), wt.dtype), _nbytes((tn, tc), wt_tail.dtype),
                                 _nbytes((tn, tc), MXU_DTYPE)]),
        name="w_relayout",
    )(starts, wt, wt_tail)


def _in_proj_kernel(x_hbm, g_ref, w_ref, o_ref, x_buf, hn_ref, x_sem, *, row_chunk):
    tm = x_buf.shape[0]
    i, j = pl.program_id(0), pl.program_id(1)

    def x_copy(row_block):
        rows = pl.ds(pl.multiple_of(row_block * tm, tm), tm)
        return pltpu.make_async_copy(x_hbm.at[rows, :], x_buf, x_sem)

    @pl.when((i == 0) & (j == 0))
    def _():
        x_copy(0).start()

    @pl.when(j == 0)
    def _():
        x_copy(i).wait()

        def body(r, carry):
            sl = pl.ds(pl.multiple_of(r * row_chunk, row_chunk), row_chunk)
            xx = x_buf[sl, :]
            ms = jnp.mean(xx * xx, axis=-1, keepdims=True)
            hn_ref[sl, :] = (xx * lax.rsqrt(ms + EPS) * g_ref[...]).astype(hn_ref.dtype)
            return carry

        lax.fori_loop(0, tm // row_chunk, body, 0)

    o_ref[...] = _nt_dot(hn_ref[...], w_ref[...]).astype(o_ref.dtype)

    @pl.when((j == 1) & (i + 1 < pl.num_programs(0)))
    def _():
        x_copy(i + 1).start()


def _in_proj(x2, g_pre, wt_pad, *, tm, tn):
    t, d = x2.shape
    n = wt_pad.shape[0]
    assert n // tn >= 2
    return pl.pallas_call(
        functools.partial(_in_proj_kernel, row_chunk=64),
        out_shape=jax.ShapeDtypeStruct((t, n), MXU_DTYPE),
        grid=(t // tm, n // tn),
        in_specs=[
            pl.BlockSpec(memory_space=pl.ANY),
            pl.BlockSpec((1, d), lambda i, j: (0, 0)),
            pl.BlockSpec((tn, d), lambda i, j: (j, 0)),
        ],
        out_specs=pl.BlockSpec((tm, tn), lambda i, j: (i, j)),
        scratch_shapes=[pltpu.VMEM((tm, d), x2.dtype), pltpu.VMEM((tm, d), MXU_DTYPE),
                        pltpu.SemaphoreType.DMA(())],
        compiler_params=_params(("arbitrary", "arbitrary"),
                                [_nbytes((1, d), F32), _nbytes((tn, d), MXU_DTYPE), _nbytes((tm, tn), MXU_DTYPE)],
                                scratch=_nbytes((tm, d), x2.dtype) + _nbytes((tm, d), MXU_DTYPE),
                                temps=2 * _nbytes((tm, tn), F32)),
        name="in_proj",
    )(x2, g_pre.reshape(1, d), wt_pad)


def _rope_lanes(r, cos_ref, sin_ref):
    return r * cos_ref[...] + pltpu.roll(r, 2 * HALF_ROPE, 1) * sin_ref[...]


def _qa_proj_kernel(cq_ref, gl_ref, w_ref, gq_ref, cos_ref, sin_ref, shift_ref, o_ref):
    c = cq_ref[...].astype(F32)
    ms = jnp.mean(c * c, axis=-1, keepdims=True)
    cn = (c * lax.rsqrt(ms + EPS) * gl_ref[...]).astype(MXU_DTYPE)
    q = jnp.dot(cn, w_ref[...], preferred_element_type=F32)
    for h in range(H_A):
        lo = h * HEAD_PAD_A
        qh = q[:, lo:lo + HEAD_PAD_A]
        ss = jnp.sum(qh * qh, axis=-1, keepdims=True) * (1.0 / QK_DIM_A)
        qn = qh * lax.rsqrt(ss + EPS) * gq_ref[:, lo:lo + HEAD_PAD_A]
        o_ref[:, lo:lo + LANE] = qn[:, :LANE].astype(o_ref.dtype)
        o_ref[:, lo + LANE:lo + HEAD_PAD_A] = (
            _rope_lanes(qn[:, LANE:], cos_ref, sin_ref) + shift_ref[...]).astype(o_ref.dtype)


def _qa_proj(proj, cq_blk, g_q_lat, w_uq_pad, gq_pad, cos_t, sin_t, q_shift, *, tm):
    t = proj.shape[0]
    nq = H_A * HEAD_PAD_A
    return pl.pallas_call(
        _qa_proj_kernel,
        out_shape=jax.ShapeDtypeStruct((t, nq), MXU_DTYPE),
        grid=(t // tm,),
        in_specs=[
            pl.BlockSpec((tm, Q_LORA), lambda i: (i, cq_blk)),
            pl.BlockSpec((1, Q_LORA), lambda i: (0, 0)),
            pl.BlockSpec((Q_LORA, nq), lambda i: (0, 0)),
            pl.BlockSpec((1, nq), lambda i: (0, 0)),
            pl.BlockSpec((tm, LANE), lambda i: (i, 0)),
            pl.BlockSpec((tm, LANE), lambda i: (i, 0)),
            pl.BlockSpec((1, LANE), lambda i: (0, 0)),
        ],
        out_specs=pl.BlockSpec((tm, nq), lambda i: (i, 0)),
        compiler_params=_params(("arbitrary",),
                                [_nbytes((tm, Q_LORA), MXU_DTYPE), _nbytes((Q_LORA, nq), MXU_DTYPE),
                                 _nbytes((1, Q_LORA + nq + LANE), F32), 2 * _nbytes((tm, LANE), F32),
                                 _nbytes((tm, nq), MXU_DTYPE)],
                                temps=_nbytes((tm, nq), F32)),
        name="qa_proj",
    )(proj, g_q_lat.reshape(1, Q_LORA), w_uq_pad, gq_pad, cos_t, sin_t, q_shift)


def _kva_proj_kernel(ckv_ref, kr_ref, gl_ref, wk_ref, wv_ref, gkn_ref, gkr_ref, cos_ref, sin_ref,
                     k_ref, v_ref):
    c = ckv_ref[...].astype(F32)
    ms = jnp.mean(c * c, axis=-1, keepdims=True)
    cn = (c * lax.rsqrt(ms + EPS) * gl_ref[...]).astype(MXU_DTYPE)
    kn = jnp.dot(cn, wk_ref[...], preferred_element_type=F32)
    v_ref[...] = jnp.dot(cn, wv_ref[...], preferred_element_type=F32).astype(v_ref.dtype)
    kr = kr_ref[...].astype(F32)
    ss_r = jnp.sum(kr * kr, axis=-1, keepdims=True)
    krr = _rope_lanes(kr * gkr_ref[...], cos_ref, sin_ref)
    shift_one = (lax.broadcasted_iota(I32, (1, LANE), 1) == SHIFT_LANE - LANE).astype(F32)
    for h in range(H_A):
        kh = kn[:, h * LANE:(h + 1) * LANE]
        ss = (jnp.sum(kh * kh, axis=-1, keepdims=True) + ss_r) * (1.0 / QK_DIM_A)
        rs = lax.rsqrt(ss + EPS)
        lo = h * HEAD_PAD_A
        k_ref[:, lo:lo + LANE] = (kh * rs * gkn_ref[...]).astype(k_ref.dtype)
        k_ref[:, lo + LANE:lo + HEAD_PAD_A] = (krr * rs + shift_one).astype(k_ref.dtype)


def _kva_proj(proj, ckv_blk, krope_blk, g_kv_lat, w_uk, w_uv, gk_nope, gk_rope, cos_t, sin_t, *, tm):
    t = proj.shape[0]
    return pl.pallas_call(
        _kva_proj_kernel,
        out_shape=(jax.ShapeDtypeStruct((t, H_A * HEAD_PAD_A), MXU_DTYPE),
                   jax.ShapeDtypeStruct((t, WIDTH_A), MXU_DTYPE)),
        grid=(t // tm,),
        in_specs=[
            pl.BlockSpec((tm, KV_LORA), lambda i: (i, ckv_blk)),
            pl.BlockSpec((tm, LANE), lambda i: (i, krope_blk)),
            pl.BlockSpec((1, KV_LORA), lambda i: (0, 0)),
            pl.BlockSpec((KV_LORA, H_A * QK_NOPE), lambda i: (0, 0)),
            pl.BlockSpec((KV_LORA, WIDTH_A), lambda i: (0, 0)),
            pl.BlockSpec((1, LANE), lambda i: (0, 0)),
            pl.BlockSpec((1, LANE), lambda i: (0, 0)),
            pl.BlockSpec((tm, LANE), lambda i: (i, 0)),
            pl.BlockSpec((tm, LANE), lambda i: (i, 0)),
        ],
        out_specs=(pl.BlockSpec((tm, H_A * HEAD_PAD_A), lambda i: (i, 0)),
                   pl.BlockSpec((tm, WIDTH_A), lambda i: (i, 0))),
        compiler_params=_params(("arbitrary",),
                                [_nbytes((tm, KV_LORA + LANE), MXU_DTYPE), 2 * _nbytes((KV_LORA, WIDTH_A), MXU_DTYPE),
                                 _nbytes((1, KV_LORA + 2 * LANE), F32), 2 * _nbytes((tm, LANE), F32),
                                 _nbytes((tm, H_A * HEAD_PAD_A + WIDTH_A), MXU_DTYPE)],
                                temps=2 * _nbytes((tm, WIDTH_A), F32)),
        name="kva_proj",
    )(proj, proj, g_kv_lat.reshape(1, KV_LORA), w_uk, w_uv, gk_nope, gk_rope, cos_t, sin_t)


def _silu(g):
    return g * (1.0 / (1.0 + jnp.exp(-g)))


def _lane_tile_reduce(x, op):
    acc = x[:, :LANE]
    for t in range(1, x.shape[1] // LANE):
        acc = op(acc, x[:, t * LANE:(t + 1) * LANE])
    return acc


def _attn_a_kernel(q_ref, k_ref, v_ref, gate_ref, o_ref, *, tq, nq, heads):
    causal = lax.broadcasted_iota(I32, (tq, tq), 0) >= lax.broadcasted_iota(I32, (tq, tq), 1)

    def query_block(qv):
        n_off = qv * tq
        rows = slice(n_off, n_off + tq)

        def finish(g, l_t, acc):
            vc = slice(g * V_DIM_A, (g + 1) * V_DIM_A)
            l = jnp.sum(l_t, axis=-1, keepdims=True)
            o = acc * (1.0 / l)
            o_ref[rows, vc] = (o * _silu(gate_ref[rows, vc].astype(F32))).astype(o_ref.dtype)

        kcs = [slice(g * HEAD_PAD_A, (g + 1) * HEAD_PAD_A) for g in range(heads)]
        vcs = [slice(g * V_DIM_A, (g + 1) * V_DIM_A) for g in range(heads)]

        l_min = jnp.full((tq, 1), POS_INF, F32)
        for g in range(heads):
            q = q_ref[rows, kcs[g]]
            p_diag = jnp.exp2(jnp.where(causal, _nt_dot(q, k_ref[n_off:n_off + tq, kcs[g]]), NEG_INF))
            l_t = _lane_tile_reduce(p_diag, jnp.add)
            acc = jnp.dot(p_diag.astype(MXU_DTYPE), v_ref[n_off:n_off + tq, vcs[g]], preferred_element_type=F32)
            if qv > 0:
                p_off = jnp.exp2(_nt_dot(q, k_ref[0:n_off, kcs[g]]))
                l_t = l_t + _lane_tile_reduce(p_off, jnp.add)
                acc = acc + jnp.dot(p_off.astype(MXU_DTYPE), v_ref[0:n_off, vcs[g]], preferred_element_type=F32)
            l_min = jnp.minimum(l_min, jnp.sum(l_t, axis=-1, keepdims=True))
            finish(g, l_t, acc)

        shift_ok = jnp.min(l_min) >= MIN_SHIFTED_SUM

        @pl.when(jnp.logical_not(shift_ok))
        def _():
            for g in range(heads):
                q = q_ref[rows, kcs[g]]
                q_hi = q[:, LANE:]
                lane = lax.broadcasted_iota(I32, q_hi.shape, 1)
                q = jnp.concatenate(
                    [q[:, :LANE], jnp.where(lane == SHIFT_LANE - LANE, 0.0, q_hi.astype(F32)).astype(q.dtype)], axis=1)
                s_diag = jnp.where(causal, _nt_dot(q, k_ref[n_off:n_off + tq, kcs[g]]), NEG_INF)
                m_t = _lane_tile_reduce(s_diag, jnp.maximum)
                if qv > 0:
                    s_off = _nt_dot(q, k_ref[0:n_off, kcs[g]])
                    m_t = jnp.maximum(m_t, _lane_tile_reduce(s_off, jnp.maximum))
                m = jnp.max(m_t, axis=-1, keepdims=True)
                p_diag = jnp.exp2(s_diag - m)
                l_t = _lane_tile_reduce(p_diag, jnp.add)
                acc = jnp.dot(p_diag.astype(MXU_DTYPE), v_ref[n_off:n_off + tq, vcs[g]],
                              preferred_element_type=F32)
                if qv > 0:
                    p_off = jnp.exp2(s_off - m)
                    l_t = l_t + _lane_tile_reduce(p_off, jnp.add)
                    acc = acc + jnp.dot(p_off.astype(MXU_DTYPE), v_ref[0:n_off, vcs[g]], preferred_element_type=F32)
                finish(g, l_t, acc)

    for qv in range(nq):
        query_block(qv)


def _attn_a(q_a, k_a, v_a, proj, gate_blk0, *, b, l, tq, heads):
    t = q_a.shape[0]
    nq = l // tq
    kw, vw = heads * HEAD_PAD_A, heads * V_DIM_A
    return pl.pallas_call(
        functools.partial(_attn_a_kernel, tq=tq, nq=nq, heads=heads),
        out_shape=jax.ShapeDtypeStruct((t, WIDTH_A), MXU_DTYPE),
        grid=(b, H_A // heads),
        in_specs=[
            pl.BlockSpec((l, kw), lambda bi, h: (bi, h)),
            pl.BlockSpec((l, kw), lambda bi, h: (bi, h)),
            pl.BlockSpec((l, vw), lambda bi, h: (bi, h)),
            pl.BlockSpec((l, vw), lambda bi, h: (bi, gate_blk0 // heads + h)),
        ],
        out_specs=pl.BlockSpec((l, vw), lambda bi, h: (bi, h)),
        compiler_params=_params(("arbitrary", "arbitrary"),
                                [2 * _nbytes((l, kw), MXU_DTYPE), 3 * _nbytes((l, vw), MXU_DTYPE)],
                                temps=heads * (_nbytes((tq, l), F32) + _nbytes((tq, l), MXU_DTYPE))),
        name="attn_a",
    )(q_a, k_a, v_a, proj)


def _t5_bucket(dist):
    max_exact = N_BUCKETS // 2
    n = jnp.maximum(dist, 0)
    nf = jnp.maximum(n, 1).astype(F32)
    large = max_exact + (jnp.log(nf / max_exact) / math.log(MAX_DISTANCE / max_exact)
                         * (N_BUCKETS - max_exact)).astype(I32)
    large = jnp.minimum(large, N_BUCKETS - 1)
    return jnp.where(n < max_exact, n, large)


def _attn_b_kernel(t5_ref, qidx_ref, kidx_ref, wt_ref, qb_ref, kb_ref, vt_ref, gate_ref, gq_ref, gk_ref,
                   o_ref, sc_ref, qn_ref, acc_ref, m_ref, l_ref, bias_ref, thr_ref, kn_ref, bound_ref,
                   *, tq, nq, topk, max_iters):
    bi = pl.program_id(0)
    qi = pl.program_id(1)
    ck = tq
    shape = (ck, tq)

    @pl.when((bi == 0) & (qi == 0))
    def _():
        s_loc = lax.broadcasted_iota(I32, shape, 0)
        t_loc = lax.broadcasted_iota(I32, shape, 1)
        for near in range(2):
            bucket = _t5_bucket(t_loc - s_loc + (1 - near) * ck)

            def per_head(h, carry, bucket=bucket, near=near):
                far = t5_ref[N_BUCKETS - 1, h]
                tab = jnp.zeros(shape, F32)
                largest = jnp.float32(0.0)
                for bk in range(N_BUCKETS - 1):
                    rel = (t5_ref[bk, h] - far) * LOG2E
                    tab = jnp.where(bucket == bk, rel, tab)
                    largest = jnp.maximum(largest, rel)
                bias_ref[h, near] = tab
                gains = jnp.max(jnp.abs(gq_ref[...])) * jnp.max(jnp.abs(gk_ref[...]))
                bound_ref[h] = HEAD_DIM_B * gains * BOUND_MARGIN + largest
                return carry

            lax.fori_loop(0, H_B, per_head, 0)

    w_all = wt_ref[...] * (IDX_HEADS ** -0.5)

    def score_chunk(c, diag):
        rows = pl.ds(pl.multiple_of(c * ck, ck), ck)
        kx = kidx_ref[rows, 0:IDX_DIM]
        zk = jnp.zeros_like(kx)
        kab = jnp.concatenate([jnp.concatenate([kx, zk], axis=1), jnp.concatenate([zk, kx], axis=1)], axis=0)
        score = jnp.zeros(shape, F32)
        for j in range(IDX_HEADS // 2):
            qp = qidx_ref[:, j * LANE:(j + 1) * LANE]
            logits = jnp.maximum(_nt_dot(kab, qp), 0.0)
            score = score + logits[:ck] * w_all[2 * j:2 * j + 1, :]
            score = score + logits[ck:] * w_all[2 * j + 1:2 * j + 2, :]
        if diag:
            adm = lax.broadcasted_iota(I32, shape, 0) <= lax.broadcasted_iota(I32, shape, 1)
            lo_src = jnp.where(adm, score, POS_INF)
            score = jnp.where(adm, score, NEG_INF)
        else:
            lo_src = score
        sc_ref[c] = score
        return jnp.max(score, axis=0, keepdims=True), jnp.min(lo_src, axis=0, keepdims=True)

    def score_body(c, carry):
        mx, mn = carry
        cmx, cmn = score_chunk(c, False)
        return jnp.maximum(mx, cmx), jnp.minimum(mn, cmn)

    mx, mn = lax.fori_loop(0, qi, score_body,
                           (jnp.full((1, tq), NEG_INF, F32), jnp.full((1, tq), POS_INF, F32)))
    dmx, dmn = score_chunk(qi, True)
    mx = jnp.maximum(mx, dmx)
    mn = jnp.minimum(mn, dmn)

    rep = (8, tq)
    n_adm = qi * tq + lax.broadcasted_iota(I32, rep, 1) + 1
    kp = jnp.minimum(n_adm, topk)
    mx8 = jnp.broadcast_to(mx, rep)
    mn8 = jnp.broadcast_to(mn, rep)

    def bisect(nchunks):
        groups = range(tq // LANE)
        split = lambda a: tuple(a[:, g * LANE:(g + 1) * LANE] for g in groups)
        join = lambda parts: jnp.concatenate(parts, axis=1)
        kp_g, mx_g = split(kp), split(mx8)

        def count_ge(x, g):
            acc = jnp.zeros((8, LANE), I32)
            for c in range(nchunks):
                ge = sc_ref[c, :, g * LANE:(g + 1) * LANE].reshape(ck // 8, 8, LANE) >= x[None]
                acc = acc + jnp.sum(ge.astype(I32), axis=0)
            for shift in (4, 2, 1):
                acc = acc + pltpu.roll(acc, shift, 0)
            return acc

        def bis_cond(st):
            it, lo, hi, mid, cnt_lo = st
            active = [(cnt_lo[g] != kp_g[g]) & (mid[g] > lo[g]) & (mid[g] < hi[g]) for g in groups]
            return jnp.logical_and(it < max_iters, jnp.max(join(active).astype(I32)) > 0)

        def bis_body(st):
            it, lo, hi, mid, cnt_lo = st
            lo, hi, mid, cnt_lo = list(lo), list(hi), list(mid), list(cnt_lo)
            for _ in range(BISECT_STEPS_PER_CHECK):
                for g in groups:
                    cnt = count_ge(mid[g], g)
                    ge = cnt >= kp_g[g]
                    lo[g] = jnp.where(ge, mid[g], lo[g])
                    cnt_lo[g] = jnp.where(ge, cnt, cnt_lo[g])
                    hi[g] = jnp.where(ge, hi[g], mid[g])
                    mid[g] = jnp.where(hi[g] == POS_INF, mx_g[g], lo[g] + 0.5 * (hi[g] - lo[g]))
            return it + 1, tuple(lo), tuple(hi), tuple(mid), tuple(cnt_lo)

        _, lo, hi, _, cnt_lo = lax.while_loop(
            bis_cond, bis_body,
            (jnp.int32(0), split(mn8), split(jnp.full(rep, POS_INF, F32)), split(mx8), split(n_adm)))
        lo, hi, cnt_lo = join(lo), join(hi), join(cnt_lo)
        thr_ref[...] = lo

        tied = cnt_lo > kp

        @pl.when(jnp.max(tied.astype(I32)) > 0)
        def _():
            n_keys = nchunks * ck
            sub = lax.broadcasted_iota(I32, (ck // 8, 8, tq), 0) * 8 + lax.broadcasted_iota(I32, (ck // 8, 8, tq), 1)

            def count_kept(j_last):
                acc = jnp.zeros(rep, I32)
                for c in range(nchunks):
                    s3 = sc_ref[c].reshape(ck // 8, 8, tq)
                    keep = (s3 >= hi[None]) | ((s3 >= lo[None]) & (sub + c * ck <= j_last[None]))
                    acc = acc + jnp.sum(keep.astype(I32), axis=0)
                for shift in (4, 2, 1):
                    acc = acc + pltpu.roll(acc, shift, 0)
                return acc

            def idx_step(_, carry):
                j_lo, j_hi = carry
                j_mid = j_lo + ((j_hi - j_lo) >> 1)
                ok = count_kept(j_mid) >= kp
                return jnp.where(ok, j_lo, j_mid), jnp.where(ok, j_mid, j_hi)

            _, j_hi = lax.fori_loop(0, max(1, (n_keys - 1).bit_length()), idx_step,
                                    (jnp.full(rep, -1, I32), jnp.full(rep, n_keys - 1, I32)))
            j_last = jnp.where(tied, j_hi, n_keys - 1)
            for c in range(nchunks):
                s3 = sc_ref[c].reshape(ck // 8, 8, tq)
                drop = (s3 >= lo[None]) & (s3 < hi[None]) & (sub + c * ck > j_last[None])
                sc_ref[c] = jnp.where(drop, NEG_INF, s3).reshape(ck, tq)

    for qv in range(nq):
        pl.when(qi == qv)(functools.partial(bisect, qv + 1))
    thr = thr_ref[0:1, :]

    @pl.when(qi == 0)
    def _():
        kf = kb_ref[...].astype(F32)
        ms = jnp.mean(kf * kf, axis=-1, keepdims=True)
        kn_ref[...] = (kf * lax.rsqrt(ms + EPS) * gk_ref[...]).astype(kn_ref.dtype)

    for h in range(H_B):
        qh = qb_ref[:, h * LANE:(h + 1) * LANE].astype(F32)
        ms = jnp.mean(qh * qh, axis=-1, keepdims=True)
        qn_ref[h * tq:(h + 1) * tq, :] = (qh * lax.rsqrt(ms + EPS) * gq_ref[...]).astype(qn_ref.dtype)

    def attend_chunk(c, near, exact):
        rows = pl.ds(pl.multiple_of(c * ck, ck), ck)
        vt = vt_ref[c]
        sel = sc_ref[c] >= thr
        s_all = _nt_dot(kn_ref[rows, :], qn_ref[...])
        for h in range(H_B):
            s = s_all[:, h * tq:(h + 1) * tq]
            if near is not None:
                s = s + bias_ref[h, near]
            if exact:
                s = jnp.where(sel, s, NEG_INF)
                m_old = m_ref[h]
                m_new = jnp.maximum(m_old, jnp.max(s, axis=0, keepdims=True))
                m_safe = jnp.where(m_new == NEG_INF, 0.0, m_new)
                p = jnp.exp2(s - m_safe)
                alpha = jnp.exp2(m_old - m_safe)
                l_ref[h] = alpha * l_ref[h] + jnp.sum(p, axis=0, keepdims=True)
                acc_ref[h] = alpha * acc_ref[h] + jnp.dot(vt, p.astype(MXU_DTYPE), preferred_element_type=F32)
                m_ref[h] = m_new
            else:
                p = jnp.exp2(jnp.where(sel, s - bound_ref[h], NEG_INF))
                l_ref[h] = l_ref[h] + jnp.sum(p, axis=0, keepdims=True)
                acc_ref[h] = acc_ref[h] + jnp.dot(vt, p.astype(MXU_DTYPE), preferred_element_type=F32)

    def attend(exact):
        if exact:
            m_ref[...] = jnp.full(m_ref.shape, NEG_INF, F32)
        l_ref[...] = jnp.zeros(l_ref.shape, F32)
        acc_ref[...] = jnp.zeros(acc_ref.shape, F32)

        def far_body(c, carry):
            attend_chunk(c, None, exact)
            return carry

        lax.fori_loop(0, jnp.maximum(qi - 1, 0), far_body, 0)

        @pl.when(qi >= 1)
        def _():
            attend_chunk(qi - 1, 0, exact)

        attend_chunk(qi, 1, exact)

    attend(False)
    l_min = l_ref[0]
    for h in range(1, H_B):
        l_min = jnp.minimum(l_min, l_ref[h])
    shift_ok = jnp.min(l_min) >= MIN_SHIFTED_SUM
    pl.when(jnp.logical_not(shift_ok))(functools.partial(attend, True))

    for h in range(H_B):
        o_t = acc_ref[h] * (1.0 / l_ref[h])
        g = gate_ref[:, h * LANE:(h + 1) * LANE].astype(F32)
        o_ref[:, h * LANE:(h + 1) * LANE] = (o_t.T * _silu(g)).astype(o_ref.dtype)


def _attn_b(proj, wt, vt, t5_bias, gq_b, gk_b, blk, *, b, l, tq):
    t = proj.shape[0]
    nq = l // tq
    topk = min(TOPK_MAX, l // 4)
    row = lambda bi, qi: bi * nq + qi
    return pl.pallas_call(
        functools.partial(_attn_b_kernel, tq=tq, nq=nq, topk=topk, max_iters=BISECT_MAX_CHECKS),
        out_shape=jax.ShapeDtypeStruct((t, WIDTH_B), MXU_DTYPE),
        grid=(b, nq),
        in_specs=[
            pl.BlockSpec(memory_space=pltpu.SMEM),
            pl.BlockSpec((tq, IDX_HEADS * IDX_DIM), lambda bi, qi: (row(bi, qi), blk["q_idx"])),
            pl.BlockSpec((l, LANE), lambda bi, qi: (bi, blk["k_idx"])),
            pl.BlockSpec((IDX_HEADS, tq), lambda bi, qi: (0, row(bi, qi))),
            pl.BlockSpec((tq, WIDTH_B), lambda bi, qi: (row(bi, qi), blk["q_b"])),
            pl.BlockSpec((l, LANE), lambda bi, qi: (bi, blk["k_b"])),
            pl.BlockSpec((None, nq, LANE, tq), lambda bi, qi: (bi, 0, 0, 0)),
            pl.BlockSpec((tq, WIDTH_B), lambda bi, qi: (row(bi, qi), blk["gate_b"])),
            pl.BlockSpec((1, LANE), lambda bi, qi: (0, 0)),
            pl.BlockSpec((1, LANE), lambda bi, qi: (0, 0)),
        ],
        out_specs=pl.BlockSpec((tq, WIDTH_B), lambda bi, qi: (row(bi, qi), 0)),
        scratch_shapes=[
            pltpu.VMEM((nq, tq, tq), F32),
            pltpu.VMEM((H_B * tq, LANE), MXU_DTYPE),
            pltpu.VMEM((H_B, LANE, tq), F32),
            pltpu.VMEM((H_B, 1, tq), F32),
            pltpu.VMEM((H_B, 1, tq), F32),
            pltpu.VMEM((H_B, 2, tq, tq), F32),
            pltpu.VMEM((8, tq), F32),
            pltpu.VMEM((l, LANE), MXU_DTYPE),
            pltpu.SMEM((H_B,), F32),
        ],
        compiler_params=_params(
            ("arbitrary", "arbitrary"),
            [_nbytes((tq, IDX_HEADS * IDX_DIM + 2 * WIDTH_B), MXU_DTYPE), _nbytes((l, 2 * LANE), MXU_DTYPE),
             _nbytes((IDX_HEADS, tq), F32), _nbytes((nq, LANE, tq), MXU_DTYPE), _nbytes((tq, WIDTH_B), MXU_DTYPE)],
            scratch=(_nbytes((nq, tq, tq), F32) + _nbytes((H_B * tq + l, LANE), MXU_DTYPE)
                     + _nbytes((H_B, LANE + 2 * 8, tq), F32) + _nbytes((H_B, 2, tq, tq), F32) + _nbytes((8, tq), F32)),
            temps=2 * _nbytes((tq, H_B * tq), F32)),
        name="attn_b",
    )(t5_bias, proj, proj, wt, proj, proj, vt, proj, gq_b, gk_b)


def _sigmoid(z):
    return 1.0 / (1.0 + jnp.exp(-z))


def _merge_kernel(oa_ref, ob_ref, pa_ref, pb_ref, ma_ref, mb_ref, o_ref):
    a = jnp.dot(oa_ref[...], pa_ref[...], preferred_element_type=F32)
    bb = jnp.dot(ob_ref[...], pb_ref[...], preferred_element_type=F32)
    o_ref[...] = (_sigmoid(ma_ref[...].astype(F32)) * a + _sigmoid(mb_ref[...].astype(F32)) * bb).astype(o_ref.dtype)


def _merge(o_a, o_b, p_a, p_b, proj, ma_off, mb_off, *, tm, tn):
    t = o_a.shape[0]
    d = p_a.shape[1]
    ma0, mb0 = ma_off // tn, mb_off // tn
    return pl.pallas_call(
        _merge_kernel,
        out_shape=jax.ShapeDtypeStruct((t, d), MXU_DTYPE),
        grid=(t // tm, d // tn),
        in_specs=[
            pl.BlockSpec((tm, WIDTH_A), lambda i, j: (i, 0)),
            pl.BlockSpec((tm, WIDTH_B), lambda i, j: (i, 0)),
            pl.BlockSpec((WIDTH_A, tn), lambda i, j: (0, j)),
            pl.BlockSpec((WIDTH_B, tn), lambda i, j: (0, j)),
            pl.BlockSpec((tm, tn), lambda i, j: (i, ma0 + j)),
            pl.BlockSpec((tm, tn), lambda i, j: (i, mb0 + j)),
        ],
        out_specs=pl.BlockSpec((tm, tn), lambda i, j: (i, j)),
        compiler_params=_params(("arbitrary", "arbitrary"),
                                [_nbytes((tm, WIDTH_A + WIDTH_B), MXU_DTYPE), _nbytes((WIDTH_A + WIDTH_B, tn), MXU_DTYPE),
                                 3 * _nbytes((tm, tn), MXU_DTYPE)],
                                temps=2 * _nbytes((tm, tn), F32)),
        name="merge",
    )(o_a, o_b, p_a, p_b, proj, proj)


def _out_proj_kernel(m_ref, w_ref, x_ref, o_ref):
    o_ref[...] = x_ref[...] + jnp.dot(m_ref[...], w_ref[...], preferred_element_type=F32)


def _out_proj(merged, w_o, x2, *, tm, tn):
    t, d = x2.shape
    return pl.pallas_call(
        _out_proj_kernel,
        out_shape=jax.ShapeDtypeStruct((t, d), x2.dtype),
        grid=(t // tm, d // tn),
        in_specs=[
            pl.BlockSpec((tm, d), lambda i, j: (i, 0)),
            pl.BlockSpec((d, tn), lambda i, j: (0, j)),
            pl.BlockSpec((tm, tn), lambda i, j: (i, j)),
        ],
        out_specs=pl.BlockSpec((tm, tn), lambda i, j: (i, j)),
        compiler_params=_params(("arbitrary", "arbitrary"),
                                [_nbytes((tm, d), MXU_DTYPE), _nbytes((d, tn), MXU_DTYPE), 2 * _nbytes((tm, tn), x2.dtype)],
                                temps=_nbytes((tm, tn), F32)),
        name="out_proj",
    )(merged, w_o, x2)


def _rope_pad(a, axis):
    a1, a2 = jnp.split(a, 2, axis=axis)
    z = jnp.zeros_like(a1)
    return jnp.concatenate([a1, z, a2, z], axis=axis)


def _layout(d):
    names = [("merge_a", d), ("merge_b", d), ("gate_a", WIDTH_A), ("gate_b", WIDTH_B),
             ("q_b", WIDTH_B), ("q_idx", IDX_HEADS * IDX_DIM), ("cq", Q_LORA), ("ckv", KV_LORA),
             ("k_idx", LANE), ("k_rope", LANE), ("k_b", LANE), ("v_b", LANE)]
    off, out = 0, {}
    for name, width in names:
        assert off % width == 0, (name, off, width)
        out[name] = off
        off += width
    out["w_idx"] = out["k_idx"] + IDX_DIM
    return out, off


def kernel(x, positions, g_pre, w_in, g_q_lat, g_kv_lat, w_uq, w_ukv, g_qn_a, g_kn_a,
           g_qn_b, g_kn_b, t5_bias, p_a, p_b, w_o):
    b, l, d = x.shape
    t = b * l
    tq = 256
    tn_in = 512
    off, n_used = _layout(d)
    n_pad = -(-n_used // tn_in) * tn_in

    names = ["cq", "ckv", "k_rope", "q_b", "k_b", "v_b", "q_idx", "k_idx", "w_idx", "gate_a", "gate_b",
             "merge_a", "merge_b"]
    sizes = [Q_LORA, KV_LORA, QK_ROPE, WIDTH_B, HEAD_DIM_B, HEAD_DIM_B, IDX_HEADS * IDX_DIM, IDX_DIM,
             IDX_HEADS, WIDTH_A, WIDTH_B, d, d]
    src, acc = {}, 0
    for name, s in zip(names, sizes):
        src[name] = (acc, s)
        acc += s
    main_groups = ["merge_a", "merge_b", "gate_a", "gate_b", "q_b", "q_idx", "cq", "ckv"]
    src_starts, dst = [], 0
    for name in main_groups:
        assert off[name] == dst and src[name][1] % tn_in == 0
        src_starts += [src[name][0] + c for c in range(0, src[name][1], tn_in)]
        dst += src[name][1]
    wt = w_in.T
    rows = lambda name: wt[src[name][0]:src[name][0] + src[name][1]]
    z = lambda n: jnp.zeros((n, d), w_in.dtype)
    assert off["k_idx"] == dst
    wt_tail = jnp.concatenate(
        [rows("k_idx"), rows("w_idx"), z(LANE - IDX_DIM - IDX_HEADS), _rope_pad(rows("k_rope"), 0),
         rows("k_b"), rows("v_b"), z(n_pad - n_used)], axis=0)
    w_pad = _w_relayout(wt, src_starts, wt_tail, tn=tn_in, tc=d)

    w_uq3 = w_uq.reshape(Q_LORA, H_A, QK_DIM_A)
    w_uq_pad = jnp.concatenate([w_uq3[:, :, :QK_NOPE], _rope_pad(w_uq3[:, :, QK_NOPE:], 2)], axis=2)
    w_uq_pad = w_uq_pad.reshape(Q_LORA, H_A * HEAD_PAD_A).astype(MXU_DTYPE)
    gq_head = jnp.concatenate([g_qn_a[:QK_NOPE], _rope_pad(g_qn_a[QK_NOPE:], 0)]) * (QK_DIM_A ** -0.5 * LOG2E)
    gq_pad = jnp.tile(gq_head, H_A).reshape(1, H_A * HEAD_PAD_A).astype(F32)
    bound_a = QK_DIM_A * jnp.max(jnp.abs(gq_head)) * jnp.max(jnp.abs(g_kn_a)) * BOUND_MARGIN
    q_shift = jnp.zeros((1, LANE), F32).at[0, SHIFT_LANE - LANE].set(-bound_a)
    w_ukv3 = w_ukv.reshape(KV_LORA, H_A, QK_NOPE + V_DIM_A)
    w_uk = w_ukv3[:, :, :QK_NOPE].reshape(KV_LORA, H_A * QK_NOPE).astype(MXU_DTYPE)
    w_uv = w_ukv3[:, :, QK_NOPE:].reshape(KV_LORA, WIDTH_A).astype(MXU_DTYPE)
    gk_nope = g_kn_a[:QK_NOPE].reshape(1, LANE).astype(F32)
    gk_rope = _rope_pad(g_kn_a[QK_NOPE:], 0).reshape(1, LANE).astype(F32)
    gq_b = (g_qn_b * (HEAD_DIM_B ** -0.5 * LOG2E)).reshape(1, LANE).astype(F32)
    gk_b = g_kn_b.reshape(1, LANE).astype(F32)

    inv = ROPE_THETA ** (-jnp.arange(HALF_ROPE, dtype=F32) / HALF_ROPE)
    ang_off = positions[:, :1, None].astype(F32) * inv
    ang_rel = jnp.arange(l, dtype=F32)[None, :, None] * inv
    cos = (jnp.cos(ang_off) * jnp.cos(ang_rel) - jnp.sin(ang_off) * jnp.sin(ang_rel)).reshape(t, HALF_ROPE)
    sin = (jnp.sin(ang_off) * jnp.cos(ang_rel) + jnp.cos(ang_off) * jnp.sin(ang_rel)).reshape(t, HALF_ROPE)
    zr = jnp.zeros_like(cos)
    cos_t = jnp.concatenate([cos, zr, cos, zr], axis=1)
    sin_t = jnp.concatenate([-sin, zr, sin, zr], axis=1)

    x2 = x.reshape(t, d)
    proj = _in_proj(x2, g_pre, w_pad, tm=1024, tn=2 * tn_in if n_pad % (2 * tn_in) == 0 else tn_in)

    q_a = _qa_proj(proj, off["cq"] // Q_LORA, g_q_lat, w_uq_pad, gq_pad, cos_t, sin_t, q_shift, tm=512)
    k_a, v_a = _kva_proj(proj, off["ckv"] // KV_LORA, off["k_rope"] // LANE, g_kv_lat, w_uk, w_uv,
                         gk_nope, gk_rope, cos_t, sin_t, tm=512)
    o_a = _attn_a(q_a, k_a, v_a, proj, off["gate_a"] // LANE, b=b, l=l, tq=tq, heads=4)

    nq = l // tq
    w_idx = proj[:, off["w_idx"]:off["w_idx"] + IDX_HEADS]
    w_idx_t = w_idx.astype(F32).T
    v_b = proj[:, off["v_b"]:off["v_b"] + HEAD_DIM_B]
    vt = v_b.reshape(b, nq, tq, HEAD_DIM_B).transpose(0, 1, 3, 2)
    blk = {"q_idx": off["q_idx"] // (IDX_HEADS * IDX_DIM), "k_idx": off["k_idx"] // LANE,
           "q_b": off["q_b"] // WIDTH_B, "k_b": off["k_b"] // LANE, "gate_b": off["gate_b"] // WIDTH_B}
    o_b = _attn_b(proj, w_idx_t, vt, t5_bias.astype(F32), gq_b, gk_b, blk, b=b, l=l, tq=tq)

    merged = _merge(o_a, o_b, p_a.astype(MXU_DTYPE), p_b.astype(MXU_DTYPE), proj,
                    off["merge_a"], off["merge_b"], tm=1024, tn=1024)
    out = _out_proj(merged, w_o.astype(MXU_DTYPE), x2, tm=1024, tn=1024)
    return out.reshape(b, l, d)
```
